```python
import jax, jax.numpy as jnp
from jax import lax
import numpy as np

D_MODEL = 2048
BATCH = 2
SEQ = 8192
DEPTH = 4

CHUNK = 64
MEM_LEN = 256
HEAD_DIM = 128
MIX_WIDTH = D_MODEL
HALF_WIDTH = MIX_WIDTH // 2
POOL_GROUPS = 4
POOL_WINDOWS = (2, 4, 8, 16)
POOL_GROUP_DIM = HALF_WIDTH // POOL_GROUPS
DSA_HEADS = HALF_WIDTH // HEAD_DIM
IDX_HEADS = 16
IDX_DIM = 64
DSA_TOPK_MAX = 256
DSA_BLOCK = 64
CONV_WIDTH = 3
CONV_DIM = HALF_WIDTH
FOX_HEADS = HALF_WIDTH // HEAD_DIM
FOX_BLOCK = 128
FORGET_BIAS_INIT = 2.0
XA_HEADS = 4
XA_WIDTH = XA_HEADS * HEAD_DIM
FFN_HIDDEN = -(-(8 * D_MODEL) // (3 * 256)) * 256
ROPE_THETA = 500000.0
ROPE_FRACTION = 4
LN_EPS = 1e-5
ALPHA = (2 * DEPTH) ** 0.25
BETA = (8 * DEPTH) ** -0.25
N_EVEN = (DEPTH + 1) // 2
N_ODD = DEPTH // 2
EV_SIZES = (HALF_WIDTH, HALF_WIDTH, HALF_WIDTH, HALF_WIDTH, IDX_HEADS * IDX_DIM, IDX_DIM, IDX_HEADS)
EV_IN = sum(EV_SIZES)
OD_SIZES = (CONV_DIM, CONV_DIM, CONV_DIM, HALF_WIDTH, HALF_WIDTH, HALF_WIDTH, FOX_HEADS)
OD_IN = sum(OD_SIZES)

kernel_name = "hybrid_pool_dsa_conv_fox_deepnorm_trunk"


def split_cols(h, sizes):
    offs, acc = [], 0
    for s in sizes[:-1]:
        acc += s
        offs.append(acc)
    return jnp.split(h, offs, axis=-1)


def layer_norm(x, g, b):
    xf = x.astype(jnp.float32)
    mu = jnp.mean(xf, axis=-1, keepdims=True)
    var = jnp.mean(jnp.square(xf - mu), axis=-1, keepdims=True)
    return ((xf - mu) * lax.rsqrt(var + LN_EPS) * g + b).astype(x.dtype)


def partial_rotary(x, positions):
    dh = x.shape[-1]
    rot = dh // ROPE_FRACTION
    half = rot // 2
    inv_freq = jnp.power(ROPE_THETA, -(jnp.arange(half, dtype=jnp.float32) * 2.0 / rot))
    ang = positions.astype(jnp.float32)[..., None] * inv_freq
    cos = jnp.cos(ang)[:, :, None, :]
    sin = jnp.sin(ang)[:, :, None, :]
    x1 = x[..., :half].astype(jnp.float32)
    x2 = x[..., half:rot].astype(jnp.float32)
    xr = jnp.concatenate([x1 * cos - x2 * sin, x2 * cos + x1 * sin], axis=-1).astype(x.dtype)
    return jnp.concatenate([xr, x[..., rot:]], axis=-1)


def pool_mixer(u, w_groups, scale):
    B, S, C = u.shape
    uf = u.astype(jnp.float32)
    cs = jnp.cumsum(uf, axis=1)
    cnt = jnp.arange(1, S + 1, dtype=jnp.float32)[None, :, None]
    ug = jnp.split(uf, POOL_GROUPS, axis=-1)
    cg = jnp.split(cs, POOL_GROUPS, axis=-1)
    outs = []
    for w, u_g, c_g in zip(POOL_WINDOWS, ug, cg):
        lag = jnp.pad(c_g, ((0, 0), (w, 0), (0, 0)))[:, :S]
        outs.append((c_g - lag) / jnp.minimum(cnt, float(w)) - u_g)
    d = jnp.stack(outs, axis=2).astype(u.dtype)
    y = jnp.einsum('bsgc,gcd->bsgd', d, w_groups)
    return y.reshape(B, S, C) * scale


def dsa_attention(q, k, v, qi, ki, wi):
    B, S, H, Dh = q.shape
    topk = min(DSA_TOPK_MAX, S // 4)
    nblk = S // DSA_BLOCK
    s_idx = jnp.arange(S, dtype=jnp.int32)
    scale = Dh ** -0.5
    gather = jax.vmap(lambda a, i: a[i])

    def block(args):
        qb, qib, wib, t0 = args
        t = t0 + jnp.arange(DSA_BLOCK, dtype=jnp.int32)
        il = jnp.einsum('bthd,bsd->bths', qib, ki) * (IDX_DIM ** -0.5)
        score = jnp.einsum('bths,bth->bts', jax.nn.relu(il), wib).astype(jnp.float32)
        admissible = (s_idx[None, :] // CHUNK) <= (t[:, None] // CHUNK)
        score = jnp.where(admissible[None], score, -jnp.inf)
        _, idx = lax.top_k(score, topk)
        valid = (idx // CHUNK) <= (t // CHUNK)[None, :, None]
        ks = gather(k, idx)
        vs = gather(v, idx)
        logits = jnp.einsum('bthd,btkhd->bthk', qb, ks).astype(jnp.float32) * scale
        logits = jnp.where(valid[:, :, None, :], logits, -jnp.inf)
        p = jax.nn.softmax(logits, axis=-1).astype(v.dtype)
        return jnp.einsum('bthk,btkhd->bthd', p, vs)

    def blocks(a):
        return a.reshape((B, nblk, DSA_BLOCK) + a.shape[2:]).swapaxes(0, 1)

    starts = jnp.arange(nblk, dtype=jnp.int32) * DSA_BLOCK
    out = lax.map(block, (blocks(q), blocks(qi), blocks(wi), starts))
    return out.swapaxes(0, 1).reshape(B, S, H * Dh)


def short_conv(u, w):
    C = u.shape[-1]
    return lax.conv_general_dilated(
        u, w[:, None, :], window_strides=(1,), padding=[(CONV_WIDTH - 1, 0)],
        dimension_numbers=('NWC', 'WIO', 'NWC'), feature_group_count=C)


def fox_attention(q, k, v, log_f):
    B, S, H, Dh = q.shape
    nblk = S // FOX_BLOCK
    Fc = jnp.cumsum(log_f, axis=1).transpose(0, 2, 1)
    s_idx = jnp.arange(S, dtype=jnp.int32)
    scale = Dh ** -0.5

    def block(args):
        qb, Fb, t0 = args
        t = t0 + jnp.arange(FOX_BLOCK, dtype=jnp.int32)
        logits = jnp.einsum('bthd,bshd->bhts', qb, k).astype(jnp.float32) * scale
        logits = logits + Fb[..., None] - Fc[:, :, None, :]
        logits = jnp.where((s_idx[None, :] <= t[:, None])[None, None], logits, -jnp.inf)
        p = jax.nn.softmax(logits, axis=-1).astype(v.dtype)
        return jnp.einsum('bhts,bshd->bthd', p, v)

    qb = q.reshape(B, nblk, FOX_BLOCK, H, Dh).swapaxes(0, 1)
    Fb = Fc.reshape(B, H, nblk, FOX_BLOCK).transpose(2, 0, 1, 3)
    starts = jnp.arange(nblk, dtype=jnp.int32) * FOX_BLOCK
    out = lax.map(block, (qb, Fb, starts))
    return out.swapaxes(0, 1).reshape(B, S, H * Dh)


def even_mixer(x, positions, w_in, pool_w, pool_scale, w_out):
    B, S, _ = x.shape
    h = x @ w_in
    u_pool, q, k, v, qi, ki, wi = split_cols(h, EV_SIZES)
    a = pool_mixer(u_pool, pool_w, pool_scale)
    q = partial_rotary(q.reshape(B, S, DSA_HEADS, HEAD_DIM), positions)
    k = partial_rotary(k.reshape(B, S, DSA_HEADS, HEAD_DIM), positions)
    v = v.reshape(B, S, DSA_HEADS, HEAD_DIM)
    qi = partial_rotary(qi.reshape(B, S, IDX_HEADS, IDX_DIM), positions)
    ki = partial_rotary(ki[:, :, None, :], positions)[:, :, 0]
    wi = wi * (IDX_HEADS ** -0.5)
    b = dsa_attention(q, k, v, qi, ki, wi)
    return jnp.concatenate([a, b], axis=-1) @ w_out


def odd_mixer(x, w_in, conv_w, forget_b, w_out):
    B, S, _ = x.shape
    h = x @ w_in
    u, gate_b, gate_c, q, k, v, f_logit = split_cols(h, OD_SIZES)
    c = gate_b * short_conv(gate_c * u, conv_w)
    log_f = jax.nn.log_sigmoid((f_logit + forget_b).astype(jnp.float32))
    d = fox_attention(q.reshape(B, S, FOX_HEADS, HEAD_DIM), k.reshape(B, S, FOX_HEADS, HEAD_DIM),
                      v.reshape(B, S, FOX_HEADS, HEAD_DIM), log_f)
    return jnp.concatenate([c, d], axis=-1) @ w_out


def memory_cross_attention(x, mem, w_q, w_kv, w_o):
    B, S, _ = x.shape
    M = mem.shape[1]
    q = (x @ w_q).reshape(B, S, XA_HEADS, HEAD_DIM)
    kv = (mem @ w_kv).reshape(B, M, 2, XA_HEADS, HEAD_DIM)
    k, v = kv[:, :, 0], kv[:, :, 1]
    logits = jnp.einsum('bshd,bmhd->bhsm', q, k).astype(jnp.float32) * (HEAD_DIM ** -0.5)
    p = jax.nn.softmax(logits, axis=-1).astype(x.dtype)
    o = jnp.einsum('bhsm,bmhd->bshd', p, v).reshape(B, S, XA_WIDTH)
    return o @ w_o


def swiglu_ffn(x, w_in, w_out):
    gate, up = jnp.split(x @ w_in, 2, axis=-1)
    return (jax.nn.silu(gate) * up) @ w_out


def setup_inputs(seed: int = 0) -> dict:
    key = jax.random.key(seed)
    ks = jax.random.split(key, 20)

    def nrm(k, shape, scale):
        return jax.random.normal(k, shape, jnp.float32) * scale

    x = nrm(ks[0], (BATCH, SEQ, D_MODEL), 1.0)
    mem = nrm(ks[1], (BATCH, MEM_LEN, D_MODEL), 1.0)
    offs = jax.random.randint(ks[2], (BATCH,), 0, 64, dtype=jnp.int32) * CHUNK
    positions = (offs[:, None] + jnp.arange(SEQ, dtype=jnp.int32)[None, :]).astype(jnp.int32)
    return {
        "x": x,
        "mem": mem,
        "positions": positions,
        "ev_w_in": nrm(ks[3], (N_EVEN, D_MODEL, EV_IN), D_MODEL ** -0.5),
        "ev_pool_w": nrm(ks[4], (N_EVEN, POOL_GROUPS, POOL_GROUP_DIM, POOL_GROUP_DIM), POOL_GROUP_DIM ** -0.5),
        "ev_pool_scale": 1.0 + nrm(ks[5], (N_EVEN, HALF_WIDTH), 0.02),
        "ev_w_out": nrm(ks[6], (N_EVEN, MIX_WIDTH, D_MODEL), BETA * MIX_WIDTH ** -0.5),
        "od_w_in": nrm(ks[7], (N_ODD, D_MODEL, OD_IN), D_MODEL ** -0.5),
        "od_conv_w": nrm(ks[8], (N_ODD, CONV_WIDTH, CONV_DIM), CONV_WIDTH ** -0.5),
        "od_forget_b": FORGET_BIAS_INIT + nrm(ks[9], (N_ODD, FOX_HEADS), 0.1),
        "od_w_out": nrm(ks[10], (N_ODD, MIX_WIDTH, D_MODEL), BETA * MIX_WIDTH ** -0.5),
        "ca_w_q": nrm(ks[11], (DEPTH, D_MODEL, XA_WIDTH), D_MODEL ** -0.5),
        "ca_w_kv": nrm(ks[12], (DEPTH, D_MODEL, 2 * XA_WIDTH), D_MODEL ** -0.5),
        "ca_w_o": nrm(ks[13], (DEPTH, XA_WIDTH, D_MODEL), BETA * XA_WIDTH ** -0.5),
        "ffn_w_in": nrm(ks[14], (DEPTH, D_MODEL, 2 * FFN_HIDDEN), D_MODEL ** -0.5),
        "ffn_w_out": nrm(ks[15], (DEPTH, FFN_HIDDEN, D_MODEL), BETA * FFN_HIDDEN ** -0.5),
        "ln_g": 1.0 + nrm(ks[16], (DEPTH, 3, D_MODEL), 0.02),
        "ln_b": nrm(ks[17], (DEPTH, 3, D_MODEL), 0.02),
    }


def reference(x, mem, positions, ev_w_in, ev_pool_w, ev_pool_scale, ev_w_out,
              od_w_in, od_conv_w, od_forget_b, od_w_out,
              ca_w_q, ca_w_kv, ca_w_o, ffn_w_in, ffn_w_out, ln_g, ln_b):
    for i in range(DEPTH):
        j = i // 2
        if i % 2 == 0:
            m = even_mixer(x, positions, ev_w_in[j], ev_pool_w[j], ev_pool_scale[j], ev_w_out[j])
        else:
            m = odd_mixer(x, od_w_in[j], od_conv_w[j], od_forget_b[j], od_w_out[j])
        x = layer_norm(ALPHA * x + m, ln_g[i, 0], ln_b[i, 0])
        x = layer_norm(ALPHA * x + memory_cross_attention(x, mem, ca_w_q[i], ca_w_kv[i], ca_w_o[i]),
                       ln_g[i, 1], ln_b[i, 1])
        x = layer_norm(ALPHA * x + swiglu_ffn(x, ffn_w_in[i], ffn_w_out[i]), ln_g[i, 2], ln_b[i, 2])
    return x
```

```python
import functools

import jax
import jax.numpy as jnp
from jax import lax
from jax.experimental import pallas as pl
from jax.experimental.pallas import tpu as pltpu

F32 = jnp.float32
BF16 = jnp.bfloat16

D_MODEL = 2048
DEPTH = 4
CHUNK = 64
HEAD_DIM = 128
HALF_WIDTH = D_MODEL // 2
POOL_WINDOWS = (2, 4, 8, 16)
POOL_GROUP_DIM = HALF_WIDTH // len(POOL_WINDOWS)
ATT_HEADS = HALF_WIDTH // HEAD_DIM
IDX_HEADS = 16
IDX_DIM = 64
DSA_TOPK_MAX = 256
CONV_WIDTH = 3
XA_HEADS = 4
XA_WIDTH = XA_HEADS * HEAD_DIM
FFN_HIDDEN = -(-(8 * D_MODEL) // (3 * 256)) * 256
ROPE_THETA = 500000.0
ROPE_FRACTION = 4
LN_EPS = 1e-5
ALPHA = (2 * DEPTH) ** 0.25

LANES = 128
NEG = -1e30
INT_MIN = -(2 ** 31)
VMEM_LIMIT = 56 * 1024 * 1024


def _params(*semantics):
    return pltpu.CompilerParams(dimension_semantics=semantics, vmem_limit_bytes=VMEM_LIMIT)


def _layer_norm(y, g, b):
    mu = jnp.mean(y, axis=-1, keepdims=True)
    d = y - mu
    var = jnp.mean(d * d, axis=-1, keepdims=True)
    return d * lax.rsqrt(var + LN_EPS) * g + b


def _proj_kernel(*refs, shift, scale):
    if shift:
        x_ref, w_ref, c_ref, s1_ref, s2_ref, o_ref = refs
    else:
        x_ref, w_ref, o_ref = refs
    h = jnp.dot(x_ref[...], w_ref[...], preferred_element_type=F32)
    if shift:
        c, s1, s2 = c_ref[...], s1_ref[...], s2_ref[...]
        for g in range(h.shape[1] // LANES):
            hg = h[:, g * LANES:(g + 1) * LANES]
            out = hg * c + pltpu.roll(hg, shift, 1) * s1 + pltpu.roll(hg, LANES - shift, 1) * s2
            o_ref[:, g * LANES:(g + 1) * LANES] = out.astype(o_ref.dtype)
    elif scale != 1.0:
        o_ref[...] = (h * scale).astype(o_ref.dtype)
    else:
        o_ref[...] = h.astype(o_ref.dtype)


def _proj(x, w, out_dtype, tables=None, shift=0, scale=1.0, tm=512, tn=512):
    n, k = x.shape
    c = w.shape[1]
    tm = min(tm, n)
    tn = min(tn, c)
    in_specs = [pl.BlockSpec((tm, k), lambda i, j: (i, 0)),
                pl.BlockSpec((k, tn), lambda i, j: (0, j))]
    args = [x, w]
    if shift:
        in_specs += [pl.BlockSpec((tm, LANES), lambda i, j: (i, 0))] * 3
        args += list(tables)
    return pl.pallas_call(
        functools.partial(_proj_kernel, shift=shift, scale=scale),
        grid=(n // tm, c // tn),
        in_specs=in_specs,
        out_specs=pl.BlockSpec((tm, tn), lambda i, j: (i, j)),
        out_shape=jax.ShapeDtypeStruct((n, c), out_dtype),
        compiler_params=_params("parallel", "arbitrary"),
        name="proj",
    )(*args)


def _out_ln_kernel(*refs, nparts):
    parts = refs[:nparts]
    w_ref, x_ref, g_ref, b_ref, of_ref, ob_ref = refs[nparts:]
    acc = None
    off = 0
    for p in parts:
        kp = p.shape[1]
        t = jnp.dot(p[...], w_ref[off:off + kp, :], preferred_element_type=F32)
        acc = t if acc is None else acc + t
        off += kp
    y = _layer_norm(ALPHA * x_ref[...] + acc, g_ref[...], b_ref[...])
    of_ref[...] = y
    ob_ref[...] = y.astype(BF16)


def _out_ln(parts, w, x, g, b, tm=256):
    n, d = x.shape
    in_specs = [pl.BlockSpec((tm, p.shape[1]), lambda i: (i, 0)) for p in parts]
    in_specs += [pl.BlockSpec(w.shape, lambda i: (0, 0)),
                 pl.BlockSpec((tm, d), lambda i: (i, 0)),
                 pl.BlockSpec((1, d), lambda i: (0, 0)),
                 pl.BlockSpec((1, d), lambda i: (0, 0))]
    return pl.pallas_call(
        functools.partial(_out_ln_kernel, nparts=len(parts)),
        grid=(n // tm,),
        in_specs=in_specs,
        out_specs=[pl.BlockSpec((tm, d), lambda i: (i, 0)), pl.BlockSpec((tm, d), lambda i: (i, 0))],
        out_shape=[jax.ShapeDtypeStruct((n, d), F32), jax.ShapeDtypeStruct((n, d), BF16)],
        compiler_params=_params("parallel"),
        name="out_ln",
    )(*parts, w, x, g, b)


POOL_HALO = 16


def _pool_kernel(u_ref, halo_ref, w_ref, sc_ref, o_ref, ext, *, ts):
    i = pl.program_id(1)
    ext[0:POOL_HALO, :] = jnp.where(i > 0, halo_ref[0], 0.0)
    ext[POOL_HALO:, :] = u_ref[0]
    cnt = i * ts + lax.broadcasted_iota(jnp.int32, (ts, 1), 0) + 1
    for g, win in enumerate(POOL_WINDOWS):
        lo, hi = g * POOL_GROUP_DIM, (g + 1) * POOL_GROUP_DIM
        cur = ext[POOL_HALO:POOL_HALO + ts, lo:hi]
        s = cur
        for j in range(1, win):
            s = s + ext[POOL_HALO - j:POOL_HALO - j + ts, lo:hi]
        d = s / jnp.minimum(cnt, win).astype(F32) - cur
        y = jnp.dot(d.astype(BF16), w_ref[g], preferred_element_type=F32)
        o_ref[0, :, lo:hi] = (y * sc_ref[:, lo:hi]).astype(o_ref.dtype)


def _pool(u, w, scale, ts=512):
    b, s, c = u.shape
    ts = min(ts, s)
    hb = ts // POOL_HALO
    return pl.pallas_call(
        functools.partial(_pool_kernel, ts=ts),
        grid=(b, s // ts),
        in_specs=[pl.BlockSpec((1, ts, c), lambda bi, i: (bi, i, 0)),
                  pl.BlockSpec((1, POOL_HALO, c), lambda bi, i: (bi, jnp.maximum(i * hb - 1, 0), 0)),
                  pl.BlockSpec(w.shape, lambda bi, i: (0, 0, 0)),
                  pl.BlockSpec((1, c), lambda bi, i: (0, 0))],
        out_specs=pl.BlockSpec((1, ts, c), lambda bi, i: (bi, i, 0)),
        out_shape=jax.ShapeDtypeStruct((b, s, c), BF16),
        scratch_shapes=[pltpu.VMEM((ts + POOL_HALO, c), F32)],
        compiler_params=_params("parallel", "arbitrary"),
        name="pool",
    )(u, u, w, scale)


def _sortable(v):
    bits = lax.bitcast_convert_type(v, jnp.int32)
    return bits ^ ((bits >> 31) & 0x7FFFFFFF)


def _idx_kernel(qi_ref, wi_ref, ki_ref, o_ref, keys, *, tq, tk, nk, topk):
    i = pl.program_id(1)
    nact = (i * tq + tq - 1) // tk + 1
    wi = wi_ref[0]
    row_chunk = (i * tq + lax.broadcasted_iota(jnp.int32, (tq, 1), 0)) // CHUNK

    def score_tile(c, carry):
        kt = ki_ref[0, c]
        acc = jnp.zeros((tq, tk), F32)
        for h in range(IDX_HEADS):
            il = jnp.dot(qi_ref[0, :, h * IDX_DIM:(h + 1) * IDX_DIM], kt, preferred_element_type=F32)
            acc = acc + jnp.maximum(il, 0.0) * wi[:, h:h + 1]
        col_chunk = (c * tk + lax.broadcasted_iota(jnp.int32, (1, tk), 1)) // CHUNK
        keys[c] = jnp.where(col_chunk <= row_chunk, _sortable(acc), INT_MIN)
        return carry

    lax.fori_loop(0, nact, score_tile, 0)

    def count_ge(cand):
        def body(c, acc):
            ge = keys[c] >= cand
            for g in range(tk // LANES):
                acc = acc + jnp.where(ge[:, g * LANES:(g + 1) * LANES], 1, 0)
            return acc
        acc = lax.fori_loop(0, nact, body, jnp.zeros((tq, LANES), jnp.int32))
        return jnp.sum(acc, axis=1, keepdims=True)

    thr = jnp.where(count_ge(jnp.zeros((tq, 1), jnp.int32)) >= topk, 0, INT_MIN)

    def bit_step(b, thr):
        cand = thr | (1 << (30 - b))
        return jnp.where(count_ge(cand) >= topk, cand, thr)

    thr = lax.fori_loop(0, 31, bit_step, thr)
    thr = jnp.maximum(thr, INT_MIN + 1)

    def write_tile(c, carry):
        o_ref[0, 0, c] = jnp.where(keys[c] >= thr, 0.0, NEG).astype(o_ref.dtype)
        return carry

    lax.fori_loop(0, nact, write_tile, 0)

    def fill_tile(c, carry):
        o_ref[0, 0, c] = jnp.full((tq, tk), NEG, o_ref.dtype)
        return carry

    lax.fori_loop(nact, nk, fill_tile, 0)


def _idx_mask(qi, wi, ki_tiles, tq, tk):
    b, s, _ = qi.shape
    nq, nk = s // tq, s // tk
    topk = min(DSA_TOPK_MAX, s // 4)
    return pl.pallas_call(
        functools.partial(_idx_kernel, tq=tq, tk=tk, nk=nk, topk=topk),
        grid=(b, nq),
        in_specs=[pl.BlockSpec((1, tq, IDX_HEADS * IDX_DIM), lambda bi, i: (bi, i, 0)),
                  pl.BlockSpec((1, tq, IDX_HEADS), lambda bi, i: (bi, i, 0)),
                  pl.BlockSpec((1, nk, IDX_DIM, tk), lambda bi, i: (bi, 0, 0, 0))],
        out_specs=pl.BlockSpec((1, 1, nk, tq, tk), lambda bi, i: (bi, i, 0, 0, 0)),
        out_shape=jax.ShapeDtypeStruct((b, nq, nk, tq, tk), BF16),
        scratch_shapes=[pltpu.VMEM((nk, tq, tk), jnp.int32)],
        compiler_params=_params("parallel", "arbitrary"),
        name="idx_mask",
    )(qi, wi, ki_tiles)


def _softmax_step(s, v, h, m_sc, l_sc, acc_sc):
    m_prev = m_sc[h]
    m_new = jnp.maximum(m_prev, jnp.max(s, axis=1, keepdims=True))
    alpha = jnp.exp(m_prev - m_new)
    p = jnp.exp(s - m_new)
    l_sc[h] = alpha * l_sc[h] + jnp.sum(p, axis=1, keepdims=True)
    acc_sc[h] = alpha * acc_sc[h] + jnp.dot(p.astype(BF16), v, preferred_element_type=F32)
    m_sc[h] = m_new


def _softmax_init(m_sc, l_sc, acc_sc):
    m_sc[...] = jnp.full(m_sc.shape, NEG, F32)
    l_sc[...] = jnp.zeros(l_sc.shape, F32)
    acc_sc[...] = jnp.zeros(acc_sc.shape, F32)


def _softmax_finish(o_ref, l_sc, acc_sc):
    for h in range(ATT_HEADS):
        o_ref[0, :, h * HEAD_DIM:(h + 1) * HEAD_DIM] = (acc_sc[h] / l_sc[h]).astype(o_ref.dtype)


def _qk(q_ref, k_ref, h):
    q = q_ref[0, :, h * HEAD_DIM:(h + 1) * HEAD_DIM]
    k = k_ref[0, :, h * HEAD_DIM:(h + 1) * HEAD_DIM]
    return lax.dot_general(q, k, (((1,), (1,)), ((), ())), preferred_element_type=F32)


def _att_scratch(tq):
    return [pltpu.VMEM((ATT_HEADS, tq, 1), F32), pltpu.VMEM((ATT_HEADS, tq, 1), F32),
            pltpu.VMEM((ATT_HEADS, tq, HEAD_DIM), F32)]


def _dsa_kernel(q_ref, k_ref, v_ref, mask_ref, o_ref, m_sc, l_sc, acc_sc, *, tq, tk):
    i, j = pl.program_id(1), pl.program_id(2)
    last = (i * tq + tq - 1) // tk

    @pl.when(j == 0)
    def _():
        _softmax_init(m_sc, l_sc, acc_sc)

    @pl.when(j <= last)
    def _():
        mask = mask_ref[0, 0, 0].astype(F32)
        for h in range(ATT_HEADS):
            s = _qk(q_ref, k_ref, h) + mask
            _softmax_step(s, v_ref[0, :, h * HEAD_DIM:(h + 1) * HEAD_DIM], h, m_sc, l_sc, acc_sc)

    @pl.when(j == last)
    def _():
        _softmax_finish(o_ref, l_sc, acc_sc)


def _dsa(q, k, v, mask, tq, tk):
    b, s, c = q.shape
    nq, nk = s // tq, s // tk

    def kv_map(bi, i, j):
        return (bi, jnp.minimum(j, (i * tq + tq - 1) // tk), 0)

    def mask_map(bi, i, j):
        return (bi, i, jnp.minimum(j, (i * tq + tq - 1) // tk), 0, 0)

    return pl.pallas_call(
        functools.partial(_dsa_kernel, tq=tq, tk=tk),
        grid=(b, nq, nk),
        in_specs=[pl.BlockSpec((1, tq, c), lambda bi, i, j: (bi, i, 0)),
                  pl.BlockSpec((1, tk, c), kv_map),
                  pl.BlockSpec((1, tk, c), kv_map),
                  pl.BlockSpec((1, 1, 1, tq, tk), mask_map)],
        out_specs=pl.BlockSpec((1, tq, c), lambda bi, i, j: (bi, i, 0)),
        out_shape=jax.ShapeDtypeStruct((b, s, c), BF16),
        scratch_shapes=_att_scratch(tq),
        compiler_params=_params("parallel", "parallel", "arbitrary"),
        name="dsa",
    )(q, k, v, mask)


CONV_HALO = 8


def _conv_kernel(u_ref, gb_ref, gc_ref, hu_ref, hgc_ref, w_ref, o_ref, ext, *, ts):
    i = pl.program_id(1)
    ext[0:CONV_HALO, :] = jnp.where(i > 0, hgc_ref[0] * hu_ref[0], 0.0)
    ext[CONV_HALO:, :] = gc_ref[0] * u_ref[0]
    conv = None
    for t in range(CONV_WIDTH):
        off = CONV_HALO - (CONV_WIDTH - 1) + t
        term = ext[off:off + ts, :] * w_ref[t:t + 1, :]
        conv = term if conv is None else conv + term
    o_ref[0] = (gb_ref[0] * conv).astype(o_ref.dtype)


def _gated_conv(h, w, ts=512):
    b, s, c3 = h.shape
    c = c3 // 3
    ts = min(ts, s)
    hb = ts // CONV_HALO

    def halo_map(col):
        return lambda bi, i: (bi, jnp.maximum(i * hb - 1, 0), col)

    return pl.pallas_call(
        functools.partial(_conv_kernel, ts=ts),
        grid=(b, s // ts),
        in_specs=[pl.BlockSpec((1, ts, c), lambda bi, i: (bi, i, 0)),
                  pl.BlockSpec((1, ts, c), lambda bi, i: (bi, i, 1)),
                  pl.BlockSpec((1, ts, c), lambda bi, i: (bi, i, 2)),
                  pl.BlockSpec((1, CONV_HALO, c), halo_map(0)),
                  pl.BlockSpec((1, CONV_HALO, c), halo_map(2)),
                  pl.BlockSpec(w.shape, lambda bi, i: (0, 0))],
        out_specs=pl.BlockSpec((1, ts, c), lambda bi, i: (bi, i, 0)),
        out_shape=jax.ShapeDtypeStruct((b, s, c), BF16),
        scratch_shapes=[pltpu.VMEM((ts + CONV_HALO, c), F32)],
        compiler_params=_params("parallel", "arbitrary"),
        name="gated_conv",
    )(h, h, h, h, h, w)


GATE_CHUNK = 256


def _fgate_kernel(f_ref, fb_ref, o_ref):
    s = f_ref.shape[1]
    r = lax.broadcasted_iota(jnp.int32, (GATE_CHUNK, GATE_CHUNK), 0)
    c = lax.broadcasted_iota(jnp.int32, (GATE_CHUNK, GATE_CHUNK), 1)
    tri = (c <= r).astype(F32)

    def body(t, carry):
        z = f_ref[0, pl.ds(t * GATE_CHUNK, GATE_CHUNK), :] + fb_ref[...]
        log_f = -(jnp.maximum(-z, 0.0) + jnp.log1p(jnp.exp(-jnp.abs(z))))
        cs = jnp.dot(tri, log_f, preferred_element_type=F32, precision=lax.Precision.HIGHEST) + carry
        o_ref[0, pl.ds(t * GATE_CHUNK, GATE_CHUNK), :] = cs
        return cs[GATE_CHUNK - 1:GATE_CHUNK, :]

    lax.fori_loop(0, s // GATE_CHUNK, body, jnp.zeros((1, LANES), F32))


def _forget_cumsum(f, fb):
    b, s, c = f.shape
    return pl.pallas_call(
        _fgate_kernel,
        grid=(b,),
        in_specs=[pl.BlockSpec((1, s, c), lambda bi: (bi, 0, 0)), pl.BlockSpec((1, c), lambda bi: (0, 0))],
        out_specs=pl.BlockSpec((1, s, c), lambda bi: (bi, 0, 0)),
        out_shape=jax.ShapeDtypeStruct((b, s, c), F32),
        compiler_params=_params("parallel"),
        name="forget_cumsum",
    )(f, fb)


def _fox_kernel(q_ref, k_ref, v_ref, fc_ref, fr_ref, o_ref, m_sc, l_sc, acc_sc, *, t):
    i, j = pl.program_id(1), pl.program_id(2)

    @pl.when(j == 0)
    def _():
        _softmax_init(m_sc, l_sc, acc_sc)

    def step(diagonal):
        if diagonal:
            causal = (lax.broadcasted_iota(jnp.int32, (t, t), 1) <= lax.broadcasted_iota(jnp.int32, (t, t), 0))
        for h in range(ATT_HEADS):
            s = _qk(q_ref, k_ref, h) + (fc_ref[0, :, h:h + 1] - fr_ref[0, h:h + 1, :])
            if diagonal:
                s = jnp.where(causal, s, NEG)
            _softmax_step(s, v_ref[0, :, h * HEAD_DIM:(h + 1) * HEAD_DIM], h, m_sc, l_sc, acc_sc)

    @pl.when(j < i)
    def _():
        step(False)

    @pl.when(j == i)
    def _():
        step(True)
        _softmax_finish(o_ref, l_sc, acc_sc)


def _fox(q, k, v, fc_col, fc_row, t=512):
    b, s, c = q.shape
    t = min(t, s)
    n = s // t

    def kv_map(bi, i, j):
        return (bi, jnp.minimum(j, i), 0)

    return pl.pallas_call(
        functools.partial(_fox_kernel, t=t),
        grid=(b, n, n),
        in_specs=[pl.BlockSpec((1, t, c), lambda bi, i, j: (bi, i, 0)),
                  pl.BlockSpec((1, t, c), kv_map),
                  pl.BlockSpec((1, t, c), kv_map),
                  pl.BlockSpec((1, t, LANES), lambda bi, i, j: (bi, i, 0)),
                  pl.BlockSpec((1, ATT_HEADS, t), lambda bi, i, j: (bi, 0, jnp.minimum(j, i)))],
        out_specs=pl.BlockSpec((1, t, c), lambda bi, i, j: (bi, i, 0)),
        out_shape=jax.ShapeDtypeStruct((b, s, c), BF16),
        scratch_shapes=_att_scratch(t),
        compiler_params=_params("parallel", "parallel", "arbitrary"),
        name="fox",
    )(q, k, v, fc_col, fc_row)


def _xattn_kernel(xb_ref, xf_ref, wq_ref, kt_ref, v_ref, wo_ref, g_ref, b_ref, of_ref, ob_ref):
    q = jnp.dot(xb_ref[...], wq_ref[...], preferred_element_type=F32) * (HEAD_DIM ** -0.5)
    q = q.astype(BF16)
    outs = []
    for h in range(XA_HEADS):
        lo, hi = h * HEAD_DIM, (h + 1) * HEAD_DIM
        s = jnp.dot(q[:, lo:hi], kt_ref[0, lo:hi, :], preferred_element_type=F32)
        p = jnp.exp(s - jnp.max(s, axis=1, keepdims=True))
        l = jnp.sum(p, axis=1, keepdims=True)
        pv = jnp.dot(p.astype(BF16), v_ref[0, :, lo:hi], preferred_element_type=F32)
        outs.append((pv / l).astype(BF16))
    o = jnp.concatenate(outs, axis=1)
    y = jnp.dot(o, wo_ref[...], preferred_element_type=F32)
    y = _layer_norm(ALPHA * xf_ref[...] + y, g_ref[...], b_ref[...])
    of_ref[...] = y
    ob_ref[...] = y.astype(BF16)


def _xattn(xb, xf, wq, kt, v, wo, g, b, seq, tm=256):
    n, d = xf.shape
    per_batch = seq // tm
    m = v.shape[1]
    return pl.pallas_call(
        _xattn_kernel,
        grid=(n // tm,),
        in_specs=[pl.BlockSpec((tm, d), lambda i: (i, 0)),
                  pl.BlockSpec((tm, d), lambda i: (i, 0)),
                  pl.BlockSpec(wq.shape, lambda i: (0, 0)),
                  pl.BlockSpec((1, XA_WIDTH, m), lambda i: (i // per_batch, 0, 0)),
                  pl.BlockSpec((1, m, XA_WIDTH), lambda i: (i // per_batch, 0, 0)),
                  pl.BlockSpec(wo.shape, lambda i: (0, 0)),
                  pl.BlockSpec((1, d), lambda i: (0, 0)),
                  pl.BlockSpec((1, d), lambda i: (0, 0))],
        out_specs=[pl.BlockSpec((tm, d), lambda i: (i, 0)), pl.BlockSpec((tm, d), lambda i: (i, 0))],
        out_shape=[jax.ShapeDtypeStruct((n, d), F32), jax.ShapeDtypeStruct((n, d), BF16)],
        compiler_params=_params("parallel"),
        name="xattn",
    )(xb, xf, wq, kt, v, wo, g, b)


def _ffn_kernel(xb_ref, wg_ref, wu_ref, wo_ref, xf_ref, g_ref, b_ref, of_ref, ob_ref, acc):
    j = pl.program_id(1)

    @pl.when(j == 0)
    def _():
        acc[...] = jnp.zeros(acc.shape, F32)

    xb = xb_ref[...]
    gate = jnp.dot(xb, wg_ref[...], preferred_element_type=F32)
    up = jnp.dot(xb, wu_ref[...], preferred_element_type=F32)
    hid = gate * (1.0 / (1.0 + jnp.exp(-gate))) * up
    acc[...] += jnp.dot(hid.astype(BF16), wo_ref[...], preferred_element_type=F32)

    @pl.when(j == pl.num_programs(1) - 1)
    def _():
        y = _layer_norm(ALPHA * xf_ref[...] + acc[...], g_ref[...], b_ref[...])
        of_ref[...] = y
        ob_ref[...] = y.astype(BF16)


def _ffn(xb, xf, w_in, w_out, g, b, tm=512, th=512):
    n, d = xf.shape
    hidden = w_out.shape[0]
    nh = hidden // th
    return pl.pallas_call(
        _ffn_kernel,
        grid=(n // tm, nh),
        in_specs=[pl.BlockSpec((tm, d), lambda i, j: (i, 0)),
                  pl.BlockSpec((d, th), lambda i, j: (0, j)),
                  pl.BlockSpec((d, th), lambda i, j: (0, j + nh)),
                  pl.BlockSpec((th, d), lambda i, j: (j, 0)),
                  pl.BlockSpec((tm, d), lambda i, j: (i, 0)),
                  pl.BlockSpec((1, d), lambda i, j: (0, 0)),
                  pl.BlockSpec((1, d), lambda i, j: (0, 0))],
        out_specs=[pl.BlockSpec((tm, d), lambda i, j: (i, 0)), pl.BlockSpec((tm, d), lambda i, j: (i, 0))],
        out_shape=[jax.ShapeDtypeStruct((n, d), F32), jax.ShapeDtypeStruct((n, d), BF16)],
        scratch_shapes=[pltpu.VMEM((tm, d), F32)],
        compiler_params=_params("parallel", "arbitrary"),
        name="ffn",
    )(xb, w_in, w_in, w_out, xf, g, b)


def _rotary_tables(positions, dh):
    rot = dh // ROPE_FRACTION
    half = rot // 2
    inv_freq = jnp.power(ROPE_THETA, -(jnp.arange(half, dtype=F32) * 2.0 / rot))
    ang = positions.astype(F32)[..., None] * inv_freq
    cos, sin = jnp.cos(ang), jnp.sin(ang)
    zh = jnp.zeros_like(sin)
    rest = jnp.zeros(ang.shape[:-1] + (dh - rot,), F32)
    c = jnp.concatenate([cos, cos, rest + 1.0], axis=-1)
    s_lo = jnp.concatenate([zh, sin, rest], axis=-1)
    s_hi = jnp.concatenate([-sin, zh, rest], axis=-1)
    return tuple(t.reshape(-1, dh) for t in (c, s_lo, s_hi))


def _even_mixer(xb, positions, w_in, pool_w, pool_scale, bsz, seq, tq, tk):
    n = xb.shape[0]
    hw = HALF_WIDTH
    head_t = _rotary_tables(positions, HEAD_DIM)
    idx_t = _rotary_tables(positions, IDX_DIM)
    idx_full = tuple(jnp.tile(t, (1, LANES // IDX_DIM)) for t in idx_t)
    wi_scale = jnp.concatenate([jnp.full((IDX_HEADS,), IDX_HEADS ** -0.5, F32),
                                jnp.ones((LANES - IDX_DIM - IDX_HEADS,), F32)])
    pad = jnp.zeros((n, LANES - IDX_DIM), F32)
    tail_t = (jnp.concatenate([idx_t[0], pad + wi_scale], axis=1),
              jnp.concatenate([idx_t[1], pad], axis=1),
              jnp.concatenate([idx_t[2], pad], axis=1))

    wb = w_in.astype(BF16)
    qscale = HEAD_DIM ** -0.5
    u = _proj(xb, wb[:, 0:hw], F32)
    q = _proj(xb, wb[:, hw:2 * hw], BF16, tuple(t * qscale for t in head_t), shift=HEAD_DIM // 8)
    k = _proj(xb, wb[:, 2 * hw:3 * hw], BF16, head_t, shift=HEAD_DIM // 8)
    v = _proj(xb, wb[:, 3 * hw:4 * hw], BF16)
    qi = _proj(xb, wb[:, 4 * hw:5 * hw], BF16, tuple(t * (IDX_DIM ** -0.5) for t in idx_full),
               shift=IDX_DIM // 8)
    w_tail = jnp.pad(wb[:, 5 * hw:], ((0, 0), (0, LANES - IDX_DIM - IDX_HEADS)))
    tail = _proj(xb, w_tail, F32, tail_t, shift=IDX_DIM // 8)

    ki = tail[:, :IDX_DIM].astype(BF16).reshape(bsz, seq // tk, tk, IDX_DIM).transpose(0, 1, 3, 2)
    wi = tail[:, IDX_DIM:IDX_DIM + IDX_HEADS].reshape(bsz, seq, IDX_HEADS)

    a = _pool(u.reshape(bsz, seq, hw), pool_w.astype(BF16), pool_scale.reshape(1, hw))
    mask = _idx_mask(qi.reshape(bsz, seq, hw), wi, ki, tq, tk)
    bb = _dsa(q.reshape(bsz, seq, hw), k.reshape(bsz, seq, hw), v.reshape(bsz, seq, hw), mask, tq, tk)
    return a.reshape(n, hw), bb.reshape(n, hw)


def _odd_mixer(xb, w_in, conv_w, forget_b, bsz, seq):
    n = xb.shape[0]
    hw = HALF_WIDTH
    wb = w_in.astype(BF16)
    ugg = _proj(xb, wb[:, 0:3 * hw], F32)
    q = _proj(xb, wb[:, 3 * hw:4 * hw], BF16, scale=HEAD_DIM ** -0.5)
    k = _proj(xb, wb[:, 4 * hw:5 * hw], BF16)
    v = _proj(xb, wb[:, 5 * hw:6 * hw], BF16)
    w_tail = jnp.pad(wb[:, 6 * hw:], ((0, 0), (0, LANES - ATT_HEADS)))
    f = _proj(xb, w_tail, F32)
    fb = jnp.pad(forget_b, (0, LANES - ATT_HEADS)).reshape(1, LANES)
    fc = _forget_cumsum(f.reshape(bsz, seq, LANES), fb)
    fc_row = fc[:, :, :ATT_HEADS].transpose(0, 2, 1)

    c = _gated_conv(ugg.reshape(bsz, seq, 3 * hw), conv_w)
    d = _fox(q.reshape(bsz, seq, hw), k.reshape(bsz, seq, hw), v.reshape(bsz, seq, hw), fc, fc_row)
    return c.reshape(n, hw), d.reshape(n, hw)


def kernel(x, mem, positions, ev_w_in, ev_pool_w, ev_pool_scale, ev_w_out, od_w_in, od_conv_w, od_forget_b,
           od_w_out, ca_w_q, ca_w_kv, ca_w_o, ffn_w_in, ffn_w_out, ln_g, ln_b):
    bsz, seq, d = x.shape
    n = bsz * seq
    m = mem.shape[1]
    tq, tk = min(256, seq), min(512, seq)
    xf = x.reshape(n, d)
    xb = xf.astype(BF16)
    memb = mem.reshape(bsz * m, d).astype(BF16)
    for i in range(DEPTH):
        j = i // 2
        if i % 2 == 0:
            parts = _even_mixer(xb, positions, ev_w_in[j], ev_pool_w[j], ev_pool_scale[j], bsz, seq, tq, tk)
            w_out = ev_w_out[j]
        else:
            parts = _odd_mixer(xb, od_w_in[j], od_conv_w[j], od_forget_b[j], bsz, seq)
            w_out = od_w_out[j]
        g, b = ln_g[i].reshape(3, 1, d), ln_b[i].reshape(3, 1, d)
        xf, xb = _out_ln(list(parts), w_out.astype(BF16), xf, g[0], b[0])

        kv = _proj(memb, ca_w_kv[i].astype(BF16), BF16).reshape(bsz, m, 2 * XA_WIDTH)
        kt = kv[:, :, :XA_WIDTH].transpose(0, 2, 1)
        xf, xb = _xattn(xb, xf, ca_w_q[i].astype(BF16), kt, kv[:, :, XA_WIDTH:], ca_w_o[i].astype(BF16),
                        g[1], b[1], seq)

        xf, xb = _ffn(xb, xf, ffn_w_in[i].astype(BF16), ffn_w_out[i].astype(BF16), g[2], b[2])
    return xf.reshape(bsz, seq, d)
```

```python
import functools

import jax
import jax.numpy as jnp
from jax import lax
from jax.experimental import pallas as pl
from jax.experimental.pallas import tpu as pltpu

F32 = jnp.float32
BF16 = jnp.bfloat16

D_MODEL = 2048
DEPTH = 4
CHUNK = 64
HEAD_DIM = 128
HALF_WIDTH = D_MODEL // 2
POOL_WINDOWS = (2, 4, 8, 16)
POOL_GROUP_DIM = HALF_WIDTH // len(POOL_WINDOWS)
ATT_HEADS = HALF_WIDTH // HEAD_DIM
IDX_HEADS = 16
IDX_DIM = 64
DSA_TOPK_MAX = 256
CONV_WIDTH = 3
XA_HEADS = 4
XA_WIDTH = XA_HEADS * HEAD_DIM
FFN_HIDDEN = -(-(8 * D_MODEL) // (3 * 256)) * 256
ROPE_THETA = 500000.0
ROPE_FRACTION = 4
LN_EPS = 1e-5
ALPHA = (2 * DEPTH) ** 0.25

LANES = 128
SUBLANES = 8
BF16_ROWS = 16
NEG = -1e30
INT_MIN = -(2 ** 31)
VMEM_LIMIT = 56 * 1024 * 1024


def _params(*semantics):
    return pltpu.CompilerParams(dimension_semantics=semantics, vmem_limit_bytes=VMEM_LIMIT)


def _layer_norm(y, g, b):
    mu = jnp.mean(y, axis=-1, keepdims=True)
    d = y - mu
    var = jnp.mean(d * d, axis=-1, keepdims=True)
    return d * lax.rsqrt(var + LN_EPS) * g + b


def _proj_kernel(*refs, shift, scale, axis):
    if shift:
        a_ref, b_ref, c_ref, s1_ref, s2_ref, o_ref = refs
    else:
        a_ref, b_ref, o_ref = refs
    h = jnp.dot(a_ref[...], b_ref[...], preferred_element_type=F32)
    if shift:
        c, s1, s2 = c_ref[...], s1_ref[...], s2_ref[...]
        for g in range(h.shape[axis] // LANES):
            sl = (slice(None),) * axis + (slice(g * LANES, (g + 1) * LANES),)
            hg = h[sl]
            out = hg * c + pltpu.roll(hg, shift, axis) * s1 + pltpu.roll(hg, LANES - shift, axis) * s2
            o_ref[sl] = out.astype(o_ref.dtype)
    elif scale != 1.0:
        o_ref[...] = (h * scale).astype(o_ref.dtype)
    else:
        o_ref[...] = h.astype(o_ref.dtype)


def _proj(x, w, out_dtype, tables=None, shift=0, scale=1.0, tm=512, tn=512):
    n, k = x.shape
    c = w.shape[1]
    tm, tn = min(tm, n), min(tn, c)
    in_specs = [pl.BlockSpec((tm, k), lambda i, j: (i, 0)),
                pl.BlockSpec((k, tn), lambda i, j: (0, j))]
    args = [x, w]
    if shift:
        in_specs += [pl.BlockSpec((tm, LANES), lambda i, j: (i, 0))] * 3
        args += list(tables)
    return pl.pallas_call(
        functools.partial(_proj_kernel, shift=shift, scale=scale, axis=1),
        grid=(n // tm, c // tn),
        in_specs=in_specs,
        out_specs=pl.BlockSpec((tm, tn), lambda i, j: (i, j)),
        out_shape=jax.ShapeDtypeStruct((n, c), out_dtype),
        compiler_params=_params("parallel", "arbitrary"),
        name="proj",
    )(*args)


def _proj_t(wt, xt, out_dtype, tables=None, shift=0, scale=1.0, tm=512, tn=512):
    c, k = wt.shape
    n = xt.shape[1]
    tm, tn = min(tm, n), min(tn, c)
    in_specs = [pl.BlockSpec((tn, k), lambda i, j: (j, 0)),
                pl.BlockSpec((k, tm), lambda i, j: (0, i))]
    args = [wt, xt]
    if shift:
        in_specs += [pl.BlockSpec((LANES, tm), lambda i, j: (0, i))] * 3
        args += list(tables)
    return pl.pallas_call(
        functools.partial(_proj_kernel, shift=shift, scale=scale, axis=0),
        grid=(n // tm, c // tn),
        in_specs=in_specs,
        out_specs=pl.BlockSpec((tn, tm), lambda i, j: (j, i)),
        out_shape=jax.ShapeDtypeStruct((c, n), out_dtype),
        compiler_params=_params("parallel", "arbitrary"),
        name="proj_t",
    )(*args)


def _out_ln_kernel(*refs, nparts):
    parts = refs[:nparts]
    w_ref, x_ref, g_ref, b_ref, of_ref, ob_ref = refs[nparts:]
    acc = None
    off = 0
    for p in parts:
        kp = p.shape[1]
        t = jnp.dot(p[...], w_ref[off:off + kp, :], preferred_element_type=F32)
        acc = t if acc is None else acc + t
        off += kp
    y = _layer_norm(ALPHA * x_ref[...] + acc, g_ref[...], b_ref[...])
    of_ref[...] = y
    ob_ref[...] = y.astype(BF16)


def _out_ln(parts, w, x, g, b, tm=256):
    n, d = x.shape
    in_specs = [pl.BlockSpec((tm, p.shape[1]), lambda i: (i, 0)) for p in parts]
    in_specs += [pl.BlockSpec(w.shape, lambda i: (0, 0)),
                 pl.BlockSpec((tm, d), lambda i: (i, 0)),
                 pl.BlockSpec((1, d), lambda i: (0, 0)),
                 pl.BlockSpec((1, d), lambda i: (0, 0))]
    return pl.pallas_call(
        functools.partial(_out_ln_kernel, nparts=len(parts)),
        grid=(n // tm,),
        in_specs=in_specs,
        out_specs=[pl.BlockSpec((tm, d), lambda i: (i, 0)), pl.BlockSpec((tm, d), lambda i: (i, 0))],
        out_shape=[jax.ShapeDtypeStruct((n, d), F32), jax.ShapeDtypeStruct((n, d), BF16)],
        compiler_params=_params("parallel"),
        name="out_ln",
    )(*parts, w, x, g, b)


POOL_HALO = 16


def _pool_kernel(u_ref, halo_ref, w_ref, sc_ref, o_ref, ext, *, ts):
    i = pl.program_id(1)
    ext[0:POOL_HALO, :] = jnp.where(i > 0, halo_ref[0], 0.0)
    ext[POOL_HALO:, :] = u_ref[0]
    cnt = i * ts + lax.broadcasted_iota(jnp.int32, (ts, 1), 0) + 1
    for g, win in enumerate(POOL_WINDOWS):
        lo, hi = g * POOL_GROUP_DIM, (g + 1) * POOL_GROUP_DIM
        cur = ext[POOL_HALO:POOL_HALO + ts, lo:hi]
        s = cur
        for j in range(1, win):
            s = s + ext[POOL_HALO - j:POOL_HALO - j + ts, lo:hi]
        d = s / jnp.minimum(cnt, win).astype(F32) - cur
        y = jnp.dot(d.astype(BF16), w_ref[g], preferred_element_type=F32)
        o_ref[0, :, lo:hi] = (y * sc_ref[:, lo:hi]).astype(o_ref.dtype)


def _pool(u, w, scale, ts=512):
    b, s, c = u.shape
    ts = min(ts, s)
    hb = ts // POOL_HALO
    return pl.pallas_call(
        functools.partial(_pool_kernel, ts=ts),
        grid=(b, s // ts),
        in_specs=[pl.BlockSpec((1, ts, c), lambda bi, i: (bi, i, 0)),
                  pl.BlockSpec((1, POOL_HALO, c), lambda bi, i: (bi, jnp.maximum(i * hb - 1, 0), 0)),
                  pl.BlockSpec(w.shape, lambda bi, i: (0, 0, 0)),
                  pl.BlockSpec((1, c), lambda bi, i: (0, 0))],
        out_specs=pl.BlockSpec((1, ts, c), lambda bi, i: (bi, i, 0)),
        out_shape=jax.ShapeDtypeStruct((b, s, c), BF16),
        scratch_shapes=[pltpu.VMEM((ts + POOL_HALO, c), F32)],
        compiler_params=_params("parallel", "arbitrary"),
        name="pool",
    )(u, u, w, scale)


def _sortable(v):
    bits = lax.bitcast_convert_type(v, jnp.int32)
    return bits ^ ((bits >> 31) & 0x7FFFFFFF)


def _colsum8(v):
    tk, tq = v.shape
    return v.reshape(tk // SUBLANES, SUBLANES, tq).sum(axis=0)


def _idx_kernel(qi_ref, wi_ref, ki_ref, o_ref, keys, *, tq, tk, nk, topk):
    i = pl.program_id(1)
    nact = (i * tq + tq - 1) // tk + 1
    q_chunk = (i * tq + lax.broadcasted_iota(jnp.int32, (1, tq), 1)) // CHUNK

    def score_tile(c, carry):
        kt = ki_ref[0, pl.ds(pl.multiple_of(c * tk, tk), tk), :]
        acc = jnp.zeros((tk, tq), F32)
        for h in range(IDX_HEADS):
            il = jnp.dot(kt, qi_ref[h * IDX_DIM:(h + 1) * IDX_DIM, :], preferred_element_type=F32)
            acc = acc + jnp.maximum(il, 0.0) * wi_ref[h:h + 1, :]
        k_chunk = (c * tk + lax.broadcasted_iota(jnp.int32, (tk, 1), 0)) // CHUNK
        keys[c] = jnp.where(k_chunk <= q_chunk, _sortable(acc), INT_MIN)
        return carry

    lax.fori_loop(0, nact, score_tile, 0)

    def count(pred):
        def body(c, acc):
            return acc + _colsum8(jnp.where(pred(keys[c], c), 1, 0))
        acc = lax.fori_loop(0, nact, body, jnp.zeros((SUBLANES, tq), jnp.int32))
        return jnp.sum(acc, axis=0, keepdims=True)

    thr = jnp.where(count(lambda kk, c: kk >= 0) >= topk, 0, INT_MIN)

    def bit_step(b, thr):
        cand = thr | (1 << (30 - b))
        return jnp.where(count(lambda kk, c: kk >= cand) >= topk, cand, thr)

    thr = lax.fori_loop(0, 31, bit_step, thr)
    thr = jnp.maximum(thr, INT_MIN + 1)
    n_ge = count(lambda kk, c: kk >= thr)

    def write_tiles(select):
        def body(c, carry):
            o_ref[0, 0, c] = jnp.where(select(keys[c], c), 0.0, NEG).astype(o_ref.dtype)
            return carry
        lax.fori_loop(0, nact, body, 0)

    has_ties = jnp.max(n_ge) > topk

    @pl.when(jnp.logical_not(has_ties))
    def _():
        write_tiles(lambda kk, c: kk >= thr)

    @pl.when(has_ties)
    def _():
        need = topk - count(lambda kk, c: kk > thr)

        def key_index(c):
            return c * tk + lax.broadcasted_iota(jnp.int32, (tk, 1), 0)

        def idx_step(b, cut):
            cand = cut | (1 << (30 - b))
            below = count(lambda kk, c: (kk == thr) & (key_index(c) < cand))
            return jnp.where(below < need, cand, cut)

        cut = lax.fori_loop(0, 31, idx_step, jnp.zeros((1, tq), jnp.int32))
        write_tiles(lambda kk, c: (kk > thr) | ((kk == thr) & (key_index(c) <= cut)))

    def fill_tile(c, carry):
        o_ref[0, 0, c] = jnp.full((tk, tq), NEG, o_ref.dtype)
        return carry

    lax.fori_loop(nact, nk, fill_tile, 0)


def _idx_mask(qi_t, wi_t, ki, tq, tk):
    b, s, _ = ki.shape
    nq, nk = s // tq, s // tk
    topk = min(DSA_TOPK_MAX, s // 4)
    return pl.pallas_call(
        functools.partial(_idx_kernel, tq=tq, tk=tk, nk=nk, topk=topk),
        grid=(b, nq),
        in_specs=[pl.BlockSpec((IDX_HEADS * IDX_DIM, tq), lambda bi, i: (0, bi * nq + i)),
                  pl.BlockSpec((IDX_HEADS, tq), lambda bi, i: (0, bi * nq + i)),
                  pl.BlockSpec((1, s, IDX_DIM), lambda bi, i: (bi, 0, 0))],
        out_specs=pl.BlockSpec((1, 1, nk, tk, tq), lambda bi, i: (bi, i, 0, 0, 0)),
        out_shape=jax.ShapeDtypeStruct((b, nq, nk, tk, tq), BF16),
        scratch_shapes=[pltpu.VMEM((nk, tk, tq), jnp.int32)],
        compiler_params=_params("parallel", "arbitrary"),
        name="idx_mask",
    )(qi_t, wi_t, ki)


def _softmax_step(s, vt, h, m_sc, l_sc, acc_sc):
    m_prev = m_sc[h]
    m_new = jnp.maximum(m_prev, jnp.max(s, axis=0, keepdims=True))
    alpha = jnp.exp(m_prev - m_new)
    p = jnp.exp(s - m_new)
    l_sc[h] = alpha * l_sc[h] + jnp.sum(p, axis=0, keepdims=True)
    acc_sc[h] = alpha * acc_sc[h] + jnp.dot(vt, p.astype(BF16), preferred_element_type=F32)
    m_sc[h] = m_new


def _softmax_init(m_sc, l_sc, acc_sc):
    m_sc[...] = jnp.full(m_sc.shape, NEG, F32)
    l_sc[...] = jnp.zeros(l_sc.shape, F32)
    acc_sc[...] = jnp.zeros(acc_sc.shape, F32)


def _softmax_finish(o_ref, l_sc, acc_sc):
    for h in range(ATT_HEADS):
        o = (acc_sc[h] / l_sc[h]).T
        o_ref[0, :, h * HEAD_DIM:(h + 1) * HEAD_DIM] = o.astype(o_ref.dtype)


def _att_scratch(tq):
    return [pltpu.VMEM((ATT_HEADS, 1, tq), F32), pltpu.VMEM((ATT_HEADS, 1, tq), F32),
            pltpu.VMEM((ATT_HEADS, HEAD_DIM, tq), F32)]


def _head(h):
    return slice(h * HEAD_DIM, (h + 1) * HEAD_DIM)


def _dsa_kernel(k_ref, qt_ref, vt_ref, mask_ref, o_ref, m_sc, l_sc, acc_sc, *, tq, tk):
    i, j = pl.program_id(1), pl.program_id(2)
    last = (i * tq + tq - 1) // tk

    @pl.when(j == 0)
    def _():
        _softmax_init(m_sc, l_sc, acc_sc)

    @pl.when(j <= last)
    def _():
        mask = mask_ref[0, 0, 0].astype(F32)
        for h in range(ATT_HEADS):
            s = jnp.dot(k_ref[0, :, _head(h)], qt_ref[_head(h), :], preferred_element_type=F32) + mask
            _softmax_step(s, vt_ref[_head(h), :], h, m_sc, l_sc, acc_sc)

    @pl.when(j == last)
    def _():
        _softmax_finish(o_ref, l_sc, acc_sc)


def _dsa(k, q_t, v_t, mask, tq, tk):
    b, s, c = k.shape
    nq, nk = s // tq, s // tk

    def last(i):
        return (i * tq + tq - 1) // tk

    return pl.pallas_call(
        functools.partial(_dsa_kernel, tq=tq, tk=tk),
        grid=(b, nq, nk),
        in_specs=[pl.BlockSpec((1, tk, c), lambda bi, i, j: (bi, jnp.minimum(j, last(i)), 0)),
                  pl.BlockSpec((c, tq), lambda bi, i, j: (0, bi * nq + i)),
                  pl.BlockSpec((c, tk), lambda bi, i, j: (0, bi * nk + jnp.minimum(j, last(i)))),
                  pl.BlockSpec((1, 1, 1, tk, tq), lambda bi, i, j: (bi, i, jnp.minimum(j, last(i)), 0, 0))],
        out_specs=pl.BlockSpec((1, tq, c), lambda bi, i, j: (bi, i, 0)),
        out_shape=jax.ShapeDtypeStruct((b, s, c), BF16),
        scratch_shapes=_att_scratch(tq),
        compiler_params=_params("parallel", "parallel", "arbitrary"),
        name="dsa",
    )(k, q_t, v_t, mask)


CONV_HALO = 8


def _conv_kernel(u_ref, gb_ref, gc_ref, hu_ref, hgc_ref, w_ref, o_ref, ext, *, ts):
    i = pl.program_id(1)
    ext[0:CONV_HALO, :] = jnp.where(i > 0, hgc_ref[0] * hu_ref[0], 0.0)
    ext[CONV_HALO:, :] = gc_ref[0] * u_ref[0]
    conv = None
    for t in range(CONV_WIDTH):
        off = CONV_HALO - (CONV_WIDTH - 1) + t
        term = ext[off:off + ts, :] * w_ref[t:t + 1, :]
        conv = term if conv is None else conv + term
    o_ref[0] = (gb_ref[0] * conv).astype(o_ref.dtype)


def _gated_conv(h, w, ts=512):
    b, s, c3 = h.shape
    c = c3 // 3
    ts = min(ts, s)
    hb = ts // CONV_HALO

    def halo_map(col):
        return lambda bi, i: (bi, jnp.maximum(i * hb - 1, 0), col)

    return pl.pallas_call(
        functools.partial(_conv_kernel, ts=ts),
        grid=(b, s // ts),
        in_specs=[pl.BlockSpec((1, ts, c), lambda bi, i: (bi, i, 0)),
                  pl.BlockSpec((1, ts, c), lambda bi, i: (bi, i, 1)),
                  pl.BlockSpec((1, ts, c), lambda bi, i: (bi, i, 2)),
                  pl.BlockSpec((1, CONV_HALO, c), halo_map(0)),
                  pl.BlockSpec((1, CONV_HALO, c), halo_map(2)),
                  pl.BlockSpec(w.shape, lambda bi, i: (0, 0))],
        out_specs=pl.BlockSpec((1, ts, c), lambda bi, i: (bi, i, 0)),
        out_shape=jax.ShapeDtypeStruct((b, s, c), BF16),
        scratch_shapes=[pltpu.VMEM((ts + CONV_HALO, c), F32)],
        compiler_params=_params("parallel", "arbitrary"),
        name="gated_conv",
    )(h, h, h, h, h, w)


GATE_CHUNK = 256
GATE_AUG = 16


def _fgate_kernel(f_ref, fb_ref, hi_ref, mid_ref, lo_ref):
    s = f_ref.shape[1]
    r = lax.broadcasted_iota(jnp.int32, (GATE_CHUNK, GATE_CHUNK), 0)
    c = lax.broadcasted_iota(jnp.int32, (GATE_CHUNK, GATE_CHUNK), 1)
    tri = (c <= r).astype(F32)

    def body(t, carry):
        rows = pl.ds(t * GATE_CHUNK, GATE_CHUNK)
        z = f_ref[0, rows, :] + fb_ref[...]
        log_f = -(jnp.maximum(-z, 0.0) + jnp.log1p(jnp.exp(-jnp.abs(z))))
        cs = jnp.dot(tri, log_f, preferred_element_type=F32, precision=lax.Precision.HIGHEST) + carry
        hi = cs.astype(BF16)
        r1 = cs - hi.astype(F32)
        mid = r1.astype(BF16)
        hi_ref[0, rows, :] = hi
        mid_ref[0, rows, :] = mid
        lo_ref[0, rows, :] = (r1 - mid.astype(F32)).astype(BF16)
        return cs[GATE_CHUNK - 1:GATE_CHUNK, :]

    lax.fori_loop(0, s // GATE_CHUNK, body, jnp.zeros((1, LANES), F32))


def _forget_cumsum(f, fb):
    b, s, c = f.shape
    spec = pl.BlockSpec((1, s, c), lambda bi: (bi, 0, 0))
    return pl.pallas_call(
        _fgate_kernel,
        grid=(b,),
        in_specs=[spec, pl.BlockSpec((1, c), lambda bi: (0, 0))],
        out_specs=[spec, spec, spec],
        out_shape=[jax.ShapeDtypeStruct((b, s, c), BF16)] * 3,
        compiler_params=_params("parallel"),
        name="forget_cumsum",
    )(f, fb)


def _fox_kernel(k_ref, ka_ref, qt_ref, qa_ref, vt_ref, o_ref, m_sc, l_sc, acc_sc, qfull, *, t):
    i, j = pl.program_id(1), pl.program_id(2)

    @pl.when(j == 0)
    def _():
        _softmax_init(m_sc, l_sc, acc_sc)
        for h in range(ATT_HEADS):
            qfull[h, 0:HEAD_DIM, :] = qt_ref[_head(h), :]
            qfull[h, HEAD_DIM:, :] = jnp.zeros((LANES, t), BF16)
            lo = HEAD_DIM + h * GATE_AUG
            qfull[h, lo:lo + GATE_AUG, :] = qa_ref[0, h]

    def step(diagonal):
        if diagonal:
            causal = (lax.broadcasted_iota(jnp.int32, (t, t), 0) <= lax.broadcasted_iota(jnp.int32, (t, t), 1))
        ka = ka_ref[0]
        for h in range(ATT_HEADS):
            kfull = jnp.concatenate([k_ref[0, :, _head(h)], ka], axis=1)
            s = jnp.dot(kfull, qfull[h], preferred_element_type=F32)
            if diagonal:
                s = jnp.where(causal, s, NEG)
            _softmax_step(s, vt_ref[_head(h), :], h, m_sc, l_sc, acc_sc)

    @pl.when(j < i)
    def _():
        step(False)

    @pl.when(j == i)
    def _():
        step(True)
        _softmax_finish(o_ref, l_sc, acc_sc)


def _fox(k, k_aug, q_t, q_aug, v_t, t=512):
    b, s, c = k.shape
    t = min(t, s)
    n = s // t
    return pl.pallas_call(
        functools.partial(_fox_kernel, t=t),
        grid=(b, n, n),
        in_specs=[pl.BlockSpec((1, t, c), lambda bi, i, j: (bi, jnp.minimum(j, i), 0)),
                  pl.BlockSpec((1, t, LANES), lambda bi, i, j: (bi, jnp.minimum(j, i), 0)),
                  pl.BlockSpec((c, t), lambda bi, i, j: (0, bi * n + i)),
                  pl.BlockSpec((1, ATT_HEADS, GATE_AUG, t), lambda bi, i, j: (bi, 0, 0, i)),
                  pl.BlockSpec((c, t), lambda bi, i, j: (0, bi * n + jnp.minimum(j, i)))],
        out_specs=pl.BlockSpec((1, t, c), lambda bi, i, j: (bi, i, 0)),
        out_shape=jax.ShapeDtypeStruct((b, s, c), BF16),
        scratch_shapes=_att_scratch(t) + [pltpu.VMEM((ATT_HEADS, 2 * HEAD_DIM, t), BF16)],
        compiler_params=_params("parallel", "parallel", "arbitrary"),
        name="fox",
    )(k, k_aug, q_t, q_aug, v_t)


def _xattn_kernel(xb_ref, xf_ref, wq_ref, kt_ref, v_ref, wo_ref, g_ref, b_ref, of_ref, ob_ref):
    q = jnp.dot(xb_ref[...], wq_ref[...], preferred_element_type=F32) * (HEAD_DIM ** -0.5)
    q = q.astype(BF16)
    outs = []
    for h in range(XA_HEADS):
        s = jnp.dot(q[:, _head(h)], kt_ref[0, _head(h), :], preferred_element_type=F32)
        p = jnp.exp(s - jnp.max(s, axis=1, keepdims=True))
        l = jnp.sum(p, axis=1, keepdims=True)
        pv = jnp.dot(p.astype(BF16), v_ref[0, :, _head(h)], preferred_element_type=F32)
        outs.append((pv / l).astype(BF16))
    o = jnp.concatenate(outs, axis=1)
    y = jnp.dot(o, wo_ref[...], preferred_element_type=F32)
    y = _layer_norm(ALPHA * xf_ref[...] + y, g_ref[...], b_ref[...])
    of_ref[...] = y
    ob_ref[...] = y.astype(BF16)


def _xattn(xb, xf, wq, kt, v, wo, g, b, seq, tm=256):
    n, d = xf.shape
    per_batch = seq // tm
    m = v.shape[1]
    return pl.pallas_call(
        _xattn_kernel,
        grid=(n // tm,),
        in_specs=[pl.BlockSpec((tm, d), lambda i: (i, 0)),
                  pl.BlockSpec((tm, d), lambda i: (i, 0)),
                  pl.BlockSpec(wq.shape, lambda i: (0, 0)),
                  pl.BlockSpec((1, XA_WIDTH, m), lambda i: (i // per_batch, 0, 0)),
                  pl.BlockSpec((1, m, XA_WIDTH), lambda i: (i // per_batch, 0, 0)),
                  pl.BlockSpec(wo.shape, lambda i: (0, 0)),
                  pl.BlockSpec((1, d), lambda i: (0, 0)),
                  pl.BlockSpec((1, d), lambda i: (0, 0))],
        out_specs=[pl.BlockSpec((tm, d), lambda i: (i, 0)), pl.BlockSpec((tm, d), lambda i: (i, 0))],
        out_shape=[jax.ShapeDtypeStruct((n, d), F32), jax.ShapeDtypeStruct((n, d), BF16)],
        compiler_params=_params("parallel"),
        name="xattn",
    )(xb, xf, wq, kt, v, wo, g, b)


def _ffn_kernel(xb_ref, wg_ref, wu_ref, wo_ref, xf_ref, g_ref, b_ref, of_ref, ob_ref, acc):
    j = pl.program_id(1)

    @pl.when(j == 0)
    def _():
        acc[...] = jnp.zeros(acc.shape, F32)

    xb = xb_ref[...]
    gate = jnp.dot(xb, wg_ref[...], preferred_element_type=F32)
    up = jnp.dot(xb, wu_ref[...], preferred_element_type=F32)
    hid = gate * (1.0 / (1.0 + jnp.exp(-gate))) * up
    acc[...] += jnp.dot(hid.astype(BF16), wo_ref[...], preferred_element_type=F32)

    @pl.when(j == pl.num_programs(1) - 1)
    def _():
        y = _layer_norm(ALPHA * xf_ref[...] + acc[...], g_ref[...], b_ref[...])
        of_ref[...] = y
        ob_ref[...] = y.astype(BF16)


def _ffn(xb, xf, w_in, w_out, g, b, tm=512, th=512):
    n, d = xf.shape
    hidden = w_out.shape[0]
    nh = hidden // th
    return pl.pallas_call(
        _ffn_kernel,
        grid=(n // tm, nh),
        in_specs=[pl.BlockSpec((tm, d), lambda i, j: (i, 0)),
                  pl.BlockSpec((d, th), lambda i, j: (0, j)),
                  pl.BlockSpec((d, th), lambda i, j: (0, j + nh)),
                  pl.BlockSpec((th, d), lambda i, j: (j, 0)),
                  pl.BlockSpec((tm, d), lambda i, j: (i, 0)),
                  pl.BlockSpec((1, d), lambda i, j: (0, 0)),
                  pl.BlockSpec((1, d), lambda i, j: (0, 0))],
        out_specs=[pl.BlockSpec((tm, d), lambda i, j: (i, 0)), pl.BlockSpec((tm, d), lambda i, j: (i, 0))],
        out_shape=[jax.ShapeDtypeStruct((n, d), F32), jax.ShapeDtypeStruct((n, d), BF16)],
        scratch_shapes=[pltpu.VMEM((tm, d), F32)],
        compiler_params=_params("parallel", "arbitrary"),
        name="ffn",
    )(xb, w_in, w_in, w_out, xf, g, b)


def _rotary_tables(positions, dh):
    rot = dh // ROPE_FRACTION
    half = rot // 2
    inv_freq = jnp.power(ROPE_THETA, -(jnp.arange(half, dtype=F32) * 2.0 / rot))
    ang = positions.astype(F32)[..., None] * inv_freq
    cos, sin = jnp.cos(ang), jnp.sin(ang)
    zh = jnp.zeros_like(sin)
    rest = jnp.zeros(ang.shape[:-1] + (dh - rot,), F32)
    c = jnp.concatenate([cos, cos, rest + 1.0], axis=-1)
    s_lo = jnp.concatenate([zh, sin, rest], axis=-1)
    s_hi = jnp.concatenate([-sin, zh, rest], axis=-1)
    return tuple(t.reshape(-1, dh) for t in (c, s_lo, s_hi))


def _even_tables(positions):
    n = positions.size
    head = _rotary_tables(positions, HEAD_DIM)
    idx = _rotary_tables(positions, IDX_DIM)
    idx_full = tuple(jnp.tile(t, (1, LANES // IDX_DIM)) for t in idx)
    wi_scale = jnp.concatenate([jnp.full((IDX_HEADS,), IDX_HEADS ** -0.5, F32),
                                jnp.ones((LANES - IDX_DIM - IDX_HEADS,), F32)])
    pad = jnp.zeros((n, LANES - IDX_DIM), F32)
    tail = (jnp.concatenate([idx[0], pad + wi_scale], axis=1),
            jnp.concatenate([idx[1], pad], axis=1),
            jnp.concatenate([idx[2], pad], axis=1))
    return {"k": head,
            "q_t": tuple((t * HEAD_DIM ** -0.5).T for t in head),
            "qi_t": tuple((t * IDX_DIM ** -0.5).T for t in idx_full),
            "tail": tail}


def _even_mixer(xb, xt, tables, w_in, pool_w, pool_scale, bsz, seq, tq, tk):
    n = xb.shape[0]
    hw = HALF_WIDTH
    wb = w_in.astype(BF16)
    wt = wb.T
    u = _proj(xb, wb[:, 0:hw], F32)
    q_t = _proj_t(wt[hw:2 * hw], xt, BF16, tables["q_t"], shift=HEAD_DIM // 8)
    k = _proj(xb, wb[:, 2 * hw:3 * hw], BF16, tables["k"], shift=HEAD_DIM // 8)
    v_t = _proj_t(wt[3 * hw:4 * hw], xt, BF16)
    qi_t = _proj_t(wt[4 * hw:5 * hw], xt, BF16, tables["qi_t"], shift=IDX_DIM // 8)
    w_tail = jnp.pad(wb[:, 5 * hw:], ((0, 0), (0, LANES - IDX_DIM - IDX_HEADS)))
    tail = _proj(xb, w_tail, F32, tables["tail"], shift=IDX_DIM // 8)

    ki = tail[:, :IDX_DIM].astype(BF16).reshape(bsz, seq, IDX_DIM)
    wi_t = tail[:, IDX_DIM:IDX_DIM + IDX_HEADS].T

    a = _pool(u.reshape(bsz, seq, hw), pool_w.astype(BF16), pool_scale.reshape(1, hw))
    mask = _idx_mask(qi_t, wi_t, ki, tq, tk)
    bb = _dsa(k.reshape(bsz, seq, hw), q_t, v_t, mask, tq, tk)
    return a.reshape(n, hw), bb.reshape(n, hw)


def _gate_operands(hi, mid, lo, bsz, seq):
    one = jnp.ones_like(hi)
    zero = jnp.zeros(hi.shape + (GATE_AUG - 6,), BF16)
    k_aug = jnp.concatenate([jnp.stack([-hi, -mid, -lo, one, one, one], axis=-1), zero], axis=-1)
    q_aug = jnp.concatenate([jnp.stack([one, one, one, hi, mid, lo], axis=-1), zero], axis=-1)
    return k_aug.reshape(bsz, seq, ATT_HEADS * GATE_AUG), q_aug.transpose(0, 2, 3, 1)


def _odd_mixer(xb, xt, w_in, conv_w, forget_b, bsz, seq):
    n = xb.shape[0]
    hw = HALF_WIDTH
    wb = w_in.astype(BF16)
    wt = wb.T
    ugg = _proj(xb, wb[:, 0:3 * hw], F32)
    q_t = _proj_t(wt[3 * hw:4 * hw], xt, BF16, scale=HEAD_DIM ** -0.5)
    k = _proj(xb, wb[:, 4 * hw:5 * hw], BF16)
    v_t = _proj_t(wt[5 * hw:6 * hw], xt, BF16)
    w_tail = jnp.pad(wb[:, 6 * hw:], ((0, 0), (0, LANES - ATT_HEADS)))
    f = _proj(xb, w_tail, F32)
    fb = jnp.pad(forget_b, (0, LANES - ATT_HEADS)).reshape(1, LANES)
    terms = _forget_cumsum(f.reshape(bsz, seq, LANES), fb)
    k_aug, q_aug = _gate_operands(*(t[:, :, :ATT_HEADS] for t in terms), bsz, seq)

    c = _gated_conv(ugg.reshape(bsz, seq, 3 * hw), conv_w)
    d = _fox(k.reshape(bsz, seq, hw), k_aug, q_t, q_aug, v_t)
    return c.reshape(n, hw), d.reshape(n, hw)


def kernel(x, mem, positions, ev_w_in, ev_pool_w, ev_pool_scale, ev_w_out, od_w_in, od_conv_w, od_forget_b,
           od_w_out, ca_w_q, ca_w_kv, ca_w_o, ffn_w_in, ffn_w_out, ln_g, ln_b):
    bsz, seq, d = x.shape
    n = bsz * seq
    m = mem.shape[1]
    tq = tk = min(512, seq)
    xf = x.reshape(n, d)
    xb = xf.astype(BF16)
    memb = mem.reshape(bsz * m, d).astype(BF16)
    tables = _even_tables(positions)
    for i in range(DEPTH):
        j = i // 2
        xt = xb.T
        if i % 2 == 0:
            parts = _even_mixer(xb, xt, tables, ev_w_in[j], ev_pool_w[j], ev_pool_scale[j], bsz, seq, tq, tk)
            w_out = ev_w_out[j]
        else:
            parts = _odd_mixer(xb, xt, od_w_in[j], od_conv_w[j], od_forget_b[j], bsz, seq)
            w_out = od_w_out[j]
        g, b = ln_g[i].reshape(3, 1, d), ln_b[i].reshape(3, 1, d)
        xf, xb = _out_ln(list(parts), w_out.astype(BF16), xf, g[0], b[0])

        kv = _proj(memb, ca_w_kv[i].astype(BF16), BF16).reshape(bsz, m, 2 * XA_WIDTH)
        kt = kv[:, :, :XA_WIDTH].transpose(0, 2, 1)
        xf, xb = _xattn(xb, xf, ca_w_q[i].astype(BF16), kt, kv[:, :, XA_WIDTH:], ca_w_o[i].astype(BF16),
                        g[1], b[1], seq)

        xf, xb = _ffn(xb, xf, ffn_w_in[i].astype(BF16), ffn_w_out[i].astype(BF16), g[2], b[2])
    return xf.reshape(bsz, seq, d)
```

```python
import functools

import jax
import jax.numpy as jnp
from jax import lax
from jax.experimental import pallas as pl
from jax.experimental.pallas import tpu as pltpu

F32 = jnp.float32
BF16 = jnp.bfloat16

D_MODEL = 2048
DEPTH = 4
CHUNK = 64
HEAD_DIM = 128
HALF_WIDTH = D_MODEL // 2
POOL_WINDOWS = (2, 4, 8, 16)
POOL_GROUP_DIM = HALF_WIDTH // len(POOL_WINDOWS)
ATT_HEADS = HALF_WIDTH // HEAD_DIM
IDX_HEADS = 16
IDX_DIM = 64
DSA_TOPK_MAX = 256
CONV_WIDTH = 3
XA_HEADS = 4
XA_WIDTH = XA_HEADS * HEAD_DIM
FFN_HIDDEN = -(-(8 * D_MODEL) // (3 * 256)) * 256
ROPE_THETA = 500000.0
ROPE_FRACTION = 4
LN_EPS = 1e-5
ALPHA = (2 * DEPTH) ** 0.25
LOG2E = 1.4426950408889634
Q_SCALE = HEAD_DIM ** -0.5 * LOG2E

LANES = 128
SUBLANES = 8
BF16_ROWS = 16
NEG = -1e30
INT_MIN = -(2 ** 31)
VMEM_LIMIT = 60000 * 1024


def _params(*semantics):
    return pltpu.CompilerParams(dimension_semantics=semantics, vmem_limit_bytes=VMEM_LIMIT)


def _layer_norm(y, g, b):
    mu = jnp.mean(y, axis=-1, keepdims=True)
    d = y - mu
    var = jnp.mean(d * d, axis=-1, keepdims=True)
    return d * lax.rsqrt(var + LN_EPS) * g + b


def _proj_kernel(*refs, shift, scale, axis):
    if shift:
        a_ref, b_ref, c_ref, s1_ref, s2_ref, o_ref = refs
    else:
        a_ref, b_ref, o_ref = refs
    h = jnp.dot(a_ref[...], b_ref[...], preferred_element_type=F32)
    if shift:
        c, s1, s2 = c_ref[...], s1_ref[...], s2_ref[...]
        for g in range(h.shape[axis] // LANES):
            sl = (slice(None),) * axis + (slice(g * LANES, (g + 1) * LANES),)
            hg = h[sl]
            out = hg * c + pltpu.roll(hg, shift, axis) * s1 + pltpu.roll(hg, LANES - shift, axis) * s2
            o_ref[sl] = out.astype(o_ref.dtype)
    elif scale != 1.0:
        o_ref[...] = (h * scale).astype(o_ref.dtype)
    else:
        o_ref[...] = h.astype(o_ref.dtype)


def _proj(x, w, out_dtype, tables=None, shift=0, scale=1.0, tm=1024, tn=1024):
    n, k = x.shape
    c = w.shape[1]
    tm, tn = min(tm, n), min(tn, c)
    in_specs = [pl.BlockSpec((tm, k), lambda i, j: (i, 0)),
                pl.BlockSpec((k, tn), lambda i, j: (0, j))]
    args = [x, w]
    if shift:
        in_specs += [pl.BlockSpec((tm, LANES), lambda i, j: (i, 0))] * 3
        args += list(tables)
    return pl.pallas_call(
        functools.partial(_proj_kernel, shift=shift, scale=scale, axis=1),
        grid=(n // tm, c // tn),
        in_specs=in_specs,
        out_specs=pl.BlockSpec((tm, tn), lambda i, j: (i, j)),
        out_shape=jax.ShapeDtypeStruct((n, c), out_dtype),
        compiler_params=_params("parallel", "arbitrary"),
        name="proj",
    )(*args)


def _proj_t(wt, xt, out_dtype, tables=None, shift=0, scale=1.0, tm=1024, tn=1024):
    c, k = wt.shape
    n = xt.shape[1]
    tm, tn = min(tm, n), min(tn, c)
    in_specs = [pl.BlockSpec((tn, k), lambda i, j: (j, 0)),
                pl.BlockSpec((k, tm), lambda i, j: (0, i))]
    args = [wt, xt]
    if shift:
        in_specs += [pl.BlockSpec((LANES, tm), lambda i, j: (0, i))] * 3
        args += list(tables)
    return pl.pallas_call(
        functools.partial(_proj_kernel, shift=shift, scale=scale, axis=0),
        grid=(n // tm, c // tn),
        in_specs=in_specs,
        out_specs=pl.BlockSpec((tn, tm), lambda i, j: (j, i)),
        out_shape=jax.ShapeDtypeStruct((c, n), out_dtype),
        compiler_params=_params("parallel", "arbitrary"),
        name="proj_t",
    )(*args)


def _out_ln_kernel(*refs, nparts):
    parts = refs[:nparts]
    w_ref, x_ref, g_ref, b_ref, of_ref, ob_ref = refs[nparts:]
    acc = None
    off = 0
    for p in parts:
        kp = p.shape[1]
        t = jnp.dot(p[...], w_ref[off:off + kp, :], preferred_element_type=F32)
        acc = t if acc is None else acc + t
        off += kp
    y = _layer_norm(ALPHA * x_ref[...] + acc, g_ref[...], b_ref[...])
    of_ref[...] = y
    ob_ref[...] = y.astype(BF16)


def _out_ln(parts, w, x, g, b, tm=256):
    n, d = x.shape
    in_specs = [pl.BlockSpec((tm, p.shape[1]), lambda i: (i, 0)) for p in parts]
    in_specs += [pl.BlockSpec(w.shape, lambda i: (0, 0)),
                 pl.BlockSpec((tm, d), lambda i: (i, 0)),
                 pl.BlockSpec((1, d), lambda i: (0, 0)),
                 pl.BlockSpec((1, d), lambda i: (0, 0))]
    return pl.pallas_call(
        functools.partial(_out_ln_kernel, nparts=len(parts)),
        grid=(n // tm,),
        in_specs=in_specs,
        out_specs=[pl.BlockSpec((tm, d), lambda i: (i, 0)), pl.BlockSpec((tm, d), lambda i: (i, 0))],
        out_shape=[jax.ShapeDtypeStruct((n, d), F32), jax.ShapeDtypeStruct((n, d), BF16)],
        compiler_params=_params("parallel"),
        name="out_ln",
    )(*parts, w, x, g, b)


POOL_HALO = 16


def _pool_kernel(u_ref, halo_ref, w_ref, sc_ref, o_ref, ext, *, ts):
    i = pl.program_id(1)
    ext[0:POOL_HALO, :] = jnp.where(i > 0, halo_ref[0], 0.0)
    ext[POOL_HALO:, :] = u_ref[0]
    cnt = i * ts + lax.broadcasted_iota(jnp.int32, (ts, 1), 0) + 1
    for g, win in enumerate(POOL_WINDOWS):
        lo, hi = g * POOL_GROUP_DIM, (g + 1) * POOL_GROUP_DIM
        cur = ext[POOL_HALO:POOL_HALO + ts, lo:hi]
        s = cur
        for j in range(1, win):
            s = s + ext[POOL_HALO - j:POOL_HALO - j + ts, lo:hi]
        d = s / jnp.minimum(cnt, win).astype(F32) - cur
        y = jnp.dot(d.astype(BF16), w_ref[g], preferred_element_type=F32)
        o_ref[0, :, lo:hi] = (y * sc_ref[:, lo:hi]).astype(o_ref.dtype)


def _pool(u, w, scale, ts=512):
    b, s, c = u.shape
    ts = min(ts, s)
    hb = ts // POOL_HALO
    return pl.pallas_call(
        functools.partial(_pool_kernel, ts=ts),
        grid=(b, s // ts),
        in_specs=[pl.BlockSpec((1, ts, c), lambda bi, i: (bi, i, 0)),
                  pl.BlockSpec((1, POOL_HALO, c), lambda bi, i: (bi, jnp.maximum(i * hb - 1, 0), 0)),
                  pl.BlockSpec(w.shape, lambda bi, i: (0, 0, 0)),
                  pl.BlockSpec((1, c), lambda bi, i: (0, 0))],
        out_specs=pl.BlockSpec((1, ts, c), lambda bi, i: (bi, i, 0)),
        out_shape=jax.ShapeDtypeStruct((b, s, c), BF16),
        scratch_shapes=[pltpu.VMEM((ts + POOL_HALO, c), F32)],
        compiler_params=_params("parallel", "arbitrary"),
        name="pool",
    )(u, u, w, scale)


def _sortable(v):
    bits = lax.bitcast_convert_type(v, jnp.int32)
    return bits ^ ((bits >> 31) & 0x7FFFFFFF)


def _colsum8(v):
    tk, tq = v.shape
    return v.reshape(tk // SUBLANES, SUBLANES, tq).sum(axis=0)


def _idx_kernel(qi_ref, wi_ref, ki_ref, o_ref, keys, *, tq, tk, nk, topk):
    i = pl.program_id(1)
    nact = (i * tq + tq - 1) // tk + 1
    q_chunk = (i * tq + lax.broadcasted_iota(jnp.int32, (1, tq), 1)) // CHUNK

    def score_tile(c, carry):
        kt = ki_ref[0, pl.ds(pl.multiple_of(c * tk, tk), tk), :]
        acc = jnp.zeros((tk, tq), F32)
        for h in range(IDX_HEADS):
            il = jnp.dot(kt, qi_ref[h * IDX_DIM:(h + 1) * IDX_DIM, :], preferred_element_type=F32)
            acc = acc + jnp.maximum(il, 0.0) * wi_ref[h:h + 1, :]
        k_chunk = (c * tk + lax.broadcasted_iota(jnp.int32, (tk, 1), 0)) // CHUNK
        keys[c] = jnp.where(k_chunk <= q_chunk, _sortable(acc), INT_MIN)
        return carry

    lax.fori_loop(0, nact, score_tile, 0)

    def count(pred):
        def body(c, acc):
            return acc + _colsum8(jnp.where(pred(keys[c], c), 1, 0))
        acc = lax.fori_loop(0, nact, body, jnp.zeros((SUBLANES, tq), jnp.int32))
        return jnp.sum(acc, axis=0, keepdims=True)

    thr = jnp.where(count(lambda kk, c: kk >= 0) >= topk, 0, INT_MIN)

    def bit_step(b, thr):
        cand = thr | (1 << (30 - b))
        return jnp.where(count(lambda kk, c: kk >= cand) >= topk, cand, thr)

    thr = lax.fori_loop(0, 31, bit_step, thr)
    thr = jnp.maximum(thr, INT_MIN + 1)
    n_ge = count(lambda kk, c: kk >= thr)

    def write_tiles(select):
        def body(c, carry):
            o_ref[0, 0, c] = jnp.where(select(keys[c], c), 0.0, NEG).astype(o_ref.dtype)
            return carry
        lax.fori_loop(0, nact, body, 0)

    has_ties = jnp.max(n_ge) > topk

    @pl.when(jnp.logical_not(has_ties))
    def _():
        write_tiles(lambda kk, c: kk >= thr)

    @pl.when(has_ties)
    def _():
        need = topk - count(lambda kk, c: kk > thr)

        def key_index(c):
            return c * tk + lax.broadcasted_iota(jnp.int32, (tk, 1), 0)

        def idx_step(b, cut):
            cand = cut | (1 << (30 - b))
            below = count(lambda kk, c: (kk == thr) & (key_index(c) < cand))
            return jnp.where(below < need, cand, cut)

        cut = lax.fori_loop(0, 31, idx_step, jnp.zeros((1, tq), jnp.int32))
        write_tiles(lambda kk, c: (kk > thr) | ((kk == thr) & (key_index(c) <= cut)))

    def fill_tile(c, carry):
        o_ref[0, 0, c] = jnp.full((tk, tq), NEG, o_ref.dtype)
        return carry

    lax.fori_loop(nact, nk, fill_tile, 0)


def _idx_mask(qi_t, wi_t, ki, tq, tk):
    b, s, _ = ki.shape
    nq, nk = s // tq, s // tk
    topk = min(DSA_TOPK_MAX, s // 4)
    return pl.pallas_call(
        functools.partial(_idx_kernel, tq=tq, tk=tk, nk=nk, topk=topk),
        grid=(b, nq),
        in_specs=[pl.BlockSpec((IDX_HEADS * IDX_DIM, tq), lambda bi, i: (0, bi * nq + i)),
                  pl.BlockSpec((IDX_HEADS, tq), lambda bi, i: (0, bi * nq + i)),
                  pl.BlockSpec((1, s, IDX_DIM), lambda bi, i: (bi, 0, 0))],
        out_specs=pl.BlockSpec((1, 1, nk, tk, tq), lambda bi, i: (bi, i, 0, 0, 0)),
        out_shape=jax.ShapeDtypeStruct((b, nq, nk, tk, tq), BF16),
        scratch_shapes=[pltpu.VMEM((nk, tk, tq), jnp.int32)],
        compiler_params=_params("parallel", "arbitrary"),
        name="idx_mask",
    )(qi_t, wi_t, ki)


def _softmax_step(s, vt, h, m_sc, l_sc, acc_sc):
    m_prev = m_sc[h]
    m_new = jnp.maximum(m_prev, jnp.max(s, axis=0, keepdims=True))
    alpha = jnp.exp2(m_prev - m_new)
    p = jnp.exp2(s - m_new)
    l_sc[h] = alpha * l_sc[h] + jnp.sum(p, axis=0, keepdims=True)
    acc_sc[h] = alpha * acc_sc[h] + jnp.dot(vt, p.astype(BF16), preferred_element_type=F32)
    m_sc[h] = m_new


def _softmax_init(m_sc, l_sc, acc_sc):
    m_sc[...] = jnp.full(m_sc.shape, NEG, F32)
    l_sc[...] = jnp.zeros(l_sc.shape, F32)
    acc_sc[...] = jnp.zeros(acc_sc.shape, F32)


def _softmax_finish(o_ref, l_sc, acc_sc):
    for h in range(ATT_HEADS):
        o = (acc_sc[h] / l_sc[h]).T
        o_ref[0, :, h * HEAD_DIM:(h + 1) * HEAD_DIM] = o.astype(o_ref.dtype)


def _att_scratch(tq):
    return [pltpu.VMEM((ATT_HEADS, 1, tq), F32), pltpu.VMEM((ATT_HEADS, 1, tq), F32),
            pltpu.VMEM((ATT_HEADS, HEAD_DIM, tq), F32)]


def _head(h):
    return slice(h * HEAD_DIM, (h + 1) * HEAD_DIM)


def _dsa_kernel(k_ref, qt_ref, vt_ref, mask_ref, o_ref, m_sc, l_sc, acc_sc, *, tq, tk):
    i, j = pl.program_id(1), pl.program_id(2)
    last = (i * tq + tq - 1) // tk

    @pl.when(j == 0)
    def _():
        _softmax_init(m_sc, l_sc, acc_sc)

    @pl.when(j <= last)
    def _():
        mask = mask_ref[0, 0, 0].astype(F32)
        for h in range(ATT_HEADS):
            s = jnp.dot(k_ref[0, :, _head(h)], qt_ref[_head(h), :], preferred_element_type=F32) + mask
            _softmax_step(s, vt_ref[_head(h), :], h, m_sc, l_sc, acc_sc)

    @pl.when(j == last)
    def _():
        _softmax_finish(o_ref, l_sc, acc_sc)


def _dsa(k, q_t, v_t, mask, tq, tk):
    b, s, c = k.shape
    nq, nk = s // tq, s // tk

    def last(i):
        return (i * tq + tq - 1) // tk

    return pl.pallas_call(
        functools.partial(_dsa_kernel, tq=tq, tk=tk),
        grid=(b, nq, nk),
        in_specs=[pl.BlockSpec((1, tk, c), lambda bi, i, j: (bi, jnp.minimum(j, last(i)), 0)),
                  pl.BlockSpec((c, tq), lambda bi, i, j: (0, bi * nq + i)),
                  pl.BlockSpec((c, tk), lambda bi, i, j: (0, bi * nk + jnp.minimum(j, last(i)))),
                  pl.BlockSpec((1, 1, 1, tk, tq), lambda bi, i, j: (bi, i, jnp.minimum(j, last(i)), 0, 0))],
        out_specs=pl.BlockSpec((1, tq, c), lambda bi, i, j: (bi, i, 0)),
        out_shape=jax.ShapeDtypeStruct((b, s, c), BF16),
        scratch_shapes=_att_scratch(tq),
        compiler_params=_params("parallel", "parallel", "arbitrary"),
        name="dsa",
    )(k, q_t, v_t, mask)


CONV_HALO = 8


def _conv_kernel(u_ref, gb_ref, gc_ref, hu_ref, hgc_ref, w_ref, o_ref, ext, *, ts):
    i = pl.program_id(1)
    ext[0:CONV_HALO, :] = jnp.where(i > 0, hgc_ref[0] * hu_ref[0], 0.0)
    ext[CONV_HALO:, :] = gc_ref[0] * u_ref[0]
    conv = None
    for t in range(CONV_WIDTH):
        off = CONV_HALO - (CONV_WIDTH - 1) + t
        term = ext[off:off + ts, :] * w_ref[t:t + 1, :]
        conv = term if conv is None else conv + term
    o_ref[0] = (gb_ref[0] * conv).astype(o_ref.dtype)


def _gated_conv(h, w, ts=512):
    b, s, c3 = h.shape
    c = c3 // 3
    ts = min(ts, s)
    hb = ts // CONV_HALO

    def halo_map(col):
        return lambda bi, i: (bi, jnp.maximum(i * hb - 1, 0), col)

    return pl.pallas_call(
        functools.partial(_conv_kernel, ts=ts),
        grid=(b, s // ts),
        in_specs=[pl.BlockSpec((1, ts, c), lambda bi, i: (bi, i, 0)),
                  pl.BlockSpec((1, ts, c), lambda bi, i: (bi, i, 1)),
                  pl.BlockSpec((1, ts, c), lambda bi, i: (bi, i, 2)),
                  pl.BlockSpec((1, CONV_HALO, c), halo_map(0)),
                  pl.BlockSpec((1, CONV_HALO, c), halo_map(2)),
                  pl.BlockSpec(w.shape, lambda bi, i: (0, 0))],
        out_specs=pl.BlockSpec((1, ts, c), lambda bi, i: (bi, i, 0)),
        out_shape=jax.ShapeDtypeStruct((b, s, c), BF16),
        scratch_shapes=[pltpu.VMEM((ts + CONV_HALO, c), F32)],
        compiler_params=_params("parallel", "arbitrary"),
        name="gated_conv",
    )(h, h, h, h, h, w)


GATE_CHUNK = 256
GATE_AUG = 16


def _fgate_kernel(f_ref, fb_ref, hi_ref, mid_ref, lo_ref):
    s = f_ref.shape[1]
    r = lax.broadcasted_iota(jnp.int32, (GATE_CHUNK, GATE_CHUNK), 0)
    c = lax.broadcasted_iota(jnp.int32, (GATE_CHUNK, GATE_CHUNK), 1)
    tri = (c <= r).astype(F32)

    def body(t, carry):
        rows = pl.ds(t * GATE_CHUNK, GATE_CHUNK)
        z = f_ref[0, rows, :] + fb_ref[...]
        log_f = -(jnp.maximum(-z, 0.0) + jnp.log1p(jnp.exp(-jnp.abs(z))))
        cs = jnp.dot(tri, log_f, preferred_element_type=F32, precision=lax.Precision.HIGHEST) + carry
        b2 = cs * LOG2E
        hi = b2.astype(BF16)
        r1 = b2 - hi.astype(F32)
        mid = r1.astype(BF16)
        hi_ref[0, rows, :] = hi
        mid_ref[0, rows, :] = mid
        lo_ref[0, rows, :] = (r1 - mid.astype(F32)).astype(BF16)
        return cs[GATE_CHUNK - 1:GATE_CHUNK, :]

    lax.fori_loop(0, s // GATE_CHUNK, body, jnp.zeros((1, LANES), F32))


def _forget_cumsum(f, fb):
    b, s, c = f.shape
    spec = pl.BlockSpec((1, s, c), lambda bi: (bi, 0, 0))
    return pl.pallas_call(
        _fgate_kernel,
        grid=(b,),
        in_specs=[spec, pl.BlockSpec((1, c), lambda bi: (0, 0))],
        out_specs=[spec, spec, spec],
        out_shape=[jax.ShapeDtypeStruct((b, s, c), BF16)] * 3,
        compiler_params=_params("parallel"),
        name="forget_cumsum",
    )(f, fb)


def _fox_kernel(k_ref, ka_ref, qt_ref, qa_ref, vt_ref, o_ref, m_sc, l_sc, acc_sc, qfull, *, t):
    i, j = pl.program_id(1), pl.program_id(2)

    @pl.when(j == 0)
    def _():
        _softmax_init(m_sc, l_sc, acc_sc)
        for h in range(ATT_HEADS):
            qfull[h, 0:HEAD_DIM, :] = qt_ref[_head(h), :]
            qfull[h, HEAD_DIM:, :] = jnp.zeros((LANES, t), BF16)
            lo = HEAD_DIM + h * GATE_AUG
            qfull[h, lo:lo + GATE_AUG, :] = qa_ref[0, h]

    def step(diagonal):
        if diagonal:
            causal = (lax.broadcasted_iota(jnp.int32, (t, t), 0) <= lax.broadcasted_iota(jnp.int32, (t, t), 1))
        ka = ka_ref[0]
        for h in range(ATT_HEADS):
            kfull = jnp.concatenate([k_ref[0, :, _head(h)], ka], axis=1)
            s = jnp.dot(kfull, qfull[h], preferred_element_type=F32)
            if diagonal:
                s = jnp.where(causal, s, NEG)
            _softmax_step(s, vt_ref[_head(h), :], h, m_sc, l_sc, acc_sc)

    @pl.when(j < i)
    def _():
        step(False)

    @pl.when(j == i)
    def _():
        step(True)
        _softmax_finish(o_ref, l_sc, acc_sc)


def _fox(k, k_aug, q_t, q_aug, v_t, t=512):
    b, s, c = k.shape
    t = min(t, s)
    n = s // t
    return pl.pallas_call(
        functools.partial(_fox_kernel, t=t),
        grid=(b, n, n),
        in_specs=[pl.BlockSpec((1, t, c), lambda bi, i, j: (bi, jnp.minimum(j, i), 0)),
                  pl.BlockSpec((1, t, LANES), lambda bi, i, j: (bi, jnp.minimum(j, i), 0)),
                  pl.BlockSpec((c, t), lambda bi, i, j: (0, bi * n + i)),
                  pl.BlockSpec((1, ATT_HEADS, GATE_AUG, t), lambda bi, i, j: (bi, 0, 0, i)),
                  pl.BlockSpec((c, t), lambda bi, i, j: (0, bi * n + jnp.minimum(j, i)))],
        out_specs=pl.BlockSpec((1, t, c), lambda bi, i, j: (bi, i, 0)),
        out_shape=jax.ShapeDtypeStruct((b, s, c), BF16),
        scratch_shapes=_att_scratch(t) + [pltpu.VMEM((ATT_HEADS, 2 * HEAD_DIM, t), BF16)],
        compiler_params=_params("parallel", "parallel", "arbitrary"),
        name="fox",
    )(k, k_aug, q_t, q_aug, v_t)


def _xattn_kernel(xb_ref, xf_ref, wq_ref, kt_ref, v_ref, wo_ref, g_ref, b_ref, of_ref, ob_ref):
    q = jnp.dot(xb_ref[...], wq_ref[...], preferred_element_type=F32) * Q_SCALE
    q = q.astype(BF16)
    outs = []
    for h in range(XA_HEADS):
        s = jnp.dot(q[:, _head(h)], kt_ref[0, _head(h), :], preferred_element_type=F32)
        p = jnp.exp2(s - jnp.max(s, axis=1, keepdims=True))
        l = jnp.sum(p, axis=1, keepdims=True)
        pv = jnp.dot(p.astype(BF16), v_ref[0, :, _head(h)], preferred_element_type=F32)
        outs.append((pv / l).astype(BF16))
    o = jnp.concatenate(outs, axis=1)
    y = jnp.dot(o, wo_ref[...], preferred_element_type=F32)
    y = _layer_norm(ALPHA * xf_ref[...] + y, g_ref[...], b_ref[...])
    of_ref[...] = y
    ob_ref[...] = y.astype(BF16)


def _xattn(xb, xf, wq, kt, v, wo, g, b, seq, tm=256):
    n, d = xf.shape
    per_batch = seq // tm
    m = v.shape[1]
    return pl.pallas_call(
        _xattn_kernel,
        grid=(n // tm,),
        in_specs=[pl.BlockSpec((tm, d), lambda i: (i, 0)),
                  pl.BlockSpec((tm, d), lambda i: (i, 0)),
                  pl.BlockSpec(wq.shape, lambda i: (0, 0)),
                  pl.BlockSpec((1, XA_WIDTH, m), lambda i: (i // per_batch, 0, 0)),
                  pl.BlockSpec((1, m, XA_WIDTH), lambda i: (i // per_batch, 0, 0)),
                  pl.BlockSpec(wo.shape, lambda i: (0, 0)),
                  pl.BlockSpec((1, d), lambda i: (0, 0)),
                  pl.BlockSpec((1, d), lambda i: (0, 0))],
        out_specs=[pl.BlockSpec((tm, d), lambda i: (i, 0)), pl.BlockSpec((tm, d), lambda i: (i, 0))],
        out_shape=[jax.ShapeDtypeStruct((n, d), F32), jax.ShapeDtypeStruct((n, d), BF16)],
        compiler_params=_params("parallel"),
        name="xattn",
    )(xb, xf, wq, kt, v, wo, g, b)


def _ffn_kernel(xb_ref, wg_ref, wu_ref, wo_ref, xf_ref, g_ref, b_ref, of_ref, ob_ref, acc):
    j = pl.program_id(1)

    @pl.when(j == 0)
    def _():
        acc[...] = jnp.zeros(acc.shape, F32)

    xb = xb_ref[...]
    gate = jnp.dot(xb, wg_ref[...], preferred_element_type=F32)
    up = jnp.dot(xb, wu_ref[...], preferred_element_type=F32)
    hid = gate * (1.0 / (1.0 + jnp.exp(-gate))) * up
    acc[...] += jnp.dot(hid.astype(BF16), wo_ref[...], preferred_element_type=F32)

    @pl.when(j == pl.num_programs(1) - 1)
    def _():
        y = _layer_norm(ALPHA * xf_ref[...] + acc[...], g_ref[...], b_ref[...])
        of_ref[...] = y
        ob_ref[...] = y.astype(BF16)


def _ffn(xb, xf, w_in, w_out, g, b, tm=1024, th=512):
    n, d = xf.shape
    hidden = w_out.shape[0]
    nh = hidden // th
    once = pl.Buffered(1)
    return pl.pallas_call(
        _ffn_kernel,
        grid=(n // tm, nh),
        in_specs=[pl.BlockSpec((tm, d), lambda i, j: (i, 0), pipeline_mode=once),
                  pl.BlockSpec((d, th), lambda i, j: (0, j)),
                  pl.BlockSpec((d, th), lambda i, j: (0, j + nh)),
                  pl.BlockSpec((th, d), lambda i, j: (j, 0)),
                  pl.BlockSpec((tm, d), lambda i, j: (i, 0), pipeline_mode=once),
                  pl.BlockSpec((1, d), lambda i, j: (0, 0)),
                  pl.BlockSpec((1, d), lambda i, j: (0, 0))],
        out_specs=[pl.BlockSpec((tm, d), lambda i, j: (i, 0), pipeline_mode=once),
                   pl.BlockSpec((tm, d), lambda i, j: (i, 0), pipeline_mode=once)],
        out_shape=[jax.ShapeDtypeStruct((n, d), F32), jax.ShapeDtypeStruct((n, d), BF16)],
        scratch_shapes=[pltpu.VMEM((tm, d), F32)],
        compiler_params=_params("parallel", "arbitrary"),
        name="ffn",
    )(xb, w_in, w_in, w_out, xf, g, b)


def _rotary_tables(positions, dh):
    rot = dh // ROPE_FRACTION
    half = rot // 2
    inv_freq = jnp.power(ROPE_THETA, -(jnp.arange(half, dtype=F32) * 2.0 / rot))
    ang = positions.astype(F32)[..., None] * inv_freq
    cos, sin = jnp.cos(ang), jnp.sin(ang)
    zh = jnp.zeros_like(sin)
    rest = jnp.zeros(ang.shape[:-1] + (dh - rot,), F32)
    c = jnp.concatenate([cos, cos, rest + 1.0], axis=-1)
    s_lo = jnp.concatenate([zh, sin, rest], axis=-1)
    s_hi = jnp.concatenate([-sin, zh, rest], axis=-1)
    return tuple(t.reshape(-1, dh) for t in (c, s_lo, s_hi))


def _even_tables(positions):
    n = positions.size
    head = _rotary_tables(positions, HEAD_DIM)
    idx = _rotary_tables(positions, IDX_DIM)
    idx_full = tuple(jnp.tile(t, (1, LANES // IDX_DIM)) for t in idx)
    wi_scale = jnp.concatenate([jnp.full((IDX_HEADS,), IDX_HEADS ** -0.5, F32),
                                jnp.ones((LANES - IDX_DIM - IDX_HEADS,), F32)])
    pad = jnp.zeros((n, LANES - IDX_DIM), F32)
    tail = (jnp.concatenate([idx[0], pad + wi_scale], axis=1),
            jnp.concatenate([idx[1], pad], axis=1),
            jnp.concatenate([idx[2], pad], axis=1))
    return {"k": head,
            "q_t": tuple((t * Q_SCALE).T for t in head),
            "qi_t": tuple((t * IDX_DIM ** -0.5).T for t in idx_full),
            "tail": tail}


def _even_mixer(xb, xt, tables, w_in, pool_w, pool_scale, bsz, seq, tq, tk):
    n = xb.shape[0]
    hw = HALF_WIDTH
    wb = w_in.astype(BF16)
    wt = wb.T
    u = _proj(xb, wb[:, 0:hw], F32)
    q_t = _proj_t(wt[hw:2 * hw], xt, BF16, tables["q_t"], shift=HEAD_DIM // 8)
    k = _proj(xb, wb[:, 2 * hw:3 * hw], BF16, tables["k"], shift=HEAD_DIM // 8)
    v_t = _proj_t(wt[3 * hw:4 * hw], xt, BF16)
    qi_t = _proj_t(wt[4 * hw:5 * hw], xt, BF16, tables["qi_t"], shift=IDX_DIM // 8)
    w_tail = jnp.pad(wb[:, 5 * hw:], ((0, 0), (0, LANES - IDX_DIM - IDX_HEADS)))
    tail = _proj(xb, w_tail, F32, tables["tail"], shift=IDX_DIM // 8)

    ki = tail[:, :IDX_DIM].astype(BF16).reshape(bsz, seq, IDX_DIM)
    wi_t = tail[:, IDX_DIM:IDX_DIM + IDX_HEADS].T

    a = _pool(u.reshape(bsz, seq, hw), pool_w.astype(BF16), pool_scale.reshape(1, hw))
    mask = _idx_mask(qi_t, wi_t, ki, tq, tk)
    bb = _dsa(k.reshape(bsz, seq, hw), q_t, v_t, mask, tq, tk)
    return a.reshape(n, hw), bb.reshape(n, hw)


def _gate_operands(hi, mid, lo, bsz, seq):
    one = jnp.ones_like(hi)
    zero = jnp.zeros(hi.shape + (GATE_AUG - 6,), BF16)
    k_aug = jnp.concatenate([jnp.stack([-hi, -mid, -lo, one, one, one], axis=-1), zero], axis=-1)
    q_aug = jnp.concatenate([jnp.stack([one, one, one, hi, mid, lo], axis=-1), zero], axis=-1)
    return k_aug.reshape(bsz, seq, ATT_HEADS * GATE_AUG), q_aug.transpose(0, 2, 3, 1)


def _odd_mixer(xb, xt, w_in, conv_w, forget_b, bsz, seq):
    n = xb.shape[0]
    hw = HALF_WIDTH
    wb = w_in.astype(BF16)
    wt = wb.T
    ugg = _proj(xb, wb[:, 0:3 * hw], F32)
    q_t = _proj_t(wt[3 * hw:4 * hw], xt, BF16, scale=Q_SCALE)
    k = _proj(xb, wb[:, 4 * hw:5 * hw], BF16)
    v_t = _proj_t(wt[5 * hw:6 * hw], xt, BF16)
    w_tail = jnp.pad(wb[:, 6 * hw:], ((0, 0), (0, LANES - ATT_HEADS)))
    f = _proj(xb, w_tail, F32)
    fb = jnp.pad(forget_b, (0, LANES - ATT_HEADS)).reshape(1, LANES)
    terms = _forget_cumsum(f.reshape(bsz, seq, LANES), fb)
    k_aug, q_aug = _gate_operands(*(t[:, :, :ATT_HEADS] for t in terms), bsz, seq)

    c = _gated_conv(ugg.reshape(bsz, seq, 3 * hw), conv_w)
    d = _fox(k.reshape(bsz, seq, hw), k_aug, q_t, q_aug, v_t)
    return c.reshape(n, hw), d.reshape(n, hw)


def kernel(x, mem, positions, ev_w_in, ev_pool_w, ev_pool_scale, ev_w_out, od_w_in, od_conv_w, od_forget_b,
           od_w_out, ca_w_q, ca_w_kv, ca_w_o, ffn_w_in, ffn_w_out, ln_g, ln_b):
    bsz, seq, d = x.shape
    n = bsz * seq
    m = mem.shape[1]
    tq = tk = min(512, seq)
    xf = x.reshape(n, d)
    xb = xf.astype(BF16)
    memb = mem.reshape(bsz * m, d).astype(BF16)
    tables = _even_tables(positions)
    for i in range(DEPTH):
        j = i // 2
        xt = xb.T
        if i % 2 == 0:
            parts = _even_mixer(xb, xt, tables, ev_w_in[j], ev_pool_w[j], ev_pool_scale[j], bsz, seq, tq, tk)
            w_out = ev_w_out[j]
        else:
            parts = _odd_mixer(xb, xt, od_w_in[j], od_conv_w[j], od_forget_b[j], bsz, seq)
            w_out = od_w_out[j]
        g, b = ln_g[i].reshape(3, 1, d), ln_b[i].reshape(3, 1, d)
        xf, xb = _out_ln(list(parts), w_out.astype(BF16), xf, g[0], b[0])

        kv = _proj(memb, ca_w_kv[i].astype(BF16), BF16).reshape(bsz, m, 2 * XA_WIDTH)
        kt = kv[:, :, :XA_WIDTH].transpose(0, 2, 1)
        xf, xb = _xattn(xb, xf, ca_w_q[i].astype(BF16), kt, kv[:, :, XA_WIDTH:], ca_w_o[i].astype(BF16),
                        g[1], b[1], seq)

        xf, xb = _ffn(xb, xf, ffn_w_in[i].astype(BF16), ffn_w_out[i].astype(BF16), g[2], b[2])
    return xf.reshape(bsz, seq, d)
```

```python
import functools

import jax
import jax.numpy as jnp
from jax import lax
from jax.experimental import pallas as pl
from jax.experimental.pallas import tpu as pltpu

F32 = jnp.float32
BF16 = jnp.bfloat16

D_MODEL = 2048
DEPTH = 4
CHUNK = 64
HEAD_DIM = 128
HALF_WIDTH = D_MODEL // 2
POOL_WINDOWS = (2, 4, 8, 16)
POOL_GROUP_DIM = HALF_WIDTH // len(POOL_WINDOWS)
ATT_HEADS = HALF_WIDTH // HEAD_DIM
IDX_HEADS = 16
IDX_DIM = 64
DSA_TOPK_MAX = 256
CONV_WIDTH = 3
XA_HEADS = 4
XA_WIDTH = XA_HEADS * HEAD_DIM
FFN_HIDDEN = -(-(8 * D_MODEL) // (3 * 256)) * 256
ROPE_THETA = 500000.0
ROPE_FRACTION = 4
LN_EPS = 1e-5
ALPHA = (2 * DEPTH) ** 0.25
LOG2E = 1.4426950408889634
Q_SCALE = HEAD_DIM ** -0.5 * LOG2E

LANES = 128
SUBLANES = 8
BF16_ROWS = 16
NEG = -1e30
INT_MIN = -(2 ** 31)
VMEM_LIMIT = 60000 * 1024


def _params(*semantics):
    return pltpu.CompilerParams(dimension_semantics=semantics, vmem_limit_bytes=VMEM_LIMIT)


def _layer_norm(y, g, b):
    mu = jnp.mean(y, axis=-1, keepdims=True)
    d = y - mu
    var = jnp.mean(d * d, axis=-1, keepdims=True)
    return d * lax.rsqrt(var + LN_EPS) * g + b


def _proj_kernel(*refs, shift, scale, axis):
    if shift:
        a_ref, b_ref, c_ref, s1_ref, s2_ref, o_ref = refs
    else:
        a_ref, b_ref, o_ref = refs
    h = jnp.dot(a_ref[...], b_ref[...], preferred_element_type=F32)
    if shift:
        c, s1, s2 = c_ref[...], s1_ref[...], s2_ref[...]
        for g in range(h.shape[axis] // LANES):
            sl = (slice(None),) * axis + (slice(g * LANES, (g + 1) * LANES),)
            hg = h[sl]
            out = hg * c + pltpu.roll(hg, shift, axis) * s1 + pltpu.roll(hg, LANES - shift, axis) * s2
            o_ref[sl] = out.astype(o_ref.dtype)
    elif scale != 1.0:
        o_ref[...] = (h * scale).astype(o_ref.dtype)
    else:
        o_ref[...] = h.astype(o_ref.dtype)


def _proj(x, w, out_dtype, tables=None, shift=0, scale=1.0, tm=1024, tn=1024):
    n, k = x.shape
    c = w.shape[1]
    tm, tn = min(tm, n), min(tn, c)
    in_specs = [pl.BlockSpec((tm, k), lambda i, j: (i, 0)),
                pl.BlockSpec((k, tn), lambda i, j: (0, j))]
    args = [x, w]
    if shift:
        in_specs += [pl.BlockSpec((tm, LANES), lambda i, j: (i, 0))] * 3
        args += list(tables)
    return pl.pallas_call(
        functools.partial(_proj_kernel, shift=shift, scale=scale, axis=1),
        grid=(n // tm, c // tn),
        in_specs=in_specs,
        out_specs=pl.BlockSpec((tm, tn), lambda i, j: (i, j)),
        out_shape=jax.ShapeDtypeStruct((n, c), out_dtype),
        compiler_params=_params("parallel", "arbitrary"),
        name="proj",
    )(*args)


def _proj_t(wt, xt, out_dtype, tables=None, shift=0, scale=1.0, tm=1024, tn=1024):
    c, k = wt.shape
    n = xt.shape[1]
    tm, tn = min(tm, n), min(tn, c)
    in_specs = [pl.BlockSpec((tn, k), lambda i, j: (j, 0)),
                pl.BlockSpec((k, tm), lambda i, j: (0, i))]
    args = [wt, xt]
    if shift:
        in_specs += [pl.BlockSpec((LANES, tm), lambda i, j: (0, i))] * 3
        args += list(tables)
    return pl.pallas_call(
        functools.partial(_proj_kernel, shift=shift, scale=scale, axis=0),
        grid=(n // tm, c // tn),
        in_specs=in_specs,
        out_specs=pl.BlockSpec((tn, tm), lambda i, j: (j, i)),
        out_shape=jax.ShapeDtypeStruct((c, n), out_dtype),
        compiler_params=_params("parallel", "arbitrary"),
        name="proj_t",
    )(*args)


def _out_ln_kernel(*refs, nparts):
    parts = refs[:nparts]
    w_ref, x_ref, g_ref, b_ref, of_ref, ob_ref = refs[nparts:]
    acc = None
    off = 0
    for p in parts:
        kp = p.shape[1]
        t = jnp.dot(p[...], w_ref[off:off + kp, :], preferred_element_type=F32)
        acc = t if acc is None else acc + t
        off += kp
    y = _layer_norm(ALPHA * x_ref[...] + acc, g_ref[...], b_ref[...])
    of_ref[...] = y
    ob_ref[...] = y.astype(BF16)


def _out_ln(parts, w, x, g, b, tm=256):
    n, d = x.shape
    in_specs = [pl.BlockSpec((tm, p.shape[1]), lambda i: (i, 0)) for p in parts]
    in_specs += [pl.BlockSpec(w.shape, lambda i: (0, 0)),
                 pl.BlockSpec((tm, d), lambda i: (i, 0)),
                 pl.BlockSpec((1, d), lambda i: (0, 0)),
                 pl.BlockSpec((1, d), lambda i: (0, 0))]
    return pl.pallas_call(
        functools.partial(_out_ln_kernel, nparts=len(parts)),
        grid=(n // tm,),
        in_specs=in_specs,
        out_specs=[pl.BlockSpec((tm, d), lambda i: (i, 0)), pl.BlockSpec((tm, d), lambda i: (i, 0))],
        out_shape=[jax.ShapeDtypeStruct((n, d), F32), jax.ShapeDtypeStruct((n, d), BF16)],
        compiler_params=_params("parallel"),
        name="out_ln",
    )(*parts, w, x, g, b)


POOL_HALO = 16


def _pool_kernel(u_ref, halo_ref, w_ref, sc_ref, o_ref, ext, *, ts):
    i = pl.program_id(1)
    ext[0:POOL_HALO, :] = jnp.where(i > 0, halo_ref[0], 0.0)
    ext[POOL_HALO:, :] = u_ref[0]
    cnt = i * ts + lax.broadcasted_iota(jnp.int32, (ts, 1), 0) + 1
    for g, win in enumerate(POOL_WINDOWS):
        lo, hi = g * POOL_GROUP_DIM, (g + 1) * POOL_GROUP_DIM
        cur = ext[POOL_HALO:POOL_HALO + ts, lo:hi]
        s = cur
        for j in range(1, win):
            s = s + ext[POOL_HALO - j:POOL_HALO - j + ts, lo:hi]
        d = s / jnp.minimum(cnt, win).astype(F32) - cur
        y = jnp.dot(d.astype(BF16), w_ref[g], preferred_element_type=F32)
        o_ref[0, :, lo:hi] = (y * sc_ref[:, lo:hi]).astype(o_ref.dtype)


def _pool(u, w, scale, ts=512):
    b, s, c = u.shape
    ts = min(ts, s)
    hb = ts // POOL_HALO
    return pl.pallas_call(
        functools.partial(_pool_kernel, ts=ts),
        grid=(b, s // ts),
        in_specs=[pl.BlockSpec((1, ts, c), lambda bi, i: (bi, i, 0)),
                  pl.BlockSpec((1, POOL_HALO, c), lambda bi, i: (bi, jnp.maximum(i * hb - 1, 0), 0)),
                  pl.BlockSpec(w.shape, lambda bi, i: (0, 0, 0)),
                  pl.BlockSpec((1, c), lambda bi, i: (0, 0))],
        out_specs=pl.BlockSpec((1, ts, c), lambda bi, i: (bi, i, 0)),
        out_shape=jax.ShapeDtypeStruct((b, s, c), BF16),
        scratch_shapes=[pltpu.VMEM((ts + POOL_HALO, c), F32)],
        compiler_params=_params("parallel", "arbitrary"),
        name="pool",
    )(u, u, w, scale)


def _sortable(v):
    bits = lax.bitcast_convert_type(v, jnp.int32)
    return bits ^ ((bits >> 31) & 0x7FFFFFFF)


def _colsum8(v):
    tk, tq = v.shape
    return v.reshape(tk // SUBLANES, SUBLANES, tq).sum(axis=0)


def _idx_kernel(qi_ref, wi_ref, ki_ref, o_ref, keys, *, tq, tk, nk, topk):
    i = pl.program_id(1)
    nact = (i * tq + tq - 1) // tk + 1
    q_chunk = (i * tq + lax.broadcasted_iota(jnp.int32, (1, tq), 1)) // CHUNK

    def score_tile(c, carry):
        kt = ki_ref[0, pl.ds(pl.multiple_of(c * tk, tk), tk), :]
        acc = jnp.zeros((tk, tq), F32)
        for h in range(IDX_HEADS):
            il = jnp.dot(kt, qi_ref[h * IDX_DIM:(h + 1) * IDX_DIM, :], preferred_element_type=F32)
            acc = acc + jnp.maximum(il, 0.0) * wi_ref[h:h + 1, :]
        k_chunk = (c * tk + lax.broadcasted_iota(jnp.int32, (tk, 1), 0)) // CHUNK
        keys[c] = jnp.where(k_chunk <= q_chunk, _sortable(acc), INT_MIN)
        return carry

    lax.fori_loop(0, nact, score_tile, 0)

    def count(pred):
        def body(c, acc):
            return acc + _colsum8(jnp.where(pred(keys[c], c), 1, 0))
        acc = lax.fori_loop(0, nact, body, jnp.zeros((SUBLANES, tq), jnp.int32))
        return jnp.sum(acc, axis=0, keepdims=True)

    thr = jnp.where(count(lambda kk, c: kk >= 0) >= topk, 0, INT_MIN)

    def bit_step(b, thr):
        cand = thr | (1 << (30 - b))
        return jnp.where(count(lambda kk, c: kk >= cand) >= topk, cand, thr)

    thr = lax.fori_loop(0, 31, bit_step, thr)
    thr = jnp.maximum(thr, INT_MIN + 1)
    n_ge = count(lambda kk, c: kk >= thr)

    def write_tiles(select):
        def body(c, carry):
            o_ref[0, 0, c] = jnp.where(select(keys[c], c), 0.0, NEG).astype(o_ref.dtype)
            return carry
        lax.fori_loop(0, nact, body, 0)

    has_ties = jnp.max(n_ge) > topk

    @pl.when(jnp.logical_not(has_ties))
    def _():
        write_tiles(lambda kk, c: kk >= thr)

    @pl.when(has_ties)
    def _():
        need = topk - count(lambda kk, c: kk > thr)

        def key_index(c):
            return c * tk + lax.broadcasted_iota(jnp.int32, (tk, 1), 0)

        def idx_step(b, cut):
            cand = cut | (1 << (30 - b))
            below = count(lambda kk, c: (kk == thr) & (key_index(c) < cand))
            return jnp.where(below < need, cand, cut)

        cut = lax.fori_loop(0, 31, idx_step, jnp.zeros((1, tq), jnp.int32))
        write_tiles(lambda kk, c: (kk > thr) | ((kk == thr) & (key_index(c) <= cut)))

    def fill_tile(c, carry):
        o_ref[0, 0, c] = jnp.full((tk, tq), NEG, o_ref.dtype)
        return carry

    lax.fori_loop(nact, nk, fill_tile, 0)


def _idx_mask(qi_t, wi_t, ki, tq, tk):
    b, s, _ = ki.shape
    nq, nk = s // tq, s // tk
    topk = min(DSA_TOPK_MAX, s // 4)
    return pl.pallas_call(
        functools.partial(_idx_kernel, tq=tq, tk=tk, nk=nk, topk=topk),
        grid=(b, nq),
        in_specs=[pl.BlockSpec((IDX_HEADS * IDX_DIM, tq), lambda bi, i: (0, bi * nq + i)),
                  pl.BlockSpec((IDX_HEADS, tq), lambda bi, i: (0, bi * nq + i)),
                  pl.BlockSpec((1, s, IDX_DIM), lambda bi, i: (bi, 0, 0))],
        out_specs=pl.BlockSpec((1, 1, nk, tk, tq), lambda bi, i: (bi, i, 0, 0, 0)),
        out_shape=jax.ShapeDtypeStruct((b, nq, nk, tk, tq), BF16),
        scratch_shapes=[pltpu.VMEM((nk, tk, tq), jnp.int32)],
        compiler_params=_params("parallel", "arbitrary"),
        name="idx_mask",
    )(qi_t, wi_t, ki)


def _softmax_step(s, vt, h, m_sc, l_sc, acc_sc):
    m_prev = m_sc[h]
    m_new = jnp.maximum(m_prev, jnp.max(s, axis=0, keepdims=True))
    alpha = jnp.exp2(m_prev - m_new)
    p = jnp.exp2(s - m_new)
    l_sc[h] = alpha * l_sc[h] + jnp.sum(p, axis=0, keepdims=True)
    acc_sc[h] = alpha * acc_sc[h] + jnp.dot(vt, p.astype(BF16), preferred_element_type=F32)
    m_sc[h] = m_new


def _softmax_init(m_sc, l_sc, acc_sc):
    m_sc[...] = jnp.full(m_sc.shape, NEG, F32)
    l_sc[...] = jnp.zeros(l_sc.shape, F32)
    acc_sc[...] = jnp.zeros(acc_sc.shape, F32)


def _softmax_finish(o_ref, l_sc, acc_sc):
    for h in range(ATT_HEADS):
        o = (acc_sc[h] / l_sc[h]).T
        o_ref[0, :, h * HEAD_DIM:(h + 1) * HEAD_DIM] = o.astype(o_ref.dtype)


def _att_scratch(tq):
    return [pltpu.VMEM((ATT_HEADS, 1, tq), F32), pltpu.VMEM((ATT_HEADS, 1, tq), F32),
            pltpu.VMEM((ATT_HEADS, HEAD_DIM, tq), F32)]


def _head(h):
    return slice(h * HEAD_DIM, (h + 1) * HEAD_DIM)


def _dsa_kernel(k_ref, qt_ref, vt_ref, mask_ref, o_ref, m_sc, l_sc, acc_sc, *, tq, tk):
    i, j = pl.program_id(1), pl.program_id(2)
    last = (i * tq + tq - 1) // tk

    @pl.when(j == 0)
    def _():
        _softmax_init(m_sc, l_sc, acc_sc)

    @pl.when(j <= last)
    def _():
        mask = mask_ref[0, 0, 0].astype(F32)
        for h in range(ATT_HEADS):
            s = jnp.dot(k_ref[0, :, _head(h)], qt_ref[_head(h), :], preferred_element_type=F32) + mask
            _softmax_step(s, vt_ref[_head(h), :], h, m_sc, l_sc, acc_sc)

    @pl.when(j == last)
    def _():
        _softmax_finish(o_ref, l_sc, acc_sc)


def _dsa(k, q_t, v_t, mask, tq, tk):
    b, s, c = k.shape
    nq, nk = s // tq, s // tk

    def last(i):
        return (i * tq + tq - 1) // tk

    return pl.pallas_call(
        functools.partial(_dsa_kernel, tq=tq, tk=tk),
        grid=(b, nq, nk),
        in_specs=[pl.BlockSpec((1, tk, c), lambda bi, i, j: (bi, jnp.minimum(j, last(i)), 0)),
                  pl.BlockSpec((c, tq), lambda bi, i, j: (0, bi * nq + i)),
                  pl.BlockSpec((c, tk), lambda bi, i, j: (0, bi * nk + jnp.minimum(j, last(i)))),
                  pl.BlockSpec((1, 1, 1, tk, tq), lambda bi, i, j: (bi, i, jnp.minimum(j, last(i)), 0, 0))],
        out_specs=pl.BlockSpec((1, tq, c), lambda bi, i, j: (bi, i, 0)),
        out_shape=jax.ShapeDtypeStruct((b, s, c), BF16),
        scratch_shapes=_att_scratch(tq),
        compiler_params=_params("parallel", "parallel", "arbitrary"),
        name="dsa",
    )(k, q_t, v_t, mask)


CONV_HALO = 8


def _conv_kernel(u_ref, gb_ref, gc_ref, hu_ref, hgc_ref, w_ref, o_ref, ext, *, ts):
    i = pl.program_id(1)
    ext[0:CONV_HALO, :] = jnp.where(i > 0, hgc_ref[0] * hu_ref[0], 0.0)
    ext[CONV_HALO:, :] = gc_ref[0] * u_ref[0]
    conv = None
    for t in range(CONV_WIDTH):
        off = CONV_HALO - (CONV_WIDTH - 1) + t
        term = ext[off:off + ts, :] * w_ref[t:t + 1, :]
        conv = term if conv is None else conv + term
    o_ref[0] = (gb_ref[0] * conv).astype(o_ref.dtype)


def _gated_conv(h, w, ts=512):
    b, s, c3 = h.shape
    c = c3 // 3
    ts = min(ts, s)
    hb = ts // CONV_HALO

    def halo_map(col):
        return lambda bi, i: (bi, jnp.maximum(i * hb - 1, 0), col)

    return pl.pallas_call(
        functools.partial(_conv_kernel, ts=ts),
        grid=(b, s // ts),
        in_specs=[pl.BlockSpec((1, ts, c), lambda bi, i: (bi, i, 0)),
                  pl.BlockSpec((1, ts, c), lambda bi, i: (bi, i, 1)),
                  pl.BlockSpec((1, ts, c), lambda bi, i: (bi, i, 2)),
                  pl.BlockSpec((1, CONV_HALO, c), halo_map(0)),
                  pl.BlockSpec((1, CONV_HALO, c), halo_map(2)),
                  pl.BlockSpec(w.shape, lambda bi, i: (0, 0))],
        out_specs=pl.BlockSpec((1, ts, c), lambda bi, i: (bi, i, 0)),
        out_shape=jax.ShapeDtypeStruct((b, s, c), BF16),
        scratch_shapes=[pltpu.VMEM((ts + CONV_HALO, c), F32)],
        compiler_params=_params("parallel", "arbitrary"),
        name="gated_conv",
    )(h, h, h, h, h, w)


GATE_CHUNK = 256
GATE_AUG = 16


def _fgate_kernel(f_ref, fb_ref, hi_ref, mid_ref, lo_ref):
    s = f_ref.shape[1]
    r = lax.broadcasted_iota(jnp.int32, (GATE_CHUNK, GATE_CHUNK), 0)
    c = lax.broadcasted_iota(jnp.int32, (GATE_CHUNK, GATE_CHUNK), 1)
    tri = (c <= r).astype(F32)

    def body(t, carry):
        rows = pl.ds(t * GATE_CHUNK, GATE_CHUNK)
        z = f_ref[0, rows, :] + fb_ref[...]
        log_f = -(jnp.maximum(-z, 0.0) + jnp.log1p(jnp.exp(-jnp.abs(z))))
        cs = jnp.dot(tri, log_f, preferred_element_type=F32, precision=lax.Precision.HIGHEST) + carry
        b2 = cs * LOG2E
        hi = b2.astype(BF16)
        r1 = b2 - hi.astype(F32)
        mid = r1.astype(BF16)
        hi_ref[0, rows, :] = hi
        mid_ref[0, rows, :] = mid
        lo_ref[0, rows, :] = (r1 - mid.astype(F32)).astype(BF16)
        return cs[GATE_CHUNK - 1:GATE_CHUNK, :]

    lax.fori_loop(0, s // GATE_CHUNK, body, jnp.zeros((1, LANES), F32))


def _forget_cumsum(f, fb):
    b, s, c = f.shape
    spec = pl.BlockSpec((1, s, c), lambda bi: (bi, 0, 0))
    return pl.pallas_call(
        _fgate_kernel,
        grid=(b,),
        in_specs=[spec, pl.BlockSpec((1, c), lambda bi: (0, 0))],
        out_specs=[spec, spec, spec],
        out_shape=[jax.ShapeDtypeStruct((b, s, c), BF16)] * 3,
        compiler_params=_params("parallel"),
        name="forget_cumsum",
    )(f, fb)


def _fox_kernel(first_ref, k_ref, ka_ref, qt_ref, qa_ref, vt_ref, o_ref, m_sc, l_sc, acc_sc, qfull, *, t):
    b, i, j = pl.program_id(0), pl.program_id(1), pl.program_id(2)
    steps = i - first_ref[b, i]

    def step(diagonal):
        if diagonal:
            causal = (lax.broadcasted_iota(jnp.int32, (t, t), 0) <= lax.broadcasted_iota(jnp.int32, (t, t), 1))
        ka = ka_ref[0]
        for h in range(ATT_HEADS):
            kfull = jnp.concatenate([k_ref[0, :, _head(h)], ka], axis=1)
            s = jnp.dot(kfull, qfull[h], preferred_element_type=F32)
            if diagonal:
                s = jnp.where(causal, s, NEG)
            _softmax_step(s, vt_ref[_head(h), :], h, m_sc, l_sc, acc_sc)

    @pl.when(j == 0)
    def _():
        _softmax_init(m_sc, l_sc, acc_sc)
        for h in range(ATT_HEADS):
            qfull[h, 0:HEAD_DIM, :] = qt_ref[_head(h), :]
            qfull[h, HEAD_DIM:, :] = jnp.zeros((LANES, t), BF16)
            lo = HEAD_DIM + h * GATE_AUG
            qfull[h, lo:lo + GATE_AUG, :] = qa_ref[0, h]
        step(True)

    @pl.when((j > 0) & (j <= steps))
    def _():
        step(False)

    @pl.when(j == steps)
    def _():
        _softmax_finish(o_ref, l_sc, acc_sc)


def _fox(first, k, k_aug, q_t, q_aug, v_t, t):
    b, s, c = k.shape
    n = s // t

    def key_tile(bi, i, j, first_ref):
        return jnp.maximum(i - j, first_ref[bi, i])

    grid_spec = pltpu.PrefetchScalarGridSpec(
        num_scalar_prefetch=1,
        grid=(b, n, n),
        in_specs=[pl.BlockSpec((1, t, c), lambda bi, i, j, f: (bi, key_tile(bi, i, j, f), 0)),
                  pl.BlockSpec((1, t, LANES), lambda bi, i, j, f: (bi, key_tile(bi, i, j, f), 0)),
                  pl.BlockSpec((c, t), lambda bi, i, j, f: (0, bi * n + i)),
                  pl.BlockSpec((1, ATT_HEADS, GATE_AUG, t), lambda bi, i, j, f: (bi, 0, 0, i)),
                  pl.BlockSpec((c, t), lambda bi, i, j, f: (0, bi * n + key_tile(bi, i, j, f)))],
        out_specs=pl.BlockSpec((1, t, c), lambda bi, i, j, f: (bi, i, 0)),
        scratch_shapes=_att_scratch(t) + [pltpu.VMEM((ATT_HEADS, 2 * HEAD_DIM, t), BF16)])
    return pl.pallas_call(
        functools.partial(_fox_kernel, t=t),
        grid_spec=grid_spec,
        out_shape=jax.ShapeDtypeStruct((b, s, c), BF16),
        compiler_params=_params("parallel", "parallel", "arbitrary"),
        name="fox",
    )(first, k, k_aug, q_t, q_aug, v_t)


def _xattn_kernel(xb_ref, xf_ref, wq_ref, kt_ref, v_ref, wo_ref, g_ref, b_ref, of_ref, ob_ref):
    q = jnp.dot(xb_ref[...], wq_ref[...], preferred_element_type=F32) * Q_SCALE
    q = q.astype(BF16)
    outs = []
    for h in range(XA_HEADS):
        s = jnp.dot(q[:, _head(h)], kt_ref[0, _head(h), :], preferred_element_type=F32)
        p = jnp.exp2(s - jnp.max(s, axis=1, keepdims=True))
        l = jnp.sum(p, axis=1, keepdims=True)
        pv = jnp.dot(p.astype(BF16), v_ref[0, :, _head(h)], preferred_element_type=F32)
        outs.append((pv / l).astype(BF16))
    o = jnp.concatenate(outs, axis=1)
    y = jnp.dot(o, wo_ref[...], preferred_element_type=F32)
    y = _layer_norm(ALPHA * xf_ref[...] + y, g_ref[...], b_ref[...])
    of_ref[...] = y
    ob_ref[...] = y.astype(BF16)


def _xattn(xb, xf, wq, kt, v, wo, g, b, seq, tm=256):
    n, d = xf.shape
    per_batch = seq // tm
    m = v.shape[1]
    return pl.pallas_call(
        _xattn_kernel,
        grid=(n // tm,),
        in_specs=[pl.BlockSpec((tm, d), lambda i: (i, 0)),
                  pl.BlockSpec((tm, d), lambda i: (i, 0)),
                  pl.BlockSpec(wq.shape, lambda i: (0, 0)),
                  pl.BlockSpec((1, XA_WIDTH, m), lambda i: (i // per_batch, 0, 0)),
                  pl.BlockSpec((1, m, XA_WIDTH), lambda i: (i // per_batch, 0, 0)),
                  pl.BlockSpec(wo.shape, lambda i: (0, 0)),
                  pl.BlockSpec((1, d), lambda i: (0, 0)),
                  pl.BlockSpec((1, d), lambda i: (0, 0))],
        out_specs=[pl.BlockSpec((tm, d), lambda i: (i, 0)), pl.BlockSpec((tm, d), lambda i: (i, 0))],
        out_shape=[jax.ShapeDtypeStruct((n, d), F32), jax.ShapeDtypeStruct((n, d), BF16)],
        compiler_params=_params("parallel"),
        name="xattn",
    )(xb, xf, wq, kt, v, wo, g, b)


def _ffn_kernel(xb_ref, wg_ref, wu_ref, wo_ref, xf_ref, g_ref, b_ref, of_ref, ob_ref, acc):
    j = pl.program_id(1)

    @pl.when(j == 0)
    def _():
        acc[...] = jnp.zeros(acc.shape, F32)

    xb = xb_ref[...]
    gate = jnp.dot(xb, wg_ref[...], preferred_element_type=F32)
    up = jnp.dot(xb, wu_ref[...], preferred_element_type=F32)
    hid = gate * (1.0 / (1.0 + jnp.exp(-gate))) * up
    acc[...] += jnp.dot(hid.astype(BF16), wo_ref[...], preferred_element_type=F32)

    @pl.when(j == pl.num_programs(1) - 1)
    def _():
        y = _layer_norm(ALPHA * xf_ref[...] + acc[...], g_ref[...], b_ref[...])
        of_ref[...] = y
        ob_ref[...] = y.astype(BF16)


def _ffn(xb, xf, w_in, w_out, g, b, tm=512, th=512):
    n, d = xf.shape
    hidden = w_out.shape[0]
    nh = hidden // th
    return pl.pallas_call(
        _ffn_kernel,
        grid=(n // tm, nh),
        in_specs=[pl.BlockSpec((tm, d), lambda i, j: (i, 0)),
                  pl.BlockSpec((d, th), lambda i, j: (0, j)),
                  pl.BlockSpec((d, th), lambda i, j: (0, j + nh)),
                  pl.BlockSpec((th, d), lambda i, j: (j, 0)),
                  pl.BlockSpec((tm, d), lambda i, j: (i, 0)),
                  pl.BlockSpec((1, d), lambda i, j: (0, 0)),
                  pl.BlockSpec((1, d), lambda i, j: (0, 0))],
        out_specs=[pl.BlockSpec((tm, d), lambda i, j: (i, 0)),
                   pl.BlockSpec((tm, d), lambda i, j: (i, 0))],
        out_shape=[jax.ShapeDtypeStruct((n, d), F32), jax.ShapeDtypeStruct((n, d), BF16)],
        scratch_shapes=[pltpu.VMEM((tm, d), F32)],
        compiler_params=_params("parallel", "arbitrary"),
        name="ffn",
    )(xb, w_in, w_in, w_out, xf, g, b)


def _rotary_tables(positions, dh):
    rot = dh // ROPE_FRACTION
    half = rot // 2
    inv_freq = jnp.power(ROPE_THETA, -(jnp.arange(half, dtype=F32) * 2.0 / rot))
    ang = positions.astype(F32)[..., None] * inv_freq
    cos, sin = jnp.cos(ang), jnp.sin(ang)
    zh = jnp.zeros_like(sin)
    rest = jnp.zeros(ang.shape[:-1] + (dh - rot,), F32)
    c = jnp.concatenate([cos, cos, rest + 1.0], axis=-1)
    s_lo = jnp.concatenate([zh, sin, rest], axis=-1)
    s_hi = jnp.concatenate([-sin, zh, rest], axis=-1)
    return tuple(t.reshape(-1, dh) for t in (c, s_lo, s_hi))


def _even_tables(positions):
    n = positions.size
    head = _rotary_tables(positions, HEAD_DIM)
    idx = _rotary_tables(positions, IDX_DIM)
    idx_full = tuple(jnp.tile(t, (1, LANES // IDX_DIM)) for t in idx)
    wi_scale = jnp.concatenate([jnp.full((IDX_HEADS,), IDX_HEADS ** -0.5, F32),
                                jnp.ones((LANES - IDX_DIM - IDX_HEADS,), F32)])
    pad = jnp.zeros((n, LANES - IDX_DIM), F32)
    tail = (jnp.concatenate([idx[0], pad + wi_scale], axis=1),
            jnp.concatenate([idx[1], pad], axis=1),
            jnp.concatenate([idx[2], pad], axis=1))
    return {"k": head,
            "q_t": tuple((t * Q_SCALE).T for t in head),
            "qi_t": tuple((t * IDX_DIM ** -0.5).T for t in idx_full),
            "tail": tail}


def _even_mixer(xb, xt, tables, w_in, pool_w, pool_scale, bsz, seq, tq, tk):
    n = xb.shape[0]
    hw = HALF_WIDTH
    wb = w_in.astype(BF16)
    wt = wb.T
    u = _proj(xb, wb[:, 0:hw], F32)
    q_t = _proj_t(wt[hw:2 * hw], xt, BF16, tables["q_t"], shift=HEAD_DIM // 8)
    k = _proj(xb, wb[:, 2 * hw:3 * hw], BF16, tables["k"], shift=HEAD_DIM // 8)
    v_t = _proj_t(wt[3 * hw:4 * hw], xt, BF16)
    qi_t = _proj_t(wt[4 * hw:5 * hw], xt, BF16, tables["qi_t"], shift=IDX_DIM // 8)
    w_tail = jnp.pad(wb[:, 5 * hw:], ((0, 0), (0, LANES - IDX_DIM - IDX_HEADS)))
    tail = _proj(xb, w_tail, F32, tables["tail"], shift=IDX_DIM // 8)

    ki = tail[:, :IDX_DIM].astype(BF16).reshape(bsz, seq, IDX_DIM)
    wi_t = tail[:, IDX_DIM:IDX_DIM + IDX_HEADS].T

    a = _pool(u.reshape(bsz, seq, hw), pool_w.astype(BF16), pool_scale.reshape(1, hw))
    mask = _idx_mask(qi_t, wi_t, ki, tq, tk)
    bb = _dsa(k.reshape(bsz, seq, hw), q_t, v_t, mask, tq, tk)
    return a.reshape(n, hw), bb.reshape(n, hw)


def _gate_operands(hi, mid, lo, bsz, seq):
    one = jnp.ones_like(hi)
    zero = jnp.zeros(hi.shape + (GATE_AUG - 6,), BF16)
    k_aug = jnp.concatenate([jnp.stack([-hi, -mid, -lo, one, one, one], axis=-1), zero], axis=-1)
    q_aug = jnp.concatenate([jnp.stack([one, one, one, hi, mid, lo], axis=-1), zero], axis=-1)
    return k_aug.reshape(bsz, seq, ATT_HEADS * GATE_AUG), q_aug.transpose(0, 2, 3, 1)


FOX_TILE = 512
UNDERFLOW_LOG2 = 160.0


def _fox_first_tile(q_t, k, terms, bsz, seq, t):
    n = seq // t
    f32sum = sum(x.astype(F32) for x in terms)[:, :, :ATT_HEADS].reshape(bsz, n, t, ATT_HEADS)
    f_first, f_last = f32sum[:, :, 0], f32sum[:, :, t - 1]
    qn = jnp.sqrt(jnp.sum(jnp.square(q_t.astype(F32)).reshape(ATT_HEADS, HEAD_DIM, bsz, n, t), axis=1))
    qn = jnp.max(qn, axis=-1).transpose(1, 2, 0)
    kn = jnp.sqrt(jnp.sum(jnp.square(k.astype(F32)).reshape(bsz, n, t, ATT_HEADS, HEAD_DIM), axis=-1))
    kn = jnp.max(kn, axis=2)
    bound = (1.01 * qn[:, :, None] * (kn[:, None, :] + kn[:, :, None])
             + f_first[:, :, None] - f_last[:, None, :] + 1.0)
    tiles = jnp.arange(n, dtype=jnp.int32)
    needed = jnp.any(bound >= -UNDERFLOW_LOG2, axis=-1) | (tiles[:, None] == tiles[None, :])
    needed = needed & (tiles[None, :] <= tiles[:, None])
    return jnp.min(jnp.where(needed, tiles[None, None, :], n), axis=-1).astype(jnp.int32)


def _odd_mixer(xb, xt, w_in, conv_w, forget_b, bsz, seq):
    n = xb.shape[0]
    hw = HALF_WIDTH
    wb = w_in.astype(BF16)
    wt = wb.T
    ugg = _proj(xb, wb[:, 0:3 * hw], F32)
    q_t = _proj_t(wt[3 * hw:4 * hw], xt, BF16, scale=Q_SCALE)
    k = _proj(xb, wb[:, 4 * hw:5 * hw], BF16)
    v_t = _proj_t(wt[5 * hw:6 * hw], xt, BF16)
    w_tail = jnp.pad(wb[:, 6 * hw:], ((0, 0), (0, LANES - ATT_HEADS)))
    f = _proj(xb, w_tail, F32)
    fb = jnp.pad(forget_b, (0, LANES - ATT_HEADS)).reshape(1, LANES)
    terms = _forget_cumsum(f.reshape(bsz, seq, LANES), fb)
    k_aug, q_aug = _gate_operands(*(t[:, :, :ATT_HEADS] for t in terms), bsz, seq)

    c = _gated_conv(ugg.reshape(bsz, seq, 3 * hw), conv_w)
    t = min(FOX_TILE, seq)
    k3 = k.reshape(bsz, seq, hw)
    d = _fox(_fox_first_tile(q_t, k3, terms, bsz, seq, t), k3, k_aug, q_t, q_aug, v_t, t)
    return c.reshape(n, hw), d.reshape(n, hw)


def kernel(x, mem, positions, ev_w_in, ev_pool_w, ev_pool_scale, ev_w_out, od_w_in, od_conv_w, od_forget_b,
           od_w_out, ca_w_q, ca_w_kv, ca_w_o, ffn_w_in, ffn_w_out, ln_g, ln_b):
    bsz, seq, d = x.shape
    n = bsz * seq
    m = mem.shape[1]
    tq = tk = min(512, seq)
    xf = x.reshape(n, d)
    xb = xf.astype(BF16)
    memb = mem.reshape(bsz * m, d).astype(BF16)
    tables = _even_tables(positions)
    for i in range(DEPTH):
        j = i // 2
        xt = xb.T
        if i % 2 == 0:
            parts = _even_mixer(xb, xt, tables, ev_w_in[j], ev_pool_w[j], ev_pool_scale[j], bsz, seq, tq, tk)
            w_out = ev_w_out[j]
        else:
            parts = _odd_mixer(xb, xt, od_w_in[j], od_conv_w[j], od_forget_b[j], bsz, seq)
            w_out = od_w_out[j]
        g, b = ln_g[i].reshape(3, 1, d), ln_b[i].reshape(3, 1, d)
        xf, xb = _out_ln(list(parts), w_out.astype(BF16), xf, g[0], b[0])

        kv = _proj(memb, ca_w_kv[i].astype(BF16), BF16).reshape(bsz, m, 2 * XA_WIDTH)
        kt = kv[:, :, :XA_WIDTH].transpose(0, 2, 1)
        xf, xb = _xattn(xb, xf, ca_w_q[i].astype(BF16), kt, kv[:, :, XA_WIDTH:], ca_w_o[i].astype(BF16),
                        g[1], b[1], seq)

        xf, xb = _ffn(xb, xf, ffn_w_in[i].astype(BF16), ffn_w_out[i].astype(BF16), g[2], b[2])
    return xf.reshape(bsz, seq, d)
```

```python
import functools

import jax
import jax.numpy as jnp
from jax import lax
from jax.experimental import pallas as pl
from jax.experimental.pallas import tpu as pltpu

F32 = jnp.float32
BF16 = jnp.bfloat16

D_MODEL = 2048
DEPTH = 4
CHUNK = 64
HEAD_DIM = 128
HALF_WIDTH = D_MODEL // 2
POOL_WINDOWS = (2, 4, 8, 16)
POOL_GROUP_DIM = HALF_WIDTH // len(POOL_WINDOWS)
ATT_HEADS = HALF_WIDTH // HEAD_DIM
IDX_HEADS = 16
IDX_DIM = 64
DSA_TOPK_MAX = 256
CONV_WIDTH = 3
XA_HEADS = 4
XA_WIDTH = XA_HEADS * HEAD_DIM
FFN_HIDDEN = -(-(8 * D_MODEL) // (3 * 256)) * 256
ROPE_THETA = 500000.0
ROPE_FRACTION = 4
LN_EPS = 1e-5
ALPHA = (2 * DEPTH) ** 0.25
LOG2E = 1.4426950408889634
Q_SCALE = HEAD_DIM ** -0.5 * LOG2E

LANES = 128
SUBLANES = 8
BF16_ROWS = 16
NEG = -1e30
INT_MIN = -(2 ** 31)
VMEM_LIMIT = 60000 * 1024


def _params(*semantics):
    return pltpu.CompilerParams(dimension_semantics=semantics, vmem_limit_bytes=VMEM_LIMIT)


def _layer_norm(y, g, b):
    mu = jnp.mean(y, axis=-1, keepdims=True)
    d = y - mu
    var = jnp.mean(d * d, axis=-1, keepdims=True)
    return d * lax.rsqrt(var + LN_EPS) * g + b


def _proj_kernel(*refs, shift, scale, axis):
    if shift:
        a_ref, b_ref, c_ref, s1_ref, s2_ref, o_ref = refs
    else:
        a_ref, b_ref, o_ref = refs
    h = jnp.dot(a_ref[...], b_ref[...], preferred_element_type=F32)
    if shift:
        c, s1, s2 = c_ref[...], s1_ref[...], s2_ref[...]
        for g in range(h.shape[axis] // LANES):
            sl = (slice(None),) * axis + (slice(g * LANES, (g + 1) * LANES),)
            hg = h[sl]
            out = hg * c + pltpu.roll(hg, shift, axis) * s1 + pltpu.roll(hg, LANES - shift, axis) * s2
            o_ref[sl] = out.astype(o_ref.dtype)
    elif scale != 1.0:
        o_ref[...] = (h * scale).astype(o_ref.dtype)
    else:
        o_ref[...] = h.astype(o_ref.dtype)


def _proj(x, w, out_dtype, tables=None, shift=0, scale=1.0, tm=1024, tn=1024):
    n, k = x.shape
    c = w.shape[1]
    tm, tn = min(tm, n), min(tn, c)
    in_specs = [pl.BlockSpec((tm, k), lambda i, j: (i, 0)),
                pl.BlockSpec((k, tn), lambda i, j: (0, j))]
    args = [x, w]
    if shift:
        in_specs += [pl.BlockSpec((tm, LANES), lambda i, j: (i, 0))] * 3
        args += list(tables)
    return pl.pallas_call(
        functools.partial(_proj_kernel, shift=shift, scale=scale, axis=1),
        grid=(n // tm, c // tn),
        in_specs=in_specs,
        out_specs=pl.BlockSpec((tm, tn), lambda i, j: (i, j)),
        out_shape=jax.ShapeDtypeStruct((n, c), out_dtype),
        compiler_params=_params("parallel", "arbitrary"),
        name="proj",
    )(*args)


def _proj_t(wt, xt, out_dtype, tables=None, shift=0, scale=1.0, tm=1024, tn=1024):
    c, k = wt.shape
    n = xt.shape[1]
    tm, tn = min(tm, n), min(tn, c)
    in_specs = [pl.BlockSpec((tn, k), lambda i, j: (j, 0)),
                pl.BlockSpec((k, tm), lambda i, j: (0, i))]
    args = [wt, xt]
    if shift:
        in_specs += [pl.BlockSpec((LANES, tm), lambda i, j: (0, i))] * 3
        args += list(tables)
    return pl.pallas_call(
        functools.partial(_proj_kernel, shift=shift, scale=scale, axis=0),
        grid=(n // tm, c // tn),
        in_specs=in_specs,
        out_specs=pl.BlockSpec((tn, tm), lambda i, j: (j, i)),
        out_shape=jax.ShapeDtypeStruct((c, n), out_dtype),
        compiler_params=_params("parallel", "arbitrary"),
        name="proj_t",
    )(*args)


def _out_ln_kernel(*refs, nparts):
    parts = refs[:nparts]
    w_ref, x_ref, g_ref, b_ref, of_ref, ob_ref = refs[nparts:]
    acc = None
    off = 0
    for p in parts:
        kp = p.shape[1]
        t = jnp.dot(p[...], w_ref[off:off + kp, :], preferred_element_type=F32)
        acc = t if acc is None else acc + t
        off += kp
    y = _layer_norm(ALPHA * x_ref[...] + acc, g_ref[...], b_ref[...])
    of_ref[...] = y
    ob_ref[...] = y.astype(BF16)


def _out_ln(parts, w, x, g, b, tm=256):
    n, d = x.shape
    in_specs = [pl.BlockSpec((tm, p.shape[1]), lambda i: (i, 0)) for p in parts]
    in_specs += [pl.BlockSpec(w.shape, lambda i: (0, 0)),
                 pl.BlockSpec((tm, d), lambda i: (i, 0)),
                 pl.BlockSpec((1, d), lambda i: (0, 0)),
                 pl.BlockSpec((1, d), lambda i: (0, 0))]
    return pl.pallas_call(
        functools.partial(_out_ln_kernel, nparts=len(parts)),
        grid=(n // tm,),
        in_specs=in_specs,
        out_specs=[pl.BlockSpec((tm, d), lambda i: (i, 0)), pl.BlockSpec((tm, d), lambda i: (i, 0))],
        out_shape=[jax.ShapeDtypeStruct((n, d), F32), jax.ShapeDtypeStruct((n, d), BF16)],
        compiler_params=_params("parallel"),
        name="out_ln",
    )(*parts, w, x, g, b)


POOL_HALO = 16


def _pool_kernel(u_ref, halo_ref, w_ref, sc_ref, o_ref, ext, *, ts):
    i = pl.program_id(1)
    ext[0:POOL_HALO, :] = jnp.where(i > 0, halo_ref[0], 0.0)
    ext[POOL_HALO:, :] = u_ref[0]
    cnt = i * ts + lax.broadcasted_iota(jnp.int32, (ts, 1), 0) + 1
    for g, win in enumerate(POOL_WINDOWS):
        lo, hi = g * POOL_GROUP_DIM, (g + 1) * POOL_GROUP_DIM
        cur = ext[POOL_HALO:POOL_HALO + ts, lo:hi]
        s = cur
        for j in range(1, win):
            s = s + ext[POOL_HALO - j:POOL_HALO - j + ts, lo:hi]
        d = s / jnp.minimum(cnt, win).astype(F32) - cur
        y = jnp.dot(d.astype(BF16), w_ref[g], preferred_element_type=F32)
        o_ref[0, :, lo:hi] = (y * sc_ref[:, lo:hi]).astype(o_ref.dtype)


def _pool(u, w, scale, ts=512):
    b, s, c = u.shape
    ts = min(ts, s)
    hb = ts // POOL_HALO
    return pl.pallas_call(
        functools.partial(_pool_kernel, ts=ts),
        grid=(b, s // ts),
        in_specs=[pl.BlockSpec((1, ts, c), lambda bi, i: (bi, i, 0)),
                  pl.BlockSpec((1, POOL_HALO, c), lambda bi, i: (bi, jnp.maximum(i * hb - 1, 0), 0)),
                  pl.BlockSpec(w.shape, lambda bi, i: (0, 0, 0)),
                  pl.BlockSpec((1, c), lambda bi, i: (0, 0))],
        out_specs=pl.BlockSpec((1, ts, c), lambda bi, i: (bi, i, 0)),
        out_shape=jax.ShapeDtypeStruct((b, s, c), BF16),
        scratch_shapes=[pltpu.VMEM((ts + POOL_HALO, c), F32)],
        compiler_params=_params("parallel", "arbitrary"),
        name="pool",
    )(u, u, w, scale)


def _sortable(v):
    bits = lax.bitcast_convert_type(v, jnp.int32)
    return bits ^ ((bits >> 31) & 0x7FFFFFFF)


def _colsum8(v):
    tk, tq = v.shape
    return v.reshape(tk // SUBLANES, SUBLANES, tq).sum(axis=0)


def _idx_kernel(qi_ref, wi_ref, ki_ref, o_ref, keys, *, tq, tk, nk, topk):
    i = pl.program_id(1)
    nact = (i * tq + tq - 1) // tk + 1
    q_chunk = (i * tq + lax.broadcasted_iota(jnp.int32, (1, tq), 1)) // CHUNK

    def score_tile(c, carry):
        kt = ki_ref[0, pl.ds(pl.multiple_of(c * tk, tk), tk), :]
        acc = jnp.zeros((tk, tq), F32)
        for h in range(IDX_HEADS):
            il = jnp.dot(kt, qi_ref[h * IDX_DIM:(h + 1) * IDX_DIM, :], preferred_element_type=F32)
            acc = acc + jnp.maximum(il, 0.0) * wi_ref[h:h + 1, :]
        k_chunk = (c * tk + lax.broadcasted_iota(jnp.int32, (tk, 1), 0)) // CHUNK
        keys[c] = jnp.where(k_chunk <= q_chunk, _sortable(acc), INT_MIN)
        return carry

    lax.fori_loop(0, nact, score_tile, 0)

    def count(pred):
        def body(c, acc):
            return acc + _colsum8(jnp.where(pred(keys[c], c), 1, 0))
        acc = lax.fori_loop(0, nact, body, jnp.zeros((SUBLANES, tq), jnp.int32))
        return jnp.sum(acc, axis=0, keepdims=True)

    thr = jnp.where(count(lambda kk, c: kk >= 0) >= topk, 0, INT_MIN)

    def bit_step(b, thr):
        cand = thr | (1 << (30 - b))
        return jnp.where(count(lambda kk, c: kk >= cand) >= topk, cand, thr)

    thr = lax.fori_loop(0, 31, bit_step, thr)
    thr = jnp.maximum(thr, INT_MIN + 1)
    n_ge = count(lambda kk, c: kk >= thr)

    def write_tiles(select):
        def body(c, carry):
            o_ref[0, 0, c] = jnp.where(select(keys[c], c), 1.0, 0.0).astype(o_ref.dtype)
            return carry
        lax.fori_loop(0, nact, body, 0)

    has_ties = jnp.max(n_ge) > topk

    @pl.when(jnp.logical_not(has_ties))
    def _():
        write_tiles(lambda kk, c: kk >= thr)

    @pl.when(has_ties)
    def _():
        need = topk - count(lambda kk, c: kk > thr)

        def key_index(c):
            return c * tk + lax.broadcasted_iota(jnp.int32, (tk, 1), 0)

        def idx_step(b, cut):
            cand = cut | (1 << (30 - b))
            below = count(lambda kk, c: (kk == thr) & (key_index(c) < cand))
            return jnp.where(below < need, cand, cut)

        cut = lax.fori_loop(0, 31, idx_step, jnp.zeros((1, tq), jnp.int32))
        write_tiles(lambda kk, c: (kk > thr) | ((kk == thr) & (key_index(c) <= cut)))

    def fill_tile(c, carry):
        o_ref[0, 0, c] = jnp.zeros((tk, tq), o_ref.dtype)
        return carry

    lax.fori_loop(nact, nk, fill_tile, 0)


def _idx_mask(qi_t, wi_t, ki, tq, tk):
    b, s, _ = ki.shape
    nq, nk = s // tq, s // tk
    topk = min(DSA_TOPK_MAX, s // 4)
    return pl.pallas_call(
        functools.partial(_idx_kernel, tq=tq, tk=tk, nk=nk, topk=topk),
        grid=(b, nq),
        in_specs=[pl.BlockSpec((IDX_HEADS * IDX_DIM, tq), lambda bi, i: (0, bi * nq + i)),
                  pl.BlockSpec((IDX_HEADS, tq), lambda bi, i: (0, bi * nq + i)),
                  pl.BlockSpec((1, s, IDX_DIM), lambda bi, i: (bi, 0, 0))],
        out_specs=pl.BlockSpec((1, 1, nk, tk, tq), lambda bi, i: (bi, i, 0, 0, 0)),
        out_shape=jax.ShapeDtypeStruct((b, nq, nk, tk, tq), BF16),
        scratch_shapes=[pltpu.VMEM((nk, tk, tq), jnp.int32)],
        compiler_params=_params("parallel", "arbitrary"),
        name="idx_mask",
    )(qi_t, wi_t, ki)


AUG = 16
V_ROWS = HEAD_DIM + AUG
FAST_RANGE_LOG2 = 120.0


def _head(h):
    return slice(h * HEAD_DIM, (h + 1) * HEAD_DIM)


def _vrows(h):
    return slice(h * V_ROWS, (h + 1) * V_ROWS)


def _att_init(qt_ref, qa_ref, m_sc, acc_sc, qfull):
    m_sc[...] = jnp.full(m_sc.shape, NEG, F32)
    acc_sc[...] = jnp.zeros(acc_sc.shape, F32)
    tq = qfull.shape[2]
    for h in range(ATT_HEADS):
        qfull[h, 0:HEAD_DIM, :] = qt_ref[_head(h), :]
        qfull[h, HEAD_DIM:, :] = jnp.zeros((LANES, tq), BF16)
        lo = HEAD_DIM + h * AUG
        qfull[h, lo:lo + AUG, :] = qa_ref[0, h]


def _scores(k_ref, ka, qfull, h):
    kfull = jnp.concatenate([k_ref[0, :, _head(h)], ka], axis=1)
    return jnp.dot(kfull, qfull[h], preferred_element_type=F32)


def _fast_step(s, keep, vt, h, acc_sc):
    p = jnp.exp2(s).astype(BF16)
    if keep is not None:
        p = p * keep
    acc_sc[h] += jnp.dot(vt, p, preferred_element_type=F32)


def _exact_step(s, vt, h, m_sc, acc_sc):
    m_prev = m_sc[h]
    m_new = jnp.maximum(m_prev, jnp.max(s, axis=0, keepdims=True))
    alpha = jnp.exp2(m_prev - m_new)
    p = jnp.exp2(s - m_new)
    acc_sc[h] = alpha * acc_sc[h] + jnp.dot(vt, p.astype(BF16), preferred_element_type=F32)
    m_sc[h] = m_new


def _att_finish(o_ref, acc_sc):
    for h in range(ATT_HEADS):
        a = acc_sc[h]
        o = (a[0:HEAD_DIM] / a[HEAD_DIM:HEAD_DIM + 1]).T
        o_ref[0, :, _head(h)] = o.astype(o_ref.dtype)


def _att_scratch(tq):
    return [pltpu.VMEM((ATT_HEADS, 1, tq), F32),
            pltpu.VMEM((ATT_HEADS, V_ROWS, tq), F32),
            pltpu.VMEM((ATT_HEADS, 2 * HEAD_DIM, tq), BF16)]


def _dsa_kernel(fast_ref, k_ref, ka_ref, qt_ref, qa_ref, vt_ref, mask_ref, o_ref, m_sc, acc_sc, qfull, *, tq, tk):
    b, i, j = pl.program_id(0), pl.program_id(1), pl.program_id(2)
    last = (i * tq + tq - 1) // tk
    fast = fast_ref[b, i] == 1

    @pl.when(j == 0)
    def _():
        _att_init(qt_ref, qa_ref, m_sc, acc_sc, qfull)

    @pl.when((j <= last) & fast)
    def _():
        keep = mask_ref[0, 0, 0]
        ka = ka_ref[0]
        for h in range(ATT_HEADS):
            _fast_step(_scores(k_ref, ka, qfull, h), keep, vt_ref[_vrows(h), :], h, acc_sc)

    @pl.when((j <= last) & jnp.logical_not(fast))
    def _():
        bias = (mask_ref[0, 0, 0].astype(F32) - 1.0) * (-NEG)
        ka = ka_ref[0]
        for h in range(ATT_HEADS):
            _exact_step(_scores(k_ref, ka, qfull, h) + bias, vt_ref[_vrows(h), :], h, m_sc, acc_sc)

    @pl.when(j == last)
    def _():
        _att_finish(o_ref, acc_sc)


def _dsa(fast, k, k_aug, q_t, q_aug, v_t, mask, tq, tk):
    b, s, c = k.shape
    nq, nk = s // tq, s // tk

    def key_tile(i, j):
        return jnp.minimum(j, (i * tq + tq - 1) // tk)

    grid_spec = pltpu.PrefetchScalarGridSpec(
        num_scalar_prefetch=1,
        grid=(b, nq, nk),
        in_specs=[pl.BlockSpec((1, tk, c), lambda bi, i, j, f: (bi, key_tile(i, j), 0)),
                  pl.BlockSpec((1, tk, LANES), lambda bi, i, j, f: (bi, key_tile(i, j), 0)),
                  pl.BlockSpec((c, tq), lambda bi, i, j, f: (0, bi * nq + i)),
                  pl.BlockSpec((1, ATT_HEADS, AUG, tq), lambda bi, i, j, f: (bi, 0, 0, i)),
                  pl.BlockSpec((ATT_HEADS * V_ROWS, tk), lambda bi, i, j, f: (0, bi * nk + key_tile(i, j))),
                  pl.BlockSpec((1, 1, 1, tk, tq), lambda bi, i, j, f: (bi, i, key_tile(i, j), 0, 0))],
        out_specs=pl.BlockSpec((1, tq, c), lambda bi, i, j, f: (bi, i, 0)),
        scratch_shapes=_att_scratch(tq))
    return pl.pallas_call(
        functools.partial(_dsa_kernel, tq=tq, tk=tk),
        grid_spec=grid_spec,
        out_shape=jax.ShapeDtypeStruct((b, s, c), BF16),
        compiler_params=_params("parallel", "parallel", "arbitrary"),
        name="dsa",
    )(fast, k, k_aug, q_t, q_aug, v_t, mask)


CONV_HALO = 8


def _conv_kernel(u_ref, gb_ref, gc_ref, hu_ref, hgc_ref, w_ref, o_ref, ext, *, ts):
    i = pl.program_id(1)
    ext[0:CONV_HALO, :] = jnp.where(i > 0, hgc_ref[0] * hu_ref[0], 0.0)
    ext[CONV_HALO:, :] = gc_ref[0] * u_ref[0]
    conv = None
    for t in range(CONV_WIDTH):
        off = CONV_HALO - (CONV_WIDTH - 1) + t
        term = ext[off:off + ts, :] * w_ref[t:t + 1, :]
        conv = term if conv is None else conv + term
    o_ref[0] = (gb_ref[0] * conv).astype(o_ref.dtype)


def _gated_conv(h, w, ts=512):
    b, s, c3 = h.shape
    c = c3 // 3
    ts = min(ts, s)
    hb = ts // CONV_HALO

    def halo_map(col):
        return lambda bi, i: (bi, jnp.maximum(i * hb - 1, 0), col)

    return pl.pallas_call(
        functools.partial(_conv_kernel, ts=ts),
        grid=(b, s // ts),
        in_specs=[pl.BlockSpec((1, ts, c), lambda bi, i: (bi, i, 0)),
                  pl.BlockSpec((1, ts, c), lambda bi, i: (bi, i, 1)),
                  pl.BlockSpec((1, ts, c), lambda bi, i: (bi, i, 2)),
                  pl.BlockSpec((1, CONV_HALO, c), halo_map(0)),
                  pl.BlockSpec((1, CONV_HALO, c), halo_map(2)),
                  pl.BlockSpec(w.shape, lambda bi, i: (0, 0))],
        out_specs=pl.BlockSpec((1, ts, c), lambda bi, i: (bi, i, 0)),
        out_shape=jax.ShapeDtypeStruct((b, s, c), BF16),
        scratch_shapes=[pltpu.VMEM((ts + CONV_HALO, c), F32)],
        compiler_params=_params("parallel", "arbitrary"),
        name="gated_conv",
    )(h, h, h, h, h, w)


GATE_CHUNK = 256


def _fgate_kernel(f_ref, fb_ref, hi_ref, mid_ref, lo_ref):
    s = f_ref.shape[1]
    r = lax.broadcasted_iota(jnp.int32, (GATE_CHUNK, GATE_CHUNK), 0)
    c = lax.broadcasted_iota(jnp.int32, (GATE_CHUNK, GATE_CHUNK), 1)
    tri = (c <= r).astype(F32)

    def body(t, carry):
        rows = pl.ds(t * GATE_CHUNK, GATE_CHUNK)
        z = f_ref[0, rows, :] + fb_ref[...]
        log_f = -(jnp.maximum(-z, 0.0) + jnp.log1p(jnp.exp(-jnp.abs(z))))
        cs = jnp.dot(tri, log_f, preferred_element_type=F32, precision=lax.Precision.HIGHEST) + carry
        b2 = cs * LOG2E
        hi = b2.astype(BF16)
        r1 = b2 - hi.astype(F32)
        mid = r1.astype(BF16)
        hi_ref[0, rows, :] = hi
        mid_ref[0, rows, :] = mid
        lo_ref[0, rows, :] = (r1 - mid.astype(F32)).astype(BF16)
        return cs[GATE_CHUNK - 1:GATE_CHUNK, :]

    lax.fori_loop(0, s // GATE_CHUNK, body, jnp.zeros((1, LANES), F32))


def _forget_cumsum(f, fb):
    b, s, c = f.shape
    spec = pl.BlockSpec((1, s, c), lambda bi: (bi, 0, 0))
    return pl.pallas_call(
        _fgate_kernel,
        grid=(b,),
        in_specs=[spec, pl.BlockSpec((1, c), lambda bi: (0, 0))],
        out_specs=[spec, spec, spec],
        out_shape=[jax.ShapeDtypeStruct((b, s, c), BF16)] * 3,
        compiler_params=_params("parallel"),
        name="forget_cumsum",
    )(f, fb)


def _fox_kernel(first_ref, fast_ref, k_ref, ka_ref, qt_ref, qa_ref, vt_ref, o_ref, m_sc, acc_sc, qfull, *, t):
    b, i, j = pl.program_id(0), pl.program_id(1), pl.program_id(2)
    steps = i - first_ref[b, i]
    fast = fast_ref[b, i] == 1

    def causal():
        return lax.broadcasted_iota(jnp.int32, (t, t), 0) <= lax.broadcasted_iota(jnp.int32, (t, t), 1)

    def fast_tile(diagonal):
        ka = ka_ref[0]
        for h in range(ATT_HEADS):
            s = _scores(k_ref, ka, qfull, h)
            if diagonal:
                s = jnp.where(causal(), s, NEG)
            _fast_step(s, None, vt_ref[_vrows(h), :], h, acc_sc)

    def exact_tile(diagonal):
        ka = ka_ref[0]
        for h in range(ATT_HEADS):
            s = _scores(k_ref, ka, qfull, h)
            if diagonal:
                s = jnp.where(causal(), s, NEG)
            _exact_step(s, vt_ref[_vrows(h), :], h, m_sc, acc_sc)

    @pl.when(j == 0)
    def _():
        _att_init(qt_ref, qa_ref, m_sc, acc_sc, qfull)

    @pl.when((j == 0) & fast)
    def _():
        fast_tile(True)

    @pl.when((j == 0) & jnp.logical_not(fast))
    def _():
        exact_tile(True)

    @pl.when((j > 0) & (j <= steps) & fast)
    def _():
        fast_tile(False)

    @pl.when((j > 0) & (j <= steps) & jnp.logical_not(fast))
    def _():
        exact_tile(False)

    @pl.when(j == steps)
    def _():
        _att_finish(o_ref, acc_sc)


def _fox(first, fast, k, k_aug, q_t, q_aug, v_t, t):
    b, s, c = k.shape
    n = s // t

    def key_tile(bi, i, j, first_ref):
        return jnp.maximum(i - j, first_ref[bi, i])

    grid_spec = pltpu.PrefetchScalarGridSpec(
        num_scalar_prefetch=2,
        grid=(b, n, n),
        in_specs=[pl.BlockSpec((1, t, c), lambda bi, i, j, f, g: (bi, key_tile(bi, i, j, f), 0)),
                  pl.BlockSpec((1, t, LANES), lambda bi, i, j, f, g: (bi, key_tile(bi, i, j, f), 0)),
                  pl.BlockSpec((c, t), lambda bi, i, j, f, g: (0, bi * n + i)),
                  pl.BlockSpec((1, ATT_HEADS, AUG, t), lambda bi, i, j, f, g: (bi, 0, 0, i)),
                  pl.BlockSpec((ATT_HEADS * V_ROWS, t), lambda bi, i, j, f, g: (0, bi * n + key_tile(bi, i, j, f)))],
        out_specs=pl.BlockSpec((1, t, c), lambda bi, i, j, f, g: (bi, i, 0)),
        scratch_shapes=_att_scratch(t))
    return pl.pallas_call(
        functools.partial(_fox_kernel, t=t),
        grid_spec=grid_spec,
        out_shape=jax.ShapeDtypeStruct((b, s, c), BF16),
        compiler_params=_params("parallel", "parallel", "arbitrary"),
        name="fox",
    )(first, fast, k, k_aug, q_t, q_aug, v_t)


def _xattn_kernel(xb_ref, xf_ref, wq_ref, kt_ref, v_ref, wo_ref, g_ref, b_ref, of_ref, ob_ref):
    q = jnp.dot(xb_ref[...], wq_ref[...], preferred_element_type=F32) * Q_SCALE
    q = q.astype(BF16)
    outs = []
    for h in range(XA_HEADS):
        s = jnp.dot(q[:, _head(h)], kt_ref[0, _head(h), :], preferred_element_type=F32)
        p = jnp.exp2(s - jnp.max(s, axis=1, keepdims=True))
        l = jnp.sum(p, axis=1, keepdims=True)
        pv = jnp.dot(p.astype(BF16), v_ref[0, :, _head(h)], preferred_element_type=F32)
        outs.append((pv / l).astype(BF16))
    o = jnp.concatenate(outs, axis=1)
    y = jnp.dot(o, wo_ref[...], preferred_element_type=F32)
    y = _layer_norm(ALPHA * xf_ref[...] + y, g_ref[...], b_ref[...])
    of_ref[...] = y
    ob_ref[...] = y.astype(BF16)


def _xattn(xb, xf, wq, kt, v, wo, g, b, seq, tm=256):
    n, d = xf.shape
    per_batch = seq // tm
    m = v.shape[1]
    return pl.pallas_call(
        _xattn_kernel,
        grid=(n // tm,),
        in_specs=[pl.BlockSpec((tm, d), lambda i: (i, 0)),
                  pl.BlockSpec((tm, d), lambda i: (i, 0)),
                  pl.BlockSpec(wq.shape, lambda i: (0, 0)),
                  pl.BlockSpec((1, XA_WIDTH, m), lambda i: (i // per_batch, 0, 0)),
                  pl.BlockSpec((1, m, XA_WIDTH), lambda i: (i // per_batch, 0, 0)),
                  pl.BlockSpec(wo.shape, lambda i: (0, 0)),
                  pl.BlockSpec((1, d), lambda i: (0, 0)),
                  pl.BlockSpec((1, d), lambda i: (0, 0))],
        out_specs=[pl.BlockSpec((tm, d), lambda i: (i, 0)), pl.BlockSpec((tm, d), lambda i: (i, 0))],
        out_shape=[jax.ShapeDtypeStruct((n, d), F32), jax.ShapeDtypeStruct((n, d), BF16)],
        compiler_params=_params("parallel"),
        name="xattn",
    )(xb, xf, wq, kt, v, wo, g, b)


def _ffn_kernel(xb_ref, wg_ref, wu_ref, wo_ref, xf_ref, g_ref, b_ref, of_ref, ob_ref, acc):
    j = pl.program_id(1)

    @pl.when(j == 0)
    def _():
        acc[...] = jnp.zeros(acc.shape, F32)

    xb = xb_ref[...]
    gate = jnp.dot(xb, wg_ref[...], preferred_element_type=F32)
    up = jnp.dot(xb, wu_ref[...], preferred_element_type=F32)
    hid = gate * (1.0 / (1.0 + jnp.exp(-gate))) * up
    acc[...] += jnp.dot(hid.astype(BF16), wo_ref[...], preferred_element_type=F32)

    @pl.when(j == pl.num_programs(1) - 1)
    def _():
        y = _layer_norm(ALPHA * xf_ref[...] + acc[...], g_ref[...], b_ref[...])
        of_ref[...] = y
        ob_ref[...] = y.astype(BF16)


def _ffn(xb, xf, w_in, w_out, g, b, tm=512, th=512):
    n, d = xf.shape
    hidden = w_out.shape[0]
    nh = hidden // th
    return pl.pallas_call(
        _ffn_kernel,
        grid=(n // tm, nh),
        in_specs=[pl.BlockSpec((tm, d), lambda i, j: (i, 0)),
                  pl.BlockSpec((d, th), lambda i, j: (0, j)),
                  pl.BlockSpec((d, th), lambda i, j: (0, j + nh)),
                  pl.BlockSpec((th, d), lambda i, j: (j, 0)),
                  pl.BlockSpec((tm, d), lambda i, j: (i, 0)),
                  pl.BlockSpec((1, d), lambda i, j: (0, 0)),
                  pl.BlockSpec((1, d), lambda i, j: (0, 0))],
        out_specs=[pl.BlockSpec((tm, d), lambda i, j: (i, 0)),
                   pl.BlockSpec((tm, d), lambda i, j: (i, 0))],
        out_shape=[jax.ShapeDtypeStruct((n, d), F32), jax.ShapeDtypeStruct((n, d), BF16)],
        scratch_shapes=[pltpu.VMEM((tm, d), F32)],
        compiler_params=_params("parallel", "arbitrary"),
        name="ffn",
    )(xb, w_in, w_in, w_out, xf, g, b)


def _rotary_tables(positions, dh):
    rot = dh // ROPE_FRACTION
    half = rot // 2
    inv_freq = jnp.power(ROPE_THETA, -(jnp.arange(half, dtype=F32) * 2.0 / rot))
    ang = positions.astype(F32)[..., None] * inv_freq
    cos, sin = jnp.cos(ang), jnp.sin(ang)
    zh = jnp.zeros_like(sin)
    rest = jnp.zeros(ang.shape[:-1] + (dh - rot,), F32)
    c = jnp.concatenate([cos, cos, rest + 1.0], axis=-1)
    s_lo = jnp.concatenate([zh, sin, rest], axis=-1)
    s_hi = jnp.concatenate([-sin, zh, rest], axis=-1)
    return tuple(t.reshape(-1, dh) for t in (c, s_lo, s_hi))


def _even_tables(positions):
    n = positions.size
    head = _rotary_tables(positions, HEAD_DIM)
    idx = _rotary_tables(positions, IDX_DIM)
    idx_full = tuple(jnp.tile(t, (1, LANES // IDX_DIM)) for t in idx)
    wi_scale = jnp.concatenate([jnp.full((IDX_HEADS,), IDX_HEADS ** -0.5, F32),
                                jnp.ones((LANES - IDX_DIM - IDX_HEADS,), F32)])
    pad = jnp.zeros((n, LANES - IDX_DIM), F32)
    tail = (jnp.concatenate([idx[0], pad + wi_scale], axis=1),
            jnp.concatenate([idx[1], pad], axis=1),
            jnp.concatenate([idx[2], pad], axis=1))
    return {"k": head,
            "q_t": tuple((t * Q_SCALE).T for t in head),
            "qi_t": tuple((t * IDX_DIM ** -0.5).T for t in idx_full),
            "tail": tail}


def _att_bounds(q_t, k, bsz, seq, t):
    n = seq // t
    qn = jnp.sqrt(jnp.sum(jnp.square(q_t.astype(F32)).reshape(ATT_HEADS, HEAD_DIM, bsz, seq), axis=1))
    qn = qn.transpose(1, 0, 2)
    kn = jnp.sqrt(jnp.sum(jnp.square(k.astype(F32)).reshape(bsz, seq, ATT_HEADS, HEAD_DIM), axis=-1))
    k_max = jnp.max(kn, axis=1)
    neg_m = -(1.01 * qn * k_max[:, :, None] + 1.0)
    qn_tile = jnp.max(qn.reshape(bsz, ATT_HEADS, n, t), axis=-1).transpose(0, 2, 1)
    kn_tile = jnp.max(kn.reshape(bsz, n, t, ATT_HEADS), axis=2)
    spread = jnp.max(2.05 * qn_tile * k_max[:, None, :], axis=-1) + 8.0
    fast = (spread <= FAST_RANGE_LOG2).astype(jnp.int32)
    return neg_m.astype(BF16), qn_tile, kn_tile, fast


def _aug_operands(q_entries, k_entries, bsz, seq):
    shape = (bsz, seq, ATT_HEADS)
    one, zero = jnp.ones(shape, BF16), jnp.zeros(shape, BF16)

    def pack(entries):
        cols = [one if e is None else e for e in entries]
        return jnp.stack(cols + [zero] * (AUG - len(cols)), axis=-1)

    return pack(k_entries).reshape(bsz, seq, ATT_HEADS * AUG), pack(q_entries).transpose(0, 2, 3, 1)


def _value_rows(v_t):
    n = v_t.shape[1]
    v3 = v_t.reshape(ATT_HEADS, HEAD_DIM, n)
    extra = jnp.zeros((ATT_HEADS, AUG, n), BF16).at[:, 0, :].set(1.0)
    return jnp.concatenate([v3, extra], axis=1).reshape(ATT_HEADS * V_ROWS, n)


def _even_mixer(xb, xt, tables, w_in, pool_w, pool_scale, bsz, seq, tq, tk):
    n = xb.shape[0]
    hw = HALF_WIDTH
    wb = w_in.astype(BF16)
    wt = wb.T
    u = _proj(xb, wb[:, 0:hw], F32)
    q_t = _proj_t(wt[hw:2 * hw], xt, BF16, tables["q_t"], shift=HEAD_DIM // 8)
    k = _proj(xb, wb[:, 2 * hw:3 * hw], BF16, tables["k"], shift=HEAD_DIM // 8)
    v_t = _proj_t(wt[3 * hw:4 * hw], xt, BF16)
    qi_t = _proj_t(wt[4 * hw:5 * hw], xt, BF16, tables["qi_t"], shift=IDX_DIM // 8)
    w_tail = jnp.pad(wb[:, 5 * hw:], ((0, 0), (0, LANES - IDX_DIM - IDX_HEADS)))
    tail = _proj(xb, w_tail, F32, tables["tail"], shift=IDX_DIM // 8)

    ki = tail[:, :IDX_DIM].astype(BF16).reshape(bsz, seq, IDX_DIM)
    wi_t = tail[:, IDX_DIM:IDX_DIM + IDX_HEADS].T

    a = _pool(u.reshape(bsz, seq, hw), pool_w.astype(BF16), pool_scale.reshape(1, hw))
    mask = _idx_mask(qi_t, wi_t, ki, tq, tk)
    k3 = k.reshape(bsz, seq, hw)
    neg_m, _, _, fast = _att_bounds(q_t, k3, bsz, seq, tq)
    k_aug, q_aug = _aug_operands([neg_m.transpose(0, 2, 1)], [None], bsz, seq)
    bb = _dsa(fast, k3, k_aug, q_t, q_aug, _value_rows(v_t), mask, tq, tk)
    return a.reshape(n, hw), bb.reshape(n, hw)


FOX_TILE = 512
UNDERFLOW_LOG2 = 160.0


def _fox_first_tile(qn, kn, terms, bsz, seq, t):
    n = seq // t
    f32sum = sum(x.astype(F32) for x in terms)[:, :, :ATT_HEADS].reshape(bsz, n, t, ATT_HEADS)
    f_first, f_last = f32sum[:, :, 0], f32sum[:, :, t - 1]
    bound = (1.01 * qn[:, :, None] * (kn[:, None, :] + kn[:, :, None])
             + f_first[:, :, None] - f_last[:, None, :] + 1.0)
    tiles = jnp.arange(n, dtype=jnp.int32)
    needed = jnp.any(bound >= -UNDERFLOW_LOG2, axis=-1) | (tiles[:, None] == tiles[None, :])
    needed = needed & (tiles[None, :] <= tiles[:, None])
    return jnp.min(jnp.where(needed, tiles[None, None, :], n), axis=-1).astype(jnp.int32)


def _odd_mixer(xb, xt, w_in, conv_w, forget_b, bsz, seq):
    n = xb.shape[0]
    hw = HALF_WIDTH
    wb = w_in.astype(BF16)
    wt = wb.T
    ugg = _proj(xb, wb[:, 0:3 * hw], F32)
    q_t = _proj_t(wt[3 * hw:4 * hw], xt, BF16, scale=Q_SCALE)
    k = _proj(xb, wb[:, 4 * hw:5 * hw], BF16)
    v_t = _proj_t(wt[5 * hw:6 * hw], xt, BF16)
    w_tail = jnp.pad(wb[:, 6 * hw:], ((0, 0), (0, LANES - ATT_HEADS)))
    f = _proj(xb, w_tail, F32)
    fb = jnp.pad(forget_b, (0, LANES - ATT_HEADS)).reshape(1, LANES)
    terms = _forget_cumsum(f.reshape(bsz, seq, LANES), fb)
    hi, mid, lo = (x[:, :, :ATT_HEADS] for x in terms)

    c = _gated_conv(ugg.reshape(bsz, seq, 3 * hw), conv_w)
    t = min(FOX_TILE, seq)
    k3 = k.reshape(bsz, seq, hw)
    neg_m, qn, kn, fast = _att_bounds(q_t, k3, bsz, seq, t)
    k_aug, q_aug = _aug_operands([neg_m.transpose(0, 2, 1), None, None, None, hi, mid, lo],
                                 [None, -hi, -mid, -lo, None, None, None], bsz, seq)
    first = _fox_first_tile(qn, kn, terms, bsz, seq, t)
    d = _fox(first, fast, k3, k_aug, q_t, q_aug, _value_rows(v_t), t)
    return c.reshape(n, hw), d.reshape(n, hw)


def kernel(x, mem, positions, ev_w_in, ev_pool_w, ev_pool_scale, ev_w_out, od_w_in, od_conv_w, od_forget_b,
           od_w_out, ca_w_q, ca_w_kv, ca_w_o, ffn_w_in, ffn_w_out, ln_g, ln_b):
    bsz, seq, d = x.shape
    n = bsz * seq
    m = mem.shape[1]
    tq = tk = min(512, seq)
    xf = x.reshape(n, d)
    xb = xf.astype(BF16)
    memb = mem.reshape(bsz * m, d).astype(BF16)
    tables = _even_tables(positions)
    for i in range(DEPTH):
        j = i // 2
        xt = xb.T
        if i % 2 == 0:
            parts = _even_mixer(xb, xt, tables, ev_w_in[j], ev_pool_w[j], ev_pool_scale[j], bsz, seq, tq, tk)
            w_out = ev_w_out[j]
        else:
            parts = _odd_mixer(xb, xt, od_w_in[j], od_conv_w[j], od_forget_b[j], bsz, seq)
            w_out = od_w_out[j]
        g, b = ln_g[i].reshape(3, 1, d), ln_b[i].reshape(3, 1, d)
        xf, xb = _out_ln(list(parts), w_out.astype(BF16), xf, g[0], b[0])

        kv = _proj(memb, ca_w_kv[i].astype(BF16), BF16).reshape(bsz, m, 2 * XA_WIDTH)
        kt = kv[:, :, :XA_WIDTH].transpose(0, 2, 1)
        xf, xb = _xattn(xb, xf, ca_w_q[i].astype(BF16), kt, kv[:, :, XA_WIDTH:], ca_w_o[i].astype(BF16),
                        g[1], b[1], seq)

        xf, xb = _ffn(xb, xf, ffn_w_in[i].astype(BF16), ffn_w_out[i].astype(BF16), g[2], b[2])
    return xf.reshape(bsz, seq, d)
```

```python
import functools

import jax
import jax.numpy as jnp
from jax import lax
from jax.experimental import pallas as pl
from jax.experimental.pallas import tpu as pltpu

F32 = jnp.float32
BF16 = jnp.bfloat16

D_MODEL = 2048
DEPTH = 4
CHUNK = 64
HEAD_DIM = 128
HALF_WIDTH = D_MODEL // 2
POOL_WINDOWS = (2, 4, 8, 16)
POOL_GROUP_DIM = HALF_WIDTH // len(POOL_WINDOWS)
ATT_HEADS = HALF_WIDTH // HEAD_DIM
IDX_HEADS = 16
IDX_DIM = 64
DSA_TOPK_MAX = 256
CONV_WIDTH = 3
XA_HEADS = 4
XA_WIDTH = XA_HEADS * HEAD_DIM
FFN_HIDDEN = -(-(8 * D_MODEL) // (3 * 256)) * 256
ROPE_THETA = 500000.0
ROPE_FRACTION = 4
LN_EPS = 1e-5
ALPHA = (2 * DEPTH) ** 0.25
LOG2E = 1.4426950408889634
Q_SCALE = HEAD_DIM ** -0.5 * LOG2E

LANES = 128
SUBLANES = 8
BF16_ROWS = 16
NEG = -1e30
INT_MIN = -(2 ** 31)
VMEM_LIMIT = 60000 * 1024


def _params(*semantics):
    return pltpu.CompilerParams(dimension_semantics=semantics, vmem_limit_bytes=VMEM_LIMIT)


def _layer_norm(y, g, b):
    mu = jnp.mean(y, axis=-1, keepdims=True)
    d = y - mu
    var = jnp.mean(d * d, axis=-1, keepdims=True)
    return d * lax.rsqrt(var + LN_EPS) * g + b


def _proj_kernel(*refs, shift, scale, axis):
    if shift:
        a_ref, b_ref, c_ref, s1_ref, s2_ref, o_ref = refs
    else:
        a_ref, b_ref, o_ref = refs
    h = jnp.dot(a_ref[...], b_ref[...], preferred_element_type=F32)
    if shift:
        c, s1, s2 = c_ref[...], s1_ref[...], s2_ref[...]
        for g in range(h.shape[axis] // LANES):
            sl = (slice(None),) * axis + (slice(g * LANES, (g + 1) * LANES),)
            hg = h[sl]
            out = hg * c + pltpu.roll(hg, shift, axis) * s1 + pltpu.roll(hg, LANES - shift, axis) * s2
            o_ref[sl] = out.astype(o_ref.dtype)
    elif scale != 1.0:
        o_ref[...] = (h * scale).astype(o_ref.dtype)
    else:
        o_ref[...] = h.astype(o_ref.dtype)


def _proj(x, w, out_dtype, tables=None, shift=0, scale=1.0, tm=1024, tn=1024):
    n, k = x.shape
    c = w.shape[1]
    tm, tn = min(tm, n), min(tn, c)
    in_specs = [pl.BlockSpec((tm, k), lambda i, j: (i, 0)),
                pl.BlockSpec((k, tn), lambda i, j: (0, j))]
    args = [x, w]
    if shift:
        in_specs += [pl.BlockSpec((tm, LANES), lambda i, j: (i, 0))] * 3
        args += list(tables)
    return pl.pallas_call(
        functools.partial(_proj_kernel, shift=shift, scale=scale, axis=1),
        grid=(n // tm, c // tn),
        in_specs=in_specs,
        out_specs=pl.BlockSpec((tm, tn), lambda i, j: (i, j)),
        out_shape=jax.ShapeDtypeStruct((n, c), out_dtype),
        compiler_params=_params("parallel", "arbitrary"),
        name="proj",
    )(*args)


def _proj_t(wt, xt, out_dtype, tables=None, shift=0, scale=1.0, tm=1024, tn=1024):
    c, k = wt.shape
    n = xt.shape[1]
    tm, tn = min(tm, n), min(tn, c)
    in_specs = [pl.BlockSpec((tn, k), lambda i, j: (j, 0)),
                pl.BlockSpec((k, tm), lambda i, j: (0, i))]
    args = [wt, xt]
    if shift:
        in_specs += [pl.BlockSpec((LANES, tm), lambda i, j: (0, i))] * 3
        args += list(tables)
    return pl.pallas_call(
        functools.partial(_proj_kernel, shift=shift, scale=scale, axis=0),
        grid=(n // tm, c // tn),
        in_specs=in_specs,
        out_specs=pl.BlockSpec((tn, tm), lambda i, j: (j, i)),
        out_shape=jax.ShapeDtypeStruct((c, n), out_dtype),
        compiler_params=_params("parallel", "arbitrary"),
        name="proj_t",
    )(*args)


def _out_ln_kernel(*refs, nparts):
    parts = refs[:nparts]
    w_ref, x_ref, g_ref, b_ref, of_ref, ob_ref = refs[nparts:]
    acc = None
    off = 0
    for p in parts:
        kp = p.shape[1]
        t = jnp.dot(p[...], w_ref[off:off + kp, :], preferred_element_type=F32)
        acc = t if acc is None else acc + t
        off += kp
    y = _layer_norm(ALPHA * x_ref[...] + acc, g_ref[...], b_ref[...])
    of_ref[...] = y
    ob_ref[...] = y.astype(BF16)


def _out_ln(parts, w, x, g, b, tm=256):
    n, d = x.shape
    in_specs = [pl.BlockSpec((tm, p.shape[1]), lambda i: (i, 0)) for p in parts]
    in_specs += [pl.BlockSpec(w.shape, lambda i: (0, 0)),
                 pl.BlockSpec((tm, d), lambda i: (i, 0)),
                 pl.BlockSpec((1, d), lambda i: (0, 0)),
                 pl.BlockSpec((1, d), lambda i: (0, 0))]
    return pl.pallas_call(
        functools.partial(_out_ln_kernel, nparts=len(parts)),
        grid=(n // tm,),
        in_specs=in_specs,
        out_specs=[pl.BlockSpec((tm, d), lambda i: (i, 0)), pl.BlockSpec((tm, d), lambda i: (i, 0))],
        out_shape=[jax.ShapeDtypeStruct((n, d), F32), jax.ShapeDtypeStruct((n, d), BF16)],
        compiler_params=_params("parallel"),
        name="out_ln",
    )(*parts, w, x, g, b)


POOL_HALO = 16


def _pool_kernel(u_ref, halo_ref, w_ref, sc_ref, o_ref, ext, *, ts):
    i = pl.program_id(1)
    ext[0:POOL_HALO, :] = jnp.where(i > 0, halo_ref[0], 0.0)
    ext[POOL_HALO:, :] = u_ref[0]
    cnt = i * ts + lax.broadcasted_iota(jnp.int32, (ts, 1), 0) + 1
    for g, win in enumerate(POOL_WINDOWS):
        lo, hi = g * POOL_GROUP_DIM, (g + 1) * POOL_GROUP_DIM
        cur = ext[POOL_HALO:POOL_HALO + ts, lo:hi]
        s = cur
        for j in range(1, win):
            s = s + ext[POOL_HALO - j:POOL_HALO - j + ts, lo:hi]
        d = s / jnp.minimum(cnt, win).astype(F32) - cur
        y = jnp.dot(d.astype(BF16), w_ref[g], preferred_element_type=F32)
        o_ref[0, :, lo:hi] = (y * sc_ref[:, lo:hi]).astype(o_ref.dtype)


def _pool(u, w, scale, ts=512):
    b, s, c = u.shape
    ts = min(ts, s)
    hb = ts // POOL_HALO
    return pl.pallas_call(
        functools.partial(_pool_kernel, ts=ts),
        grid=(b, s // ts),
        in_specs=[pl.BlockSpec((1, ts, c), lambda bi, i: (bi, i, 0)),
                  pl.BlockSpec((1, POOL_HALO, c), lambda bi, i: (bi, jnp.maximum(i * hb - 1, 0), 0)),
                  pl.BlockSpec(w.shape, lambda bi, i: (0, 0, 0)),
                  pl.BlockSpec((1, c), lambda bi, i: (0, 0))],
        out_specs=pl.BlockSpec((1, ts, c), lambda bi, i: (bi, i, 0)),
        out_shape=jax.ShapeDtypeStruct((b, s, c), BF16),
        scratch_shapes=[pltpu.VMEM((ts + POOL_HALO, c), F32)],
        compiler_params=_params("parallel", "arbitrary"),
        name="pool",
    )(u, u, w, scale)


def _sortable(v):
    bits = lax.bitcast_convert_type(v, jnp.int32)
    return bits ^ ((bits >> 31) & 0x7FFFFFFF)


def _colsum8(v):
    tk, tq = v.shape
    return v.reshape(tk // SUBLANES, SUBLANES, tq).sum(axis=0)


def _idx_kernel(qi_ref, wi_ref, ki_ref, o_ref, keys, *, tq, tk, nk, topk):
    i = pl.program_id(1)
    nact = (i * tq + tq - 1) // tk + 1
    q_chunk = (i * tq + lax.broadcasted_iota(jnp.int32, (1, tq), 1)) // CHUNK

    def score_tile(c, carry):
        kt = ki_ref[0, pl.ds(pl.multiple_of(c * tk, tk), tk), :]
        acc = jnp.zeros((tk, tq), F32)
        for h in range(IDX_HEADS):
            il = jnp.dot(kt, qi_ref[h * IDX_DIM:(h + 1) * IDX_DIM, :], preferred_element_type=F32)
            acc = acc + jnp.maximum(il, 0.0) * wi_ref[h:h + 1, :]
        k_chunk = (c * tk + lax.broadcasted_iota(jnp.int32, (tk, 1), 0)) // CHUNK
        keys[c] = jnp.where(k_chunk <= q_chunk, _sortable(acc), INT_MIN)
        return carry

    lax.fori_loop(0, nact, score_tile, 0)

    def count(pred):
        def body(c, acc):
            return acc + _colsum8(jnp.where(pred(keys[c], c), 1, 0))
        acc = lax.fori_loop(0, nact, body, jnp.zeros((SUBLANES, tq), jnp.int32))
        return jnp.sum(acc, axis=0, keepdims=True)

    thr = jnp.where(count(lambda kk, c: kk >= 0) >= topk, 0, INT_MIN)

    def bit_step(b, thr):
        cand = thr | (1 << (30 - b))
        return jnp.where(count(lambda kk, c: kk >= cand) >= topk, cand, thr)

    thr = lax.fori_loop(0, 31, bit_step, thr)
    thr = jnp.maximum(thr, INT_MIN + 1)
    n_ge = count(lambda kk, c: kk >= thr)

    def write_tiles(select):
        def body(c, carry):
            o_ref[0, 0, c] = jnp.where(select(keys[c], c), 1.0, 0.0).astype(o_ref.dtype)
            return carry
        lax.fori_loop(0, nact, body, 0)

    has_ties = jnp.max(n_ge) > topk

    @pl.when(jnp.logical_not(has_ties))
    def _():
        write_tiles(lambda kk, c: kk >= thr)

    @pl.when(has_ties)
    def _():
        need = topk - count(lambda kk, c: kk > thr)

        def key_index(c):
            return c * tk + lax.broadcasted_iota(jnp.int32, (tk, 1), 0)

        def idx_step(b, cut):
            cand = cut | (1 << (30 - b))
            below = count(lambda kk, c: (kk == thr) & (key_index(c) < cand))
            return jnp.where(below < need, cand, cut)

        cut = lax.fori_loop(0, 31, idx_step, jnp.zeros((1, tq), jnp.int32))
        write_tiles(lambda kk, c: (kk > thr) | ((kk == thr) & (key_index(c) <= cut)))

    def fill_tile(c, carry):
        o_ref[0, 0, c] = jnp.zeros((tk, tq), o_ref.dtype)
        return carry

    lax.fori_loop(nact, nk, fill_tile, 0)


def _idx_mask(qi_t, wi_t, ki, tq, tk):
    b, s, _ = ki.shape
    nq, nk = s // tq, s // tk
    topk = min(DSA_TOPK_MAX, s // 4)
    return pl.pallas_call(
        functools.partial(_idx_kernel, tq=tq, tk=tk, nk=nk, topk=topk),
        grid=(b, nq),
        in_specs=[pl.BlockSpec((IDX_HEADS * IDX_DIM, tq), lambda bi, i: (0, bi * nq + i)),
                  pl.BlockSpec((IDX_HEADS, tq), lambda bi, i: (0, bi * nq + i)),
                  pl.BlockSpec((1, s, IDX_DIM), lambda bi, i: (bi, 0, 0))],
        out_specs=pl.BlockSpec((1, 1, nk, tk, tq), lambda bi, i: (bi, i, 0, 0, 0)),
        out_shape=jax.ShapeDtypeStruct((b, nq, nk, tk, tq), BF16),
        scratch_shapes=[pltpu.VMEM((nk, tk, tq), jnp.int32)],
        compiler_params=_params("parallel", "arbitrary"),
        name="idx_mask",
    )(qi_t, wi_t, ki)


AUG = 16
V_ROWS = HEAD_DIM + AUG
FAST_RANGE_LOG2 = 120.0


def _head(h):
    return slice(h * HEAD_DIM, (h + 1) * HEAD_DIM)


def _vrows(h):
    return slice(h * V_ROWS, (h + 1) * V_ROWS)


def _att_init(qt_ref, qa_ref, m_sc, acc_sc, qfull):
    m_sc[...] = jnp.full(m_sc.shape, NEG, F32)
    acc_sc[...] = jnp.zeros(acc_sc.shape, F32)
    tq = qfull.shape[2]
    for h in range(ATT_HEADS):
        qfull[h, 0:HEAD_DIM, :] = qt_ref[_head(h), :]
        qfull[h, HEAD_DIM:, :] = jnp.zeros((LANES, tq), BF16)
        lo = HEAD_DIM + h * AUG
        qfull[h, lo:lo + AUG, :] = qa_ref[0, h]


def _scores(k_ref, ka, qfull, h):
    kfull = jnp.concatenate([k_ref[0, :, _head(h)], ka], axis=1)
    return jnp.dot(kfull, qfull[h], preferred_element_type=F32)


def _for_heads(score_fn, step_fn):
    s_next = score_fn(0)
    for h in range(ATT_HEADS):
        s = s_next
        if h + 1 < ATT_HEADS:
            s_next = score_fn(h + 1)
        step_fn(s, h)


def _fast_step(s, keep, vt, h, acc_sc):
    p = jnp.exp2(s).astype(BF16)
    if keep is not None:
        p = p * keep
    acc_sc[h] += jnp.dot(vt, p, preferred_element_type=F32)


def _exact_step(s, vt, h, m_sc, acc_sc):
    m_prev = m_sc[h]
    m_new = jnp.maximum(m_prev, jnp.max(s, axis=0, keepdims=True))
    alpha = jnp.exp2(m_prev - m_new)
    p = jnp.exp2(s - m_new)
    acc_sc[h] = alpha * acc_sc[h] + jnp.dot(vt, p.astype(BF16), preferred_element_type=F32)
    m_sc[h] = m_new


def _att_finish(o_ref, acc_sc):
    for h in range(ATT_HEADS):
        a = acc_sc[h]
        o = (a[0:HEAD_DIM] / a[HEAD_DIM:HEAD_DIM + 1]).T
        o_ref[0, :, _head(h)] = o.astype(o_ref.dtype)


def _att_scratch(tq):
    return [pltpu.VMEM((ATT_HEADS, 1, tq), F32),
            pltpu.VMEM((ATT_HEADS, V_ROWS, tq), F32),
            pltpu.VMEM((ATT_HEADS, 2 * HEAD_DIM, tq), BF16)]


def _pair_tiles(r, j, n):
    second = j > r
    return jnp.where(second, n - 1 - r, r), jnp.where(second, j - r - 1, j)


def _dsa_kernel(fast_ref, k_ref, ka_ref, qt_ref, qa_ref, vt_ref, mask_ref, o_ref, m_sc, acc_sc, qfull, *, n):
    b = pl.program_id(0)
    i, j = _pair_tiles(pl.program_id(1), pl.program_id(2), n)
    fast = fast_ref[b, i] == 1

    @pl.when(j == 0)
    def _():
        _att_init(qt_ref, qa_ref, m_sc, acc_sc, qfull)

    @pl.when(fast)
    def _():
        keep = mask_ref[0, 0, 0]
        ka = ka_ref[0]
        _for_heads(lambda h: _scores(k_ref, ka, qfull, h),
                   lambda s, h: _fast_step(s, keep, vt_ref[_vrows(h), :], h, acc_sc))

    @pl.when(jnp.logical_not(fast))
    def _():
        bias = (mask_ref[0, 0, 0].astype(F32) - 1.0) * (-NEG)
        ka = ka_ref[0]
        _for_heads(lambda h: _scores(k_ref, ka, qfull, h) + bias,
                   lambda s, h: _exact_step(s, vt_ref[_vrows(h), :], h, m_sc, acc_sc))

    @pl.when(j == i)
    def _():
        _att_finish(o_ref, acc_sc)


def _dsa(fast, k, k_aug, q_t, q_aug, v_t, mask, t):
    b, s, c = k.shape
    n = s // t
    assert n % 2 == 0, "query tiles are processed in pairs"

    def q_tile(r, j):
        return _pair_tiles(r, j, n)[0]

    def k_tile(r, j):
        return _pair_tiles(r, j, n)[1]

    grid_spec = pltpu.PrefetchScalarGridSpec(
        num_scalar_prefetch=1,
        grid=(b, n // 2, n + 1),
        in_specs=[pl.BlockSpec((1, t, c), lambda bi, r, j, f: (bi, k_tile(r, j), 0)),
                  pl.BlockSpec((1, t, LANES), lambda bi, r, j, f: (bi, k_tile(r, j), 0)),
                  pl.BlockSpec((c, t), lambda bi, r, j, f: (0, bi * n + q_tile(r, j))),
                  pl.BlockSpec((1, ATT_HEADS, AUG, t), lambda bi, r, j, f: (bi, 0, 0, q_tile(r, j))),
                  pl.BlockSpec((ATT_HEADS * V_ROWS, t), lambda bi, r, j, f: (0, bi * n + k_tile(r, j))),
                  pl.BlockSpec((1, 1, 1, t, t), lambda bi, r, j, f: (bi, q_tile(r, j), k_tile(r, j), 0, 0))],
        out_specs=pl.BlockSpec((1, t, c), lambda bi, r, j, f: (bi, q_tile(r, j), 0)),
        scratch_shapes=_att_scratch(t))
    return pl.pallas_call(
        functools.partial(_dsa_kernel, n=n),
        grid_spec=grid_spec,
        out_shape=jax.ShapeDtypeStruct((b, s, c), BF16),
        compiler_params=_params("parallel", "parallel", "arbitrary"),
        name="dsa",
    )(fast, k, k_aug, q_t, q_aug, v_t, mask)


CONV_HALO = 8


def _conv_kernel(u_ref, gb_ref, gc_ref, hu_ref, hgc_ref, w_ref, o_ref, ext, *, ts):
    i = pl.program_id(1)
    ext[0:CONV_HALO, :] = jnp.where(i > 0, hgc_ref[0] * hu_ref[0], 0.0)
    ext[CONV_HALO:, :] = gc_ref[0] * u_ref[0]
    conv = None
    for t in range(CONV_WIDTH):
        off = CONV_HALO - (CONV_WIDTH - 1) + t
        term = ext[off:off + ts, :] * w_ref[t:t + 1, :]
        conv = term if conv is None else conv + term
    o_ref[0] = (gb_ref[0] * conv).astype(o_ref.dtype)


def _gated_conv(h, w, ts=512):
    b, s, c3 = h.shape
    c = c3 // 3
    ts = min(ts, s)
    hb = ts // CONV_HALO

    def halo_map(col):
        return lambda bi, i: (bi, jnp.maximum(i * hb - 1, 0), col)

    return pl.pallas_call(
        functools.partial(_conv_kernel, ts=ts),
        grid=(b, s // ts),
        in_specs=[pl.BlockSpec((1, ts, c), lambda bi, i: (bi, i, 0)),
                  pl.BlockSpec((1, ts, c), lambda bi, i: (bi, i, 1)),
                  pl.BlockSpec((1, ts, c), lambda bi, i: (bi, i, 2)),
                  pl.BlockSpec((1, CONV_HALO, c), halo_map(0)),
                  pl.BlockSpec((1, CONV_HALO, c), halo_map(2)),
                  pl.BlockSpec(w.shape, lambda bi, i: (0, 0))],
        out_specs=pl.BlockSpec((1, ts, c), lambda bi, i: (bi, i, 0)),
        out_shape=jax.ShapeDtypeStruct((b, s, c), BF16),
        scratch_shapes=[pltpu.VMEM((ts + CONV_HALO, c), F32)],
        compiler_params=_params("parallel", "arbitrary"),
        name="gated_conv",
    )(h, h, h, h, h, w)


GATE_CHUNK = 256


def _fgate_kernel(f_ref, fb_ref, hi_ref, mid_ref, lo_ref):
    s = f_ref.shape[1]
    r = lax.broadcasted_iota(jnp.int32, (GATE_CHUNK, GATE_CHUNK), 0)
    c = lax.broadcasted_iota(jnp.int32, (GATE_CHUNK, GATE_CHUNK), 1)
    tri = (c <= r).astype(F32)

    def body(t, carry):
        rows = pl.ds(t * GATE_CHUNK, GATE_CHUNK)
        z = f_ref[0, rows, :] + fb_ref[...]
        log_f = -(jnp.maximum(-z, 0.0) + jnp.log1p(jnp.exp(-jnp.abs(z))))
        cs = jnp.dot(tri, log_f, preferred_element_type=F32, precision=lax.Precision.HIGHEST) + carry
        b2 = cs * LOG2E
        hi = b2.astype(BF16)
        r1 = b2 - hi.astype(F32)
        mid = r1.astype(BF16)
        hi_ref[0, rows, :] = hi
        mid_ref[0, rows, :] = mid
        lo_ref[0, rows, :] = (r1 - mid.astype(F32)).astype(BF16)
        return cs[GATE_CHUNK - 1:GATE_CHUNK, :]

    lax.fori_loop(0, s // GATE_CHUNK, body, jnp.zeros((1, LANES), F32))


def _forget_cumsum(f, fb):
    b, s, c = f.shape
    spec = pl.BlockSpec((1, s, c), lambda bi: (bi, 0, 0))
    return pl.pallas_call(
        _fgate_kernel,
        grid=(b,),
        in_specs=[spec, pl.BlockSpec((1, c), lambda bi: (0, 0))],
        out_specs=[spec, spec, spec],
        out_shape=[jax.ShapeDtypeStruct((b, s, c), BF16)] * 3,
        compiler_params=_params("parallel"),
        name="forget_cumsum",
    )(f, fb)


def _fox_kernel(first_ref, fast_ref, k_ref, ka_ref, qt_ref, qa_ref, vt_ref, o_ref, m_sc, acc_sc, qfull, *, t):
    b, i, j = pl.program_id(0), pl.program_id(1), pl.program_id(2)
    steps = i - first_ref[b, i]
    fast = fast_ref[b, i] == 1

    def causal():
        return lax.broadcasted_iota(jnp.int32, (t, t), 0) <= lax.broadcasted_iota(jnp.int32, (t, t), 1)

    def scores(diagonal):
        ka = ka_ref[0]

        def fn(h):
            s = _scores(k_ref, ka, qfull, h)
            return jnp.where(causal(), s, NEG) if diagonal else s
        return fn

    def fast_tile(diagonal):
        _for_heads(scores(diagonal), lambda s, h: _fast_step(s, None, vt_ref[_vrows(h), :], h, acc_sc))

    def exact_tile(diagonal):
        _for_heads(scores(diagonal), lambda s, h: _exact_step(s, vt_ref[_vrows(h), :], h, m_sc, acc_sc))

    @pl.when(j == 0)
    def _():
        _att_init(qt_ref, qa_ref, m_sc, acc_sc, qfull)

    @pl.when((j == 0) & fast)
    def _():
        fast_tile(True)

    @pl.when((j == 0) & jnp.logical_not(fast))
    def _():
        exact_tile(True)

    @pl.when((j > 0) & (j <= steps) & fast)
    def _():
        fast_tile(False)

    @pl.when((j > 0) & (j <= steps) & jnp.logical_not(fast))
    def _():
        exact_tile(False)

    @pl.when(j == steps)
    def _():
        _att_finish(o_ref, acc_sc)


def _fox(first, fast, k, k_aug, q_t, q_aug, v_t, t):
    b, s, c = k.shape
    n = s // t

    def key_tile(bi, i, j, first_ref):
        return jnp.maximum(i - j, first_ref[bi, i])

    grid_spec = pltpu.PrefetchScalarGridSpec(
        num_scalar_prefetch=2,
        grid=(b, n, n),
        in_specs=[pl.BlockSpec((1, t, c), lambda bi, i, j, f, g: (bi, key_tile(bi, i, j, f), 0)),
                  pl.BlockSpec((1, t, LANES), lambda bi, i, j, f, g: (bi, key_tile(bi, i, j, f), 0)),
                  pl.BlockSpec((c, t), lambda bi, i, j, f, g: (0, bi * n + i)),
                  pl.BlockSpec((1, ATT_HEADS, AUG, t), lambda bi, i, j, f, g: (bi, 0, 0, i)),
                  pl.BlockSpec((ATT_HEADS * V_ROWS, t), lambda bi, i, j, f, g: (0, bi * n + key_tile(bi, i, j, f)))],
        out_specs=pl.BlockSpec((1, t, c), lambda bi, i, j, f, g: (bi, i, 0)),
        scratch_shapes=_att_scratch(t))
    return pl.pallas_call(
        functools.partial(_fox_kernel, t=t),
        grid_spec=grid_spec,
        out_shape=jax.ShapeDtypeStruct((b, s, c), BF16),
        compiler_params=_params("parallel", "parallel", "arbitrary"),
        name="fox",
    )(first, fast, k, k_aug, q_t, q_aug, v_t)


def _xattn_kernel(xb_ref, xf_ref, wq_ref, kt_ref, v_ref, wo_ref, g_ref, b_ref, of_ref, ob_ref):
    q = jnp.dot(xb_ref[...], wq_ref[...], preferred_element_type=F32) * Q_SCALE
    q = q.astype(BF16)
    outs = []
    for h in range(XA_HEADS):
        s = jnp.dot(q[:, _head(h)], kt_ref[0, _head(h), :], preferred_element_type=F32)
        p = jnp.exp2(s - jnp.max(s, axis=1, keepdims=True))
        l = jnp.sum(p, axis=1, keepdims=True)
        pv = jnp.dot(p.astype(BF16), v_ref[0, :, _head(h)], preferred_element_type=F32)
        outs.append((pv / l).astype(BF16))
    o = jnp.concatenate(outs, axis=1)
    y = jnp.dot(o, wo_ref[...], preferred_element_type=F32)
    y = _layer_norm(ALPHA * xf_ref[...] + y, g_ref[...], b_ref[...])
    of_ref[...] = y
    ob_ref[...] = y.astype(BF16)


def _xattn(xb, xf, wq, kt, v, wo, g, b, seq, tm=256):
    n, d = xf.shape
    per_batch = seq // tm
    m = v.shape[1]
    return pl.pallas_call(
        _xattn_kernel,
        grid=(n // tm,),
        in_specs=[pl.BlockSpec((tm, d), lambda i: (i, 0)),
                  pl.BlockSpec((tm, d), lambda i: (i, 0)),
                  pl.BlockSpec(wq.shape, lambda i: (0, 0)),
                  pl.BlockSpec((1, XA_WIDTH, m), lambda i: (i // per_batch, 0, 0)),
                  pl.BlockSpec((1, m, XA_WIDTH), lambda i: (i // per_batch, 0, 0)),
                  pl.BlockSpec(wo.shape, lambda i: (0, 0)),
                  pl.BlockSpec((1, d), lambda i: (0, 0)),
                  pl.BlockSpec((1, d), lambda i: (0, 0))],
        out_specs=[pl.BlockSpec((tm, d), lambda i: (i, 0)), pl.BlockSpec((tm, d), lambda i: (i, 0))],
        out_shape=[jax.ShapeDtypeStruct((n, d), F32), jax.ShapeDtypeStruct((n, d), BF16)],
        compiler_params=_params("parallel"),
        name="xattn",
    )(xb, xf, wq, kt, v, wo, g, b)


def _ffn_kernel(xb_ref, wg_ref, wu_ref, wo_ref, xf_ref, g_ref, b_ref, of_ref, ob_ref, acc):
    j = pl.program_id(1)

    @pl.when(j == 0)
    def _():
        acc[...] = jnp.zeros(acc.shape, F32)

    xb = xb_ref[...]
    gate = jnp.dot(xb, wg_ref[...], preferred_element_type=F32)
    up = jnp.dot(xb, wu_ref[...], preferred_element_type=F32)
    hid = gate * (1.0 / (1.0 + jnp.exp(-gate))) * up
    acc[...] += jnp.dot(hid.astype(BF16), wo_ref[...], preferred_element_type=F32)

    @pl.when(j == pl.num_programs(1) - 1)
    def _():
        y = _layer_norm(ALPHA * xf_ref[...] + acc[...], g_ref[...], b_ref[...])
        of_ref[...] = y
        ob_ref[...] = y.astype(BF16)


def _ffn(xb, xf, w_in, w_out, g, b, tm=512, th=512):
    n, d = xf.shape
    hidden = w_out.shape[0]
    nh = hidden // th
    return pl.pallas_call(
        _ffn_kernel,
        grid=(n // tm, nh),
        in_specs=[pl.BlockSpec((tm, d), lambda i, j: (i, 0)),
                  pl.BlockSpec((d, th), lambda i, j: (0, j)),
                  pl.BlockSpec((d, th), lambda i, j: (0, j + nh)),
                  pl.BlockSpec((th, d), lambda i, j: (j, 0)),
                  pl.BlockSpec((tm, d), lambda i, j: (i, 0)),
                  pl.BlockSpec((1, d), lambda i, j: (0, 0)),
                  pl.BlockSpec((1, d), lambda i, j: (0, 0))],
        out_specs=[pl.BlockSpec((tm, d), lambda i, j: (i, 0)),
                   pl.BlockSpec((tm, d), lambda i, j: (i, 0))],
        out_shape=[jax.ShapeDtypeStruct((n, d), F32), jax.ShapeDtypeStruct((n, d), BF16)],
        scratch_shapes=[pltpu.VMEM((tm, d), F32)],
        compiler_params=_params("parallel", "arbitrary"),
        name="ffn",
    )(xb, w_in, w_in, w_out, xf, g, b)


def _rotary_tables(positions, dh):
    rot = dh // ROPE_FRACTION
    half = rot // 2
    inv_freq = jnp.power(ROPE_THETA, -(jnp.arange(half, dtype=F32) * 2.0 / rot))
    ang = positions.astype(F32)[..., None] * inv_freq
    cos, sin = jnp.cos(ang), jnp.sin(ang)
    zh = jnp.zeros_like(sin)
    rest = jnp.zeros(ang.shape[:-1] + (dh - rot,), F32)
    c = jnp.concatenate([cos, cos, rest + 1.0], axis=-1)
    s_lo = jnp.concatenate([zh, sin, rest], axis=-1)
    s_hi = jnp.concatenate([-sin, zh, rest], axis=-1)
    return tuple(t.reshape(-1, dh) for t in (c, s_lo, s_hi))


def _even_tables(positions):
    n = positions.size
    head = _rotary_tables(positions, HEAD_DIM)
    idx = _rotary_tables(positions, IDX_DIM)
    idx_full = tuple(jnp.tile(t, (1, LANES // IDX_DIM)) for t in idx)
    wi_scale = jnp.concatenate([jnp.full((IDX_HEADS,), IDX_HEADS ** -0.5, F32),
                                jnp.ones((LANES - IDX_DIM - IDX_HEADS,), F32)])
    pad = jnp.zeros((n, LANES - IDX_DIM), F32)
    tail = (jnp.concatenate([idx[0], pad + wi_scale], axis=1),
            jnp.concatenate([idx[1], pad], axis=1),
            jnp.concatenate([idx[2], pad], axis=1))
    return {"k": head,
            "q_t": tuple((t * Q_SCALE).T for t in head),
            "qi_t": tuple((t * IDX_DIM ** -0.5).T for t in idx_full),
            "tail": tail}


def _att_bounds(q_t, k, bsz, seq, t):
    n = seq // t
    qn = jnp.sqrt(jnp.sum(jnp.square(q_t.astype(F32)).reshape(ATT_HEADS, HEAD_DIM, bsz, seq), axis=1))
    qn = qn.transpose(1, 0, 2)
    kn = jnp.sqrt(jnp.sum(jnp.square(k.astype(F32)).reshape(bsz, seq, ATT_HEADS, HEAD_DIM), axis=-1))
    k_max = jnp.max(kn, axis=1)
    neg_m = -(1.01 * qn * k_max[:, :, None] + 1.0)
    qn_tile = jnp.max(qn.reshape(bsz, ATT_HEADS, n, t), axis=-1).transpose(0, 2, 1)
    kn_tile = jnp.max(kn.reshape(bsz, n, t, ATT_HEADS), axis=2)
    spread = jnp.max(2.05 * qn_tile * k_max[:, None, :], axis=-1) + 8.0
    fast = (spread <= FAST_RANGE_LOG2).astype(jnp.int32)
    return neg_m.astype(BF16), qn_tile, kn_tile, fast


def _aug_operands(q_entries, k_entries, bsz, seq):
    shape = (bsz, seq, ATT_HEADS)
    one, zero = jnp.ones(shape, BF16), jnp.zeros(shape, BF16)

    def pack(entries):
        cols = [one if e is None else e for e in entries]
        return jnp.stack(cols + [zero] * (AUG - len(cols)), axis=-1)

    return pack(k_entries).reshape(bsz, seq, ATT_HEADS * AUG), pack(q_entries).transpose(0, 2, 3, 1)


def _value_rows(v_t):
    n = v_t.shape[1]
    v3 = v_t.reshape(ATT_HEADS, HEAD_DIM, n)
    extra = jnp.zeros((ATT_HEADS, AUG, n), BF16).at[:, 0, :].set(1.0)
    return jnp.concatenate([v3, extra], axis=1).reshape(ATT_HEADS * V_ROWS, n)


def _even_mixer(xb, xt, tables, w_in, pool_w, pool_scale, bsz, seq, tq):
    n = xb.shape[0]
    hw = HALF_WIDTH
    wb = w_in.astype(BF16)
    wt = wb.T
    u = _proj(xb, wb[:, 0:hw], F32)
    q_t = _proj_t(wt[hw:2 * hw], xt, BF16, tables["q_t"], shift=HEAD_DIM // 8)
    k = _proj(xb, wb[:, 2 * hw:3 * hw], BF16, tables["k"], shift=HEAD_DIM // 8)
    v_t = _proj_t(wt[3 * hw:4 * hw], xt, BF16)
    qi_t = _proj_t(wt[4 * hw:5 * hw], xt, BF16, tables["qi_t"], shift=IDX_DIM // 8)
    w_tail = jnp.pad(wb[:, 5 * hw:], ((0, 0), (0, LANES - IDX_DIM - IDX_HEADS)))
    tail = _proj(xb, w_tail, F32, tables["tail"], shift=IDX_DIM // 8)

    ki = tail[:, :IDX_DIM].astype(BF16).reshape(bsz, seq, IDX_DIM)
    wi_t = tail[:, IDX_DIM:IDX_DIM + IDX_HEADS].T

    a = _pool(u.reshape(bsz, seq, hw), pool_w.astype(BF16), pool_scale.reshape(1, hw))
    mask = _idx_mask(qi_t, wi_t, ki, tq, tq)
    k3 = k.reshape(bsz, seq, hw)
    neg_m, _, _, fast = _att_bounds(q_t, k3, bsz, seq, tq)
    k_aug, q_aug = _aug_operands([neg_m.transpose(0, 2, 1)], [None], bsz, seq)
    bb = _dsa(fast, k3, k_aug, q_t, q_aug, _value_rows(v_t), mask, tq)
    return a.reshape(n, hw), bb.reshape(n, hw)


DSA_TILE = 512
FOX_TILE = 512
UNDERFLOW_LOG2 = 160.0


def _fox_first_tile(qn, kn, terms, bsz, seq, t):
    n = seq // t
    f32sum = sum(x.astype(F32) for x in terms)[:, :, :ATT_HEADS].reshape(bsz, n, t, ATT_HEADS)
    f_first, f_last = f32sum[:, :, 0], f32sum[:, :, t - 1]
    bound = (1.01 * qn[:, :, None] * (kn[:, None, :] + kn[:, :, None])
             + f_first[:, :, None] - f_last[:, None, :] + 1.0)
    tiles = jnp.arange(n, dtype=jnp.int32)
    needed = jnp.any(bound >= -UNDERFLOW_LOG2, axis=-1) | (tiles[:, None] == tiles[None, :])
    needed = needed & (tiles[None, :] <= tiles[:, None])
    return jnp.min(jnp.where(needed, tiles[None, None, :], n), axis=-1).astype(jnp.int32)


def _odd_mixer(xb, xt, w_in, conv_w, forget_b, bsz, seq):
    n = xb.shape[0]
    hw = HALF_WIDTH
    wb = w_in.astype(BF16)
    wt = wb.T
    ugg = _proj(xb, wb[:, 0:3 * hw], F32)
    q_t = _proj_t(wt[3 * hw:4 * hw], xt, BF16, scale=Q_SCALE)
    k = _proj(xb, wb[:, 4 * hw:5 * hw], BF16)
    v_t = _proj_t(wt[5 * hw:6 * hw], xt, BF16)
    w_tail = jnp.pad(wb[:, 6 * hw:], ((0, 0), (0, LANES - ATT_HEADS)))
    f = _proj(xb, w_tail, F32)
    fb = jnp.pad(forget_b, (0, LANES - ATT_HEADS)).reshape(1, LANES)
    terms = _forget_cumsum(f.reshape(bsz, seq, LANES), fb)
    hi, mid, lo = (x[:, :, :ATT_HEADS] for x in terms)

    c = _gated_conv(ugg.reshape(bsz, seq, 3 * hw), conv_w)
    t = min(FOX_TILE, seq)
    k3 = k.reshape(bsz, seq, hw)
    neg_m, qn, kn, fast = _att_bounds(q_t, k3, bsz, seq, t)
    k_aug, q_aug = _aug_operands([neg_m.transpose(0, 2, 1), None, None, None, hi, mid, lo],
                                 [None, -hi, -mid, -lo, None, None, None], bsz, seq)
    first = _fox_first_tile(qn, kn, terms, bsz, seq, t)
    d = _fox(first, fast, k3, k_aug, q_t, q_aug, _value_rows(v_t), t)
    return c.reshape(n, hw), d.reshape(n, hw)


def kernel(x, mem, positions, ev_w_in, ev_pool_w, ev_pool_scale, ev_w_out, od_w_in, od_conv_w, od_forget_b,
           od_w_out, ca_w_q, ca_w_kv, ca_w_o, ffn_w_in, ffn_w_out, ln_g, ln_b):
    bsz, seq, d = x.shape
    n = bsz * seq
    m = mem.shape[1]
    tq = min(DSA_TILE, seq)
    xf = x.reshape(n, d)
    xb = xf.astype(BF16)
    memb = mem.reshape(bsz * m, d).astype(BF16)
    tables = _even_tables(positions)
    for i in range(DEPTH):
        j = i // 2
        xt = xb.T
        if i % 2 == 0:
            parts = _even_mixer(xb, xt, tables, ev_w_in[j], ev_pool_w[j], ev_pool_scale[j], bsz, seq, tq)
            w_out = ev_w_out[j]
        else:
            parts = _odd_mixer(xb, xt, od_w_in[j], od_conv_w[j], od_forget_b[j], bsz, seq)
            w_out = od_w_out[j]
        g, b = ln_g[i].reshape(3, 1, d), ln_b[i].reshape(3, 1, d)
        xf, xb = _out_ln(list(parts), w_out.astype(BF16), xf, g[0], b[0])

        kv = _proj(memb, ca_w_kv[i].astype(BF16), BF16).reshape(bsz, m, 2 * XA_WIDTH)
        kt = kv[:, :, :XA_WIDTH].transpose(0, 2, 1)
        xf, xb = _xattn(xb, xf, ca_w_q[i].astype(BF16), kt, kv[:, :, XA_WIDTH:], ca_w_o[i].astype(BF16),
                        g[1], b[1], seq)

        xf, xb = _ffn(xb, xf, ffn_w_in[i].astype(BF16), ffn_w_out[i].astype(BF16), g[2], b[2])
    return xf.reshape(bsz, seq, d)
```

```python
import functools

import jax
import jax.numpy as jnp
from jax import lax
from jax.experimental import pallas as pl
from jax.experimental.pallas import tpu as pltpu

F32 = jnp.float32
BF16 = jnp.bfloat16

D_MODEL = 2048
DEPTH = 4
CHUNK = 64
HEAD_DIM = 128
HALF_WIDTH = D_MODEL // 2
POOL_WINDOWS = (2, 4, 8, 16)
POOL_GROUP_DIM = HALF_WIDTH // len(POOL_WINDOWS)
ATT_HEADS = HALF_WIDTH // HEAD_DIM
IDX_HEADS = 16
IDX_DIM = 64
DSA_TOPK_MAX = 256
CONV_WIDTH = 3
XA_HEADS = 4
XA_WIDTH = XA_HEADS * HEAD_DIM
FFN_HIDDEN = -(-(8 * D_MODEL) // (3 * 256)) * 256
ROPE_THETA = 500000.0
ROPE_FRACTION = 4
LN_EPS = 1e-5
ALPHA = (2 * DEPTH) ** 0.25
LOG2E = 1.4426950408889634
Q_SCALE = HEAD_DIM ** -0.5 * LOG2E

LANES = 128
SUBLANES = 8
PACKED_ROWS = 16
NEG = -1e30
INT_MIN = -(2 ** 31)
VMEM_LIMIT = 60000 * 1024


def _params(*semantics):
    return pltpu.CompilerParams(dimension_semantics=semantics, vmem_limit_bytes=VMEM_LIMIT)


def _layer_norm(y, g, b):
    mu = jnp.mean(y, axis=-1, keepdims=True)
    d = y - mu
    var = jnp.mean(d * d, axis=-1, keepdims=True)
    return d * lax.rsqrt(var + LN_EPS) * g + b


def _proj_kernel(*refs, shift, scale, axis):
    if shift:
        a_ref, b_ref, c_ref, s1_ref, s2_ref, o_ref = refs
    else:
        a_ref, b_ref, o_ref = refs
    h = jnp.dot(a_ref[...], b_ref[...], preferred_element_type=F32)
    if shift:
        c, s1, s2 = c_ref[...], s1_ref[...], s2_ref[...]
        for g in range(h.shape[axis] // LANES):
            sl = (slice(None),) * axis + (slice(g * LANES, (g + 1) * LANES),)
            hg = h[sl]
            out = hg * c + pltpu.roll(hg, shift, axis) * s1 + pltpu.roll(hg, LANES - shift, axis) * s2
            o_ref[sl] = out.astype(o_ref.dtype)
    elif scale != 1.0:
        o_ref[...] = (h * scale).astype(o_ref.dtype)
    else:
        o_ref[...] = h.astype(o_ref.dtype)


def _proj(x, w, out_dtype, tables=None, shift=0, scale=1.0, tm=1024, tn=1024):
    n, k = x.shape
    c = w.shape[1]
    tm, tn = min(tm, n), min(tn, c)
    in_specs = [pl.BlockSpec((tm, k), lambda i, j: (i, 0)),
                pl.BlockSpec((k, tn), lambda i, j: (0, j))]
    args = [x, w]
    if shift:
        in_specs += [pl.BlockSpec((tm, LANES), lambda i, j: (i, 0))] * 3
        args += list(tables)
    return pl.pallas_call(
        functools.partial(_proj_kernel, shift=shift, scale=scale, axis=1),
        grid=(n // tm, c // tn),
        in_specs=in_specs,
        out_specs=pl.BlockSpec((tm, tn), lambda i, j: (i, j)),
        out_shape=jax.ShapeDtypeStruct((n, c), out_dtype),
        compiler_params=_params("parallel", "arbitrary"),
        name="proj",
    )(*args)


def _proj_t(wt, xt, out_dtype, tables=None, shift=0, scale=1.0, tm=1024, tn=1024):
    c, k = wt.shape
    n = xt.shape[1]
    tm, tn = min(tm, n), min(tn, c)
    in_specs = [pl.BlockSpec((tn, k), lambda i, j: (j, 0)),
                pl.BlockSpec((k, tm), lambda i, j: (0, i))]
    args = [wt, xt]
    if shift:
        in_specs += [pl.BlockSpec((LANES, tm), lambda i, j: (0, i))] * 3
        args += list(tables)
    return pl.pallas_call(
        functools.partial(_proj_kernel, shift=shift, scale=scale, axis=0),
        grid=(n // tm, c // tn),
        in_specs=in_specs,
        out_specs=pl.BlockSpec((tn, tm), lambda i, j: (j, i)),
        out_shape=jax.ShapeDtypeStruct((c, n), out_dtype),
        compiler_params=_params("parallel", "arbitrary"),
        name="proj_t",
    )(*args)


def _out_ln_kernel(*refs, nparts):
    parts = refs[:nparts]
    w_ref, x_ref, g_ref, b_ref, of_ref, ob_ref = refs[nparts:]
    acc = None
    off = 0
    for p in parts:
        kp = p.shape[1]
        t = jnp.dot(p[...], w_ref[off:off + kp, :], preferred_element_type=F32)
        acc = t if acc is None else acc + t
        off += kp
    y = _layer_norm(ALPHA * x_ref[...] + acc, g_ref[...], b_ref[...])
    of_ref[...] = y
    ob_ref[...] = y.astype(BF16)


def _out_ln(parts, w, x, g, b, tm=256):
    n, d = x.shape
    in_specs = [pl.BlockSpec((tm, p.shape[1]), lambda i: (i, 0)) for p in parts]
    in_specs += [pl.BlockSpec(w.shape, lambda i: (0, 0)),
                 pl.BlockSpec((tm, d), lambda i: (i, 0)),
                 pl.BlockSpec((1, d), lambda i: (0, 0)),
                 pl.BlockSpec((1, d), lambda i: (0, 0))]
    return pl.pallas_call(
        functools.partial(_out_ln_kernel, nparts=len(parts)),
        grid=(n // tm,),
        in_specs=in_specs,
        out_specs=[pl.BlockSpec((tm, d), lambda i: (i, 0)), pl.BlockSpec((tm, d), lambda i: (i, 0))],
        out_shape=[jax.ShapeDtypeStruct((n, d), F32), jax.ShapeDtypeStruct((n, d), BF16)],
        compiler_params=_params("parallel"),
        name="out_ln",
    )(*parts, w, x, g, b)


POOL_HALO = 16


def _pool_kernel(u_ref, halo_ref, w_ref, sc_ref, o_ref, ext, *, ts):
    i = pl.program_id(1)
    ext[0:POOL_HALO, :] = jnp.where(i > 0, halo_ref[0], 0.0)
    ext[POOL_HALO:, :] = u_ref[0]
    cnt = i * ts + lax.broadcasted_iota(jnp.int32, (ts, 1), 0) + 1
    for g, win in enumerate(POOL_WINDOWS):
        lo, hi = g * POOL_GROUP_DIM, (g + 1) * POOL_GROUP_DIM
        cur = ext[POOL_HALO:POOL_HALO + ts, lo:hi]
        s = cur
        for j in range(1, win):
            s = s + ext[POOL_HALO - j:POOL_HALO - j + ts, lo:hi]
        d = s / jnp.minimum(cnt, win).astype(F32) - cur
        y = jnp.dot(d.astype(BF16), w_ref[g], preferred_element_type=F32)
        o_ref[0, :, lo:hi] = (y * sc_ref[:, lo:hi]).astype(o_ref.dtype)


def _pool(u, w, scale, ts=512):
    b, s, c = u.shape
    ts = min(ts, s)
    hb = ts // POOL_HALO
    return pl.pallas_call(
        functools.partial(_pool_kernel, ts=ts),
        grid=(b, s // ts),
        in_specs=[pl.BlockSpec((1, ts, c), lambda bi, i: (bi, i, 0)),
                  pl.BlockSpec((1, POOL_HALO, c), lambda bi, i: (bi, jnp.maximum(i * hb - 1, 0), 0)),
                  pl.BlockSpec(w.shape, lambda bi, i: (0, 0, 0)),
                  pl.BlockSpec((1, c), lambda bi, i: (0, 0))],
        out_specs=pl.BlockSpec((1, ts, c), lambda bi, i: (bi, i, 0)),
        out_shape=jax.ShapeDtypeStruct((b, s, c), BF16),
        scratch_shapes=[pltpu.VMEM((ts + POOL_HALO, c), F32)],
        compiler_params=_params("parallel", "arbitrary"),
        name="pool",
    )(u, u, w, scale)


def _sortable(v):
    bits = lax.bitcast_convert_type(v, jnp.int32)
    return bits ^ ((bits >> 31) & 0x7FFFFFFF)


def _colsum8(v):
    tk, tq = v.shape
    return v.reshape(tk // SUBLANES, SUBLANES, tq).sum(axis=0)


def _idx_kernel(qi_ref, wi_ref, ki_ref, o_ref, keys, half, *, tq, tk, nk, topk):
    i = pl.program_id(1)
    nact = (i * tq + tq - 1) // tk + 1
    q_chunk = (i * tq + lax.broadcasted_iota(jnp.int32, (1, tq), 1)) // CHUNK

    def score_tile(c, carry):
        kt = ki_ref[0, pl.ds(pl.multiple_of(c * tk, tk), tk), :]
        acc = jnp.zeros((tk, tq), F32)
        for h in range(IDX_HEADS):
            il = jnp.dot(kt, qi_ref[h * IDX_DIM:(h + 1) * IDX_DIM, :], preferred_element_type=F32)
            acc = acc + jnp.maximum(il, 0.0) * wi_ref[h:h + 1, :]
        k_chunk = (c * tk + lax.broadcasted_iota(jnp.int32, (tk, 1), 0)) // CHUNK
        kk = jnp.where(k_chunk <= q_chunk, _sortable(acc), INT_MIN)
        keys[c] = kk
        half[c] = (kk >> 16).astype(jnp.int16)
        return carry

    lax.fori_loop(0, nact, score_tile, 0)

    def count(pred):
        def body(c, acc):
            return acc + _colsum8(jnp.where(pred(keys[c], c), 1, 0))
        acc = lax.fori_loop(0, nact, body, jnp.zeros((SUBLANES, tq), jnp.int32))
        return jnp.sum(acc, axis=0, keepdims=True)

    def count16(cand, strict):
        cand16 = jnp.broadcast_to(cand, (PACKED_ROWS, tq)).astype(jnp.int16)[None]

        def body(c, acc):
            h3 = half[c].reshape(tk // PACKED_ROWS, PACKED_ROWS, tq)
            hit = (h3 > cand16) if strict else (h3 >= cand16)
            ones = jnp.where(hit, jnp.int16(1), jnp.int16(0))
            for r in range(tk // PACKED_ROWS):
                acc = acc + ones[r]
            return acc

        acc = lax.fori_loop(0, nact, body, jnp.zeros((PACKED_ROWS, tq), jnp.int16))
        return jnp.sum(acc.astype(jnp.int32), axis=0, keepdims=True)

    hi = jnp.where(count16(jnp.zeros((1, tq), jnp.int32), False) >= topk, 0, -(2 ** 15))

    def hi_step(b, hi):
        cand = hi | (1 << (14 - b))
        return jnp.where(count16(cand, False) >= topk, cand, hi)

    hi = lax.fori_loop(0, 15, hi_step, hi)
    n_above = count16(hi, True)

    def low_tile(c, carry):
        kk = keys[c]
        half[c] = jnp.where((kk >> 16) == hi, (kk & 0xFFFF) - 2 ** 15, -(2 ** 15)).astype(jnp.int16)
        return carry

    lax.fori_loop(0, nact, low_tile, 0)

    def lo_step(b, lo):
        cand = lo | (1 << (15 - b))
        return jnp.where(n_above + count16(cand - 2 ** 15, False) >= topk, cand, lo)

    lo = lax.fori_loop(0, 16, lo_step, jnp.zeros((1, tq), jnp.int32))
    thr = (hi << 16) | lo
    thr = jnp.maximum(thr, INT_MIN + 1)
    n_ge = count(lambda kk, c: kk >= thr)

    def write_tiles(select):
        def body(c, carry):
            o_ref[0, 0, c] = jnp.where(select(keys[c], c), 1.0, 0.0).astype(o_ref.dtype)
            return carry
        lax.fori_loop(0, nact, body, 0)

    has_ties = jnp.max(n_ge) > topk

    @pl.when(jnp.logical_not(has_ties))
    def _():
        write_tiles(lambda kk, c: kk >= thr)

    @pl.when(has_ties)
    def _():
        need = topk - count(lambda kk, c: kk > thr)

        def key_index(c):
            return c * tk + lax.broadcasted_iota(jnp.int32, (tk, 1), 0)

        def idx_step(b, cut):
            cand = cut | (1 << (30 - b))
            below = count(lambda kk, c: (kk == thr) & (key_index(c) < cand))
            return jnp.where(below < need, cand, cut)

        cut = lax.fori_loop(0, 31, idx_step, jnp.zeros((1, tq), jnp.int32))
        write_tiles(lambda kk, c: (kk > thr) | ((kk == thr) & (key_index(c) <= cut)))

    def fill_tile(c, carry):
        o_ref[0, 0, c] = jnp.zeros((tk, tq), o_ref.dtype)
        return carry

    lax.fori_loop(nact, nk, fill_tile, 0)


def _idx_mask(qi_t, wi_t, ki, tq, tk):
    b, s, _ = ki.shape
    nq, nk = s // tq, s // tk
    topk = min(DSA_TOPK_MAX, s // 4)
    return pl.pallas_call(
        functools.partial(_idx_kernel, tq=tq, tk=tk, nk=nk, topk=topk),
        grid=(b, nq),
        in_specs=[pl.BlockSpec((IDX_HEADS * IDX_DIM, tq), lambda bi, i: (0, bi * nq + i)),
                  pl.BlockSpec((IDX_HEADS, tq), lambda bi, i: (0, bi * nq + i)),
                  pl.BlockSpec((1, s, IDX_DIM), lambda bi, i: (bi, 0, 0))],
        out_specs=pl.BlockSpec((1, 1, nk, tk, tq), lambda bi, i: (bi, i, 0, 0, 0)),
        out_shape=jax.ShapeDtypeStruct((b, nq, nk, tk, tq), BF16),
        scratch_shapes=[pltpu.VMEM((nk, tk, tq), jnp.int32), pltpu.VMEM((nk, tk, tq), jnp.int16)],
        compiler_params=_params("parallel", "arbitrary"),
        name="idx_mask",
    )(qi_t, wi_t, ki)


AUG = 16
V_ROWS = HEAD_DIM + AUG
FAST_RANGE_LOG2 = 120.0


def _head(h):
    return slice(h * HEAD_DIM, (h + 1) * HEAD_DIM)


def _vrows(h):
    return slice(h * V_ROWS, (h + 1) * V_ROWS)


def _att_init(qt_ref, qa_ref, m_sc, acc_sc, qfull):
    m_sc[...] = jnp.full(m_sc.shape, NEG, F32)
    acc_sc[...] = jnp.zeros(acc_sc.shape, F32)
    tq = qfull.shape[2]
    for h in range(ATT_HEADS):
        qfull[h, 0:HEAD_DIM, :] = qt_ref[_head(h), :]
        qfull[h, HEAD_DIM:, :] = jnp.zeros((LANES, tq), BF16)
        lo = HEAD_DIM + h * AUG
        qfull[h, lo:lo + AUG, :] = qa_ref[0, h]


def _scores(k_ref, ka, qfull, h):
    kfull = jnp.concatenate([k_ref[0, :, _head(h)], ka], axis=1)
    return jnp.dot(kfull, qfull[h], preferred_element_type=F32)


def _for_heads(score_fn, step_fn):
    s_next = score_fn(0)
    for h in range(ATT_HEADS):
        s = s_next
        if h + 1 < ATT_HEADS:
            s_next = score_fn(h + 1)
        step_fn(s, h)


def _fast_step(s, keep, vt, h, acc_sc):
    p = jnp.exp2(s).astype(BF16)
    if keep is not None:
        p = p * keep
    acc_sc[h] += jnp.dot(vt, p, preferred_element_type=F32)


def _exact_step(s, vt, h, m_sc, acc_sc):
    m_prev = m_sc[h]
    m_new = jnp.maximum(m_prev, jnp.max(s, axis=0, keepdims=True))
    alpha = jnp.exp2(m_prev - m_new)
    p = jnp.exp2(s - m_new)
    acc_sc[h] = alpha * acc_sc[h] + jnp.dot(vt, p.astype(BF16), preferred_element_type=F32)
    m_sc[h] = m_new


def _att_finish(o_ref, acc_sc):
    for h in range(ATT_HEADS):
        a = acc_sc[h]
        o = (a[0:HEAD_DIM] / a[HEAD_DIM:HEAD_DIM + 1]).T
        o_ref[0, :, _head(h)] = o.astype(o_ref.dtype)


def _att_scratch(tq):
    return [pltpu.VMEM((ATT_HEADS, 1, tq), F32),
            pltpu.VMEM((ATT_HEADS, V_ROWS, tq), F32),
            pltpu.VMEM((ATT_HEADS, 2 * HEAD_DIM, tq), BF16)]


def _pair_tiles(r, j, n):
    second = j > r
    return jnp.where(second, n - 1 - r, r), jnp.where(second, j - r - 1, j)


def _dsa_kernel(fast_ref, k_ref, ka_ref, qt_ref, qa_ref, vt_ref, mask_ref, o_ref, m_sc, acc_sc, qfull, *, n):
    b = pl.program_id(0)
    i, j = _pair_tiles(pl.program_id(1), pl.program_id(2), n)
    fast = fast_ref[b, i] == 1

    @pl.when(j == 0)
    def _():
        _att_init(qt_ref, qa_ref, m_sc, acc_sc, qfull)

    @pl.when(fast)
    def _():
        keep = mask_ref[0, 0, 0]
        ka = ka_ref[0]
        _for_heads(lambda h: _scores(k_ref, ka, qfull, h),
                   lambda s, h: _fast_step(s, keep, vt_ref[_vrows(h), :], h, acc_sc))

    @pl.when(jnp.logical_not(fast))
    def _():
        bias = (mask_ref[0, 0, 0].astype(F32) - 1.0) * (-NEG)
        ka = ka_ref[0]
        _for_heads(lambda h: _scores(k_ref, ka, qfull, h) + bias,
                   lambda s, h: _exact_step(s, vt_ref[_vrows(h), :], h, m_sc, acc_sc))

    @pl.when(j == i)
    def _():
        _att_finish(o_ref, acc_sc)


def _dsa(fast, k, k_aug, q_t, q_aug, v_t, mask, t):
    b, s, c = k.shape
    n = s // t
    assert n % 2 == 0, "query tiles are processed in pairs"

    def q_tile(r, j):
        return _pair_tiles(r, j, n)[0]

    def k_tile(r, j):
        return _pair_tiles(r, j, n)[1]

    grid_spec = pltpu.PrefetchScalarGridSpec(
        num_scalar_prefetch=1,
        grid=(b, n // 2, n + 1),
        in_specs=[pl.BlockSpec((1, t, c), lambda bi, r, j, f: (bi, k_tile(r, j), 0)),
                  pl.BlockSpec((1, t, LANES), lambda bi, r, j, f: (bi, k_tile(r, j), 0)),
                  pl.BlockSpec((c, t), lambda bi, r, j, f: (0, bi * n + q_tile(r, j))),
                  pl.BlockSpec((1, ATT_HEADS, AUG, t), lambda bi, r, j, f: (bi, 0, 0, q_tile(r, j))),
                  pl.BlockSpec((ATT_HEADS * V_ROWS, t), lambda bi, r, j, f: (0, bi * n + k_tile(r, j))),
                  pl.BlockSpec((1, 1, 1, t, t), lambda bi, r, j, f: (bi, q_tile(r, j), k_tile(r, j), 0, 0))],
        out_specs=pl.BlockSpec((1, t, c), lambda bi, r, j, f: (bi, q_tile(r, j), 0)),
        scratch_shapes=_att_scratch(t))
    return pl.pallas_call(
        functools.partial(_dsa_kernel, n=n),
        grid_spec=grid_spec,
        out_shape=jax.ShapeDtypeStruct((b, s, c), BF16),
        compiler_params=_params("parallel", "parallel", "arbitrary"),
        name="dsa",
    )(fast, k, k_aug, q_t, q_aug, v_t, mask)


CONV_HALO = 8


def _conv_kernel(u_ref, gb_ref, gc_ref, hu_ref, hgc_ref, w_ref, o_ref, ext, *, ts):
    i = pl.program_id(1)
    ext[0:CONV_HALO, :] = jnp.where(i > 0, hgc_ref[0] * hu_ref[0], 0.0)
    ext[CONV_HALO:, :] = gc_ref[0] * u_ref[0]
    conv = None
    for t in range(CONV_WIDTH):
        off = CONV_HALO - (CONV_WIDTH - 1) + t
        term = ext[off:off + ts, :] * w_ref[t:t + 1, :]
        conv = term if conv is None else conv + term
    o_ref[0] = (gb_ref[0] * conv).astype(o_ref.dtype)


def _gated_conv(h, w, ts=512):
    b, s, c3 = h.shape
    c = c3 // 3
    ts = min(ts, s)
    hb = ts // CONV_HALO

    def halo_map(col):
        return lambda bi, i: (bi, jnp.maximum(i * hb - 1, 0), col)

    return pl.pallas_call(
        functools.partial(_conv_kernel, ts=ts),
        grid=(b, s // ts),
        in_specs=[pl.BlockSpec((1, ts, c), lambda bi, i: (bi, i, 0)),
                  pl.BlockSpec((1, ts, c), lambda bi, i: (bi, i, 1)),
                  pl.BlockSpec((1, ts, c), lambda bi, i: (bi, i, 2)),
                  pl.BlockSpec((1, CONV_HALO, c), halo_map(0)),
                  pl.BlockSpec((1, CONV_HALO, c), halo_map(2)),
                  pl.BlockSpec(w.shape, lambda bi, i: (0, 0))],
        out_specs=pl.BlockSpec((1, ts, c), lambda bi, i: (bi, i, 0)),
        out_shape=jax.ShapeDtypeStruct((b, s, c), BF16),
        scratch_shapes=[pltpu.VMEM((ts + CONV_HALO, c), F32)],
        compiler_params=_params("parallel", "arbitrary"),
        name="gated_conv",
    )(h, h, h, h, h, w)


GATE_CHUNK = 256


def _fgate_kernel(f_ref, fb_ref, hi_ref, mid_ref, lo_ref):
    s = f_ref.shape[1]
    r = lax.broadcasted_iota(jnp.int32, (GATE_CHUNK, GATE_CHUNK), 0)
    c = lax.broadcasted_iota(jnp.int32, (GATE_CHUNK, GATE_CHUNK), 1)
    tri = (c <= r).astype(F32)

    def body(t, carry):
        rows = pl.ds(t * GATE_CHUNK, GATE_CHUNK)
        z = f_ref[0, rows, :] + fb_ref[...]
        log_f = -(jnp.maximum(-z, 0.0) + jnp.log1p(jnp.exp(-jnp.abs(z))))
        cs = jnp.dot(tri, log_f, preferred_element_type=F32, precision=lax.Precision.HIGHEST) + carry
        b2 = cs * LOG2E
        hi = b2.astype(BF16)
        r1 = b2 - hi.astype(F32)
        mid = r1.astype(BF16)
        hi_ref[0, rows, :] = hi
        mid_ref[0, rows, :] = mid
        lo_ref[0, rows, :] = (r1 - mid.astype(F32)).astype(BF16)
        return cs[GATE_CHUNK - 1:GATE_CHUNK, :]

    lax.fori_loop(0, s // GATE_CHUNK, body, jnp.zeros((1, LANES), F32))


def _forget_cumsum(f, fb):
    b, s, c = f.shape
    spec = pl.BlockSpec((1, s, c), lambda bi: (bi, 0, 0))
    return pl.pallas_call(
        _fgate_kernel,
        grid=(b,),
        in_specs=[spec, pl.BlockSpec((1, c), lambda bi: (0, 0))],
        out_specs=[spec, spec, spec],
        out_shape=[jax.ShapeDtypeStruct((b, s, c), BF16)] * 3,
        compiler_params=_params("parallel"),
        name="forget_cumsum",
    )(f, fb)


def _fox_kernel(first_ref, fast_ref, k_ref, ka_ref, qt_ref, qa_ref, vt_ref, o_ref, m_sc, acc_sc, qfull, *, t):
    b, i, j = pl.program_id(0), pl.program_id(1), pl.program_id(2)
    steps = i - first_ref[b, i]
    fast = fast_ref[b, i] == 1

    def causal():
        return lax.broadcasted_iota(jnp.int32, (t, t), 0) <= lax.broadcasted_iota(jnp.int32, (t, t), 1)

    def scores(diagonal):
        ka = ka_ref[0]

        def fn(h):
            s = _scores(k_ref, ka, qfull, h)
            return jnp.where(causal(), s, NEG) if diagonal else s
        return fn

    def fast_tile(diagonal):
        _for_heads(scores(diagonal), lambda s, h: _fast_step(s, None, vt_ref[_vrows(h), :], h, acc_sc))

    def exact_tile(diagonal):
        _for_heads(scores(diagonal), lambda s, h: _exact_step(s, vt_ref[_vrows(h), :], h, m_sc, acc_sc))

    @pl.when(j == 0)
    def _():
        _att_init(qt_ref, qa_ref, m_sc, acc_sc, qfull)

    @pl.when((j == 0) & fast)
    def _():
        fast_tile(True)

    @pl.when((j == 0) & jnp.logical_not(fast))
    def _():
        exact_tile(True)

    @pl.when((j > 0) & (j <= steps) & fast)
    def _():
        fast_tile(False)

    @pl.when((j > 0) & (j <= steps) & jnp.logical_not(fast))
    def _():
        exact_tile(False)

    @pl.when(j == steps)
    def _():
        _att_finish(o_ref, acc_sc)


def _fox(first, fast, k, k_aug, q_t, q_aug, v_t, t):
    b, s, c = k.shape
    n = s // t

    def key_tile(bi, i, j, first_ref):
        return jnp.maximum(i - j, first_ref[bi, i])

    grid_spec = pltpu.PrefetchScalarGridSpec(
        num_scalar_prefetch=2,
        grid=(b, n, n),
        in_specs=[pl.BlockSpec((1, t, c), lambda bi, i, j, f, g: (bi, key_tile(bi, i, j, f), 0)),
                  pl.BlockSpec((1, t, LANES), lambda bi, i, j, f, g: (bi, key_tile(bi, i, j, f), 0)),
                  pl.BlockSpec((c, t), lambda bi, i, j, f, g: (0, bi * n + i)),
                  pl.BlockSpec((1, ATT_HEADS, AUG, t), lambda bi, i, j, f, g: (bi, 0, 0, i)),
                  pl.BlockSpec((ATT_HEADS * V_ROWS, t), lambda bi, i, j, f, g: (0, bi * n + key_tile(bi, i, j, f)))],
        out_specs=pl.BlockSpec((1, t, c), lambda bi, i, j, f, g: (bi, i, 0)),
        scratch_shapes=_att_scratch(t))
    return pl.pallas_call(
        functools.partial(_fox_kernel, t=t),
        grid_spec=grid_spec,
        out_shape=jax.ShapeDtypeStruct((b, s, c), BF16),
        compiler_params=_params("parallel", "parallel", "arbitrary"),
        name="fox",
    )(first, fast, k, k_aug, q_t, q_aug, v_t)


def _xattn_kernel(xb_ref, xf_ref, wq_ref, kt_ref, v_ref, wo_ref, g_ref, b_ref, of_ref, ob_ref):
    q = jnp.dot(xb_ref[...], wq_ref[...], preferred_element_type=F32) * Q_SCALE
    q = q.astype(BF16)
    outs = []
    for h in range(XA_HEADS):
        s = jnp.dot(q[:, _head(h)], kt_ref[0, _head(h), :], preferred_element_type=F32)
        p = jnp.exp2(s - jnp.max(s, axis=1, keepdims=True))
        l = jnp.sum(p, axis=1, keepdims=True)
        pv = jnp.dot(p.astype(BF16), v_ref[0, :, _head(h)], preferred_element_type=F32)
        outs.append((pv / l).astype(BF16))
    o = jnp.concatenate(outs, axis=1)
    y = jnp.dot(o, wo_ref[...], preferred_element_type=F32)
    y = _layer_norm(ALPHA * xf_ref[...] + y, g_ref[...], b_ref[...])
    of_ref[...] = y
    ob_ref[...] = y.astype(BF16)


def _xattn(xb, xf, wq, kt, v, wo, g, b, seq, tm=256):
    n, d = xf.shape
    per_batch = seq // tm
    m = v.shape[1]
    return pl.pallas_call(
        _xattn_kernel,
        grid=(n // tm,),
        in_specs=[pl.BlockSpec((tm, d), lambda i: (i, 0)),
                  pl.BlockSpec((tm, d), lambda i: (i, 0)),
                  pl.BlockSpec(wq.shape, lambda i: (0, 0)),
                  pl.BlockSpec((1, XA_WIDTH, m), lambda i: (i // per_batch, 0, 0)),
                  pl.BlockSpec((1, m, XA_WIDTH), lambda i: (i // per_batch, 0, 0)),
                  pl.BlockSpec(wo.shape, lambda i: (0, 0)),
                  pl.BlockSpec((1, d), lambda i: (0, 0)),
                  pl.BlockSpec((1, d), lambda i: (0, 0))],
        out_specs=[pl.BlockSpec((tm, d), lambda i: (i, 0)), pl.BlockSpec((tm, d), lambda i: (i, 0))],
        out_shape=[jax.ShapeDtypeStruct((n, d), F32), jax.ShapeDtypeStruct((n, d), BF16)],
        compiler_params=_params("parallel"),
        name="xattn",
    )(xb, xf, wq, kt, v, wo, g, b)


def _ffn_kernel(xb_ref, wg_ref, wu_ref, wo_ref, xf_ref, g_ref, b_ref, of_ref, ob_ref, acc):
    j = pl.program_id(1)

    @pl.when(j == 0)
    def _():
        acc[...] = jnp.zeros(acc.shape, F32)

    xb = xb_ref[...]
    gate = jnp.dot(xb, wg_ref[...], preferred_element_type=F32)
    up = jnp.dot(xb, wu_ref[...], preferred_element_type=F32)
    hid = gate * (1.0 / (1.0 + jnp.exp(-gate))) * up
    acc[...] += jnp.dot(hid.astype(BF16), wo_ref[...], preferred_element_type=F32)

    @pl.when(j == pl.num_programs(1) - 1)
    def _():
        y = _layer_norm(ALPHA * xf_ref[...] + acc[...], g_ref[...], b_ref[...])
        of_ref[...] = y
        ob_ref[...] = y.astype(BF16)


def _ffn(xb, xf, w_in, w_out, g, b, tm=512, th=512):
    n, d = xf.shape
    hidden = w_out.shape[0]
    nh = hidden // th
    return pl.pallas_call(
        _ffn_kernel,
        grid=(n // tm, nh),
        in_specs=[pl.BlockSpec((tm, d), lambda i, j: (i, 0)),
                  pl.BlockSpec((d, th), lambda i, j: (0, j)),
                  pl.BlockSpec((d, th), lambda i, j: (0, j + nh)),
                  pl.BlockSpec((th, d), lambda i, j: (j, 0)),
                  pl.BlockSpec((tm, d), lambda i, j: (i, 0)),
                  pl.BlockSpec((1, d), lambda i, j: (0, 0)),
                  pl.BlockSpec((1, d), lambda i, j: (0, 0))],
        out_specs=[pl.BlockSpec((tm, d), lambda i, j: (i, 0)),
                   pl.BlockSpec((tm, d), lambda i, j: (i, 0))],
        out_shape=[jax.ShapeDtypeStruct((n, d), F32), jax.ShapeDtypeStruct((n, d), BF16)],
        scratch_shapes=[pltpu.VMEM((tm, d), F32)],
        compiler_params=_params("parallel", "arbitrary"),
        name="ffn",
    )(xb, w_in, w_in, w_out, xf, g, b)


def _rotary_tables(positions, dh):
    rot = dh // ROPE_FRACTION
    half = rot // 2
    inv_freq = jnp.power(ROPE_THETA, -(jnp.arange(half, dtype=F32) * 2.0 / rot))
    ang = positions.astype(F32)[..., None] * inv_freq
    cos, sin = jnp.cos(ang), jnp.sin(ang)
    zh = jnp.zeros_like(sin)
    rest = jnp.zeros(ang.shape[:-1] + (dh - rot,), F32)
    c = jnp.concatenate([cos, cos, rest + 1.0], axis=-1)
    s_lo = jnp.concatenate([zh, sin, rest], axis=-1)
    s_hi = jnp.concatenate([-sin, zh, rest], axis=-1)
    return tuple(t.reshape(-1, dh) for t in (c, s_lo, s_hi))


def _even_tables(positions):
    n = positions.size
    head = _rotary_tables(positions, HEAD_DIM)
    idx = _rotary_tables(positions, IDX_DIM)
    idx_full = tuple(jnp.tile(t, (1, LANES // IDX_DIM)) for t in idx)
    wi_scale = jnp.concatenate([jnp.full((IDX_HEADS,), IDX_HEADS ** -0.5, F32),
                                jnp.ones((LANES - IDX_DIM - IDX_HEADS,), F32)])
    pad = jnp.zeros((n, LANES - IDX_DIM), F32)
    tail = (jnp.concatenate([idx[0], pad + wi_scale], axis=1),
            jnp.concatenate([idx[1], pad], axis=1),
            jnp.concatenate([idx[2], pad], axis=1))
    return {"k": head,
            "q_t": tuple((t * Q_SCALE).T for t in head),
            "qi_t": tuple((t * IDX_DIM ** -0.5).T for t in idx_full),
            "tail": tail}


def _att_bounds(q_t, k, bsz, seq, t):
    n = seq // t
    qn = jnp.sqrt(jnp.sum(jnp.square(q_t.astype(F32)).reshape(ATT_HEADS, HEAD_DIM, bsz, seq), axis=1))
    qn = qn.transpose(1, 0, 2)
    kn = jnp.sqrt(jnp.sum(jnp.square(k.astype(F32)).reshape(bsz, seq, ATT_HEADS, HEAD_DIM), axis=-1))
    k_max = jnp.max(kn, axis=1)
    neg_m = -(1.01 * qn * k_max[:, :, None] + 1.0)
    qn_tile = jnp.max(qn.reshape(bsz, ATT_HEADS, n, t), axis=-1).transpose(0, 2, 1)
    kn_tile = jnp.max(kn.reshape(bsz, n, t, ATT_HEADS), axis=2)
    spread = jnp.max(2.05 * qn_tile * k_max[:, None, :], axis=-1) + 8.0
    fast = (spread <= FAST_RANGE_LOG2).astype(jnp.int32)
    return neg_m.astype(BF16), qn_tile, kn_tile, fast


def _aug_operands(q_entries, k_entries, bsz, seq):
    shape = (bsz, seq, ATT_HEADS)
    one, zero = jnp.ones(shape, BF16), jnp.zeros(shape, BF16)

    def pack(entries):
        cols = [one if e is None else e for e in entries]
        return jnp.stack(cols + [zero] * (AUG - len(cols)), axis=-1)

    return pack(k_entries).reshape(bsz, seq, ATT_HEADS * AUG), pack(q_entries).transpose(0, 2, 3, 1)


def _value_rows(v_t):
    n = v_t.shape[1]
    v3 = v_t.reshape(ATT_HEADS, HEAD_DIM, n)
    extra = jnp.zeros((ATT_HEADS, AUG, n), BF16).at[:, 0, :].set(1.0)
    return jnp.concatenate([v3, extra], axis=1).reshape(ATT_HEADS * V_ROWS, n)


def _even_mixer(xb, xt, tables, w_in, pool_w, pool_scale, bsz, seq, tq):
    n = xb.shape[0]
    hw = HALF_WIDTH
    wb = w_in.astype(BF16)
    wt = wb.T
    u = _proj(xb, wb[:, 0:hw], F32)
    q_t = _proj_t(wt[hw:2 * hw], xt, BF16, tables["q_t"], shift=HEAD_DIM // 8)
    k = _proj(xb, wb[:, 2 * hw:3 * hw], BF16, tables["k"], shift=HEAD_DIM // 8)
    v_t = _proj_t(wt[3 * hw:4 * hw], xt, BF16)
    qi_t = _proj_t(wt[4 * hw:5 * hw], xt, BF16, tables["qi_t"], shift=IDX_DIM // 8)
    w_tail = jnp.pad(wb[:, 5 * hw:], ((0, 0), (0, LANES - IDX_DIM - IDX_HEADS)))
    tail = _proj(xb, w_tail, F32, tables["tail"], shift=IDX_DIM // 8)

    ki = tail[:, :IDX_DIM].astype(BF16).reshape(bsz, seq, IDX_DIM)
    wi_t = tail[:, IDX_DIM:IDX_DIM + IDX_HEADS].T

    a = _pool(u.reshape(bsz, seq, hw), pool_w.astype(BF16), pool_scale.reshape(1, hw))
    mask = _idx_mask(qi_t, wi_t, ki, tq, tq)
    k3 = k.reshape(bsz, seq, hw)
    neg_m, _, _, fast = _att_bounds(q_t, k3, bsz, seq, tq)
    k_aug, q_aug = _aug_operands([neg_m.transpose(0, 2, 1)], [None], bsz, seq)
    bb = _dsa(fast, k3, k_aug, q_t, q_aug, _value_rows(v_t), mask, tq)
    return a.reshape(n, hw), bb.reshape(n, hw)


DSA_TILE = 512
FOX_TILE = 512
UNDERFLOW_LOG2 = 160.0


def _fox_first_tile(qn, kn, terms, bsz, seq, t):
    n = seq // t
    f32sum = sum(x.astype(F32) for x in terms)[:, :, :ATT_HEADS].reshape(bsz, n, t, ATT_HEADS)
    f_first, f_last = f32sum[:, :, 0], f32sum[:, :, t - 1]
    bound = (1.01 * qn[:, :, None] * (kn[:, None, :] + kn[:, :, None])
             + f_first[:, :, None] - f_last[:, None, :] + 1.0)
    tiles = jnp.arange(n, dtype=jnp.int32)
    needed = jnp.any(bound >= -UNDERFLOW_LOG2, axis=-1) | (tiles[:, None] == tiles[None, :])
    needed = needed & (tiles[None, :] <= tiles[:, None])
    return jnp.min(jnp.where(needed, tiles[None, None, :], n), axis=-1).astype(jnp.int32)


def _odd_mixer(xb, xt, w_in, conv_w, forget_b, bsz, seq):
    n = xb.shape[0]
    hw = HALF_WIDTH
    wb = w_in.astype(BF16)
    wt = wb.T
    ugg = _proj(xb, wb[:, 0:3 * hw], F32)
    q_t = _proj_t(wt[3 * hw:4 * hw], xt, BF16, scale=Q_SCALE)
    k = _proj(xb, wb[:, 4 * hw:5 * hw], BF16)
    v_t = _proj_t(wt[5 * hw:6 * hw], xt, BF16)
    w_tail = jnp.pad(wb[:, 6 * hw:], ((0, 0), (0, LANES - ATT_HEADS)))
    f = _proj(xb, w_tail, F32)
    fb = jnp.pad(forget_b, (0, LANES - ATT_HEADS)).reshape(1, LANES)
    terms = _forget_cumsum(f.reshape(bsz, seq, LANES), fb)
    hi, mid, lo = (x[:, :, :ATT_HEADS] for x in terms)

    c = _gated_conv(ugg.reshape(bsz, seq, 3 * hw), conv_w)
    t = min(FOX_TILE, seq)
    k3 = k.reshape(bsz, seq, hw)
    neg_m, qn, kn, fast = _att_bounds(q_t, k3, bsz, seq, t)
    k_aug, q_aug = _aug_operands([neg_m.transpose(0, 2, 1), None, None, None, hi, mid, lo],
                                 [None, -hi, -mid, -lo, None, None, None], bsz, seq)
    first = _fox_first_tile(qn, kn, terms, bsz, seq, t)
    d = _fox(first, fast, k3, k_aug, q_t, q_aug, _value_rows(v_t), t)
    return c.reshape(n, hw), d.reshape(n, hw)


def kernel(x, mem, positions, ev_w_in, ev_pool_w, ev_pool_scale, ev_w_out, od_w_in, od_conv_w, od_forget_b,
           od_w_out, ca_w_q, ca_w_kv, ca_w_o, ffn_w_in, ffn_w_out, ln_g, ln_b):
    bsz, seq, d = x.shape
    n = bsz * seq
    m = mem.shape[1]
    tq = min(DSA_TILE, seq)
    xf = x.reshape(n, d)
    xb = xf.astype(BF16)
    memb = mem.reshape(bsz * m, d).astype(BF16)
    tables = _even_tables(positions)
    for i in range(DEPTH):
        j = i // 2
        xt = xb.T
        if i % 2 == 0:
            parts = _even_mixer(xb, xt, tables, ev_w_in[j], ev_pool_w[j], ev_pool_scale[j], bsz, seq, tq)
            w_out = ev_w_out[j]
        else:
            parts = _odd_mixer(xb, xt, od_w_in[j], od_conv_w[j], od_forget_b[j], bsz, seq)
            w_out = od_w_out[j]
        g, b = ln_g[i].reshape(3, 1, d), ln_b[i].reshape(3, 1, d)
        xf, xb = _out_ln(list(parts), w_out.astype(BF16), xf, g[0], b[0])

        kv = _proj(memb, ca_w_kv[i].astype(BF16), BF16).reshape(bsz, m, 2 * XA_WIDTH)
        kt = kv[:, :, :XA_WIDTH].transpose(0, 2, 1)
        xf, xb = _xattn(xb, xf, ca_w_q[i].astype(BF16), kt, kv[:, :, XA_WIDTH:], ca_w_o[i].astype(BF16),
                        g[1], b[1], seq)

        xf, xb = _ffn(xb, xf, ffn_w_in[i].astype(BF16), ffn_w_out[i].astype(BF16), g[2], b[2])
    return xf.reshape(bsz, seq, d)
```

```python
import functools

import jax
import jax.numpy as jnp
from jax import lax
from jax.experimental import pallas as pl
from jax.experimental.pallas import tpu as pltpu

F32 = jnp.float32
BF16 = jnp.bfloat16

D_MODEL = 2048
DEPTH = 4
CHUNK = 64
HEAD_DIM = 128
HALF_WIDTH = D_MODEL // 2
POOL_WINDOWS = (2, 4, 8, 16)
POOL_GROUP_DIM = HALF_WIDTH // len(POOL_WINDOWS)
ATT_HEADS = HALF_WIDTH // HEAD_DIM
IDX_HEADS = 16
IDX_DIM = 64
DSA_TOPK_MAX = 256
CONV_WIDTH = 3
XA_HEADS = 4
XA_WIDTH = XA_HEADS * HEAD_DIM
FFN_HIDDEN = -(-(8 * D_MODEL) // (3 * 256)) * 256
ROPE_THETA = 500000.0
ROPE_FRACTION = 4
LN_EPS = 1e-5
ALPHA = (2 * DEPTH) ** 0.25
LOG2E = 1.4426950408889634
Q_SCALE = HEAD_DIM ** -0.5 * LOG2E

LANES = 128
SUBLANES = 8
PACKED_ROWS = 16
NEG = -1e30
INT_MIN = -(2 ** 31)
VMEM_LIMIT = 60000 * 1024


def _params(*semantics):
    return pltpu.CompilerParams(dimension_semantics=semantics, vmem_limit_bytes=VMEM_LIMIT)


def _layer_norm(y, g, b):
    mu = jnp.mean(y, axis=-1, keepdims=True)
    d = y - mu
    var = jnp.mean(d * d, axis=-1, keepdims=True)
    return d * lax.rsqrt(var + LN_EPS) * g + b


AUG = 16
V_ROWS = HEAD_DIM + AUG


def _proj_kernel(*refs, shift, scale, axis, norms, value_rows):
    refs = list(refs)
    a_ref, b_ref = refs[0], refs[1]
    n_ref = refs.pop() if norms else None
    o_ref = refs.pop()
    h = jnp.dot(a_ref[...], b_ref[...], preferred_element_type=F32)
    if not (shift or norms or value_rows):
        o_ref[...] = (h * scale if scale != 1.0 else h).astype(o_ref.dtype)
        return
    if shift:
        c, s1, s2 = refs[2][...], refs[3][...], refs[4][...]
    for g in range(h.shape[axis] // LANES):
        sl = (slice(None),) * axis + (slice(g * LANES, (g + 1) * LANES),)
        hg = h[sl]
        if shift:
            hg = hg * c + pltpu.roll(hg, shift, axis) * s1 + pltpu.roll(hg, LANES - shift, axis) * s2
        elif scale != 1.0:
            hg = hg * scale
        og = hg.astype(o_ref.dtype)
        if value_rows:
            tm = og.shape[1]
            o_ref[g * V_ROWS:g * V_ROWS + LANES, :] = og
            row = lax.broadcasted_iota(jnp.int32, (AUG, tm), 0)
            o_ref[g * V_ROWS + LANES:(g + 1) * V_ROWS, :] = jnp.where(row == 0, 1.0, 0.0).astype(o_ref.dtype)
        else:
            o_ref[sl] = og
        if norms:
            sq = jnp.square(og.astype(F32))
            if axis == 0:
                n_ref[g:g + 1, :] = jnp.sum(sq, axis=0, keepdims=True)
            else:
                n_ref[:, g:g + 1] = jnp.sum(sq, axis=1, keepdims=True)


def _proj(x, w, out_dtype, col0, ncols, tables=None, shift=0, scale=1.0, norms=False, tm=1024, tn=1024):
    n, k = x.shape
    tm, tn = min(tm, n), min(tn, ncols)
    assert col0 % tn == 0 and ncols % tn == 0
    in_specs = [pl.BlockSpec((tm, k), lambda i, j: (i, 0)),
                pl.BlockSpec((k, tn), lambda i, j: (0, col0 // tn + j))]
    args = [x, w]
    if shift:
        in_specs += [pl.BlockSpec((tm, LANES), lambda i, j: (i, 0))] * 3
        args += list(tables)
    out_specs = [pl.BlockSpec((tm, tn), lambda i, j: (i, j))]
    out_shape = [jax.ShapeDtypeStruct((n, ncols), out_dtype)]
    if norms:
        assert tn == ncols
        out_specs.append(pl.BlockSpec((tm, tn // LANES), lambda i, j: (i, 0)))
        out_shape.append(jax.ShapeDtypeStruct((n, ncols // LANES), F32))
    out = pl.pallas_call(
        functools.partial(_proj_kernel, shift=shift, scale=scale, axis=1, norms=norms, value_rows=False),
        grid=(n // tm, ncols // tn),
        in_specs=in_specs,
        out_specs=out_specs,
        out_shape=out_shape,
        compiler_params=_params("parallel", "arbitrary"),
        name="proj",
    )(*args)
    return out if norms else out[0]


def _proj_t(wt, xt, out_dtype, row0, nrows, tables=None, shift=0, scale=1.0, norms=False, value_rows=False,
            tm=1024, tn=1024):
    k, n = xt.shape
    tm, tn = min(tm, n), min(tn, nrows)
    assert row0 % tn == 0 and nrows % tn == 0
    in_specs = [pl.BlockSpec((tn, k), lambda i, j: (row0 // tn + j, 0)),
                pl.BlockSpec((k, tm), lambda i, j: (0, i))]
    args = [wt, xt]
    if shift:
        in_specs += [pl.BlockSpec((LANES, tm), lambda i, j: (0, i))] * 3
        args += list(tables)
    rows_out = tn // LANES * V_ROWS if value_rows else tn
    out_specs = [pl.BlockSpec((rows_out, tm), lambda i, j: (j, i))]
    out_shape = [jax.ShapeDtypeStruct((nrows // tn * rows_out, n), out_dtype)]
    if norms:
        assert tn == nrows
        out_specs.append(pl.BlockSpec((tn // LANES, tm), lambda i, j: (0, i)))
        out_shape.append(jax.ShapeDtypeStruct((nrows // LANES, n), F32))
    out = pl.pallas_call(
        functools.partial(_proj_kernel, shift=shift, scale=scale, axis=0, norms=norms, value_rows=value_rows),
        grid=(n // tm, nrows // tn),
        in_specs=in_specs,
        out_specs=out_specs,
        out_shape=out_shape,
        compiler_params=_params("parallel", "arbitrary"),
        name="proj_t",
    )(*args)
    return out if norms else out[0]


def _out_ln_kernel(*refs, nparts):
    parts = refs[:nparts]
    w_ref, x_ref, g_ref, b_ref, of_ref, ob_ref = refs[nparts:]
    acc = None
    off = 0
    for p in parts:
        kp = p.shape[1]
        t = jnp.dot(p[...], w_ref[off:off + kp, :], preferred_element_type=F32)
        acc = t if acc is None else acc + t
        off += kp
    y = _layer_norm(ALPHA * x_ref[...] + acc, g_ref[...], b_ref[...])
    of_ref[...] = y
    ob_ref[...] = y.astype(BF16)


def _out_ln(parts, w, x, g, b, tm=256):
    n, d = x.shape
    in_specs = [pl.BlockSpec((tm, p.shape[1]), lambda i: (i, 0)) for p in parts]
    in_specs += [pl.BlockSpec(w.shape, lambda i: (0, 0)),
                 pl.BlockSpec((tm, d), lambda i: (i, 0)),
                 pl.BlockSpec((1, d), lambda i: (0, 0)),
                 pl.BlockSpec((1, d), lambda i: (0, 0))]
    return pl.pallas_call(
        functools.partial(_out_ln_kernel, nparts=len(parts)),
        grid=(n // tm,),
        in_specs=in_specs,
        out_specs=[pl.BlockSpec((tm, d), lambda i: (i, 0)), pl.BlockSpec((tm, d), lambda i: (i, 0))],
        out_shape=[jax.ShapeDtypeStruct((n, d), F32), jax.ShapeDtypeStruct((n, d), BF16)],
        compiler_params=_params("parallel"),
        name="out_ln",
    )(*parts, w, x, g, b)


POOL_HALO = 16


def _pool_kernel(u_ref, halo_ref, w_ref, sc_ref, o_ref, ext, *, ts):
    i = pl.program_id(1)
    ext[0:POOL_HALO, :] = jnp.where(i > 0, halo_ref[0], 0.0)
    ext[POOL_HALO:, :] = u_ref[0]
    cnt = i * ts + lax.broadcasted_iota(jnp.int32, (ts, 1), 0) + 1
    for g, win in enumerate(POOL_WINDOWS):
        lo, hi = g * POOL_GROUP_DIM, (g + 1) * POOL_GROUP_DIM
        cur = ext[POOL_HALO:POOL_HALO + ts, lo:hi]
        s = cur
        for j in range(1, win):
            s = s + ext[POOL_HALO - j:POOL_HALO - j + ts, lo:hi]
        d = s / jnp.minimum(cnt, win).astype(F32) - cur
        y = jnp.dot(d.astype(BF16), w_ref[g], preferred_element_type=F32)
        o_ref[0, :, lo:hi] = (y * sc_ref[:, lo:hi]).astype(o_ref.dtype)


def _pool(u, w, scale, ts=512):
    b, s, c = u.shape
    ts = min(ts, s)
    hb = ts // POOL_HALO
    return pl.pallas_call(
        functools.partial(_pool_kernel, ts=ts),
        grid=(b, s // ts),
        in_specs=[pl.BlockSpec((1, ts, c), lambda bi, i: (bi, i, 0)),
                  pl.BlockSpec((1, POOL_HALO, c), lambda bi, i: (bi, jnp.maximum(i * hb - 1, 0), 0)),
                  pl.BlockSpec(w.shape, lambda bi, i: (0, 0, 0)),
                  pl.BlockSpec((1, c), lambda bi, i: (0, 0))],
        out_specs=pl.BlockSpec((1, ts, c), lambda bi, i: (bi, i, 0)),
        out_shape=jax.ShapeDtypeStruct((b, s, c), BF16),
        scratch_shapes=[pltpu.VMEM((ts + POOL_HALO, c), F32)],
        compiler_params=_params("parallel", "arbitrary"),
        name="pool",
    )(u, u, w, scale)


def _sortable(v):
    bits = lax.bitcast_convert_type(v, jnp.int32)
    return bits ^ ((bits >> 31) & 0x7FFFFFFF)


def _colsum8(v):
    tk, tq = v.shape
    return v.reshape(tk // SUBLANES, SUBLANES, tq).sum(axis=0)


def _idx_kernel(qi_ref, wi_ref, ki_ref, o_ref, keys, half, *, tq, tk, nk, topk):
    i = pl.program_id(1)
    nact = (i * tq + tq - 1) // tk + 1
    q_chunk = (i * tq + lax.broadcasted_iota(jnp.int32, (1, tq), 1)) // CHUNK

    def score_tile(c, carry):
        kt = ki_ref[0, pl.ds(pl.multiple_of(c * tk, tk), tk), :]
        acc = jnp.zeros((tk, tq), F32)
        for h in range(IDX_HEADS):
            il = jnp.dot(kt, qi_ref[h * IDX_DIM:(h + 1) * IDX_DIM, :], preferred_element_type=F32)
            acc = acc + jnp.maximum(il, 0.0) * wi_ref[h:h + 1, :]
        k_chunk = (c * tk + lax.broadcasted_iota(jnp.int32, (tk, 1), 0)) // CHUNK
        kk = jnp.where(k_chunk <= q_chunk, _sortable(acc), INT_MIN)
        keys[c] = kk
        half[c] = (kk >> 16).astype(jnp.int16)
        return carry

    lax.fori_loop(0, nact, score_tile, 0)

    def count(pred):
        def body(c, acc):
            return acc + _colsum8(jnp.where(pred(keys[c], c), 1, 0))
        acc = lax.fori_loop(0, nact, body, jnp.zeros((SUBLANES, tq), jnp.int32))
        return jnp.sum(acc, axis=0, keepdims=True)

    def count16(cand, strict):
        cand16 = jnp.broadcast_to(cand, (PACKED_ROWS, tq)).astype(jnp.int16)[None]

        def body(c, acc):
            h3 = half[c].reshape(tk // PACKED_ROWS, PACKED_ROWS, tq)
            hit = (h3 > cand16) if strict else (h3 >= cand16)
            ones = jnp.where(hit, jnp.int16(1), jnp.int16(0))
            for r in range(tk // PACKED_ROWS):
                acc = acc + ones[r]
            return acc

        acc = lax.fori_loop(0, nact, body, jnp.zeros((PACKED_ROWS, tq), jnp.int16))
        return jnp.sum(acc.astype(jnp.int32), axis=0, keepdims=True)

    hi = jnp.where(count16(jnp.zeros((1, tq), jnp.int32), False) >= topk, 0, -(2 ** 15))

    def hi_step(b, hi):
        cand = hi | (1 << (14 - b))
        return jnp.where(count16(cand, False) >= topk, cand, hi)

    hi = lax.fori_loop(0, 15, hi_step, hi)
    n_above = count16(hi, True)

    def low_tile(c, carry):
        kk = keys[c]
        half[c] = jnp.where((kk >> 16) == hi, (kk & 0xFFFF) - 2 ** 15, -(2 ** 15)).astype(jnp.int16)
        return carry

    lax.fori_loop(0, nact, low_tile, 0)

    def lo_step(b, lo):
        cand = lo | (1 << (15 - b))
        return jnp.where(n_above + count16(cand - 2 ** 15, False) >= topk, cand, lo)

    lo = lax.fori_loop(0, 16, lo_step, jnp.zeros((1, tq), jnp.int32))
    thr = (hi << 16) | lo
    thr = jnp.maximum(thr, INT_MIN + 1)
    n_ge = count(lambda kk, c: kk >= thr)

    def write_tiles(select):
        def body(c, carry):
            o_ref[0, 0, c] = jnp.where(select(keys[c], c), 1.0, 0.0).astype(o_ref.dtype)
            return carry
        lax.fori_loop(0, nact, body, 0)

    has_ties = jnp.max(n_ge) > topk

    @pl.when(jnp.logical_not(has_ties))
    def _():
        write_tiles(lambda kk, c: kk >= thr)

    @pl.when(has_ties)
    def _():
        need = topk - count(lambda kk, c: kk > thr)

        def key_index(c):
            return c * tk + lax.broadcasted_iota(jnp.int32, (tk, 1), 0)

        def idx_step(b, cut):
            cand = cut | (1 << (30 - b))
            below = count(lambda kk, c: (kk == thr) & (key_index(c) < cand))
            return jnp.where(below < need, cand, cut)

        cut = lax.fori_loop(0, 31, idx_step, jnp.zeros((1, tq), jnp.int32))
        write_tiles(lambda kk, c: (kk > thr) | ((kk == thr) & (key_index(c) <= cut)))

    def fill_tile(c, carry):
        o_ref[0, 0, c] = jnp.zeros((tk, tq), o_ref.dtype)
        return carry

    lax.fori_loop(nact, nk, fill_tile, 0)


def _idx_mask(qi_t, wi_t, ki, tq, tk):
    b, s, _ = ki.shape
    nq, nk = s // tq, s // tk
    topk = min(DSA_TOPK_MAX, s // 4)
    return pl.pallas_call(
        functools.partial(_idx_kernel, tq=tq, tk=tk, nk=nk, topk=topk),
        grid=(b, nq),
        in_specs=[pl.BlockSpec((IDX_HEADS * IDX_DIM, tq), lambda bi, i: (0, bi * nq + i)),
                  pl.BlockSpec((IDX_HEADS, tq), lambda bi, i: (0, bi * nq + i)),
                  pl.BlockSpec((1, s, IDX_DIM), lambda bi, i: (bi, 0, 0))],
        out_specs=pl.BlockSpec((1, 1, nk, tk, tq), lambda bi, i: (bi, i, 0, 0, 0)),
        out_shape=jax.ShapeDtypeStruct((b, nq, nk, tk, tq), BF16),
        scratch_shapes=[pltpu.VMEM((nk, tk, tq), jnp.int32), pltpu.VMEM((nk, tk, tq), jnp.int16)],
        compiler_params=_params("parallel", "arbitrary"),
        name="idx_mask",
    )(qi_t, wi_t, ki)


FAST_RANGE_LOG2 = 120.0


def _head(h):
    return slice(h * HEAD_DIM, (h + 1) * HEAD_DIM)


def _vrows(h):
    return slice(h * V_ROWS, (h + 1) * V_ROWS)


def _att_init(qt_ref, qa_ref, m_sc, acc_sc, qfull):
    m_sc[...] = jnp.full(m_sc.shape, NEG, F32)
    acc_sc[...] = jnp.zeros(acc_sc.shape, F32)
    tq = qfull.shape[2]
    for h in range(ATT_HEADS):
        qfull[h, 0:HEAD_DIM, :] = qt_ref[_head(h), :]
        qfull[h, HEAD_DIM:, :] = jnp.zeros((LANES, tq), BF16)
        lo = HEAD_DIM + h * AUG
        qfull[h, lo:lo + AUG, :] = qa_ref[0, h]


def _scores(k_ref, ka, qfull, h):
    kfull = jnp.concatenate([k_ref[0, :, _head(h)], ka], axis=1)
    return jnp.dot(kfull, qfull[h], preferred_element_type=F32)


def _for_heads(score_fn, step_fn):
    s_next = score_fn(0)
    for h in range(ATT_HEADS):
        s = s_next
        if h + 1 < ATT_HEADS:
            s_next = score_fn(h + 1)
        step_fn(s, h)


def _fast_step(s, keep, vt, h, acc_sc):
    p = jnp.exp2(s).astype(BF16)
    if keep is not None:
        p = p * keep
    acc_sc[h] += jnp.dot(vt, p, preferred_element_type=F32)


def _exact_step(s, vt, h, m_sc, acc_sc):
    m_prev = m_sc[h]
    m_new = jnp.maximum(m_prev, jnp.max(s, axis=0, keepdims=True))
    alpha = jnp.exp2(m_prev - m_new)
    p = jnp.exp2(s - m_new)
    acc_sc[h] = alpha * acc_sc[h] + jnp.dot(vt, p.astype(BF16), preferred_element_type=F32)
    m_sc[h] = m_new


def _att_finish(o_ref, acc_sc):
    for h in range(ATT_HEADS):
        a = acc_sc[h]
        o = (a[0:HEAD_DIM] / a[HEAD_DIM:HEAD_DIM + 1]).T
        o_ref[0, :, _head(h)] = o.astype(o_ref.dtype)


def _att_scratch(tq):
    return [pltpu.VMEM((ATT_HEADS, 1, tq), F32),
            pltpu.VMEM((ATT_HEADS, V_ROWS, tq), F32),
            pltpu.VMEM((ATT_HEADS, 2 * HEAD_DIM, tq), BF16)]


def _pair_tiles(r, j, n):
    second = j > r
    return jnp.where(second, n - 1 - r, r), jnp.where(second, j - r - 1, j)


def _dsa_kernel(fast_ref, k_ref, ka_ref, qt_ref, qa_ref, vt_ref, mask_ref, o_ref, m_sc, acc_sc, qfull, *, n):
    b = pl.program_id(0)
    i, j = _pair_tiles(pl.program_id(1), pl.program_id(2), n)
    fast = fast_ref[b, i] == 1

    @pl.when(j == 0)
    def _():
        _att_init(qt_ref, qa_ref, m_sc, acc_sc, qfull)

    @pl.when(fast)
    def _():
        keep = mask_ref[0, 0, 0]
        ka = ka_ref[0]
        _for_heads(lambda h: _scores(k_ref, ka, qfull, h),
                   lambda s, h: _fast_step(s, keep, vt_ref[_vrows(h), :], h, acc_sc))

    @pl.when(jnp.logical_not(fast))
    def _():
        bias = (mask_ref[0, 0, 0].astype(F32) - 1.0) * (-NEG)
        ka = ka_ref[0]
        _for_heads(lambda h: _scores(k_ref, ka, qfull, h) + bias,
                   lambda s, h: _exact_step(s, vt_ref[_vrows(h), :], h, m_sc, acc_sc))

    @pl.when(j == i)
    def _():
        _att_finish(o_ref, acc_sc)


def _dsa(fast, k, k_aug, q_t, q_aug, v_t, mask, t):
    b, s, c = k.shape
    n = s // t
    assert n % 2 == 0, "query tiles are processed in pairs"

    def q_tile(r, j):
        return _pair_tiles(r, j, n)[0]

    def k_tile(r, j):
        return _pair_tiles(r, j, n)[1]

    grid_spec = pltpu.PrefetchScalarGridSpec(
        num_scalar_prefetch=1,
        grid=(b, n // 2, n + 1),
        in_specs=[pl.BlockSpec((1, t, c), lambda bi, r, j, f: (bi, k_tile(r, j), 0)),
                  pl.BlockSpec((1, t, LANES), lambda bi, r, j, f: (bi, k_tile(r, j), 0)),
                  pl.BlockSpec((c, t), lambda bi, r, j, f: (0, bi * n + q_tile(r, j))),
                  pl.BlockSpec((1, ATT_HEADS, AUG, t), lambda bi, r, j, f: (bi, 0, 0, q_tile(r, j))),
                  pl.BlockSpec((ATT_HEADS * V_ROWS, t), lambda bi, r, j, f: (0, bi * n + k_tile(r, j))),
                  pl.BlockSpec((1, 1, 1, t, t), lambda bi, r, j, f: (bi, q_tile(r, j), k_tile(r, j), 0, 0))],
        out_specs=pl.BlockSpec((1, t, c), lambda bi, r, j, f: (bi, q_tile(r, j), 0)),
        scratch_shapes=_att_scratch(t))
    return pl.pallas_call(
        functools.partial(_dsa_kernel, n=n),
        grid_spec=grid_spec,
        out_shape=jax.ShapeDtypeStruct((b, s, c), BF16),
        compiler_params=_params("parallel", "parallel", "arbitrary"),
        name="dsa",
    )(fast, k, k_aug, q_t, q_aug, v_t, mask)


CONV_HALO = 8


def _conv_kernel(u_ref, gb_ref, gc_ref, hu_ref, hgc_ref, w_ref, o_ref, ext, *, ts):
    i = pl.program_id(1)
    ext[0:CONV_HALO, :] = jnp.where(i > 0, hgc_ref[0] * hu_ref[0], 0.0)
    ext[CONV_HALO:, :] = gc_ref[0] * u_ref[0]
    conv = None
    for t in range(CONV_WIDTH):
        off = CONV_HALO - (CONV_WIDTH - 1) + t
        term = ext[off:off + ts, :] * w_ref[t:t + 1, :]
        conv = term if conv is None else conv + term
    o_ref[0] = (gb_ref[0] * conv).astype(o_ref.dtype)


def _gated_conv(h, w, ts=512):
    b, s, c3 = h.shape
    c = c3 // 3
    ts = min(ts, s)
    hb = ts // CONV_HALO

    def halo_map(col):
        return lambda bi, i: (bi, jnp.maximum(i * hb - 1, 0), col)

    return pl.pallas_call(
        functools.partial(_conv_kernel, ts=ts),
        grid=(b, s // ts),
        in_specs=[pl.BlockSpec((1, ts, c), lambda bi, i: (bi, i, 0)),
                  pl.BlockSpec((1, ts, c), lambda bi, i: (bi, i, 1)),
                  pl.BlockSpec((1, ts, c), lambda bi, i: (bi, i, 2)),
                  pl.BlockSpec((1, CONV_HALO, c), halo_map(0)),
                  pl.BlockSpec((1, CONV_HALO, c), halo_map(2)),
                  pl.BlockSpec(w.shape, lambda bi, i: (0, 0))],
        out_specs=pl.BlockSpec((1, ts, c), lambda bi, i: (bi, i, 0)),
        out_shape=jax.ShapeDtypeStruct((b, s, c), BF16),
        scratch_shapes=[pltpu.VMEM((ts + CONV_HALO, c), F32)],
        compiler_params=_params("parallel", "arbitrary"),
        name="gated_conv",
    )(h, h, h, h, h, w)


GATE_CHUNK = 256


def _fgate_kernel(f_ref, fb_ref, hi_ref, mid_ref, lo_ref):
    s = f_ref.shape[1]
    r = lax.broadcasted_iota(jnp.int32, (GATE_CHUNK, GATE_CHUNK), 0)
    c = lax.broadcasted_iota(jnp.int32, (GATE_CHUNK, GATE_CHUNK), 1)
    tri = (c <= r).astype(F32)

    def body(t, carry):
        rows = pl.ds(t * GATE_CHUNK, GATE_CHUNK)
        z = f_ref[0, rows, :] + fb_ref[...]
        log_f = -(jnp.maximum(-z, 0.0) + jnp.log1p(jnp.exp(-jnp.abs(z))))
        cs = jnp.dot(tri, log_f, preferred_element_type=F32, precision=lax.Precision.HIGHEST) + carry
        b2 = cs * LOG2E
        hi = b2.astype(BF16)
        r1 = b2 - hi.astype(F32)
        mid = r1.astype(BF16)
        hi_ref[0, rows, :] = hi
        mid_ref[0, rows, :] = mid
        lo_ref[0, rows, :] = (r1 - mid.astype(F32)).astype(BF16)
        return cs[GATE_CHUNK - 1:GATE_CHUNK, :]

    lax.fori_loop(0, s // GATE_CHUNK, body, jnp.zeros((1, LANES), F32))


def _forget_cumsum(f, fb):
    b, s, c = f.shape
    spec = pl.BlockSpec((1, s, c), lambda bi: (bi, 0, 0))
    return pl.pallas_call(
        _fgate_kernel,
        grid=(b,),
        in_specs=[spec, pl.BlockSpec((1, c), lambda bi: (0, 0))],
        out_specs=[spec, spec, spec],
        out_shape=[jax.ShapeDtypeStruct((b, s, c), BF16)] * 3,
        compiler_params=_params("parallel"),
        name="forget_cumsum",
    )(f, fb)


def _fox_kernel(first_ref, fast_ref, k_ref, ka_ref, qt_ref, qa_ref, vt_ref, o_ref, m_sc, acc_sc, qfull, *, t):
    b, i, j = pl.program_id(0), pl.program_id(1), pl.program_id(2)
    steps = i - first_ref[b, i]
    fast = fast_ref[b, i] == 1

    def causal():
        return lax.broadcasted_iota(jnp.int32, (t, t), 0) <= lax.broadcasted_iota(jnp.int32, (t, t), 1)

    def scores(diagonal):
        ka = ka_ref[0]

        def fn(h):
            s = _scores(k_ref, ka, qfull, h)
            return jnp.where(causal(), s, NEG) if diagonal else s
        return fn

    def fast_tile(diagonal):
        _for_heads(scores(diagonal), lambda s, h: _fast_step(s, None, vt_ref[_vrows(h), :], h, acc_sc))

    def exact_tile(diagonal):
        _for_heads(scores(diagonal), lambda s, h: _exact_step(s, vt_ref[_vrows(h), :], h, m_sc, acc_sc))

    @pl.when(j == 0)
    def _():
        _att_init(qt_ref, qa_ref, m_sc, acc_sc, qfull)

    @pl.when((j == 0) & fast)
    def _():
        fast_tile(True)

    @pl.when((j == 0) & jnp.logical_not(fast))
    def _():
        exact_tile(True)

    @pl.when((j > 0) & (j <= steps) & fast)
    def _():
        fast_tile(False)

    @pl.when((j > 0) & (j <= steps) & jnp.logical_not(fast))
    def _():
        exact_tile(False)

    @pl.when(j == steps)
    def _():
        _att_finish(o_ref, acc_sc)


def _fox(first, fast, k, k_aug, q_t, q_aug, v_t, t):
    b, s, c = k.shape
    n = s // t

    def key_tile(bi, i, j, first_ref):
        return jnp.maximum(i - j, first_ref[bi, i])

    grid_spec = pltpu.PrefetchScalarGridSpec(
        num_scalar_prefetch=2,
        grid=(b, n, n),
        in_specs=[pl.BlockSpec((1, t, c), lambda bi, i, j, f, g: (bi, key_tile(bi, i, j, f), 0)),
                  pl.BlockSpec((1, t, LANES), lambda bi, i, j, f, g: (bi, key_tile(bi, i, j, f), 0)),
                  pl.BlockSpec((c, t), lambda bi, i, j, f, g: (0, bi * n + i)),
                  pl.BlockSpec((1, ATT_HEADS, AUG, t), lambda bi, i, j, f, g: (bi, 0, 0, i)),
                  pl.BlockSpec((ATT_HEADS * V_ROWS, t), lambda bi, i, j, f, g: (0, bi * n + key_tile(bi, i, j, f)))],
        out_specs=pl.BlockSpec((1, t, c), lambda bi, i, j, f, g: (bi, i, 0)),
        scratch_shapes=_att_scratch(t))
    return pl.pallas_call(
        functools.partial(_fox_kernel, t=t),
        grid_spec=grid_spec,
        out_shape=jax.ShapeDtypeStruct((b, s, c), BF16),
        compiler_params=_params("parallel", "parallel", "arbitrary"),
        name="fox",
    )(first, fast, k, k_aug, q_t, q_aug, v_t)


def _xattn_kernel(xb_ref, xf_ref, wq_ref, kt_ref, v_ref, wo_ref, g_ref, b_ref, of_ref, ob_ref):
    q = jnp.dot(xb_ref[...], wq_ref[...], preferred_element_type=F32) * Q_SCALE
    q = q.astype(BF16)
    outs = []
    for h in range(XA_HEADS):
        s = jnp.dot(q[:, _head(h)], kt_ref[0, _head(h), :], preferred_element_type=F32)
        p = jnp.exp2(s - jnp.max(s, axis=1, keepdims=True))
        l = jnp.sum(p, axis=1, keepdims=True)
        pv = jnp.dot(p.astype(BF16), v_ref[0, :, _head(h)], preferred_element_type=F32)
        outs.append((pv / l).astype(BF16))
    o = jnp.concatenate(outs, axis=1)
    y = jnp.dot(o, wo_ref[...], preferred_element_type=F32)
    y = _layer_norm(ALPHA * xf_ref[...] + y, g_ref[...], b_ref[...])
    of_ref[...] = y
    ob_ref[...] = y.astype(BF16)


def _xattn(xb, xf, wq, kt, v, wo, g, b, seq, tm=256):
    n, d = xf.shape
    per_batch = seq // tm
    m = v.shape[1]
    return pl.pallas_call(
        _xattn_kernel,
        grid=(n // tm,),
        in_specs=[pl.BlockSpec((tm, d), lambda i: (i, 0)),
                  pl.BlockSpec((tm, d), lambda i: (i, 0)),
                  pl.BlockSpec(wq.shape, lambda i: (0, 0)),
                  pl.BlockSpec((1, XA_WIDTH, m), lambda i: (i // per_batch, 0, 0)),
                  pl.BlockSpec((1, m, XA_WIDTH), lambda i: (i // per_batch, 0, 0)),
                  pl.BlockSpec(wo.shape, lambda i: (0, 0)),
                  pl.BlockSpec((1, d), lambda i: (0, 0)),
                  pl.BlockSpec((1, d), lambda i: (0, 0))],
        out_specs=[pl.BlockSpec((tm, d), lambda i: (i, 0)), pl.BlockSpec((tm, d), lambda i: (i, 0))],
        out_shape=[jax.ShapeDtypeStruct((n, d), F32), jax.ShapeDtypeStruct((n, d), BF16)],
        compiler_params=_params("parallel"),
        name="xattn",
    )(xb, xf, wq, kt, v, wo, g, b)


def _ffn_kernel(xb_ref, wg_ref, wu_ref, wo_ref, xf_ref, g_ref, b_ref, of_ref, ob_ref, ot_ref, acc):
    j = pl.program_id(1)

    @pl.when(j == 0)
    def _():
        acc[...] = jnp.zeros(acc.shape, F32)

    xb = xb_ref[...]
    gate = jnp.dot(xb, wg_ref[...], preferred_element_type=F32)
    up = jnp.dot(xb, wu_ref[...], preferred_element_type=F32)
    hid = gate * (1.0 / (1.0 + jnp.exp(-gate))) * up
    acc[...] += jnp.dot(hid.astype(BF16), wo_ref[...], preferred_element_type=F32)

    @pl.when(j == pl.num_programs(1) - 1)
    def _():
        y = _layer_norm(ALPHA * xf_ref[...] + acc[...], g_ref[...], b_ref[...])
        of_ref[...] = y
        ob_ref[...] = y.astype(BF16)
        ot_ref[...] = y.T.astype(BF16)


def _ffn(xb, xf, w_in, w_out, g, b, tm=512, th=512):
    n, d = xf.shape
    hidden = w_out.shape[0]
    nh = hidden // th
    return pl.pallas_call(
        _ffn_kernel,
        grid=(n // tm, nh),
        in_specs=[pl.BlockSpec((tm, d), lambda i, j: (i, 0)),
                  pl.BlockSpec((d, th), lambda i, j: (0, j)),
                  pl.BlockSpec((d, th), lambda i, j: (0, j + nh)),
                  pl.BlockSpec((th, d), lambda i, j: (j, 0)),
                  pl.BlockSpec((tm, d), lambda i, j: (i, 0)),
                  pl.BlockSpec((1, d), lambda i, j: (0, 0)),
                  pl.BlockSpec((1, d), lambda i, j: (0, 0))],
        out_specs=[pl.BlockSpec((tm, d), lambda i, j: (i, 0)),
                   pl.BlockSpec((tm, d), lambda i, j: (i, 0)),
                   pl.BlockSpec((d, tm), lambda i, j: (0, i))],
        out_shape=[jax.ShapeDtypeStruct((n, d), F32), jax.ShapeDtypeStruct((n, d), BF16),
                   jax.ShapeDtypeStruct((d, n), BF16)],
        scratch_shapes=[pltpu.VMEM((tm, d), F32)],
        compiler_params=_params("parallel", "arbitrary"),
        name="ffn",
    )(xb, w_in, w_in, w_out, xf, g, b)


def _rotary_tables(positions, dh):
    rot = dh // ROPE_FRACTION
    half = rot // 2
    inv_freq = jnp.power(ROPE_THETA, -(jnp.arange(half, dtype=F32) * 2.0 / rot))
    ang = positions.astype(F32)[..., None] * inv_freq
    cos, sin = jnp.cos(ang), jnp.sin(ang)
    zh = jnp.zeros_like(sin)
    rest = jnp.zeros(ang.shape[:-1] + (dh - rot,), F32)
    c = jnp.concatenate([cos, cos, rest + 1.0], axis=-1)
    s_lo = jnp.concatenate([zh, sin, rest], axis=-1)
    s_hi = jnp.concatenate([-sin, zh, rest], axis=-1)
    return tuple(t.reshape(-1, dh) for t in (c, s_lo, s_hi))


def _even_tables(positions):
    n = positions.size
    head = _rotary_tables(positions, HEAD_DIM)
    idx = _rotary_tables(positions, IDX_DIM)
    idx_full = tuple(jnp.tile(t, (1, LANES // IDX_DIM)) for t in idx)
    wi_scale = jnp.concatenate([jnp.full((IDX_HEADS,), IDX_HEADS ** -0.5, F32),
                                jnp.ones((LANES - IDX_DIM - IDX_HEADS,), F32)])
    pad = jnp.zeros((n, LANES - IDX_DIM), F32)
    tail = (jnp.concatenate([idx[0], pad + wi_scale], axis=1),
            jnp.concatenate([idx[1], pad], axis=1),
            jnp.concatenate([idx[2], pad], axis=1))
    return {"k": head,
            "q_t": tuple((t * Q_SCALE).T for t in head),
            "qi_t": tuple((t * IDX_DIM ** -0.5).T for t in idx_full),
            "tail": tail}


def _att_bounds(qn2, kn2, bsz, seq, t):
    n = seq // t
    qn = jnp.sqrt(qn2).reshape(ATT_HEADS, bsz, seq).transpose(1, 2, 0)
    kn = jnp.sqrt(kn2).reshape(bsz, seq, ATT_HEADS)
    k_max = jnp.max(kn, axis=1, keepdims=True)
    neg_m = -(1.01 * qn * k_max + 1.0)
    qn_tile = jnp.max(qn.reshape(bsz, n, t, ATT_HEADS), axis=2)
    kn_tile = jnp.max(kn.reshape(bsz, n, t, ATT_HEADS), axis=2)
    spread = jnp.max(2.05 * qn_tile * k_max, axis=-1) + 8.0
    fast = (spread <= FAST_RANGE_LOG2).astype(jnp.int32)
    return neg_m.astype(BF16), qn_tile, kn_tile, fast


def _aug_operands(q_entries, k_entries, bsz, seq):
    shape = (bsz, seq, ATT_HEADS)
    one, zero = jnp.ones(shape, BF16), jnp.zeros(shape, BF16)

    def pack(entries):
        cols = [one if e is None else e for e in entries]
        return jnp.stack(cols + [zero] * (AUG - len(cols)), axis=-1)

    return pack(k_entries).reshape(bsz, seq, ATT_HEADS * AUG), pack(q_entries).transpose(0, 2, 3, 1)


def _even_mixer(xb, xt, tables, w_in, pool_w, pool_scale, bsz, seq, tq):
    n = xb.shape[0]
    hw = HALF_WIDTH
    wb = w_in.astype(BF16)
    wt = wb.T
    u = _proj(xb, wb, F32, 0, hw)
    q_t, qn2 = _proj_t(wt, xt, BF16, hw, hw, tables["q_t"], shift=HEAD_DIM // 8, norms=True)
    k, kn2 = _proj(xb, wb, BF16, 2 * hw, hw, tables["k"], shift=HEAD_DIM // 8, norms=True)
    v_t = _proj_t(wt, xt, BF16, 3 * hw, hw, value_rows=True)
    qi_t = _proj_t(wt, xt, BF16, 4 * hw, hw, tables["qi_t"], shift=IDX_DIM // 8)
    w_tail = jnp.pad(wb[:, 5 * hw:], ((0, 0), (0, LANES - IDX_DIM - IDX_HEADS)))
    tail = _proj(xb, w_tail, F32, 0, LANES, tables["tail"], shift=IDX_DIM // 8)

    ki = tail[:, :IDX_DIM].astype(BF16).reshape(bsz, seq, IDX_DIM)
    wi_t = tail[:, IDX_DIM:IDX_DIM + IDX_HEADS].T

    a = _pool(u.reshape(bsz, seq, hw), pool_w.astype(BF16), pool_scale.reshape(1, hw))
    mask = _idx_mask(qi_t, wi_t, ki, tq, tq)
    neg_m, _, _, fast = _att_bounds(qn2, kn2, bsz, seq, tq)
    k_aug, q_aug = _aug_operands([neg_m], [None], bsz, seq)
    bb = _dsa(fast, k.reshape(bsz, seq, hw), k_aug, q_t, q_aug, v_t, mask, tq)
    return a.reshape(n, hw), bb.reshape(n, hw)


DSA_TILE = 512
FOX_TILE = 512
UNDERFLOW_LOG2 = 160.0


def _fox_first_tile(qn, kn, terms, bsz, seq, t):
    n = seq // t
    f32sum = sum(x.astype(F32) for x in terms)[:, :, :ATT_HEADS].reshape(bsz, n, t, ATT_HEADS)
    f_first, f_last = f32sum[:, :, 0], f32sum[:, :, t - 1]
    bound = (1.01 * qn[:, :, None] * (kn[:, None, :] + kn[:, :, None])
             + f_first[:, :, None] - f_last[:, None, :] + 1.0)
    tiles = jnp.arange(n, dtype=jnp.int32)
    needed = jnp.any(bound >= -UNDERFLOW_LOG2, axis=-1) | (tiles[:, None] == tiles[None, :])
    needed = needed & (tiles[None, :] <= tiles[:, None])
    return jnp.min(jnp.where(needed, tiles[None, None, :], n), axis=-1).astype(jnp.int32)


def _odd_mixer(xb, xt, w_in, conv_w, forget_b, bsz, seq):
    n = xb.shape[0]
    hw = HALF_WIDTH
    wb = w_in.astype(BF16)
    wt = wb.T
    ugg = _proj(xb, wb, F32, 0, 3 * hw)
    q_t, qn2 = _proj_t(wt, xt, BF16, 3 * hw, hw, scale=Q_SCALE, norms=True)
    k, kn2 = _proj(xb, wb, BF16, 4 * hw, hw, norms=True)
    v_t = _proj_t(wt, xt, BF16, 5 * hw, hw, value_rows=True)
    w_tail = jnp.pad(wb[:, 6 * hw:], ((0, 0), (0, LANES - ATT_HEADS)))
    f = _proj(xb, w_tail, F32, 0, LANES)
    fb = jnp.pad(forget_b, (0, LANES - ATT_HEADS)).reshape(1, LANES)
    terms = _forget_cumsum(f.reshape(bsz, seq, LANES), fb)
    hi, mid, lo = (x[:, :, :ATT_HEADS] for x in terms)

    c = _gated_conv(ugg.reshape(bsz, seq, 3 * hw), conv_w)
    t = min(FOX_TILE, seq)
    neg_m, qn, kn, fast = _att_bounds(qn2, kn2, bsz, seq, t)
    k_aug, q_aug = _aug_operands([neg_m, None, None, None, hi, mid, lo],
                                 [None, -hi, -mid, -lo, None, None, None], bsz, seq)
    first = _fox_first_tile(qn, kn, terms, bsz, seq, t)
    d = _fox(first, fast, k.reshape(bsz, seq, hw), k_aug, q_t, q_aug, v_t, t)
    return c.reshape(n, hw), d.reshape(n, hw)


def kernel(x, mem, positions, ev_w_in, ev_pool_w, ev_pool_scale, ev_w_out, od_w_in, od_conv_w, od_forget_b,
           od_w_out, ca_w_q, ca_w_kv, ca_w_o, ffn_w_in, ffn_w_out, ln_g, ln_b):
    bsz, seq, d = x.shape
    n = bsz * seq
    m = mem.shape[1]
    tq = min(DSA_TILE, seq)
    xf = x.reshape(n, d)
    xb = xf.astype(BF16)
    memb = mem.reshape(bsz * m, d).astype(BF16)
    tables = _even_tables(positions)
    xt = xb.T
    for i in range(DEPTH):
        j = i // 2
        if i % 2 == 0:
            parts = _even_mixer(xb, xt, tables, ev_w_in[j], ev_pool_w[j], ev_pool_scale[j], bsz, seq, tq)
            w_out = ev_w_out[j]
        else:
            parts = _odd_mixer(xb, xt, od_w_in[j], od_conv_w[j], od_forget_b[j], bsz, seq)
            w_out = od_w_out[j]
        g, b = ln_g[i].reshape(3, 1, d), ln_b[i].reshape(3, 1, d)
        xf, xb = _out_ln(list(parts), w_out.astype(BF16), xf, g[0], b[0])

        kv = _proj(memb, ca_w_kv[i].astype(BF16), BF16, 0, 2 * XA_WIDTH).reshape(bsz, m, 2 * XA_WIDTH)
        kt = kv[:, :, :XA_WIDTH].transpose(0, 2, 1)
        xf, xb = _xattn(xb, xf, ca_w_q[i].astype(BF16), kt, kv[:, :, XA_WIDTH:], ca_w_o[i].astype(BF16),
                        g[1], b[1], seq)

        xf, xb, xt = _ffn(xb, xf, ffn_w_in[i].astype(BF16), ffn_w_out[i].astype(BF16), g[2], b[2])
    return xf.reshape(bsz, seq, d)
```

```python
import functools

import jax
import jax.numpy as jnp
from jax import lax
from jax.experimental import pallas as pl
from jax.experimental.pallas import tpu as pltpu

F32 = jnp.float32
BF16 = jnp.bfloat16

D_MODEL = 2048
DEPTH = 4
CHUNK = 64
HEAD_DIM = 128
HALF_WIDTH = D_MODEL // 2
POOL_WINDOWS = (2, 4, 8, 16)
POOL_GROUP_DIM = HALF_WIDTH // len(POOL_WINDOWS)
ATT_HEADS = HALF_WIDTH // HEAD_DIM
IDX_HEADS = 16
IDX_DIM = 64
DSA_TOPK_MAX = 256
CONV_WIDTH = 3
XA_HEADS = 4
XA_WIDTH = XA_HEADS * HEAD_DIM
FFN_HIDDEN = -(-(8 * D_MODEL) // (3 * 256)) * 256
ROPE_THETA = 500000.0
ROPE_FRACTION = 4
LN_EPS = 1e-5
ALPHA = (2 * DEPTH) ** 0.25
LOG2E = 1.4426950408889634
Q_SCALE = HEAD_DIM ** -0.5 * LOG2E

LANES = 128
SUBLANES = 8
PACKED_ROWS = 16
NEG = -1e30
INT_MIN = -(2 ** 31)
VMEM_LIMIT = 60000 * 1024


def _params(*semantics):
    return pltpu.CompilerParams(dimension_semantics=semantics, vmem_limit_bytes=VMEM_LIMIT)


def _layer_norm(y, g, b):
    mu = jnp.mean(y, axis=-1, keepdims=True)
    d = y - mu
    var = jnp.mean(d * d, axis=-1, keepdims=True)
    return d * lax.rsqrt(var + LN_EPS) * g + b


AUG = 16
V_ROWS = HEAD_DIM + AUG


def _proj_kernel(*refs, shift, scale, axis, norms, value_rows):
    refs = list(refs)
    a_ref, b_ref = refs[0], refs[1]
    n_ref = refs.pop() if norms else None
    o_ref = refs.pop()
    h = jnp.dot(a_ref[...], b_ref[...], preferred_element_type=F32)
    if not (shift or norms or value_rows):
        o_ref[...] = (h * scale if scale != 1.0 else h).astype(o_ref.dtype)
        return
    if shift:
        c, s1, s2 = refs[2][...], refs[3][...], refs[4][...]
    for g in range(h.shape[axis] // LANES):
        sl = (slice(None),) * axis + (slice(g * LANES, (g + 1) * LANES),)
        hg = h[sl]
        if shift:
            hg = hg * c + pltpu.roll(hg, shift, axis) * s1 + pltpu.roll(hg, LANES - shift, axis) * s2
        elif scale != 1.0:
            hg = hg * scale
        og = hg.astype(o_ref.dtype)
        if value_rows:
            tm = og.shape[1]
            o_ref[g * V_ROWS:g * V_ROWS + LANES, :] = og
            row = lax.broadcasted_iota(jnp.int32, (AUG, tm), 0)
            o_ref[g * V_ROWS + LANES:(g + 1) * V_ROWS, :] = jnp.where(row == 0, 1.0, 0.0).astype(o_ref.dtype)
        else:
            o_ref[sl] = og
        if norms:
            sq = jnp.square(og.astype(F32))
            if axis == 0:
                n_ref[g:g + 1, :] = jnp.sum(sq, axis=0, keepdims=True)
            else:
                n_ref[:, g:g + 1] = jnp.sum(sq, axis=1, keepdims=True)


def _proj(x, w, layer, out_dtype, col0, ncols, tables=None, shift=0, scale=1.0, norms=False, tm=1024, tn=1024):
    n, k = x.shape
    tm, tn = min(tm, n), min(tn, ncols)
    assert col0 % tn == 0 and ncols % tn == 0
    in_specs = [pl.BlockSpec((tm, k), lambda i, j: (i, 0)),
                pl.BlockSpec((None, k, tn), lambda i, j: (layer, 0, col0 // tn + j))]
    args = [x, w]
    if shift:
        in_specs += [pl.BlockSpec((tm, LANES), lambda i, j: (i, 0))] * 3
        args += list(tables)
    out_specs = [pl.BlockSpec((tm, tn), lambda i, j: (i, j))]
    out_shape = [jax.ShapeDtypeStruct((n, ncols), out_dtype)]
    if norms:
        assert tn == ncols
        out_specs.append(pl.BlockSpec((tm, tn // LANES), lambda i, j: (i, 0)))
        out_shape.append(jax.ShapeDtypeStruct((n, ncols // LANES), F32))
    out = pl.pallas_call(
        functools.partial(_proj_kernel, shift=shift, scale=scale, axis=1, norms=norms, value_rows=False),
        grid=(n // tm, ncols // tn),
        in_specs=in_specs,
        out_specs=out_specs,
        out_shape=out_shape,
        compiler_params=_params("parallel", "arbitrary"),
        name="proj",
    )(*args)
    return out if norms else out[0]


def _proj_t(wt, layer, xt, out_dtype, row0, nrows, tables=None, shift=0, scale=1.0, norms=False,
            value_rows=False, tm=1024, tn=1024):
    k, n = xt.shape
    tm, tn = min(tm, n), min(tn, nrows)
    assert row0 % tn == 0 and nrows % tn == 0
    in_specs = [pl.BlockSpec((None, tn, k), lambda i, j: (layer, row0 // tn + j, 0)),
                pl.BlockSpec((k, tm), lambda i, j: (0, i))]
    args = [wt, xt]
    if shift:
        in_specs += [pl.BlockSpec((LANES, tm), lambda i, j: (0, i))] * 3
        args += list(tables)
    rows_out = tn // LANES * V_ROWS if value_rows else tn
    out_specs = [pl.BlockSpec((rows_out, tm), lambda i, j: (j, i))]
    out_shape = [jax.ShapeDtypeStruct((nrows // tn * rows_out, n), out_dtype)]
    if norms:
        assert tn == nrows
        out_specs.append(pl.BlockSpec((tn // LANES, tm), lambda i, j: (0, i)))
        out_shape.append(jax.ShapeDtypeStruct((nrows // LANES, n), F32))
    out = pl.pallas_call(
        functools.partial(_proj_kernel, shift=shift, scale=scale, axis=0, norms=norms, value_rows=value_rows),
        grid=(n // tm, nrows // tn),
        in_specs=in_specs,
        out_specs=out_specs,
        out_shape=out_shape,
        compiler_params=_params("parallel", "arbitrary"),
        name="proj_t",
    )(*args)
    return out if norms else out[0]


def _out_ln_kernel(*refs, nparts):
    parts = refs[:nparts]
    w_ref, x_ref, g_ref, b_ref, of_ref, ob_ref = refs[nparts:]
    acc = None
    off = 0
    for p in parts:
        kp = p.shape[1]
        t = jnp.dot(p[...], w_ref[off:off + kp, :], preferred_element_type=F32)
        acc = t if acc is None else acc + t
        off += kp
    y = _layer_norm(ALPHA * x_ref[...] + acc, g_ref[...], b_ref[...])
    of_ref[...] = y
    ob_ref[...] = y.astype(BF16)


def _out_ln(parts, w, layer, x, g, b, tm=256):
    n, d = x.shape
    in_specs = [pl.BlockSpec((tm, p.shape[1]), lambda i: (i, 0)) for p in parts]
    in_specs += [pl.BlockSpec((None,) + w.shape[1:], lambda i: (layer, 0, 0)),
                 pl.BlockSpec((tm, d), lambda i: (i, 0)),
                 pl.BlockSpec((1, d), lambda i: (0, 0)),
                 pl.BlockSpec((1, d), lambda i: (0, 0))]
    return pl.pallas_call(
        functools.partial(_out_ln_kernel, nparts=len(parts)),
        grid=(n // tm,),
        in_specs=in_specs,
        out_specs=[pl.BlockSpec((tm, d), lambda i: (i, 0)), pl.BlockSpec((tm, d), lambda i: (i, 0))],
        out_shape=[jax.ShapeDtypeStruct((n, d), F32), jax.ShapeDtypeStruct((n, d), BF16)],
        compiler_params=_params("parallel"),
        name="out_ln",
    )(*parts, w, x, g, b)


POOL_HALO = 16


def _pool_kernel(u_ref, halo_ref, w_ref, sc_ref, o_ref, ext, *, ts):
    i = pl.program_id(1)
    ext[0:POOL_HALO, :] = jnp.where(i > 0, halo_ref[0], 0.0)
    ext[POOL_HALO:, :] = u_ref[0]
    cnt = i * ts + lax.broadcasted_iota(jnp.int32, (ts, 1), 0) + 1
    for g, win in enumerate(POOL_WINDOWS):
        lo, hi = g * POOL_GROUP_DIM, (g + 1) * POOL_GROUP_DIM
        cur = ext[POOL_HALO:POOL_HALO + ts, lo:hi]
        s = cur
        for j in range(1, win):
            s = s + ext[POOL_HALO - j:POOL_HALO - j + ts, lo:hi]
        d = s / jnp.minimum(cnt, win).astype(F32) - cur
        y = jnp.dot(d.astype(BF16), w_ref[g], preferred_element_type=F32)
        o_ref[0, :, lo:hi] = (y * sc_ref[:, lo:hi]).astype(o_ref.dtype)


def _pool(u, w, layer, scale, ts=512):
    b, s, c = u.shape
    ts = min(ts, s)
    hb = ts // POOL_HALO
    return pl.pallas_call(
        functools.partial(_pool_kernel, ts=ts),
        grid=(b, s // ts),
        in_specs=[pl.BlockSpec((1, ts, c), lambda bi, i: (bi, i, 0)),
                  pl.BlockSpec((1, POOL_HALO, c), lambda bi, i: (bi, jnp.maximum(i * hb - 1, 0), 0)),
                  pl.BlockSpec((None,) + w.shape[1:], lambda bi, i: (layer, 0, 0, 0)),
                  pl.BlockSpec((1, c), lambda bi, i: (0, 0))],
        out_specs=pl.BlockSpec((1, ts, c), lambda bi, i: (bi, i, 0)),
        out_shape=jax.ShapeDtypeStruct((b, s, c), BF16),
        scratch_shapes=[pltpu.VMEM((ts + POOL_HALO, c), F32)],
        compiler_params=_params("parallel", "arbitrary"),
        name="pool",
    )(u, u, w, scale)


def _sortable(v):
    bits = lax.bitcast_convert_type(v, jnp.int32)
    return bits ^ ((bits >> 31) & 0x7FFFFFFF)


def _colsum8(v):
    tk, tq = v.shape
    return v.reshape(tk // SUBLANES, SUBLANES, tq).sum(axis=0)


def _idx_kernel(qi_ref, wi_ref, ki_ref, o_ref, keys, half, *, tq, tk, nk, topk):
    i = pl.program_id(1)
    nact = (i * tq + tq - 1) // tk + 1
    q_chunk = (i * tq + lax.broadcasted_iota(jnp.int32, (1, tq), 1)) // CHUNK

    def score_tile(c, carry):
        kt = ki_ref[0, pl.ds(pl.multiple_of(c * tk, tk), tk), :]
        acc = jnp.zeros((tk, tq), F32)
        for h in range(IDX_HEADS):
            il = jnp.dot(kt, qi_ref[h * IDX_DIM:(h + 1) * IDX_DIM, :], preferred_element_type=F32)
            acc = acc + jnp.maximum(il, 0.0) * wi_ref[h:h + 1, :]
        k_chunk = (c * tk + lax.broadcasted_iota(jnp.int32, (tk, 1), 0)) // CHUNK
        kk = jnp.where(k_chunk <= q_chunk, _sortable(acc), INT_MIN)
        keys[c] = kk
        half[c] = (kk >> 16).astype(jnp.int16)
        return carry

    lax.fori_loop(0, nact, score_tile, 0)

    def count(pred):
        def body(c, acc):
            return acc + _colsum8(jnp.where(pred(keys[c], c), 1, 0))
        acc = lax.fori_loop(0, nact, body, jnp.zeros((SUBLANES, tq), jnp.int32))
        return jnp.sum(acc, axis=0, keepdims=True)

    def count16(cand, strict):
        cand16 = jnp.broadcast_to(cand, (PACKED_ROWS, tq)).astype(jnp.int16)[None]

        def body(c, acc):
            h3 = half[c].reshape(tk // PACKED_ROWS, PACKED_ROWS, tq)
            hit = (h3 > cand16) if strict else (h3 >= cand16)
            ones = jnp.where(hit, jnp.int16(1), jnp.int16(0))
            for r in range(tk // PACKED_ROWS):
                acc = acc + ones[r]
            return acc

        acc = lax.fori_loop(0, nact, body, jnp.zeros((PACKED_ROWS, tq), jnp.int16))
        return jnp.sum(acc.astype(jnp.int32), axis=0, keepdims=True)

    hi = jnp.where(count16(jnp.zeros((1, tq), jnp.int32), False) >= topk, 0, -(2 ** 15))

    def hi_step(b, hi):
        cand = hi | (1 << (14 - b))
        return jnp.where(count16(cand, False) >= topk, cand, hi)

    hi = lax.fori_loop(0, 15, hi_step, hi)
    n_above = count16(hi, True)

    def low_tile(c, carry):
        kk = keys[c]
        half[c] = jnp.where((kk >> 16) == hi, (kk & 0xFFFF) - 2 ** 15, -(2 ** 15)).astype(jnp.int16)
        return carry

    lax.fori_loop(0, nact, low_tile, 0)

    def lo_step(b, lo):
        cand = lo | (1 << (15 - b))
        return jnp.where(n_above + count16(cand - 2 ** 15, False) >= topk, cand, lo)

    lo = lax.fori_loop(0, 16, lo_step, jnp.zeros((1, tq), jnp.int32))
    thr = (hi << 16) | lo
    thr = jnp.maximum(thr, INT_MIN + 1)
    n_ge = count(lambda kk, c: kk >= thr)

    def write_tiles(select):
        def body(c, carry):
            o_ref[0, 0, c] = jnp.where(select(keys[c], c), 1.0, 0.0).astype(o_ref.dtype)
            return carry
        lax.fori_loop(0, nact, body, 0)

    has_ties = jnp.max(n_ge) > topk

    @pl.when(jnp.logical_not(has_ties))
    def _():
        write_tiles(lambda kk, c: kk >= thr)

    @pl.when(has_ties)
    def _():
        need = topk - count(lambda kk, c: kk > thr)

        def key_index(c):
            return c * tk + lax.broadcasted_iota(jnp.int32, (tk, 1), 0)

        def idx_step(b, cut):
            cand = cut | (1 << (30 - b))
            below = count(lambda kk, c: (kk == thr) & (key_index(c) < cand))
            return jnp.where(below < need, cand, cut)

        cut = lax.fori_loop(0, 31, idx_step, jnp.zeros((1, tq), jnp.int32))
        write_tiles(lambda kk, c: (kk > thr) | ((kk == thr) & (key_index(c) <= cut)))

    def fill_tile(c, carry):
        o_ref[0, 0, c] = jnp.zeros((tk, tq), o_ref.dtype)
        return carry

    lax.fori_loop(nact, nk, fill_tile, 0)


def _idx_mask(qi_t, wi_t, ki, tq, tk):
    b, s, _ = ki.shape
    nq, nk = s // tq, s // tk
    topk = min(DSA_TOPK_MAX, s // 4)
    return pl.pallas_call(
        functools.partial(_idx_kernel, tq=tq, tk=tk, nk=nk, topk=topk),
        grid=(b, nq),
        in_specs=[pl.BlockSpec((IDX_HEADS * IDX_DIM, tq), lambda bi, i: (0, bi * nq + i)),
                  pl.BlockSpec((IDX_HEADS, tq), lambda bi, i: (0, bi * nq + i)),
                  pl.BlockSpec((1, s, IDX_DIM), lambda bi, i: (bi, 0, 0))],
        out_specs=pl.BlockSpec((1, 1, nk, tk, tq), lambda bi, i: (bi, i, 0, 0, 0)),
        out_shape=jax.ShapeDtypeStruct((b, nq, nk, tk, tq), BF16),
        scratch_shapes=[pltpu.VMEM((nk, tk, tq), jnp.int32), pltpu.VMEM((nk, tk, tq), jnp.int16)],
        compiler_params=_params("parallel", "arbitrary"),
        name="idx_mask",
    )(qi_t, wi_t, ki)


FAST_RANGE_LOG2 = 120.0


def _head(h):
    return slice(h * HEAD_DIM, (h + 1) * HEAD_DIM)


def _vrows(h):
    return slice(h * V_ROWS, (h + 1) * V_ROWS)


def _att_init(qt_ref, qa_ref, m_sc, acc_sc, qfull):
    m_sc[...] = jnp.full(m_sc.shape, NEG, F32)
    acc_sc[...] = jnp.zeros(acc_sc.shape, F32)
    tq = qfull.shape[2]
    for h in range(ATT_HEADS):
        qfull[h, 0:HEAD_DIM, :] = qt_ref[_head(h), :]
        qfull[h, HEAD_DIM:, :] = jnp.zeros((LANES, tq), BF16)
        lo = HEAD_DIM + h * AUG
        qfull[h, lo:lo + AUG, :] = qa_ref[0, h]


def _scores(k_ref, ka, qfull, h):
    kfull = jnp.concatenate([k_ref[0, :, _head(h)], ka], axis=1)
    return jnp.dot(kfull, qfull[h], preferred_element_type=F32)


def _for_heads(score_fn, step_fn):
    s_next = score_fn(0)
    for h in range(ATT_HEADS):
        s = s_next
        if h + 1 < ATT_HEADS:
            s_next = score_fn(h + 1)
        step_fn(s, h)


def _fast_step(s, keep, vt, h, acc_sc):
    p = jnp.exp2(s).astype(BF16)
    if keep is not None:
        p = p * keep
    acc_sc[h] += jnp.dot(vt, p, preferred_element_type=F32)


def _exact_step(s, vt, h, m_sc, acc_sc):
    m_prev = m_sc[h]
    m_new = jnp.maximum(m_prev, jnp.max(s, axis=0, keepdims=True))
    alpha = jnp.exp2(m_prev - m_new)
    p = jnp.exp2(s - m_new)
    acc_sc[h] = alpha * acc_sc[h] + jnp.dot(vt, p.astype(BF16), preferred_element_type=F32)
    m_sc[h] = m_new


def _att_finish(o_ref, acc_sc):
    for h in range(ATT_HEADS):
        a = acc_sc[h]
        o = (a[0:HEAD_DIM] / a[HEAD_DIM:HEAD_DIM + 1]).T
        o_ref[0, :, _head(h)] = o.astype(o_ref.dtype)


def _att_scratch(tq):
    return [pltpu.VMEM((ATT_HEADS, 1, tq), F32),
            pltpu.VMEM((ATT_HEADS, V_ROWS, tq), F32),
            pltpu.VMEM((ATT_HEADS, 2 * HEAD_DIM, tq), BF16)]


def _pair_tiles(r, j, n):
    second = j > r
    return jnp.where(second, n - 1 - r, r), jnp.where(second, j - r - 1, j)


def _dsa_kernel(fast_ref, k_ref, ka_ref, qt_ref, qa_ref, vt_ref, mask_ref, o_ref, m_sc, acc_sc, qfull, *, n):
    b = pl.program_id(0)
    i, j = _pair_tiles(pl.program_id(1), pl.program_id(2), n)
    fast = fast_ref[b, i] == 1

    @pl.when(j == 0)
    def _():
        _att_init(qt_ref, qa_ref, m_sc, acc_sc, qfull)

    @pl.when(fast)
    def _():
        keep = mask_ref[0, 0, 0]
        ka = ka_ref[0]
        _for_heads(lambda h: _scores(k_ref, ka, qfull, h),
                   lambda s, h: _fast_step(s, keep, vt_ref[_vrows(h), :], h, acc_sc))

    @pl.when(jnp.logical_not(fast))
    def _():
        bias = (mask_ref[0, 0, 0].astype(F32) - 1.0) * (-NEG)
        ka = ka_ref[0]
        _for_heads(lambda h: _scores(k_ref, ka, qfull, h) + bias,
                   lambda s, h: _exact_step(s, vt_ref[_vrows(h), :], h, m_sc, acc_sc))

    @pl.when(j == i)
    def _():
        _att_finish(o_ref, acc_sc)


def _dsa(fast, k, k_aug, q_t, q_aug, v_t, mask, t):
    b, s, c = k.shape
    n = s // t
    assert n % 2 == 0, "query tiles are processed in pairs"

    def q_tile(r, j):
        return _pair_tiles(r, j, n)[0]

    def k_tile(r, j):
        return _pair_tiles(r, j, n)[1]

    grid_spec = pltpu.PrefetchScalarGridSpec(
        num_scalar_prefetch=1,
        grid=(b, n // 2, n + 1),
        in_specs=[pl.BlockSpec((1, t, c), lambda bi, r, j, f: (bi, k_tile(r, j), 0)),
                  pl.BlockSpec((1, t, LANES), lambda bi, r, j, f: (bi, k_tile(r, j), 0)),
                  pl.BlockSpec((c, t), lambda bi, r, j, f: (0, bi * n + q_tile(r, j))),
                  pl.BlockSpec((1, ATT_HEADS, AUG, t), lambda bi, r, j, f: (bi, 0, 0, q_tile(r, j))),
                  pl.BlockSpec((ATT_HEADS * V_ROWS, t), lambda bi, r, j, f: (0, bi * n + k_tile(r, j))),
                  pl.BlockSpec((1, 1, 1, t, t), lambda bi, r, j, f: (bi, q_tile(r, j), k_tile(r, j), 0, 0))],
        out_specs=pl.BlockSpec((1, t, c), lambda bi, r, j, f: (bi, q_tile(r, j), 0)),
        scratch_shapes=_att_scratch(t))
    return pl.pallas_call(
        functools.partial(_dsa_kernel, n=n),
        grid_spec=grid_spec,
        out_shape=jax.ShapeDtypeStruct((b, s, c), BF16),
        compiler_params=_params("parallel", "parallel", "arbitrary"),
        name="dsa",
    )(fast, k, k_aug, q_t, q_aug, v_t, mask)


CONV_HALO = 8


def _conv_kernel(u_ref, gb_ref, gc_ref, hu_ref, hgc_ref, w_ref, o_ref, ext, *, ts):
    i = pl.program_id(1)
    ext[0:CONV_HALO, :] = jnp.where(i > 0, hgc_ref[0] * hu_ref[0], 0.0)
    ext[CONV_HALO:, :] = gc_ref[0] * u_ref[0]
    conv = None
    for t in range(CONV_WIDTH):
        off = CONV_HALO - (CONV_WIDTH - 1) + t
        term = ext[off:off + ts, :] * w_ref[t:t + 1, :]
        conv = term if conv is None else conv + term
    o_ref[0] = (gb_ref[0] * conv).astype(o_ref.dtype)


def _gated_conv(h, w, ts=512):
    b, s, c3 = h.shape
    c = c3 // 3
    ts = min(ts, s)
    hb = ts // CONV_HALO

    def halo_map(col):
        return lambda bi, i: (bi, jnp.maximum(i * hb - 1, 0), col)

    return pl.pallas_call(
        functools.partial(_conv_kernel, ts=ts),
        grid=(b, s // ts),
        in_specs=[pl.BlockSpec((1, ts, c), lambda bi, i: (bi, i, 0)),
                  pl.BlockSpec((1, ts, c), lambda bi, i: (bi, i, 1)),
                  pl.BlockSpec((1, ts, c), lambda bi, i: (bi, i, 2)),
                  pl.BlockSpec((1, CONV_HALO, c), halo_map(0)),
                  pl.BlockSpec((1, CONV_HALO, c), halo_map(2)),
                  pl.BlockSpec(w.shape, lambda bi, i: (0, 0))],
        out_specs=pl.BlockSpec((1, ts, c), lambda bi, i: (bi, i, 0)),
        out_shape=jax.ShapeDtypeStruct((b, s, c), BF16),
        scratch_shapes=[pltpu.VMEM((ts + CONV_HALO, c), F32)],
        compiler_params=_params("parallel", "arbitrary"),
        name="gated_conv",
    )(h, h, h, h, h, w)


GATE_CHUNK = 256


def _fgate_kernel(f_ref, fb_ref, hi_ref, mid_ref, lo_ref):
    s = f_ref.shape[1]
    r = lax.broadcasted_iota(jnp.int32, (GATE_CHUNK, GATE_CHUNK), 0)
    c = lax.broadcasted_iota(jnp.int32, (GATE_CHUNK, GATE_CHUNK), 1)
    tri = (c <= r).astype(F32)

    def body(t, carry):
        rows = pl.ds(t * GATE_CHUNK, GATE_CHUNK)
        z = f_ref[0, rows, :] + fb_ref[...]
        log_f = -(jnp.maximum(-z, 0.0) + jnp.log1p(jnp.exp(-jnp.abs(z))))
        cs = jnp.dot(tri, log_f, preferred_element_type=F32, precision=lax.Precision.HIGHEST) + carry
        b2 = cs * LOG2E
        hi = b2.astype(BF16)
        r1 = b2 - hi.astype(F32)
        mid = r1.astype(BF16)
        hi_ref[0, rows, :] = hi
        mid_ref[0, rows, :] = mid
        lo_ref[0, rows, :] = (r1 - mid.astype(F32)).astype(BF16)
        return cs[GATE_CHUNK - 1:GATE_CHUNK, :]

    lax.fori_loop(0, s // GATE_CHUNK, body, jnp.zeros((1, LANES), F32))


def _forget_cumsum(f, fb):
    b, s, c = f.shape
    spec = pl.BlockSpec((1, s, c), lambda bi: (bi, 0, 0))
    return pl.pallas_call(
        _fgate_kernel,
        grid=(b,),
        in_specs=[spec, pl.BlockSpec((1, c), lambda bi: (0, 0))],
        out_specs=[spec, spec, spec],
        out_shape=[jax.ShapeDtypeStruct((b, s, c), BF16)] * 3,
        compiler_params=_params("parallel"),
        name="forget_cumsum",
    )(f, fb)


def _fox_kernel(first_ref, fast_ref, k_ref, ka_ref, qt_ref, qa_ref, vt_ref, o_ref, m_sc, acc_sc, qfull, *, t):
    b, i, j = pl.program_id(0), pl.program_id(1), pl.program_id(2)
    steps = i - first_ref[b, i]
    fast = fast_ref[b, i] == 1

    def causal():
        return lax.broadcasted_iota(jnp.int32, (t, t), 0) <= lax.broadcasted_iota(jnp.int32, (t, t), 1)

    def scores(diagonal):
        ka = ka_ref[0]

        def fn(h):
            s = _scores(k_ref, ka, qfull, h)
            return jnp.where(causal(), s, NEG) if diagonal else s
        return fn

    def fast_tile(diagonal):
        _for_heads(scores(diagonal), lambda s, h: _fast_step(s, None, vt_ref[_vrows(h), :], h, acc_sc))

    def exact_tile(diagonal):
        _for_heads(scores(diagonal), lambda s, h: _exact_step(s, vt_ref[_vrows(h), :], h, m_sc, acc_sc))

    @pl.when(j == 0)
    def _():
        _att_init(qt_ref, qa_ref, m_sc, acc_sc, qfull)

    @pl.when((j == 0) & fast)
    def _():
        fast_tile(True)

    @pl.when((j == 0) & jnp.logical_not(fast))
    def _():
        exact_tile(True)

    @pl.when((j > 0) & (j <= steps) & fast)
    def _():
        fast_tile(False)

    @pl.when((j > 0) & (j <= steps) & jnp.logical_not(fast))
    def _():
        exact_tile(False)

    @pl.when(j == steps)
    def _():
        _att_finish(o_ref, acc_sc)


def _fox(first, fast, k, k_aug, q_t, q_aug, v_t, t):
    b, s, c = k.shape
    n = s // t

    def key_tile(bi, i, j, first_ref):
        return jnp.maximum(i - j, first_ref[bi, i])

    grid_spec = pltpu.PrefetchScalarGridSpec(
        num_scalar_prefetch=2,
        grid=(b, n, n),
        in_specs=[pl.BlockSpec((1, t, c), lambda bi, i, j, f, g: (bi, key_tile(bi, i, j, f), 0)),
                  pl.BlockSpec((1, t, LANES), lambda bi, i, j, f, g: (bi, key_tile(bi, i, j, f), 0)),
                  pl.BlockSpec((c, t), lambda bi, i, j, f, g: (0, bi * n + i)),
                  pl.BlockSpec((1, ATT_HEADS, AUG, t), lambda bi, i, j, f, g: (bi, 0, 0, i)),
                  pl.BlockSpec((ATT_HEADS * V_ROWS, t), lambda bi, i, j, f, g: (0, bi * n + key_tile(bi, i, j, f)))],
        out_specs=pl.BlockSpec((1, t, c), lambda bi, i, j, f, g: (bi, i, 0)),
        scratch_shapes=_att_scratch(t))
    return pl.pallas_call(
        functools.partial(_fox_kernel, t=t),
        grid_spec=grid_spec,
        out_shape=jax.ShapeDtypeStruct((b, s, c), BF16),
        compiler_params=_params("parallel", "parallel", "arbitrary"),
        name="fox",
    )(first, fast, k, k_aug, q_t, q_aug, v_t)


def _xattn_kernel(xb_ref, xf_ref, wq_ref, kt_ref, v_ref, wo_ref, g_ref, b_ref, of_ref, ob_ref):
    q = jnp.dot(xb_ref[...], wq_ref[...], preferred_element_type=F32) * Q_SCALE
    q = q.astype(BF16)
    outs = []
    for h in range(XA_HEADS):
        s = jnp.dot(q[:, _head(h)], kt_ref[0, _head(h), :], preferred_element_type=F32)
        p = jnp.exp2(s - jnp.max(s, axis=1, keepdims=True))
        l = jnp.sum(p, axis=1, keepdims=True)
        pv = jnp.dot(p.astype(BF16), v_ref[0, :, _head(h)], preferred_element_type=F32)
        outs.append((pv / l).astype(BF16))
    o = jnp.concatenate(outs, axis=1)
    y = jnp.dot(o, wo_ref[...], preferred_element_type=F32)
    y = _layer_norm(ALPHA * xf_ref[...] + y, g_ref[...], b_ref[...])
    of_ref[...] = y
    ob_ref[...] = y.astype(BF16)


def _xattn(xb, xf, wq, kt, v, wo, layer, g, b, seq, tm=256):
    n, d = xf.shape
    per_batch = seq // tm
    m = v.shape[1]
    return pl.pallas_call(
        _xattn_kernel,
        grid=(n // tm,),
        in_specs=[pl.BlockSpec((tm, d), lambda i: (i, 0)),
                  pl.BlockSpec((tm, d), lambda i: (i, 0)),
                  pl.BlockSpec((None,) + wq.shape[1:], lambda i: (layer, 0, 0)),
                  pl.BlockSpec((1, XA_WIDTH, m), lambda i: (i // per_batch, 0, 0)),
                  pl.BlockSpec((1, m, XA_WIDTH), lambda i: (i // per_batch, 0, 0)),
                  pl.BlockSpec((None,) + wo.shape[1:], lambda i: (layer, 0, 0)),
                  pl.BlockSpec((1, d), lambda i: (0, 0)),
                  pl.BlockSpec((1, d), lambda i: (0, 0))],
        out_specs=[pl.BlockSpec((tm, d), lambda i: (i, 0)), pl.BlockSpec((tm, d), lambda i: (i, 0))],
        out_shape=[jax.ShapeDtypeStruct((n, d), F32), jax.ShapeDtypeStruct((n, d), BF16)],
        compiler_params=_params("parallel"),
        name="xattn",
    )(xb, xf, wq, kt, v, wo, g, b)


def _ffn_kernel(xb_ref, wg_ref, wu_ref, wo_ref, xf_ref, g_ref, b_ref, of_ref, ob_ref, ot_ref, acc):
    j = pl.program_id(1)

    @pl.when(j == 0)
    def _():
        acc[...] = jnp.zeros(acc.shape, F32)

    xb = xb_ref[...]
    gate = jnp.dot(xb, wg_ref[...], preferred_element_type=F32)
    up = jnp.dot(xb, wu_ref[...], preferred_element_type=F32)
    hid = gate * (1.0 / (1.0 + jnp.exp(-gate))) * up
    acc[...] += jnp.dot(hid.astype(BF16), wo_ref[...], preferred_element_type=F32)

    @pl.when(j == pl.num_programs(1) - 1)
    def _():
        y = _layer_norm(ALPHA * xf_ref[...] + acc[...], g_ref[...], b_ref[...])
        of_ref[...] = y
        ob_ref[...] = y.astype(BF16)
        ot_ref[...] = y.T.astype(BF16)


def _ffn(xb, xf, w_in, w_out, layer, g, b, tm=512, th=512):
    n, d = xf.shape
    hidden = w_out.shape[1]
    nh = hidden // th
    return pl.pallas_call(
        _ffn_kernel,
        grid=(n // tm, nh),
        in_specs=[pl.BlockSpec((tm, d), lambda i, j: (i, 0)),
                  pl.BlockSpec((None, d, th), lambda i, j: (layer, 0, j)),
                  pl.BlockSpec((None, d, th), lambda i, j: (layer, 0, j + nh)),
                  pl.BlockSpec((None, th, d), lambda i, j: (layer, j, 0)),
                  pl.BlockSpec((tm, d), lambda i, j: (i, 0)),
                  pl.BlockSpec((1, d), lambda i, j: (0, 0)),
                  pl.BlockSpec((1, d), lambda i, j: (0, 0))],
        out_specs=[pl.BlockSpec((tm, d), lambda i, j: (i, 0)),
                   pl.BlockSpec((tm, d), lambda i, j: (i, 0)),
                   pl.BlockSpec((d, tm), lambda i, j: (0, i))],
        out_shape=[jax.ShapeDtypeStruct((n, d), F32), jax.ShapeDtypeStruct((n, d), BF16),
                   jax.ShapeDtypeStruct((d, n), BF16)],
        scratch_shapes=[pltpu.VMEM((tm, d), F32)],
        compiler_params=_params("parallel", "arbitrary"),
        name="ffn",
    )(xb, w_in, w_in, w_out, xf, g, b)


def _rotary_tables(positions, dh):
    rot = dh // ROPE_FRACTION
    half = rot // 2
    inv_freq = jnp.power(ROPE_THETA, -(jnp.arange(half, dtype=F32) * 2.0 / rot))
    ang = positions.astype(F32)[..., None] * inv_freq
    cos, sin = jnp.cos(ang), jnp.sin(ang)
    zh = jnp.zeros_like(sin)
    rest = jnp.zeros(ang.shape[:-1] + (dh - rot,), F32)
    c = jnp.concatenate([cos, cos, rest + 1.0], axis=-1)
    s_lo = jnp.concatenate([zh, sin, rest], axis=-1)
    s_hi = jnp.concatenate([-sin, zh, rest], axis=-1)
    return tuple(t.reshape(-1, dh) for t in (c, s_lo, s_hi))


def _even_tables(positions):
    n = positions.size
    head = _rotary_tables(positions, HEAD_DIM)
    idx = _rotary_tables(positions, IDX_DIM)
    idx_full = tuple(jnp.tile(t, (1, LANES // IDX_DIM)) for t in idx)
    wi_scale = jnp.concatenate([jnp.full((IDX_HEADS,), IDX_HEADS ** -0.5, F32),
                                jnp.ones((LANES - IDX_DIM - IDX_HEADS,), F32)])
    pad = jnp.zeros((n, LANES - IDX_DIM), F32)
    tail = (jnp.concatenate([idx[0], pad + wi_scale], axis=1),
            jnp.concatenate([idx[1], pad], axis=1),
            jnp.concatenate([idx[2], pad], axis=1))
    return {"k": head,
            "q_t": tuple((t * Q_SCALE).T for t in head),
            "qi_t": tuple((t * IDX_DIM ** -0.5).T for t in idx_full),
            "tail": tail}


def _att_bounds(qn2, kn2, bsz, seq, t):
    n = seq // t
    qn = jnp.sqrt(qn2).reshape(ATT_HEADS, bsz, seq).transpose(1, 2, 0)
    kn = jnp.sqrt(kn2).reshape(bsz, seq, ATT_HEADS)
    k_max = jnp.max(kn, axis=1, keepdims=True)
    neg_m = -(1.01 * qn * k_max + 1.0)
    qn_tile = jnp.max(qn.reshape(bsz, n, t, ATT_HEADS), axis=2)
    kn_tile = jnp.max(kn.reshape(bsz, n, t, ATT_HEADS), axis=2)
    spread = jnp.max(2.05 * qn_tile * k_max, axis=-1) + 8.0
    fast = (spread <= FAST_RANGE_LOG2).astype(jnp.int32)
    return neg_m.astype(BF16), qn_tile, kn_tile, fast


def _aug_operands(q_entries, k_entries, bsz, seq):
    shape = (bsz, seq, ATT_HEADS)
    one, zero = jnp.ones(shape, BF16), jnp.zeros(shape, BF16)

    def pack(entries):
        cols = [one if e is None else e for e in entries]
        return jnp.stack(cols + [zero] * (AUG - len(cols)), axis=-1)

    return pack(k_entries).reshape(bsz, seq, ATT_HEADS * AUG), pack(q_entries).transpose(0, 2, 3, 1)


def _even_mixer(xb, xt, tables, wb, wt, j, pool_w, pool_scale, bsz, seq, tq):
    n = xb.shape[0]
    hw = HALF_WIDTH
    u = _proj(xb, wb, j, F32, 0, hw)
    q_t, qn2 = _proj_t(wt, j, xt, BF16, hw, hw, tables["q_t"], shift=HEAD_DIM // 8, norms=True)
    k, kn2 = _proj(xb, wb, j, BF16, 2 * hw, hw, tables["k"], shift=HEAD_DIM // 8, norms=True)
    v_t = _proj_t(wt, j, xt, BF16, 3 * hw, hw, value_rows=True)
    qi_t = _proj_t(wt, j, xt, BF16, 4 * hw, hw, tables["qi_t"], shift=IDX_DIM // 8)
    w_tail = jnp.pad(wb[j:j + 1, :, 5 * hw:], ((0, 0), (0, 0), (0, LANES - IDX_DIM - IDX_HEADS)))
    tail = _proj(xb, w_tail, 0, F32, 0, LANES, tables["tail"], shift=IDX_DIM // 8)

    ki = tail[:, :IDX_DIM].astype(BF16).reshape(bsz, seq, IDX_DIM)
    wi_t = tail[:, IDX_DIM:IDX_DIM + IDX_HEADS].T

    a = _pool(u.reshape(bsz, seq, hw), pool_w, j, pool_scale.reshape(1, hw))
    mask = _idx_mask(qi_t, wi_t, ki, tq, tq)
    neg_m, _, _, fast = _att_bounds(qn2, kn2, bsz, seq, tq)
    k_aug, q_aug = _aug_operands([neg_m], [None], bsz, seq)
    bb = _dsa(fast, k.reshape(bsz, seq, hw), k_aug, q_t, q_aug, v_t, mask, tq)
    return a.reshape(n, hw), bb.reshape(n, hw)


DSA_TILE = 512
FOX_TILE = 512
UNDERFLOW_LOG2 = 160.0


def _fox_first_tile(qn, kn, terms, bsz, seq, t):
    n = seq // t
    f32sum = sum(x.astype(F32) for x in terms)[:, :, :ATT_HEADS].reshape(bsz, n, t, ATT_HEADS)
    f_first, f_last = f32sum[:, :, 0], f32sum[:, :, t - 1]
    bound = (1.01 * qn[:, :, None] * (kn[:, None, :] + kn[:, :, None])
             + f_first[:, :, None] - f_last[:, None, :] + 1.0)
    tiles = jnp.arange(n, dtype=jnp.int32)
    needed = jnp.any(bound >= -UNDERFLOW_LOG2, axis=-1) | (tiles[:, None] == tiles[None, :])
    needed = needed & (tiles[None, :] <= tiles[:, None])
    return jnp.min(jnp.where(needed, tiles[None, None, :], n), axis=-1).astype(jnp.int32)


def _odd_mixer(xb, xt, wb, wt, j, conv_w, forget_b, bsz, seq):
    n = xb.shape[0]
    hw = HALF_WIDTH
    ugg = _proj(xb, wb, j, F32, 0, 3 * hw)
    q_t, qn2 = _proj_t(wt, j, xt, BF16, 3 * hw, hw, scale=Q_SCALE, norms=True)
    k, kn2 = _proj(xb, wb, j, BF16, 4 * hw, hw, norms=True)
    v_t = _proj_t(wt, j, xt, BF16, 5 * hw, hw, value_rows=True)
    w_tail = jnp.pad(wb[j:j + 1, :, 6 * hw:], ((0, 0), (0, 0), (0, LANES - ATT_HEADS)))
    f = _proj(xb, w_tail, 0, F32, 0, LANES)
    fb = jnp.pad(forget_b, (0, LANES - ATT_HEADS)).reshape(1, LANES)
    terms = _forget_cumsum(f.reshape(bsz, seq, LANES), fb)
    hi, mid, lo = (x[:, :, :ATT_HEADS] for x in terms)

    c = _gated_conv(ugg.reshape(bsz, seq, 3 * hw), conv_w)
    t = min(FOX_TILE, seq)
    neg_m, qn, kn, fast = _att_bounds(qn2, kn2, bsz, seq, t)
    k_aug, q_aug = _aug_operands([neg_m, None, None, None, hi, mid, lo],
                                 [None, -hi, -mid, -lo, None, None, None], bsz, seq)
    first = _fox_first_tile(qn, kn, terms, bsz, seq, t)
    d = _fox(first, fast, k.reshape(bsz, seq, hw), k_aug, q_t, q_aug, v_t, t)
    return c.reshape(n, hw), d.reshape(n, hw)


def kernel(x, mem, positions, ev_w_in, ev_pool_w, ev_pool_scale, ev_w_out, od_w_in, od_conv_w, od_forget_b,
           od_w_out, ca_w_q, ca_w_kv, ca_w_o, ffn_w_in, ffn_w_out, ln_g, ln_b):
    bsz, seq, d = x.shape
    n = bsz * seq
    m = mem.shape[1]
    tq = min(DSA_TILE, seq)
    xf = x.reshape(n, d)
    xb = xf.astype(BF16)
    memb = mem.reshape(bsz * m, d).astype(BF16)
    tables = _even_tables(positions)
    xt = xb.T
    ev_wb, od_wb = ev_w_in.astype(BF16), od_w_in.astype(BF16)
    ev_wt, od_wt = ev_wb.transpose(0, 2, 1), od_wb.transpose(0, 2, 1)
    ev_wo, od_wo, pool_w = ev_w_out.astype(BF16), od_w_out.astype(BF16), ev_pool_w.astype(BF16)
    ca_q, ca_kv, ca_o = ca_w_q.astype(BF16), ca_w_kv.astype(BF16), ca_w_o.astype(BF16)
    ffn_in, ffn_out = ffn_w_in.astype(BF16), ffn_w_out.astype(BF16)
    for i in range(DEPTH):
        j = i // 2
        if i % 2 == 0:
            parts = _even_mixer(xb, xt, tables, ev_wb, ev_wt, j, pool_w, ev_pool_scale[j], bsz, seq, tq)
            w_out = ev_wo
        else:
            parts = _odd_mixer(xb, xt, od_wb, od_wt, j, od_conv_w[j], od_forget_b[j], bsz, seq)
            w_out = od_wo
        g, b = ln_g[i].reshape(3, 1, d), ln_b[i].reshape(3, 1, d)
        xf, xb = _out_ln(list(parts), w_out, j, xf, g[0], b[0])

        kv = _proj(memb, ca_kv, i, BF16, 0, 2 * XA_WIDTH).reshape(bsz, m, 2 * XA_WIDTH)
        kt = kv[:, :, :XA_WIDTH].transpose(0, 2, 1)
        xf, xb = _xattn(xb, xf, ca_q, kt, kv[:, :, XA_WIDTH:], ca_o, i, g[1], b[1], seq)

        xf, xb, xt = _ffn(xb, xf, ffn_in, ffn_out, i, g[2], b[2])
    return xf.reshape(bsz, seq, d)
```

```python
import functools

import jax
import jax.numpy as jnp
from jax import lax
from jax.experimental import pallas as pl
from jax.experimental.pallas import tpu as pltpu

F32 = jnp.float32
BF16 = jnp.bfloat16

D_MODEL = 2048
DEPTH = 4
CHUNK = 64
HEAD_DIM = 128
HALF_WIDTH = D_MODEL // 2
POOL_WINDOWS = (2, 4, 8, 16)
POOL_GROUP_DIM = HALF_WIDTH // len(POOL_WINDOWS)
ATT_HEADS = HALF_WIDTH // HEAD_DIM
IDX_HEADS = 16
IDX_DIM = 64
DSA_TOPK_MAX = 256
CONV_WIDTH = 3
XA_HEADS = 4
XA_WIDTH = XA_HEADS * HEAD_DIM
FFN_HIDDEN = -(-(8 * D_MODEL) // (3 * 256)) * 256
ROPE_THETA = 500000.0
ROPE_FRACTION = 4
LN_EPS = 1e-5
ALPHA = (2 * DEPTH) ** 0.25
LOG2E = 1.4426950408889634
Q_SCALE = HEAD_DIM ** -0.5 * LOG2E

LANES = 128
SUBLANES = 8
PACKED_ROWS = 16
NEG = -1e30
INT_MIN = -(2 ** 31)
VMEM_LIMIT = 60000 * 1024


def _params(*semantics):
    return pltpu.CompilerParams(dimension_semantics=semantics, vmem_limit_bytes=VMEM_LIMIT)


def _layer_norm(y, g, b):
    mu = jnp.mean(y, axis=-1, keepdims=True)
    d = y - mu
    var = jnp.mean(d * d, axis=-1, keepdims=True)
    return d * lax.rsqrt(var + LN_EPS) * g + b


AUG = 16
V_ROWS = HEAD_DIM + AUG


def _proj_kernel(*refs, shift, scale, axis, norms, value_rows):
    refs = list(refs)
    a_ref, b_ref = refs[0], refs[1]
    n_ref = refs.pop() if norms else None
    o_ref = refs.pop()
    h = jnp.dot(a_ref[...], b_ref[...], preferred_element_type=F32)
    if not (shift or norms or value_rows):
        o_ref[...] = (h * scale if scale != 1.0 else h).astype(o_ref.dtype)
        return
    if shift:
        c, s1, s2 = refs[2][...], refs[3][...], refs[4][...]
    for g in range(h.shape[axis] // LANES):
        sl = (slice(None),) * axis + (slice(g * LANES, (g + 1) * LANES),)
        hg = h[sl]
        if shift:
            hg = hg * c + pltpu.roll(hg, shift, axis) * s1 + pltpu.roll(hg, LANES - shift, axis) * s2
        elif scale != 1.0:
            hg = hg * scale
        og = hg.astype(o_ref.dtype)
        if value_rows:
            tm = og.shape[1]
            o_ref[g * V_ROWS:g * V_ROWS + LANES, :] = og
            row = lax.broadcasted_iota(jnp.int32, (AUG, tm), 0)
            o_ref[g * V_ROWS + LANES:(g + 1) * V_ROWS, :] = jnp.where(row == 0, 1.0, 0.0).astype(o_ref.dtype)
        else:
            o_ref[sl] = og
        if norms:
            sq = jnp.square(og.astype(F32))
            if axis == 0:
                n_ref[g:g + 1, :] = jnp.sum(sq, axis=0, keepdims=True)
            else:
                n_ref[:, g:g + 1] = jnp.sum(sq, axis=1, keepdims=True)


def _proj(x, w, layer, out_dtype, col0, ncols, tables=None, shift=0, scale=1.0, norms=False, tm=1024, tn=1024):
    n, k = x.shape
    tm, tn = min(tm, n), min(tn, ncols)
    assert col0 % tn == 0 and ncols % tn == 0
    in_specs = [pl.BlockSpec((tm, k), lambda i, j: (i, 0)),
                pl.BlockSpec((None, k, tn), lambda i, j: (layer, 0, col0 // tn + j))]
    args = [x, w]
    if shift:
        in_specs += [pl.BlockSpec((tm, LANES), lambda i, j: (i, 0))] * 3
        args += list(tables)
    out_specs = [pl.BlockSpec((tm, tn), lambda i, j: (i, j))]
    out_shape = [jax.ShapeDtypeStruct((n, ncols), out_dtype)]
    if norms:
        assert tn == ncols
        out_specs.append(pl.BlockSpec((tm, tn // LANES), lambda i, j: (i, 0)))
        out_shape.append(jax.ShapeDtypeStruct((n, ncols // LANES), F32))
    out = pl.pallas_call(
        functools.partial(_proj_kernel, shift=shift, scale=scale, axis=1, norms=norms, value_rows=False),
        grid=(n // tm, ncols // tn),
        in_specs=in_specs,
        out_specs=out_specs,
        out_shape=out_shape,
        compiler_params=_params("parallel", "arbitrary"),
        name="proj",
    )(*args)
    return out if norms else out[0]


def _proj_t(wt, layer, xt, out_dtype, row0, nrows, tables=None, shift=0, scale=1.0, norms=False,
            value_rows=False, tm=1024, tn=1024):
    k, n = xt.shape
    tm, tn = min(tm, n), min(tn, nrows)
    assert row0 % tn == 0 and nrows % tn == 0
    in_specs = [pl.BlockSpec((None, tn, k), lambda i, j: (layer, row0 // tn + j, 0)),
                pl.BlockSpec((k, tm), lambda i, j: (0, i))]
    args = [wt, xt]
    if shift:
        in_specs += [pl.BlockSpec((LANES, tm), lambda i, j: (0, i))] * 3
        args += list(tables)
    rows_out = tn // LANES * V_ROWS if value_rows else tn
    out_specs = [pl.BlockSpec((rows_out, tm), lambda i, j: (j, i))]
    out_shape = [jax.ShapeDtypeStruct((nrows // tn * rows_out, n), out_dtype)]
    if norms:
        assert tn == nrows
        out_specs.append(pl.BlockSpec((tn // LANES, tm), lambda i, j: (0, i)))
        out_shape.append(jax.ShapeDtypeStruct((nrows // LANES, n), F32))
    out = pl.pallas_call(
        functools.partial(_proj_kernel, shift=shift, scale=scale, axis=0, norms=norms, value_rows=value_rows),
        grid=(n // tm, nrows // tn),
        in_specs=in_specs,
        out_specs=out_specs,
        out_shape=out_shape,
        compiler_params=_params("parallel", "arbitrary"),
        name="proj_t",
    )(*args)
    return out if norms else out[0]


POOL_HALO = 16


def _pool_kernel(u_ref, halo_ref, w_ref, sc_ref, o_ref, ext, *, ts):
    i = pl.program_id(1)
    ext[0:POOL_HALO, :] = jnp.where(i > 0, halo_ref[0], 0.0)
    ext[POOL_HALO:, :] = u_ref[0]
    cnt = i * ts + lax.broadcasted_iota(jnp.int32, (ts, 1), 0) + 1
    for g, win in enumerate(POOL_WINDOWS):
        lo, hi = g * POOL_GROUP_DIM, (g + 1) * POOL_GROUP_DIM
        cur = ext[POOL_HALO:POOL_HALO + ts, lo:hi]
        s = cur
        for j in range(1, win):
            s = s + ext[POOL_HALO - j:POOL_HALO - j + ts, lo:hi]
        d = s / jnp.minimum(cnt, win).astype(F32) - cur
        y = jnp.dot(d.astype(BF16), w_ref[g], preferred_element_type=F32)
        o_ref[0, :, lo:hi] = (y * sc_ref[:, lo:hi]).astype(o_ref.dtype)


def _pool(u, w, layer, scale, ts=512):
    b, s, c = u.shape
    ts = min(ts, s)
    hb = ts // POOL_HALO
    return pl.pallas_call(
        functools.partial(_pool_kernel, ts=ts),
        grid=(b, s // ts),
        in_specs=[pl.BlockSpec((1, ts, c), lambda bi, i: (bi, i, 0)),
                  pl.BlockSpec((1, POOL_HALO, c), lambda bi, i: (bi, jnp.maximum(i * hb - 1, 0), 0)),
                  pl.BlockSpec((None,) + w.shape[1:], lambda bi, i: (layer, 0, 0, 0)),
                  pl.BlockSpec((1, c), lambda bi, i: (0, 0))],
        out_specs=pl.BlockSpec((1, ts, c), lambda bi, i: (bi, i, 0)),
        out_shape=jax.ShapeDtypeStruct((b, s, c), BF16),
        scratch_shapes=[pltpu.VMEM((ts + POOL_HALO, c), F32)],
        compiler_params=_params("parallel", "arbitrary"),
        name="pool",
    )(u, u, w, scale)


def _sortable(v):
    bits = lax.bitcast_convert_type(v, jnp.int32)
    return bits ^ ((bits >> 31) & 0x7FFFFFFF)


def _colsum8(v):
    tk, tq = v.shape
    return v.reshape(tk // SUBLANES, SUBLANES, tq).sum(axis=0)


def _idx_kernel(qi_ref, wi_ref, ki_ref, o_ref, keys, half, *, tq, tk, nk, topk):
    i = pl.program_id(1)
    nact = (i * tq + tq - 1) // tk + 1
    q_chunk = (i * tq + lax.broadcasted_iota(jnp.int32, (1, tq), 1)) // CHUNK

    def score_tile(c, carry):
        kt = ki_ref[0, pl.ds(pl.multiple_of(c * tk, tk), tk), :]
        acc = jnp.zeros((tk, tq), F32)
        for h in range(IDX_HEADS):
            il = jnp.dot(kt, qi_ref[h * IDX_DIM:(h + 1) * IDX_DIM, :], preferred_element_type=F32)
            acc = acc + jnp.maximum(il, 0.0) * wi_ref[h:h + 1, :]
        k_chunk = (c * tk + lax.broadcasted_iota(jnp.int32, (tk, 1), 0)) // CHUNK
        kk = jnp.where(k_chunk <= q_chunk, _sortable(acc), INT_MIN)
        keys[c] = kk
        half[c] = (kk >> 16).astype(jnp.int16)
        return carry

    lax.fori_loop(0, nact, score_tile, 0)

    def count(pred):
        def body(c, acc):
            return acc + _colsum8(jnp.where(pred(keys[c], c), 1, 0))
        acc = lax.fori_loop(0, nact, body, jnp.zeros((SUBLANES, tq), jnp.int32))
        return jnp.sum(acc, axis=0, keepdims=True)

    def count16(cand, strict):
        cand16 = jnp.broadcast_to(cand, (PACKED_ROWS, tq)).astype(jnp.int16)[None]

        def body(c, acc):
            h3 = half[c].reshape(tk // PACKED_ROWS, PACKED_ROWS, tq)
            hit = (h3 > cand16) if strict else (h3 >= cand16)
            ones = jnp.where(hit, jnp.int16(1), jnp.int16(0))
            for r in range(tk // PACKED_ROWS):
                acc = acc + ones[r]
            return acc

        acc = lax.fori_loop(0, nact, body, jnp.zeros((PACKED_ROWS, tq), jnp.int16))
        return jnp.sum(acc.astype(jnp.int32), axis=0, keepdims=True)

    hi = jnp.where(count16(jnp.zeros((1, tq), jnp.int32), False) >= topk, 0, -(2 ** 15))

    def hi_step(b, hi):
        cand = hi | (1 << (14 - b))
        return jnp.where(count16(cand, False) >= topk, cand, hi)

    hi = lax.fori_loop(0, 15, hi_step, hi)
    n_above = count16(hi, True)

    def low_tile(c, carry):
        kk = keys[c]
        half[c] = jnp.where((kk >> 16) == hi, (kk & 0xFFFF) - 2 ** 15, -(2 ** 15)).astype(jnp.int16)
        return carry

    lax.fori_loop(0, nact, low_tile, 0)

    def lo_step(b, lo):
        cand = lo | (1 << (15 - b))
        return jnp.where(n_above + count16(cand - 2 ** 15, False) >= topk, cand, lo)

    lo = lax.fori_loop(0, 16, lo_step, jnp.zeros((1, tq), jnp.int32))
    thr = (hi << 16) | lo
    thr = jnp.maximum(thr, INT_MIN + 1)
    n_ge = count(lambda kk, c: kk >= thr)

    def write_tiles(select):
        def body(c, carry):
            o_ref[0, 0, c] = jnp.where(select(keys[c], c), 1.0, 0.0).astype(o_ref.dtype)
            return carry
        lax.fori_loop(0, nact, body, 0)

    has_ties = jnp.max(n_ge) > topk

    @pl.when(jnp.logical_not(has_ties))
    def _():
        write_tiles(lambda kk, c: kk >= thr)

    @pl.when(has_ties)
    def _():
        need = topk - count(lambda kk, c: kk > thr)

        def key_index(c):
            return c * tk + lax.broadcasted_iota(jnp.int32, (tk, 1), 0)

        def idx_step(b, cut):
            cand = cut | (1 << (30 - b))
            below = count(lambda kk, c: (kk == thr) & (key_index(c) < cand))
            return jnp.where(below < need, cand, cut)

        cut = lax.fori_loop(0, 31, idx_step, jnp.zeros((1, tq), jnp.int32))
        write_tiles(lambda kk, c: (kk > thr) | ((kk == thr) & (key_index(c) <= cut)))

    def fill_tile(c, carry):
        o_ref[0, 0, c] = jnp.zeros((tk, tq), o_ref.dtype)
        return carry

    lax.fori_loop(nact, nk, fill_tile, 0)


def _idx_mask(qi_t, wi_t, ki, tq, tk):
    b, s, _ = ki.shape
    nq, nk = s // tq, s // tk
    topk = min(DSA_TOPK_MAX, s // 4)
    return pl.pallas_call(
        functools.partial(_idx_kernel, tq=tq, tk=tk, nk=nk, topk=topk),
        grid=(b, nq),
        in_specs=[pl.BlockSpec((IDX_HEADS * IDX_DIM, tq), lambda bi, i: (0, bi * nq + i)),
                  pl.BlockSpec((IDX_HEADS, tq), lambda bi, i: (0, bi * nq + i)),
                  pl.BlockSpec((1, s, IDX_DIM), lambda bi, i: (bi, 0, 0))],
        out_specs=pl.BlockSpec((1, 1, nk, tk, tq), lambda bi, i: (bi, i, 0, 0, 0)),
        out_shape=jax.ShapeDtypeStruct((b, nq, nk, tk, tq), BF16),
        scratch_shapes=[pltpu.VMEM((nk, tk, tq), jnp.int32), pltpu.VMEM((nk, tk, tq), jnp.int16)],
        compiler_params=_params("parallel", "arbitrary"),
        name="idx_mask",
    )(qi_t, wi_t, ki)


FAST_RANGE_LOG2 = 120.0


def _head(h):
    return slice(h * HEAD_DIM, (h + 1) * HEAD_DIM)


def _vrows(h):
    return slice(h * V_ROWS, (h + 1) * V_ROWS)


def _att_init(qt_ref, qa_ref, m_sc, acc_sc, qfull):
    m_sc[...] = jnp.full(m_sc.shape, NEG, F32)
    acc_sc[...] = jnp.zeros(acc_sc.shape, F32)
    tq = qfull.shape[2]
    for h in range(ATT_HEADS):
        qfull[h, 0:HEAD_DIM, :] = qt_ref[_head(h), :]
        qfull[h, HEAD_DIM:, :] = jnp.zeros((LANES, tq), BF16)
        lo = HEAD_DIM + h * AUG
        qfull[h, lo:lo + AUG, :] = qa_ref[0, h]


def _scores(k_ref, ka, qfull, h):
    kfull = jnp.concatenate([k_ref[0, :, _head(h)], ka], axis=1)
    return jnp.dot(kfull, qfull[h], preferred_element_type=F32)


def _for_heads(score_fn, step_fn):
    s_next = score_fn(0)
    for h in range(ATT_HEADS):
        s = s_next
        if h + 1 < ATT_HEADS:
            s_next = score_fn(h + 1)
        step_fn(s, h)


def _fast_step(s, keep, vt, h, acc_sc):
    p = jnp.exp2(s).astype(BF16)
    if keep is not None:
        p = p * keep
    acc_sc[h] += jnp.dot(vt, p, preferred_element_type=F32)


def _exact_step(s, vt, h, m_sc, acc_sc):
    m_prev = m_sc[h]
    m_new = jnp.maximum(m_prev, jnp.max(s, axis=0, keepdims=True))
    alpha = jnp.exp2(m_prev - m_new)
    p = jnp.exp2(s - m_new)
    acc_sc[h] = alpha * acc_sc[h] + jnp.dot(vt, p.astype(BF16), preferred_element_type=F32)
    m_sc[h] = m_new


def _att_finish(o_ref, acc_sc):
    for h in range(ATT_HEADS):
        a = acc_sc[h]
        o = (a[0:HEAD_DIM] / a[HEAD_DIM:HEAD_DIM + 1]).T
        o_ref[0, :, _head(h)] = o.astype(o_ref.dtype)


def _att_scratch(tq):
    return [pltpu.VMEM((ATT_HEADS, 1, tq), F32),
            pltpu.VMEM((ATT_HEADS, V_ROWS, tq), F32),
            pltpu.VMEM((ATT_HEADS, 2 * HEAD_DIM, tq), BF16)]


def _pair_tiles(r, j, n):
    second = j > r
    return jnp.where(second, n - 1 - r, r), jnp.where(second, j - r - 1, j)


def _dsa_kernel(fast_ref, k_ref, ka_ref, qt_ref, qa_ref, vt_ref, mask_ref, o_ref, m_sc, acc_sc, qfull, *, n):
    b = pl.program_id(0)
    i, j = _pair_tiles(pl.program_id(1), pl.program_id(2), n)
    fast = fast_ref[b, i] == 1

    @pl.when(j == 0)
    def _():
        _att_init(qt_ref, qa_ref, m_sc, acc_sc, qfull)

    @pl.when(fast)
    def _():
        keep = mask_ref[0, 0, 0]
        ka = ka_ref[0]
        _for_heads(lambda h: _scores(k_ref, ka, qfull, h),
                   lambda s, h: _fast_step(s, keep, vt_ref[_vrows(h), :], h, acc_sc))

    @pl.when(jnp.logical_not(fast))
    def _():
        bias = (mask_ref[0, 0, 0].astype(F32) - 1.0) * (-NEG)
        ka = ka_ref[0]
        _for_heads(lambda h: _scores(k_ref, ka, qfull, h) + bias,
                   lambda s, h: _exact_step(s, vt_ref[_vrows(h), :], h, m_sc, acc_sc))

    @pl.when(j == i)
    def _():
        _att_finish(o_ref, acc_sc)


def _dsa(fast, k, k_aug, q_t, q_aug, v_t, mask, t):
    b, s, c = k.shape
    n = s // t
    assert n % 2 == 0, "query tiles are processed in pairs"

    def q_tile(r, j):
        return _pair_tiles(r, j, n)[0]

    def k_tile(r, j):
        return _pair_tiles(r, j, n)[1]

    grid_spec = pltpu.PrefetchScalarGridSpec(
        num_scalar_prefetch=1,
        grid=(b, n // 2, n + 1),
        in_specs=[pl.BlockSpec((1, t, c), lambda bi, r, j, f: (bi, k_tile(r, j), 0)),
                  pl.BlockSpec((1, t, LANES), lambda bi, r, j, f: (bi, k_tile(r, j), 0)),
                  pl.BlockSpec((c, t), lambda bi, r, j, f: (0, bi * n + q_tile(r, j))),
                  pl.BlockSpec((1, ATT_HEADS, AUG, t), lambda bi, r, j, f: (bi, 0, 0, q_tile(r, j))),
                  pl.BlockSpec((ATT_HEADS * V_ROWS, t), lambda bi, r, j, f: (0, bi * n + k_tile(r, j))),
                  pl.BlockSpec((1, 1, 1, t, t), lambda bi, r, j, f: (bi, q_tile(r, j), k_tile(r, j), 0, 0))],
        out_specs=pl.BlockSpec((1, t, c), lambda bi, r, j, f: (bi, q_tile(r, j), 0)),
        scratch_shapes=_att_scratch(t))
    return pl.pallas_call(
        functools.partial(_dsa_kernel, n=n),
        grid_spec=grid_spec,
        out_shape=jax.ShapeDtypeStruct((b, s, c), BF16),
        compiler_params=_params("parallel", "parallel", "arbitrary"),
        name="dsa",
    )(fast, k, k_aug, q_t, q_aug, v_t, mask)


CONV_HALO = 8


def _conv_kernel(u_ref, gb_ref, gc_ref, hu_ref, hgc_ref, w_ref, o_ref, ext, *, ts):
    i = pl.program_id(1)
    ext[0:CONV_HALO, :] = jnp.where(i > 0, hgc_ref[0] * hu_ref[0], 0.0)
    ext[CONV_HALO:, :] = gc_ref[0] * u_ref[0]
    conv = None
    for t in range(CONV_WIDTH):
        off = CONV_HALO - (CONV_WIDTH - 1) + t
        term = ext[off:off + ts, :] * w_ref[t:t + 1, :]
        conv = term if conv is None else conv + term
    o_ref[0] = (gb_ref[0] * conv).astype(o_ref.dtype)


def _gated_conv(h, w, ts=512):
    b, s, c3 = h.shape
    c = c3 // 3
    ts = min(ts, s)
    hb = ts // CONV_HALO

    def halo_map(col):
        return lambda bi, i: (bi, jnp.maximum(i * hb - 1, 0), col)

    return pl.pallas_call(
        functools.partial(_conv_kernel, ts=ts),
        grid=(b, s // ts),
        in_specs=[pl.BlockSpec((1, ts, c), lambda bi, i: (bi, i, 0)),
                  pl.BlockSpec((1, ts, c), lambda bi, i: (bi, i, 1)),
                  pl.BlockSpec((1, ts, c), lambda bi, i: (bi, i, 2)),
                  pl.BlockSpec((1, CONV_HALO, c), halo_map(0)),
                  pl.BlockSpec((1, CONV_HALO, c), halo_map(2)),
                  pl.BlockSpec(w.shape, lambda bi, i: (0, 0))],
        out_specs=pl.BlockSpec((1, ts, c), lambda bi, i: (bi, i, 0)),
        out_shape=jax.ShapeDtypeStruct((b, s, c), BF16),
        scratch_shapes=[pltpu.VMEM((ts + CONV_HALO, c), F32)],
        compiler_params=_params("parallel", "arbitrary"),
        name="gated_conv",
    )(h, h, h, h, h, w)


GATE_CHUNK = 256


def _fgate_kernel(f_ref, fb_ref, hi_ref, mid_ref, lo_ref):
    s = f_ref.shape[1]
    r = lax.broadcasted_iota(jnp.int32, (GATE_CHUNK, GATE_CHUNK), 0)
    c = lax.broadcasted_iota(jnp.int32, (GATE_CHUNK, GATE_CHUNK), 1)
    tri = (c <= r).astype(F32)

    def body(t, carry):
        rows = pl.ds(t * GATE_CHUNK, GATE_CHUNK)
        z = f_ref[0, rows, :] + fb_ref[...]
        log_f = -(jnp.maximum(-z, 0.0) + jnp.log1p(jnp.exp(-jnp.abs(z))))
        cs = jnp.dot(tri, log_f, preferred_element_type=F32, precision=lax.Precision.HIGHEST) + carry
        b2 = cs * LOG2E
        hi = b2.astype(BF16)
        r1 = b2 - hi.astype(F32)
        mid = r1.astype(BF16)
        hi_ref[0, rows, :] = hi
        mid_ref[0, rows, :] = mid
        lo_ref[0, rows, :] = (r1 - mid.astype(F32)).astype(BF16)
        return cs[GATE_CHUNK - 1:GATE_CHUNK, :]

    lax.fori_loop(0, s // GATE_CHUNK, body, jnp.zeros((1, LANES), F32))


def _forget_cumsum(f, fb):
    b, s, c = f.shape
    spec = pl.BlockSpec((1, s, c), lambda bi: (bi, 0, 0))
    return pl.pallas_call(
        _fgate_kernel,
        grid=(b,),
        in_specs=[spec, pl.BlockSpec((1, c), lambda bi: (0, 0))],
        out_specs=[spec, spec, spec],
        out_shape=[jax.ShapeDtypeStruct((b, s, c), BF16)] * 3,
        compiler_params=_params("parallel"),
        name="forget_cumsum",
    )(f, fb)


def _fox_kernel(first_ref, fast_ref, k_ref, ka_ref, qt_ref, qa_ref, vt_ref, o_ref, m_sc, acc_sc, qfull, *, t):
    b, i, j = pl.program_id(0), pl.program_id(1), pl.program_id(2)
    steps = i - first_ref[b, i]
    fast = fast_ref[b, i] == 1

    def causal():
        return lax.broadcasted_iota(jnp.int32, (t, t), 0) <= lax.broadcasted_iota(jnp.int32, (t, t), 1)

    def scores(diagonal):
        ka = ka_ref[0]

        def fn(h):
            s = _scores(k_ref, ka, qfull, h)
            return jnp.where(causal(), s, NEG) if diagonal else s
        return fn

    def fast_tile(diagonal):
        _for_heads(scores(diagonal), lambda s, h: _fast_step(s, None, vt_ref[_vrows(h), :], h, acc_sc))

    def exact_tile(diagonal):
        _for_heads(scores(diagonal), lambda s, h: _exact_step(s, vt_ref[_vrows(h), :], h, m_sc, acc_sc))

    @pl.when(j == 0)
    def _():
        _att_init(qt_ref, qa_ref, m_sc, acc_sc, qfull)

    @pl.when((j == 0) & fast)
    def _():
        fast_tile(True)

    @pl.when((j == 0) & jnp.logical_not(fast))
    def _():
        exact_tile(True)

    @pl.when((j > 0) & (j <= steps) & fast)
    def _():
        fast_tile(False)

    @pl.when((j > 0) & (j <= steps) & jnp.logical_not(fast))
    def _():
        exact_tile(False)

    @pl.when(j == steps)
    def _():
        _att_finish(o_ref, acc_sc)


def _fox(first, fast, k, k_aug, q_t, q_aug, v_t, t):
    b, s, c = k.shape
    n = s // t

    def key_tile(bi, i, j, first_ref):
        return jnp.maximum(i - j, first_ref[bi, i])

    grid_spec = pltpu.PrefetchScalarGridSpec(
        num_scalar_prefetch=2,
        grid=(b, n, n),
        in_specs=[pl.BlockSpec((1, t, c), lambda bi, i, j, f, g: (bi, key_tile(bi, i, j, f), 0)),
                  pl.BlockSpec((1, t, LANES), lambda bi, i, j, f, g: (bi, key_tile(bi, i, j, f), 0)),
                  pl.BlockSpec((c, t), lambda bi, i, j, f, g: (0, bi * n + i)),
                  pl.BlockSpec((1, ATT_HEADS, AUG, t), lambda bi, i, j, f, g: (bi, 0, 0, i)),
                  pl.BlockSpec((ATT_HEADS * V_ROWS, t), lambda bi, i, j, f, g: (0, bi * n + key_tile(bi, i, j, f)))],
        out_specs=pl.BlockSpec((1, t, c), lambda bi, i, j, f, g: (bi, i, 0)),
        scratch_shapes=_att_scratch(t))
    return pl.pallas_call(
        functools.partial(_fox_kernel, t=t),
        grid_spec=grid_spec,
        out_shape=jax.ShapeDtypeStruct((b, s, c), BF16),
        compiler_params=_params("parallel", "parallel", "arbitrary"),
        name="fox",
    )(first, fast, k, k_aug, q_t, q_aug, v_t)


def _mix_xattn_kernel(pa_ref, pb_ref, wm_ref, xf_ref, wq_ref, kt_ref, v_ref, wo_ref, g_ref, b_ref, of_ref, ob_ref):
    half = pa_ref.shape[1]
    mixed = (jnp.dot(pa_ref[...], wm_ref[0:half, :], preferred_element_type=F32)
             + jnp.dot(pb_ref[...], wm_ref[half:, :], preferred_element_type=F32))
    x1 = _layer_norm(ALPHA * xf_ref[...] + mixed, g_ref[0:1, :], b_ref[0:1, :])
    q = jnp.dot(x1.astype(BF16), wq_ref[...], preferred_element_type=F32) * Q_SCALE
    q = q.astype(BF16)
    outs = []
    for h in range(XA_HEADS):
        s = jnp.dot(q[:, _head(h)], kt_ref[0, _head(h), :], preferred_element_type=F32)
        p = jnp.exp2(s - jnp.max(s, axis=1, keepdims=True))
        l = jnp.sum(p, axis=1, keepdims=True)
        pv = jnp.dot(p.astype(BF16), v_ref[0, :, _head(h)], preferred_element_type=F32)
        outs.append((pv / l).astype(BF16))
    o = jnp.concatenate(outs, axis=1)
    y = jnp.dot(o, wo_ref[...], preferred_element_type=F32)
    x2 = _layer_norm(ALPHA * x1 + y, g_ref[1:2, :], b_ref[1:2, :])
    of_ref[...] = x2
    ob_ref[...] = x2.astype(BF16)


def _mix_xattn(pa, pb, w_mix, mix_layer, xf, wq, kt, v, wo, layer, g, b, seq, tm=256):
    n, d = xf.shape
    per_batch = seq // tm
    m = v.shape[1]
    row = lambda c: pl.BlockSpec((tm, c), lambda i: (i, 0))
    return pl.pallas_call(
        _mix_xattn_kernel,
        grid=(n // tm,),
        in_specs=[row(pa.shape[1]), row(pb.shape[1]),
                  pl.BlockSpec((None,) + w_mix.shape[1:], lambda i: (mix_layer, 0, 0)),
                  row(d),
                  pl.BlockSpec((None,) + wq.shape[1:], lambda i: (layer, 0, 0)),
                  pl.BlockSpec((1, XA_WIDTH, m), lambda i: (i // per_batch, 0, 0)),
                  pl.BlockSpec((1, m, XA_WIDTH), lambda i: (i // per_batch, 0, 0)),
                  pl.BlockSpec((None,) + wo.shape[1:], lambda i: (layer, 0, 0)),
                  pl.BlockSpec((2, d), lambda i: (0, 0)),
                  pl.BlockSpec((2, d), lambda i: (0, 0))],
        out_specs=[row(d), row(d)],
        out_shape=[jax.ShapeDtypeStruct((n, d), F32), jax.ShapeDtypeStruct((n, d), BF16)],
        compiler_params=_params("parallel"),
        name="mix_xattn",
    )(pa, pb, w_mix, xf, wq, kt, v, wo, g, b)


def _ffn_kernel(xb_ref, wg_ref, wu_ref, wo_ref, xf_ref, g_ref, b_ref, of_ref, ob_ref, ot_ref, acc):
    j = pl.program_id(1)

    @pl.when(j == 0)
    def _():
        acc[...] = jnp.zeros(acc.shape, F32)

    xb = xb_ref[...]
    gate = jnp.dot(xb, wg_ref[...], preferred_element_type=F32)
    up = jnp.dot(xb, wu_ref[...], preferred_element_type=F32)
    hid = gate * (1.0 / (1.0 + jnp.exp(-gate))) * up
    acc[...] += jnp.dot(hid.astype(BF16), wo_ref[...], preferred_element_type=F32)

    @pl.when(j == pl.num_programs(1) - 1)
    def _():
        y = _layer_norm(ALPHA * xf_ref[...] + acc[...], g_ref[...], b_ref[...])
        of_ref[...] = y
        ob_ref[...] = y.astype(BF16)
        ot_ref[...] = y.T.astype(BF16)


def _ffn(xb, xf, w_in, w_out, layer, g, b, tm=512, th=512):
    n, d = xf.shape
    hidden = w_out.shape[1]
    nh = hidden // th
    return pl.pallas_call(
        _ffn_kernel,
        grid=(n // tm, nh),
        in_specs=[pl.BlockSpec((tm, d), lambda i, j: (i, 0)),
                  pl.BlockSpec((None, d, th), lambda i, j: (layer, 0, j)),
                  pl.BlockSpec((None, d, th), lambda i, j: (layer, 0, j + nh)),
                  pl.BlockSpec((None, th, d), lambda i, j: (layer, j, 0)),
                  pl.BlockSpec((tm, d), lambda i, j: (i, 0)),
                  pl.BlockSpec((1, d), lambda i, j: (0, 0)),
                  pl.BlockSpec((1, d), lambda i, j: (0, 0))],
        out_specs=[pl.BlockSpec((tm, d), lambda i, j: (i, 0)),
                   pl.BlockSpec((tm, d), lambda i, j: (i, 0)),
                   pl.BlockSpec((d, tm), lambda i, j: (0, i))],
        out_shape=[jax.ShapeDtypeStruct((n, d), F32), jax.ShapeDtypeStruct((n, d), BF16),
                   jax.ShapeDtypeStruct((d, n), BF16)],
        scratch_shapes=[pltpu.VMEM((tm, d), F32)],
        compiler_params=_params("parallel", "arbitrary"),
        name="ffn",
    )(xb, w_in, w_in, w_out, xf, g, b)


def _rotary_tables(positions, dh):
    rot = dh // ROPE_FRACTION
    half = rot // 2
    inv_freq = jnp.power(ROPE_THETA, -(jnp.arange(half, dtype=F32) * 2.0 / rot))
    ang = positions.astype(F32)[..., None] * inv_freq
    cos, sin = jnp.cos(ang), jnp.sin(ang)
    zh = jnp.zeros_like(sin)
    rest = jnp.zeros(ang.shape[:-1] + (dh - rot,), F32)
    c = jnp.concatenate([cos, cos, rest + 1.0], axis=-1)
    s_lo = jnp.concatenate([zh, sin, rest], axis=-1)
    s_hi = jnp.concatenate([-sin, zh, rest], axis=-1)
    return tuple(t.reshape(-1, dh) for t in (c, s_lo, s_hi))


def _even_tables(positions):
    n = positions.size
    head = _rotary_tables(positions, HEAD_DIM)
    idx = _rotary_tables(positions, IDX_DIM)
    idx_full = tuple(jnp.tile(t, (1, LANES // IDX_DIM)) for t in idx)
    wi_scale = jnp.concatenate([jnp.full((IDX_HEADS,), IDX_HEADS ** -0.5, F32),
                                jnp.ones((LANES - IDX_DIM - IDX_HEADS,), F32)])
    pad = jnp.zeros((n, LANES - IDX_DIM), F32)
    tail = (jnp.concatenate([idx[0], pad + wi_scale], axis=1),
            jnp.concatenate([idx[1], pad], axis=1),
            jnp.concatenate([idx[2], pad], axis=1))
    return {"k": head,
            "q_t": tuple((t * Q_SCALE).T for t in head),
            "qi_t": tuple((t * IDX_DIM ** -0.5).T for t in idx_full),
            "tail": tail}


def _att_bounds(qn2, kn2, bsz, seq, t):
    n = seq // t
    qn = jnp.sqrt(qn2).reshape(ATT_HEADS, bsz, seq).transpose(1, 2, 0)
    kn = jnp.sqrt(kn2).reshape(bsz, seq, ATT_HEADS)
    k_max = jnp.max(kn, axis=1, keepdims=True)
    neg_m = -(1.01 * qn * k_max + 1.0)
    qn_tile = jnp.max(qn.reshape(bsz, n, t, ATT_HEADS), axis=2)
    kn_tile = jnp.max(kn.reshape(bsz, n, t, ATT_HEADS), axis=2)
    spread = jnp.max(2.05 * qn_tile * k_max, axis=-1) + 8.0
    fast = (spread <= FAST_RANGE_LOG2).astype(jnp.int32)
    return neg_m.astype(BF16), qn_tile, kn_tile, fast


def _aug_operands(q_entries, k_entries, bsz, seq):
    shape = (bsz, seq, ATT_HEADS)
    one, zero = jnp.ones(shape, BF16), jnp.zeros(shape, BF16)

    def pack(entries):
        cols = [one if e is None else e for e in entries]
        return jnp.stack(cols + [zero] * (AUG - len(cols)), axis=-1)

    return pack(k_entries).reshape(bsz, seq, ATT_HEADS * AUG), pack(q_entries).transpose(0, 2, 3, 1)


def _even_mixer(xb, xt, tables, wb, wt, j, pool_w, pool_scale, bsz, seq, tq):
    n = xb.shape[0]
    hw = HALF_WIDTH
    u = _proj(xb, wb, j, F32, 0, hw)
    q_t, qn2 = _proj_t(wt, j, xt, BF16, hw, hw, tables["q_t"], shift=HEAD_DIM // 8, norms=True)
    k, kn2 = _proj(xb, wb, j, BF16, 2 * hw, hw, tables["k"], shift=HEAD_DIM // 8, norms=True)
    v_t = _proj_t(wt, j, xt, BF16, 3 * hw, hw, value_rows=True)
    qi_t = _proj_t(wt, j, xt, BF16, 4 * hw, hw, tables["qi_t"], shift=IDX_DIM // 8)
    w_tail = jnp.pad(wb[j:j + 1, :, 5 * hw:], ((0, 0), (0, 0), (0, LANES - IDX_DIM - IDX_HEADS)))
    tail = _proj(xb, w_tail, 0, F32, 0, LANES, tables["tail"], shift=IDX_DIM // 8)

    ki = tail[:, :IDX_DIM].astype(BF16).reshape(bsz, seq, IDX_DIM)
    wi_t = tail[:, IDX_DIM:IDX_DIM + IDX_HEADS].T

    a = _pool(u.reshape(bsz, seq, hw), pool_w, j, pool_scale.reshape(1, hw))
    mask = _idx_mask(qi_t, wi_t, ki, tq, tq)
    neg_m, _, _, fast = _att_bounds(qn2, kn2, bsz, seq, tq)
    k_aug, q_aug = _aug_operands([neg_m], [None], bsz, seq)
    bb = _dsa(fast, k.reshape(bsz, seq, hw), k_aug, q_t, q_aug, v_t, mask, tq)
    return a.reshape(n, hw), bb.reshape(n, hw)


DSA_TILE = 512
FOX_TILE = 512
UNDERFLOW_LOG2 = 160.0


def _fox_first_tile(qn, kn, terms, bsz, seq, t):
    n = seq // t
    f32sum = sum(x.astype(F32) for x in terms)[:, :, :ATT_HEADS].reshape(bsz, n, t, ATT_HEADS)
    f_first, f_last = f32sum[:, :, 0], f32sum[:, :, t - 1]
    bound = (1.01 * qn[:, :, None] * (kn[:, None, :] + kn[:, :, None])
             + f_first[:, :, None] - f_last[:, None, :] + 1.0)
    tiles = jnp.arange(n, dtype=jnp.int32)
    needed = jnp.any(bound >= -UNDERFLOW_LOG2, axis=-1) | (tiles[:, None] == tiles[None, :])
    needed = needed & (tiles[None, :] <= tiles[:, None])
    return jnp.min(jnp.where(needed, tiles[None, None, :], n), axis=-1).astype(jnp.int32)


def _odd_mixer(xb, xt, wb, wt, j, conv_w, forget_b, bsz, seq):
    n = xb.shape[0]
    hw = HALF_WIDTH
    ugg = _proj(xb, wb, j, F32, 0, 3 * hw)
    q_t, qn2 = _proj_t(wt, j, xt, BF16, 3 * hw, hw, scale=Q_SCALE, norms=True)
    k, kn2 = _proj(xb, wb, j, BF16, 4 * hw, hw, norms=True)
    v_t = _proj_t(wt, j, xt, BF16, 5 * hw, hw, value_rows=True)
    w_tail = jnp.pad(wb[j:j + 1, :, 6 * hw:], ((0, 0), (0, 0), (0, LANES - ATT_HEADS)))
    f = _proj(xb, w_tail, 0, F32, 0, LANES)
    fb = jnp.pad(forget_b, (0, LANES - ATT_HEADS)).reshape(1, LANES)
    terms = _forget_cumsum(f.reshape(bsz, seq, LANES), fb)
    hi, mid, lo = (x[:, :, :ATT_HEADS] for x in terms)

    c = _gated_conv(ugg.reshape(bsz, seq, 3 * hw), conv_w)
    t = min(FOX_TILE, seq)
    neg_m, qn, kn, fast = _att_bounds(qn2, kn2, bsz, seq, t)
    k_aug, q_aug = _aug_operands([neg_m, None, None, None, hi, mid, lo],
                                 [None, -hi, -mid, -lo, None, None, None], bsz, seq)
    first = _fox_first_tile(qn, kn, terms, bsz, seq, t)
    d = _fox(first, fast, k.reshape(bsz, seq, hw), k_aug, q_t, q_aug, v_t, t)
    return c.reshape(n, hw), d.reshape(n, hw)


def kernel(x, mem, positions, ev_w_in, ev_pool_w, ev_pool_scale, ev_w_out, od_w_in, od_conv_w, od_forget_b,
           od_w_out, ca_w_q, ca_w_kv, ca_w_o, ffn_w_in, ffn_w_out, ln_g, ln_b):
    bsz, seq, d = x.shape
    n = bsz * seq
    m = mem.shape[1]
    tq = min(DSA_TILE, seq)
    xf = x.reshape(n, d)
    xb = xf.astype(BF16)
    memb = mem.reshape(bsz * m, d).astype(BF16)
    tables = _even_tables(positions)
    xt = xb.T
    ev_wb, od_wb = ev_w_in.astype(BF16), od_w_in.astype(BF16)
    ev_wt, od_wt = ev_wb.transpose(0, 2, 1), od_wb.transpose(0, 2, 1)
    ev_wo, od_wo, pool_w = ev_w_out.astype(BF16), od_w_out.astype(BF16), ev_pool_w.astype(BF16)
    ca_q, ca_kv, ca_o = ca_w_q.astype(BF16), ca_w_kv.astype(BF16), ca_w_o.astype(BF16)
    ffn_in, ffn_out = ffn_w_in.astype(BF16), ffn_w_out.astype(BF16)
    for i in range(DEPTH):
        j = i // 2
        if i % 2 == 0:
            parts = _even_mixer(xb, xt, tables, ev_wb, ev_wt, j, pool_w, ev_pool_scale[j], bsz, seq, tq)
            w_out = ev_wo
        else:
            parts = _odd_mixer(xb, xt, od_wb, od_wt, j, od_conv_w[j], od_forget_b[j], bsz, seq)
            w_out = od_wo
        kv = _proj(memb, ca_kv, i, BF16, 0, 2 * XA_WIDTH).reshape(bsz, m, 2 * XA_WIDTH)
        kt = kv[:, :, :XA_WIDTH].transpose(0, 2, 1)
        xf, xb = _mix_xattn(parts[0], parts[1], w_out, j, xf, ca_q, kt, kv[:, :, XA_WIDTH:], ca_o, i,
                            ln_g[i, 0:2], ln_b[i, 0:2], seq)
        xf, xb, xt = _ffn(xb, xf, ffn_in, ffn_out, i, ln_g[i, 2:3], ln_b[i, 2:3])
    return xf.reshape(bsz, seq, d)
```

```python
import functools

import jax
import jax.numpy as jnp
from jax import lax
from jax.experimental import pallas as pl
from jax.experimental.pallas import tpu as pltpu

F32 = jnp.float32
BF16 = jnp.bfloat16

D_MODEL = 2048
DEPTH = 4
CHUNK = 64
HEAD_DIM = 128
HALF_WIDTH = D_MODEL // 2
POOL_WINDOWS = (2, 4, 8, 16)
POOL_GROUP_DIM = HALF_WIDTH // len(POOL_WINDOWS)
ATT_HEADS = HALF_WIDTH // HEAD_DIM
IDX_HEADS = 16
IDX_DIM = 64
DSA_TOPK_MAX = 256
CONV_WIDTH = 3
XA_HEADS = 4
XA_WIDTH = XA_HEADS * HEAD_DIM
FFN_HIDDEN = -(-(8 * D_MODEL) // (3 * 256)) * 256
ROPE_THETA = 500000.0
ROPE_FRACTION = 4
LN_EPS = 1e-5
ALPHA = (2 * DEPTH) ** 0.25
LOG2E = 1.4426950408889634
Q_SCALE = HEAD_DIM ** -0.5 * LOG2E

LANES = 128
SUBLANES = 8
PACKED_ROWS = 16
NEG = -1e30
INT_MIN = -(2 ** 31)
VMEM_LIMIT = 60000 * 1024


def _params(*semantics):
    return pltpu.CompilerParams(dimension_semantics=semantics, vmem_limit_bytes=VMEM_LIMIT)


def _layer_norm(y, g, b):
    mu = jnp.mean(y, axis=-1, keepdims=True)
    d = y - mu
    var = jnp.mean(d * d, axis=-1, keepdims=True)
    return d * lax.rsqrt(var + LN_EPS) * g + b


AUG = 16
V_ROWS = HEAD_DIM + AUG


def _proj_kernel(*refs, shift, scale, axis, norms, value_rows):
    refs = list(refs)
    a_ref, b_ref = refs[0], refs[1]
    n_ref = refs.pop() if norms else None
    o_ref = refs.pop()
    h = jnp.dot(a_ref[...], b_ref[...], preferred_element_type=F32)
    if not (shift or norms or value_rows):
        o_ref[...] = (h * scale if scale != 1.0 else h).astype(o_ref.dtype)
        return
    if shift:
        c, s1, s2 = refs[2][...], refs[3][...], refs[4][...]
    for g in range(h.shape[axis] // LANES):
        sl = (slice(None),) * axis + (slice(g * LANES, (g + 1) * LANES),)
        hg = h[sl]
        if shift:
            hg = hg * c + pltpu.roll(hg, shift, axis) * s1 + pltpu.roll(hg, LANES - shift, axis) * s2
        elif scale != 1.0:
            hg = hg * scale
        og = hg.astype(o_ref.dtype)
        if value_rows:
            tm = og.shape[1]
            o_ref[g * V_ROWS:g * V_ROWS + LANES, :] = og
            row = lax.broadcasted_iota(jnp.int32, (AUG, tm), 0)
            o_ref[g * V_ROWS + LANES:(g + 1) * V_ROWS, :] = jnp.where(row == 0, 1.0, 0.0).astype(o_ref.dtype)
        else:
            o_ref[sl] = og
        if norms:
            sq = jnp.square(og.astype(F32))
            if axis == 0:
                n_ref[g:g + 1, :] = jnp.sum(sq, axis=0, keepdims=True)
            else:
                n_ref[:, g:g + 1] = jnp.sum(sq, axis=1, keepdims=True)


def _proj(x, w, layer, out_dtype, col0, ncols, tables=None, shift=0, scale=1.0, norms=False, tm=1024, tn=1024):
    n, k = x.shape
    tm, tn = min(tm, n), min(tn, ncols)
    assert col0 % tn == 0 and ncols % tn == 0
    in_specs = [pl.BlockSpec((tm, k), lambda i, j: (i, 0)),
                pl.BlockSpec((None, k, tn), lambda i, j: (layer, 0, col0 // tn + j))]
    args = [x, w]
    if shift:
        in_specs += [pl.BlockSpec((tm, LANES), lambda i, j: (i, 0))] * 3
        args += list(tables)
    out_specs = [pl.BlockSpec((tm, tn), lambda i, j: (i, j))]
    out_shape = [jax.ShapeDtypeStruct((n, ncols), out_dtype)]
    if norms:
        assert tn == ncols
        out_specs.append(pl.BlockSpec((tm, tn // LANES), lambda i, j: (i, 0)))
        out_shape.append(jax.ShapeDtypeStruct((n, ncols // LANES), F32))
    out = pl.pallas_call(
        functools.partial(_proj_kernel, shift=shift, scale=scale, axis=1, norms=norms, value_rows=False),
        grid=(n // tm, ncols // tn),
        in_specs=in_specs,
        out_specs=out_specs,
        out_shape=out_shape,
        compiler_params=_params("parallel", "arbitrary"),
        name="proj",
    )(*args)
    return out if norms else out[0]


def _proj_t(wt, layer, xt, out_dtype, row0, nrows, tables=None, shift=0, scale=1.0, norms=False,
            value_rows=False, tm=1024, tn=1024):
    k, n = xt.shape
    tm, tn = min(tm, n), min(tn, nrows)
    assert row0 % tn == 0 and nrows % tn == 0
    in_specs = [pl.BlockSpec((None, tn, k), lambda i, j: (layer, row0 // tn + j, 0)),
                pl.BlockSpec((k, tm), lambda i, j: (0, i))]
    args = [wt, xt]
    if shift:
        in_specs += [pl.BlockSpec((LANES, tm), lambda i, j: (0, i))] * 3
        args += list(tables)
    rows_out = tn // LANES * V_ROWS if value_rows else tn
    out_specs = [pl.BlockSpec((rows_out, tm), lambda i, j: (j, i))]
    out_shape = [jax.ShapeDtypeStruct((nrows // tn * rows_out, n), out_dtype)]
    if norms:
        assert tn == nrows
        out_specs.append(pl.BlockSpec((tn // LANES, tm), lambda i, j: (0, i)))
        out_shape.append(jax.ShapeDtypeStruct((nrows // LANES, n), F32))
    out = pl.pallas_call(
        functools.partial(_proj_kernel, shift=shift, scale=scale, axis=0, norms=norms, value_rows=value_rows),
        grid=(n // tm, nrows // tn),
        in_specs=in_specs,
        out_specs=out_specs,
        out_shape=out_shape,
        compiler_params=_params("parallel", "arbitrary"),
        name="proj_t",
    )(*args)
    return out if norms else out[0]


POOL_HALO = 16


def _pool_kernel(u_ref, halo_ref, w_ref, sc_ref, o_ref, ext, *, ts):
    i = pl.program_id(1)
    ext[0:POOL_HALO, :] = jnp.where(i > 0, halo_ref[0], 0.0)
    ext[POOL_HALO:, :] = u_ref[0]
    cnt = i * ts + lax.broadcasted_iota(jnp.int32, (ts, 1), 0) + 1
    for g, win in enumerate(POOL_WINDOWS):
        lo, hi = g * POOL_GROUP_DIM, (g + 1) * POOL_GROUP_DIM
        cur = ext[POOL_HALO:POOL_HALO + ts, lo:hi]
        s = cur
        for j in range(1, win):
            s = s + ext[POOL_HALO - j:POOL_HALO - j + ts, lo:hi]
        d = s / jnp.minimum(cnt, win).astype(F32) - cur
        y = jnp.dot(d.astype(BF16), w_ref[g], preferred_element_type=F32)
        o_ref[0, :, lo:hi] = (y * sc_ref[:, lo:hi]).astype(o_ref.dtype)


def _pool(u, w, layer, scale, ts=512):
    b, s, c = u.shape
    ts = min(ts, s)
    hb = ts // POOL_HALO
    return pl.pallas_call(
        functools.partial(_pool_kernel, ts=ts),
        grid=(b, s // ts),
        in_specs=[pl.BlockSpec((1, ts, c), lambda bi, i: (bi, i, 0)),
                  pl.BlockSpec((1, POOL_HALO, c), lambda bi, i: (bi, jnp.maximum(i * hb - 1, 0), 0)),
                  pl.BlockSpec((None,) + w.shape[1:], lambda bi, i: (layer, 0, 0, 0)),
                  pl.BlockSpec((1, c), lambda bi, i: (0, 0))],
        out_specs=pl.BlockSpec((1, ts, c), lambda bi, i: (bi, i, 0)),
        out_shape=jax.ShapeDtypeStruct((b, s, c), BF16),
        scratch_shapes=[pltpu.VMEM((ts + POOL_HALO, c), F32)],
        compiler_params=_params("parallel", "arbitrary"),
        name="pool",
    )(u, u, w, scale)


def _sortable(v):
    bits = lax.bitcast_convert_type(v, jnp.int32)
    return bits ^ ((bits >> 31) & 0x7FFFFFFF)


def _colsum8(v):
    tk, tq = v.shape
    return v.reshape(tk // SUBLANES, SUBLANES, tq).sum(axis=0)


def _idx_kernel(qi_ref, wi_ref, ki_ref, o_ref, keys, half, *, tq, tk, nk, topk):
    i = pl.program_id(1)
    nact = (i * tq + tq - 1) // tk + 1
    q_chunk = (i * tq + lax.broadcasted_iota(jnp.int32, (1, tq), 1)) // CHUNK

    def score_tile(c, carry):
        kt = ki_ref[0, pl.ds(pl.multiple_of(c * tk, tk), tk), :]
        acc = jnp.zeros((tk, tq), F32)
        for h in range(IDX_HEADS):
            il = jnp.dot(kt, qi_ref[h * IDX_DIM:(h + 1) * IDX_DIM, :], preferred_element_type=F32)
            acc = acc + jnp.maximum(il, 0.0) * wi_ref[h:h + 1, :]
        k_chunk = (c * tk + lax.broadcasted_iota(jnp.int32, (tk, 1), 0)) // CHUNK
        kk = jnp.where(k_chunk <= q_chunk, _sortable(acc), INT_MIN)
        keys[c] = kk
        half[c] = (kk >> 16).astype(jnp.int16)
        return carry

    lax.fori_loop(0, nact, score_tile, 0)

    def count(pred):
        def body(c, acc):
            return acc + _colsum8(jnp.where(pred(keys[c], c), 1, 0))
        acc = lax.fori_loop(0, nact, body, jnp.zeros((SUBLANES, tq), jnp.int32))
        return jnp.sum(acc, axis=0, keepdims=True)

    def count16(cand, strict):
        cand16 = jnp.broadcast_to(cand, (PACKED_ROWS, tq)).astype(jnp.int16)[None]

        def body(c, acc):
            h3 = half[c].reshape(tk // PACKED_ROWS, PACKED_ROWS, tq)
            hit = (h3 > cand16) if strict else (h3 >= cand16)
            ones = jnp.where(hit, jnp.int16(1), jnp.int16(0))
            for r in range(tk // PACKED_ROWS):
                acc = acc + ones[r]
            return acc

        acc = lax.fori_loop(0, nact, body, jnp.zeros((PACKED_ROWS, tq), jnp.int16))
        return jnp.sum(acc.astype(jnp.int32), axis=0, keepdims=True)

    hi = jnp.where(count16(jnp.zeros((1, tq), jnp.int32), False) >= topk, 0, -(2 ** 15))

    def hi_step(b, hi):
        cand = hi | (1 << (14 - b))
        return jnp.where(count16(cand, False) >= topk, cand, hi)

    hi = lax.fori_loop(0, 15, hi_step, hi)
    n_above = count16(hi, True)

    def low_tile(c, carry):
        kk = keys[c]
        half[c] = jnp.where((kk >> 16) == hi, (kk & 0xFFFF) - 2 ** 15, -(2 ** 15)).astype(jnp.int16)
        return carry

    lax.fori_loop(0, nact, low_tile, 0)

    def lo_step(b, lo):
        cand = lo | (1 << (15 - b))
        return jnp.where(n_above + count16(cand - 2 ** 15, False) >= topk, cand, lo)

    lo = lax.fori_loop(0, 16, lo_step, jnp.zeros((1, tq), jnp.int32))
    thr = (hi << 16) | lo
    thr = jnp.maximum(thr, INT_MIN + 1)
    n_ge = count(lambda kk, c: kk >= thr)

    def write_tiles(select):
        def body(c, carry):
            o_ref[0, 0, c] = jnp.where(select(keys[c], c), 1.0, 0.0).astype(o_ref.dtype)
            return carry
        lax.fori_loop(0, nact, body, 0)

    has_ties = jnp.max(n_ge) > topk

    @pl.when(jnp.logical_not(has_ties))
    def _():
        write_tiles(lambda kk, c: kk >= thr)

    @pl.when(has_ties)
    def _():
        need = topk - count(lambda kk, c: kk > thr)

        def key_index(c):
            return c * tk + lax.broadcasted_iota(jnp.int32, (tk, 1), 0)

        def idx_step(b, cut):
            cand = cut | (1 << (30 - b))
            below = count(lambda kk, c: (kk == thr) & (key_index(c) < cand))
            return jnp.where(below < need, cand, cut)

        cut = lax.fori_loop(0, 31, idx_step, jnp.zeros((1, tq), jnp.int32))
        write_tiles(lambda kk, c: (kk > thr) | ((kk == thr) & (key_index(c) <= cut)))

    def fill_tile(c, carry):
        o_ref[0, 0, c] = jnp.zeros((tk, tq), o_ref.dtype)
        return carry

    lax.fori_loop(nact, nk, fill_tile, 0)


def _idx_mask(qi_t, wi_t, ki, tq, tk):
    b, s, _ = ki.shape
    nq, nk = s // tq, s // tk
    topk = min(DSA_TOPK_MAX, s // 4)
    return pl.pallas_call(
        functools.partial(_idx_kernel, tq=tq, tk=tk, nk=nk, topk=topk),
        grid=(b, nq),
        in_specs=[pl.BlockSpec((IDX_HEADS * IDX_DIM, tq), lambda bi, i: (0, bi * nq + i)),
                  pl.BlockSpec((IDX_HEADS, tq), lambda bi, i: (0, bi * nq + i)),
                  pl.BlockSpec((1, s, IDX_DIM), lambda bi, i: (bi, 0, 0))],
        out_specs=pl.BlockSpec((1, 1, nk, tk, tq), lambda bi, i: (bi, i, 0, 0, 0)),
        out_shape=jax.ShapeDtypeStruct((b, nq, nk, tk, tq), BF16),
        scratch_shapes=[pltpu.VMEM((nk, tk, tq), jnp.int32), pltpu.VMEM((nk, tk, tq), jnp.int16)],
        compiler_params=_params("parallel", "arbitrary"),
        name="idx_mask",
    )(qi_t, wi_t, ki)


FAST_RANGE_LOG2 = 120.0


def _head(h):
    return slice(h * HEAD_DIM, (h + 1) * HEAD_DIM)


def _vrows(h):
    return slice(h * V_ROWS, (h + 1) * V_ROWS)


def _att_init(qt_ref, qa_ref, m_sc, acc_sc, qfull):
    m_sc[...] = jnp.full(m_sc.shape, NEG, F32)
    acc_sc[...] = jnp.zeros(acc_sc.shape, F32)
    tq = qfull.shape[2]
    for h in range(ATT_HEADS):
        qfull[h, 0:HEAD_DIM, :] = qt_ref[_head(h), :]
        qfull[h, HEAD_DIM:, :] = jnp.zeros((LANES, tq), BF16)
        lo = HEAD_DIM + h * AUG
        qfull[h, lo:lo + AUG, :] = qa_ref[0, h]


def _scores(k_ref, ka, qfull, h):
    kfull = jnp.concatenate([k_ref[0, :, _head(h)], ka], axis=1)
    return jnp.dot(kfull, qfull[h], preferred_element_type=F32)


def _for_heads(score_fn, step_fn):
    s_next = score_fn(0)
    for h in range(ATT_HEADS):
        s = s_next
        if h + 1 < ATT_HEADS:
            s_next = score_fn(h + 1)
        step_fn(s, h)


def _fast_step(s, keep, vt, h, acc_sc):
    p = jnp.exp2(s).astype(BF16)
    if keep is not None:
        p = p * keep
    acc_sc[h] += jnp.dot(vt, p, preferred_element_type=F32)


def _exact_step(s, vt, h, m_sc, acc_sc):
    m_prev = m_sc[h]
    m_new = jnp.maximum(m_prev, jnp.max(s, axis=0, keepdims=True))
    alpha = jnp.exp2(m_prev - m_new)
    p = jnp.exp2(s - m_new)
    acc_sc[h] = alpha * acc_sc[h] + jnp.dot(vt, p.astype(BF16), preferred_element_type=F32)
    m_sc[h] = m_new


def _att_finish(o_ref, acc_sc):
    for h in range(ATT_HEADS):
        a = acc_sc[h]
        o = (a[0:HEAD_DIM] / a[HEAD_DIM:HEAD_DIM + 1]).T
        o_ref[0, :, _head(h)] = o.astype(o_ref.dtype)


def _att_scratch(tq):
    return [pltpu.VMEM((ATT_HEADS, 1, tq), F32),
            pltpu.VMEM((ATT_HEADS, V_ROWS, tq), F32),
            pltpu.VMEM((ATT_HEADS, 2 * HEAD_DIM, tq), BF16)]


def _pair_tiles(r, j, n):
    second = j > r
    return jnp.where(second, n - 1 - r, r), jnp.where(second, j - r - 1, j)


def _dsa_kernel(fast_ref, k_ref, ka_ref, qt_ref, qa_ref, vt_ref, mask_ref, o_ref, m_sc, acc_sc, qfull, *, n):
    b = pl.program_id(0)
    i, j = _pair_tiles(pl.program_id(1), pl.program_id(2), n)
    fast = fast_ref[b, i] == 1

    @pl.when(j == 0)
    def _():
        _att_init(qt_ref, qa_ref, m_sc, acc_sc, qfull)

    @pl.when(fast)
    def _():
        keep = mask_ref[0, 0, 0]
        ka = ka_ref[0]
        _for_heads(lambda h: _scores(k_ref, ka, qfull, h),
                   lambda s, h: _fast_step(s, keep, vt_ref[_vrows(h), :], h, acc_sc))

    @pl.when(jnp.logical_not(fast))
    def _():
        bias = (mask_ref[0, 0, 0].astype(F32) - 1.0) * (-NEG)
        ka = ka_ref[0]
        _for_heads(lambda h: _scores(k_ref, ka, qfull, h) + bias,
                   lambda s, h: _exact_step(s, vt_ref[_vrows(h), :], h, m_sc, acc_sc))

    @pl.when(j == i)
    def _():
        _att_finish(o_ref, acc_sc)


def _dsa(fast, k, k_aug, q_t, q_aug, v_t, mask, t):
    b, s, c = k.shape
    n = s // t
    assert n % 2 == 0, "query tiles are processed in pairs"

    def q_tile(r, j):
        return _pair_tiles(r, j, n)[0]

    def k_tile(r, j):
        return _pair_tiles(r, j, n)[1]

    grid_spec = pltpu.PrefetchScalarGridSpec(
        num_scalar_prefetch=1,
        grid=(b, n // 2, n + 1),
        in_specs=[pl.BlockSpec((1, t, c), lambda bi, r, j, f: (bi, k_tile(r, j), 0)),
                  pl.BlockSpec((1, t, LANES), lambda bi, r, j, f: (bi, k_tile(r, j), 0)),
                  pl.BlockSpec((c, t), lambda bi, r, j, f: (0, bi * n + q_tile(r, j))),
                  pl.BlockSpec((1, ATT_HEADS, AUG, t), lambda bi, r, j, f: (bi, 0, 0, q_tile(r, j))),
                  pl.BlockSpec((ATT_HEADS * V_ROWS, t), lambda bi, r, j, f: (0, bi * n + k_tile(r, j))),
                  pl.BlockSpec((1, 1, 1, t, t), lambda bi, r, j, f: (bi, q_tile(r, j), k_tile(r, j), 0, 0))],
        out_specs=pl.BlockSpec((1, t, c), lambda bi, r, j, f: (bi, q_tile(r, j), 0)),
        scratch_shapes=_att_scratch(t))
    return pl.pallas_call(
        functools.partial(_dsa_kernel, n=n),
        grid_spec=grid_spec,
        out_shape=jax.ShapeDtypeStruct((b, s, c), BF16),
        compiler_params=_params("parallel", "parallel", "arbitrary"),
        name="dsa",
    )(fast, k, k_aug, q_t, q_aug, v_t, mask)


CONV_HALO = 8


def _conv_kernel(u_ref, gb_ref, gc_ref, hu_ref, hgc_ref, w_ref, o_ref, ext, *, ts):
    i = pl.program_id(1)
    ext[0:CONV_HALO, :] = jnp.where(i > 0, hgc_ref[0] * hu_ref[0], 0.0)
    ext[CONV_HALO:, :] = gc_ref[0] * u_ref[0]
    conv = None
    for t in range(CONV_WIDTH):
        off = CONV_HALO - (CONV_WIDTH - 1) + t
        term = ext[off:off + ts, :] * w_ref[t:t + 1, :]
        conv = term if conv is None else conv + term
    o_ref[0] = (gb_ref[0] * conv).astype(o_ref.dtype)


def _gated_conv(h, w, ts=512):
    b, s, c3 = h.shape
    c = c3 // 3
    ts = min(ts, s)
    hb = ts // CONV_HALO

    def halo_map(col):
        return lambda bi, i: (bi, jnp.maximum(i * hb - 1, 0), col)

    return pl.pallas_call(
        functools.partial(_conv_kernel, ts=ts),
        grid=(b, s // ts),
        in_specs=[pl.BlockSpec((1, ts, c), lambda bi, i: (bi, i, 0)),
                  pl.BlockSpec((1, ts, c), lambda bi, i: (bi, i, 1)),
                  pl.BlockSpec((1, ts, c), lambda bi, i: (bi, i, 2)),
                  pl.BlockSpec((1, CONV_HALO, c), halo_map(0)),
                  pl.BlockSpec((1, CONV_HALO, c), halo_map(2)),
                  pl.BlockSpec(w.shape, lambda bi, i: (0, 0))],
        out_specs=pl.BlockSpec((1, ts, c), lambda bi, i: (bi, i, 0)),
        out_shape=jax.ShapeDtypeStruct((b, s, c), BF16),
        scratch_shapes=[pltpu.VMEM((ts + CONV_HALO, c), F32)],
        compiler_params=_params("parallel", "arbitrary"),
        name="gated_conv",
    )(h, h, h, h, h, w)


GATE_CHUNK = 256


def _fgate_kernel(f_ref, fb_ref, hi_ref, mid_ref, lo_ref):
    s = f_ref.shape[1]
    r = lax.broadcasted_iota(jnp.int32, (GATE_CHUNK, GATE_CHUNK), 0)
    c = lax.broadcasted_iota(jnp.int32, (GATE_CHUNK, GATE_CHUNK), 1)
    tri = (c <= r).astype(F32)

    def body(t, carry):
        rows = pl.ds(t * GATE_CHUNK, GATE_CHUNK)
        z = f_ref[0, rows, :] + fb_ref[...]
        log_f = -(jnp.maximum(-z, 0.0) + jnp.log1p(jnp.exp(-jnp.abs(z))))
        cs = jnp.dot(tri, log_f, preferred_element_type=F32, precision=lax.Precision.HIGHEST) + carry
        b2 = cs * LOG2E
        hi = b2.astype(BF16)
        r1 = b2 - hi.astype(F32)
        mid = r1.astype(BF16)
        hi_ref[0, rows, :] = hi
        mid_ref[0, rows, :] = mid
        lo_ref[0, rows, :] = (r1 - mid.astype(F32)).astype(BF16)
        return cs[GATE_CHUNK - 1:GATE_CHUNK, :]

    lax.fori_loop(0, s // GATE_CHUNK, body, jnp.zeros((1, LANES), F32))


def _forget_cumsum(f, fb):
    b, s, c = f.shape
    spec = pl.BlockSpec((1, s, c), lambda bi: (bi, 0, 0))
    return pl.pallas_call(
        _fgate_kernel,
        grid=(b,),
        in_specs=[spec, pl.BlockSpec((1, c), lambda bi: (0, 0))],
        out_specs=[spec, spec, spec],
        out_shape=[jax.ShapeDtypeStruct((b, s, c), BF16)] * 3,
        compiler_params=_params("parallel"),
        name="forget_cumsum",
    )(f, fb)


def _fox_schedule(first, fast, n):
    bsz = first.shape[0]
    steps = bsz * n * (n + 1) // 2
    tiles = jnp.arange(n, dtype=jnp.int32)
    cnt = (tiles[None, :] - first + 1).reshape(-1)
    end = jnp.cumsum(cnt)
    t = jnp.arange(steps, dtype=jnp.int32)
    tc = jnp.minimum(t, end[-1] - 1)
    row = jnp.searchsorted(end, tc, side="right").astype(jnp.int32)
    rel = tc - (end - cnt)[row]
    q_tile = row % n
    active = t < end[-1]
    flags = active.astype(jnp.int32) + 2 * (active & (rel == cnt[row] - 1)).astype(jnp.int32)
    return jnp.stack([row // n, q_tile, q_tile - rel, flags, fast.reshape(-1)[row]]).astype(jnp.int32)


def _fox_kernel(sched_ref, k_ref, ka_ref, qt_ref, qa_ref, vt_ref, o_ref, m_sc, acc_sc, qfull, *, t):
    step = pl.program_id(0)
    i, kj, flags = sched_ref[1, step], sched_ref[2, step], sched_ref[3, step]
    fast = sched_ref[4, step] == 1
    active = (flags & 1) == 1
    diagonal = active & (kj == i)
    older = active & (kj != i)

    def causal():
        return lax.broadcasted_iota(jnp.int32, (t, t), 0) <= lax.broadcasted_iota(jnp.int32, (t, t), 1)

    def scores(masked):
        ka = ka_ref[0]

        def fn(h):
            s = _scores(k_ref, ka, qfull, h)
            return jnp.where(causal(), s, NEG) if masked else s
        return fn

    def fast_tile(masked):
        _for_heads(scores(masked), lambda s, h: _fast_step(s, None, vt_ref[_vrows(h), :], h, acc_sc))

    def exact_tile(masked):
        _for_heads(scores(masked), lambda s, h: _exact_step(s, vt_ref[_vrows(h), :], h, m_sc, acc_sc))

    @pl.when(diagonal)
    def _():
        _att_init(qt_ref, qa_ref, m_sc, acc_sc, qfull)

    @pl.when(diagonal & fast)
    def _():
        fast_tile(True)

    @pl.when(diagonal & jnp.logical_not(fast))
    def _():
        exact_tile(True)

    @pl.when(older & fast)
    def _():
        fast_tile(False)

    @pl.when(older & jnp.logical_not(fast))
    def _():
        exact_tile(False)

    @pl.when((flags & 2) == 2)
    def _():
        _att_finish(o_ref, acc_sc)


def _fox(first, fast, k, k_aug, q_t, q_aug, v_t, t):
    b, s, c = k.shape
    n = s // t
    sched = _fox_schedule(first, fast, n)
    grid_spec = pltpu.PrefetchScalarGridSpec(
        num_scalar_prefetch=1,
        grid=(sched.shape[1],),
        in_specs=[pl.BlockSpec((1, t, c), lambda st, sc: (sc[0, st], sc[2, st], 0)),
                  pl.BlockSpec((1, t, LANES), lambda st, sc: (sc[0, st], sc[2, st], 0)),
                  pl.BlockSpec((c, t), lambda st, sc: (0, sc[0, st] * n + sc[1, st])),
                  pl.BlockSpec((1, ATT_HEADS, AUG, t), lambda st, sc: (sc[0, st], 0, 0, sc[1, st])),
                  pl.BlockSpec((ATT_HEADS * V_ROWS, t), lambda st, sc: (0, sc[0, st] * n + sc[2, st]))],
        out_specs=pl.BlockSpec((1, t, c), lambda st, sc: (sc[0, st], sc[1, st], 0)),
        scratch_shapes=_att_scratch(t))
    return pl.pallas_call(
        functools.partial(_fox_kernel, t=t),
        grid_spec=grid_spec,
        out_shape=jax.ShapeDtypeStruct((b, s, c), BF16),
        compiler_params=_params("arbitrary"),
        name="fox",
    )(sched, k, k_aug, q_t, q_aug, v_t)


def _mix_xattn_kernel(pa_ref, pb_ref, wm_ref, xf_ref, wq_ref, kt_ref, v_ref, wo_ref, g_ref, b_ref, of_ref, ob_ref):
    half = pa_ref.shape[1]
    mixed = (jnp.dot(pa_ref[...], wm_ref[0:half, :], preferred_element_type=F32)
             + jnp.dot(pb_ref[...], wm_ref[half:, :], preferred_element_type=F32))
    x1 = _layer_norm(ALPHA * xf_ref[...] + mixed, g_ref[0:1, :], b_ref[0:1, :])
    q = jnp.dot(x1.astype(BF16), wq_ref[...], preferred_element_type=F32) * Q_SCALE
    q = q.astype(BF16)
    outs = []
    for h in range(XA_HEADS):
        s = jnp.dot(q[:, _head(h)], kt_ref[0, _head(h), :], preferred_element_type=F32)
        p = jnp.exp2(s - jnp.max(s, axis=1, keepdims=True))
        l = jnp.sum(p, axis=1, keepdims=True)
        pv = jnp.dot(p.astype(BF16), v_ref[0, :, _head(h)], preferred_element_type=F32)
        outs.append((pv / l).astype(BF16))
    o = jnp.concatenate(outs, axis=1)
    y = jnp.dot(o, wo_ref[...], preferred_element_type=F32)
    x2 = _layer_norm(ALPHA * x1 + y, g_ref[1:2, :], b_ref[1:2, :])
    of_ref[...] = x2
    ob_ref[...] = x2.astype(BF16)


def _mix_xattn(pa, pb, w_mix, mix_layer, xf, wq, kt, v, wo, layer, g, b, seq, tm=512):
    n, d = xf.shape
    per_batch = seq // tm
    m = v.shape[1]
    row = lambda c: pl.BlockSpec((tm, c), lambda i: (i, 0))
    return pl.pallas_call(
        _mix_xattn_kernel,
        grid=(n // tm,),
        in_specs=[row(pa.shape[1]), row(pb.shape[1]),
                  pl.BlockSpec((None,) + w_mix.shape[1:], lambda i: (mix_layer, 0, 0)),
                  row(d),
                  pl.BlockSpec((None,) + wq.shape[1:], lambda i: (layer, 0, 0)),
                  pl.BlockSpec((1, XA_WIDTH, m), lambda i: (i // per_batch, 0, 0)),
                  pl.BlockSpec((1, m, XA_WIDTH), lambda i: (i // per_batch, 0, 0)),
                  pl.BlockSpec((None,) + wo.shape[1:], lambda i: (layer, 0, 0)),
                  pl.BlockSpec((2, d), lambda i: (0, 0)),
                  pl.BlockSpec((2, d), lambda i: (0, 0))],
        out_specs=[row(d), row(d)],
        out_shape=[jax.ShapeDtypeStruct((n, d), F32), jax.ShapeDtypeStruct((n, d), BF16)],
        compiler_params=_params("parallel"),
        name="mix_xattn",
    )(pa, pb, w_mix, xf, wq, kt, v, wo, g, b)


def _ffn_kernel(xb_ref, wg_ref, wu_ref, wo_ref, xf_ref, g_ref, b_ref, of_ref, ob_ref, ot_ref, acc):
    j = pl.program_id(1)

    @pl.when(j == 0)
    def _():
        acc[...] = jnp.zeros(acc.shape, F32)

    xb = xb_ref[...]
    gate = jnp.dot(xb, wg_ref[...], preferred_element_type=F32)
    up = jnp.dot(xb, wu_ref[...], preferred_element_type=F32)
    hid = gate * (1.0 / (1.0 + jnp.exp(-gate))) * up
    acc[...] += jnp.dot(hid.astype(BF16), wo_ref[...], preferred_element_type=F32)

    @pl.when(j == pl.num_programs(1) - 1)
    def _():
        y = _layer_norm(ALPHA * xf_ref[...] + acc[...], g_ref[...], b_ref[...])
        of_ref[...] = y
        ob_ref[...] = y.astype(BF16)
        ot_ref[...] = y.T.astype(BF16)


def _ffn(xb, xf, w_in, w_out, layer, g, b, tm=512, th=512):
    n, d = xf.shape
    hidden = w_out.shape[1]
    nh = hidden // th
    return pl.pallas_call(
        _ffn_kernel,
        grid=(n // tm, nh),
        in_specs=[pl.BlockSpec((tm, d), lambda i, j: (i, 0)),
                  pl.BlockSpec((None, d, th), lambda i, j: (layer, 0, j)),
                  pl.BlockSpec((None, d, th), lambda i, j: (layer, 0, j + nh)),
                  pl.BlockSpec((None, th, d), lambda i, j: (layer, j, 0)),
                  pl.BlockSpec((tm, d), lambda i, j: (i, 0)),
                  pl.BlockSpec((1, d), lambda i, j: (0, 0)),
                  pl.BlockSpec((1, d), lambda i, j: (0, 0))],
        out_specs=[pl.BlockSpec((tm, d), lambda i, j: (i, 0)),
                   pl.BlockSpec((tm, d), lambda i, j: (i, 0)),
                   pl.BlockSpec((d, tm), lambda i, j: (0, i))],
        out_shape=[jax.ShapeDtypeStruct((n, d), F32), jax.ShapeDtypeStruct((n, d), BF16),
                   jax.ShapeDtypeStruct((d, n), BF16)],
        scratch_shapes=[pltpu.VMEM((tm, d), F32)],
        compiler_params=_params("parallel", "arbitrary"),
        name="ffn",
    )(xb, w_in, w_in, w_out, xf, g, b)


def _rotary_tables(positions, dh):
    rot = dh // ROPE_FRACTION
    half = rot // 2
    inv_freq = jnp.power(ROPE_THETA, -(jnp.arange(half, dtype=F32) * 2.0 / rot))
    ang = positions.astype(F32)[..., None] * inv_freq
    cos, sin = jnp.cos(ang), jnp.sin(ang)
    zh = jnp.zeros_like(sin)
    rest = jnp.zeros(ang.shape[:-1] + (dh - rot,), F32)
    c = jnp.concatenate([cos, cos, rest + 1.0], axis=-1)
    s_lo = jnp.concatenate([zh, sin, rest], axis=-1)
    s_hi = jnp.concatenate([-sin, zh, rest], axis=-1)
    return tuple(t.reshape(-1, dh) for t in (c, s_lo, s_hi))


def _even_tables(positions):
    n = positions.size
    head = _rotary_tables(positions, HEAD_DIM)
    idx = _rotary_tables(positions, IDX_DIM)
    idx_full = tuple(jnp.tile(t, (1, LANES // IDX_DIM)) for t in idx)
    wi_scale = jnp.concatenate([jnp.full((IDX_HEADS,), IDX_HEADS ** -0.5, F32),
                                jnp.ones((LANES - IDX_DIM - IDX_HEADS,), F32)])
    pad = jnp.zeros((n, LANES - IDX_DIM), F32)
    tail = (jnp.concatenate([idx[0], pad + wi_scale], axis=1),
            jnp.concatenate([idx[1], pad], axis=1),
            jnp.concatenate([idx[2], pad], axis=1))
    return {"k": head,
            "q_t": tuple((t * Q_SCALE).T for t in head),
            "qi_t": tuple((t * IDX_DIM ** -0.5).T for t in idx_full),
            "tail": tail}


def _att_bounds(qn2, kn2, bsz, seq, t):
    n = seq // t
    qn = jnp.sqrt(qn2).reshape(ATT_HEADS, bsz, seq).transpose(1, 2, 0)
    kn = jnp.sqrt(kn2).reshape(bsz, seq, ATT_HEADS)
    k_max = jnp.max(kn, axis=1, keepdims=True)
    neg_m = -(1.01 * qn * k_max + 1.0)
    qn_tile = jnp.max(qn.reshape(bsz, n, t, ATT_HEADS), axis=2)
    kn_tile = jnp.max(kn.reshape(bsz, n, t, ATT_HEADS), axis=2)
    spread = jnp.max(2.05 * qn_tile * k_max, axis=-1) + 8.0
    fast = (spread <= FAST_RANGE_LOG2).astype(jnp.int32)
    return neg_m.astype(BF16), qn_tile, kn_tile, fast


def _aug_operands(q_entries, k_entries, bsz, seq):
    shape = (bsz, seq, ATT_HEADS)
    one, zero = jnp.ones(shape, BF16), jnp.zeros(shape, BF16)

    def pack(entries):
        cols = [one if e is None else e for e in entries]
        return jnp.stack(cols + [zero] * (AUG - len(cols)), axis=-1)

    return pack(k_entries).reshape(bsz, seq, ATT_HEADS * AUG), pack(q_entries).transpose(0, 2, 3, 1)


def _even_mixer(xb, xt, tables, wb, wt, j, pool_w, pool_scale, bsz, seq, tq):
    n = xb.shape[0]
    hw = HALF_WIDTH
    u = _proj(xb, wb, j, F32, 0, hw)
    q_t, qn2 = _proj_t(wt, j, xt, BF16, hw, hw, tables["q_t"], shift=HEAD_DIM // 8, norms=True)
    k, kn2 = _proj(xb, wb, j, BF16, 2 * hw, hw, tables["k"], shift=HEAD_DIM // 8, norms=True)
    v_t = _proj_t(wt, j, xt, BF16, 3 * hw, hw, value_rows=True)
    qi_t = _proj_t(wt, j, xt, BF16, 4 * hw, hw, tables["qi_t"], shift=IDX_DIM // 8)
    w_tail = jnp.pad(wb[j:j + 1, :, 5 * hw:], ((0, 0), (0, 0), (0, LANES - IDX_DIM - IDX_HEADS)))
    tail = _proj(xb, w_tail, 0, F32, 0, LANES, tables["tail"], shift=IDX_DIM // 8)

    ki = tail[:, :IDX_DIM].astype(BF16).reshape(bsz, seq, IDX_DIM)
    wi_t = tail[:, IDX_DIM:IDX_DIM + IDX_HEADS].T

    a = _pool(u.reshape(bsz, seq, hw), pool_w, j, pool_scale.reshape(1, hw))
    mask = _idx_mask(qi_t, wi_t, ki, tq, tq)
    neg_m, _, _, fast = _att_bounds(qn2, kn2, bsz, seq, tq)
    k_aug, q_aug = _aug_operands([neg_m], [None], bsz, seq)
    bb = _dsa(fast, k.reshape(bsz, seq, hw), k_aug, q_t, q_aug, v_t, mask, tq)
    return a.reshape(n, hw), bb.reshape(n, hw)


DSA_TILE = 512
FOX_TILE = 512
UNDERFLOW_LOG2 = 160.0


def _fox_first_tile(qn, kn, terms, bsz, seq, t):
    n = seq // t
    f32sum = sum(x.astype(F32) for x in terms)[:, :, :ATT_HEADS].reshape(bsz, n, t, ATT_HEADS)
    f_first, f_last = f32sum[:, :, 0], f32sum[:, :, t - 1]
    bound = (1.01 * qn[:, :, None] * (kn[:, None, :] + kn[:, :, None])
             + f_first[:, :, None] - f_last[:, None, :] + 1.0)
    tiles = jnp.arange(n, dtype=jnp.int32)
    needed = jnp.any(bound >= -UNDERFLOW_LOG2, axis=-1) | (tiles[:, None] == tiles[None, :])
    needed = needed & (tiles[None, :] <= tiles[:, None])
    return jnp.min(jnp.where(needed, tiles[None, None, :], n), axis=-1).astype(jnp.int32)


def _odd_mixer(xb, xt, wb, wt, j, conv_w, forget_b, bsz, seq):
    n = xb.shape[0]
    hw = HALF_WIDTH
    ugg = _proj(xb, wb, j, F32, 0, 3 * hw)
    q_t, qn2 = _proj_t(wt, j, xt, BF16, 3 * hw, hw, scale=Q_SCALE, norms=True)
    k, kn2 = _proj(xb, wb, j, BF16, 4 * hw, hw, norms=True)
    v_t = _proj_t(wt, j, xt, BF16, 5 * hw, hw, value_rows=True)
    w_tail = jnp.pad(wb[j:j + 1, :, 6 * hw:], ((0, 0), (0, 0), (0, LANES - ATT_HEADS)))
    f = _proj(xb, w_tail, 0, F32, 0, LANES)
    fb = jnp.pad(forget_b, (0, LANES - ATT_HEADS)).reshape(1, LANES)
    terms = _forget_cumsum(f.reshape(bsz, seq, LANES), fb)
    hi, mid, lo = (x[:, :, :ATT_HEADS] for x in terms)

    c = _gated_conv(ugg.reshape(bsz, seq, 3 * hw), conv_w)
    t = min(FOX_TILE, seq)
    neg_m, qn, kn, fast = _att_bounds(qn2, kn2, bsz, seq, t)
    k_aug, q_aug = _aug_operands([neg_m, None, None, None, hi, mid, lo],
                                 [None, -hi, -mid, -lo, None, None, None], bsz, seq)
    first = _fox_first_tile(qn, kn, terms, bsz, seq, t)
    d = _fox(first, fast, k.reshape(bsz, seq, hw), k_aug, q_t, q_aug, v_t, t)
    return c.reshape(n, hw), d.reshape(n, hw)


def kernel(x, mem, positions, ev_w_in, ev_pool_w, ev_pool_scale, ev_w_out, od_w_in, od_conv_w, od_forget_b,
           od_w_out, ca_w_q, ca_w_kv, ca_w_o, ffn_w_in, ffn_w_out, ln_g, ln_b):
    bsz, seq, d = x.shape
    n = bsz * seq
    m = mem.shape[1]
    tq = min(DSA_TILE, seq)
    xf = x.reshape(n, d)
    xb = xf.astype(BF16)
    memb = mem.reshape(bsz * m, d).astype(BF16)
    tables = _even_tables(positions)
    xt = xb.T
    ev_wb, od_wb = ev_w_in.astype(BF16), od_w_in.astype(BF16)
    ev_wt, od_wt = ev_wb.transpose(0, 2, 1), od_wb.transpose(0, 2, 1)
    ev_wo, od_wo, pool_w = ev_w_out.astype(BF16), od_w_out.astype(BF16), ev_pool_w.astype(BF16)
    ca_q, ca_kv, ca_o = ca_w_q.astype(BF16), ca_w_kv.astype(BF16), ca_w_o.astype(BF16)
    ffn_in, ffn_out = ffn_w_in.astype(BF16), ffn_w_out.astype(BF16)
    for i in range(DEPTH):
        j = i // 2
        if i % 2 == 0:
            parts = _even_mixer(xb, xt, tables, ev_wb, ev_wt, j, pool_w, ev_pool_scale[j], bsz, seq, tq)
            w_out = ev_wo
        else:
            parts = _odd_mixer(xb, xt, od_wb, od_wt, j, od_conv_w[j], od_forget_b[j], bsz, seq)
            w_out = od_wo
        kv = _proj(memb, ca_kv, i, BF16, 0, 2 * XA_WIDTH).reshape(bsz, m, 2 * XA_WIDTH)
        kt = kv[:, :, :XA_WIDTH].transpose(0, 2, 1)
        xf, xb = _mix_xattn(parts[0], parts[1], w_out, j, xf, ca_q, kt, kv[:, :, XA_WIDTH:], ca_o, i,
                            ln_g[i, 0:2], ln_b[i, 0:2], seq)
        xf, xb, xt = _ffn(xb, xf, ffn_in, ffn_out, i, ln_g[i, 2:3], ln_b[i, 2:3])
    return xf.reshape(bsz, seq, d)
```

```python
import functools

import jax
import jax.numpy as jnp
from jax import lax
from jax.experimental import pallas as pl
from jax.experimental.pallas import tpu as pltpu

F32 = jnp.float32
BF16 = jnp.bfloat16

D_MODEL = 2048
DEPTH = 4
CHUNK = 64
HEAD_DIM = 128
HALF_WIDTH = D_MODEL // 2
POOL_WINDOWS = (2, 4, 8, 16)
POOL_GROUP_DIM = HALF_WIDTH // len(POOL_WINDOWS)
ATT_HEADS = HALF_WIDTH // HEAD_DIM
IDX_HEADS = 16
IDX_DIM = 64
DSA_TOPK_MAX = 256
CONV_WIDTH = 3
XA_HEADS = 4
XA_WIDTH = XA_HEADS * HEAD_DIM
FFN_HIDDEN = -(-(8 * D_MODEL) // (3 * 256)) * 256
ROPE_THETA = 500000.0
ROPE_FRACTION = 4
LN_EPS = 1e-5
ALPHA = (2 * DEPTH) ** 0.25
LOG2E = 1.4426950408889634
Q_SCALE = HEAD_DIM ** -0.5 * LOG2E

LANES = 128
SUBLANES = 8
PACKED_ROWS = 16
NEG = -1e30
INT_MIN = -(2 ** 31)
VMEM_LIMIT = 60000 * 1024


def _params(*semantics):
    return pltpu.CompilerParams(dimension_semantics=semantics, vmem_limit_bytes=VMEM_LIMIT)


def _layer_norm(y, g, b):
    mu = jnp.mean(y, axis=-1, keepdims=True)
    d = y - mu
    var = jnp.mean(d * d, axis=-1, keepdims=True)
    return d * lax.rsqrt(var + LN_EPS) * g + b


AUG = 16
V_ROWS = HEAD_DIM + AUG


def _proj_kernel(*refs, shift, scale, axis, norms, value_rows):
    refs = list(refs)
    a_ref, b_ref = refs[0], refs[1]
    n_ref = refs.pop() if norms else None
    o_ref = refs.pop()
    h = jnp.dot(a_ref[...], b_ref[...], preferred_element_type=F32)
    if not (shift or norms or value_rows):
        o_ref[...] = (h * scale if scale != 1.0 else h).astype(o_ref.dtype)
        return
    if shift:
        c, s1, s2 = refs[2][...], refs[3][...], refs[4][...]
    for g in range(h.shape[axis] // LANES):
        sl = (slice(None),) * axis + (slice(g * LANES, (g + 1) * LANES),)
        hg = h[sl]
        if shift:
            hg = hg * c + pltpu.roll(hg, shift, axis) * s1 + pltpu.roll(hg, LANES - shift, axis) * s2
        elif scale != 1.0:
            hg = hg * scale
        og = hg.astype(o_ref.dtype)
        if value_rows:
            tm = og.shape[1]
            o_ref[g * V_ROWS:g * V_ROWS + LANES, :] = og
            row = lax.broadcasted_iota(jnp.int32, (AUG, tm), 0)
            o_ref[g * V_ROWS + LANES:(g + 1) * V_ROWS, :] = jnp.where(row == 0, 1.0, 0.0).astype(o_ref.dtype)
        else:
            o_ref[sl] = og
        if norms:
            sq = jnp.square(og.astype(F32))
            if axis == 0:
                n_ref[g:g + 1, :] = jnp.sum(sq, axis=0, keepdims=True)
            else:
                n_ref[:, g:g + 1] = jnp.sum(sq, axis=1, keepdims=True)


def _proj(x, w, layer, out_dtype, col0, ncols, tables=None, shift=0, scale=1.0, norms=False, tm=1024, tn=1024):
    n, k = x.shape
    tm, tn = min(tm, n), min(tn, ncols)
    assert col0 % tn == 0 and ncols % tn == 0
    in_specs = [pl.BlockSpec((tm, k), lambda i, j: (i, 0)),
                pl.BlockSpec((None, k, tn), lambda i, j: (layer, 0, col0 // tn + j))]
    args = [x, w]
    if shift:
        in_specs += [pl.BlockSpec((tm, LANES), lambda i, j: (i, 0))] * 3
        args += list(tables)
    out_specs = [pl.BlockSpec((tm, tn), lambda i, j: (i, j))]
    out_shape = [jax.ShapeDtypeStruct((n, ncols), out_dtype)]
    if norms:
        assert tn == ncols
        out_specs.append(pl.BlockSpec((tm, tn // LANES), lambda i, j: (i, 0)))
        out_shape.append(jax.ShapeDtypeStruct((n, ncols // LANES), F32))
    out = pl.pallas_call(
        functools.partial(_proj_kernel, shift=shift, scale=scale, axis=1, norms=norms, value_rows=False),
        grid=(n // tm, ncols // tn),
        in_specs=in_specs,
        out_specs=out_specs,
        out_shape=out_shape,
        compiler_params=_params("parallel", "arbitrary"),
        name="proj",
    )(*args)
    return out if norms else out[0]


def _proj_t(wt, layer, xt, out_dtype, row0, nrows, tables=None, shift=0, scale=1.0, norms=False,
            value_rows=False, tm=1024, tn=1024):
    k, n = xt.shape
    tm, tn = min(tm, n), min(tn, nrows)
    assert row0 % tn == 0 and nrows % tn == 0
    in_specs = [pl.BlockSpec((None, tn, k), lambda i, j: (layer, row0 // tn + j, 0)),
                pl.BlockSpec((k, tm), lambda i, j: (0, i))]
    args = [wt, xt]
    if shift:
        in_specs += [pl.BlockSpec((LANES, tm), lambda i, j: (0, i))] * 3
        args += list(tables)
    rows_out = tn // LANES * V_ROWS if value_rows else tn
    out_specs = [pl.BlockSpec((rows_out, tm), lambda i, j: (j, i))]
    out_shape = [jax.ShapeDtypeStruct((nrows // tn * rows_out, n), out_dtype)]
    if norms:
        assert tn == nrows
        out_specs.append(pl.BlockSpec((tn // LANES, tm), lambda i, j: (0, i)))
        out_shape.append(jax.ShapeDtypeStruct((nrows // LANES, n), F32))
    out = pl.pallas_call(
        functools.partial(_proj_kernel, shift=shift, scale=scale, axis=0, norms=norms, value_rows=value_rows),
        grid=(n // tm, nrows // tn),
        in_specs=in_specs,
        out_specs=out_specs,
        out_shape=out_shape,
        compiler_params=_params("parallel", "arbitrary"),
        name="proj_t",
    )(*args)
    return out if norms else out[0]


POOL_HALO = 16


def _pool_kernel(u_ref, halo_ref, w_ref, sc_ref, o_ref, ext, *, ts):
    i = pl.program_id(1)
    ext[0:POOL_HALO, :] = jnp.where(i > 0, halo_ref[0], 0.0)
    ext[POOL_HALO:, :] = u_ref[0]
    cnt = i * ts + lax.broadcasted_iota(jnp.int32, (ts, 1), 0) + 1
    for g, win in enumerate(POOL_WINDOWS):
        lo, hi = g * POOL_GROUP_DIM, (g + 1) * POOL_GROUP_DIM
        cur = ext[POOL_HALO:POOL_HALO + ts, lo:hi]
        s = cur
        for j in range(1, win):
            s = s + ext[POOL_HALO - j:POOL_HALO - j + ts, lo:hi]
        d = s / jnp.minimum(cnt, win).astype(F32) - cur
        y = jnp.dot(d.astype(BF16), w_ref[g], preferred_element_type=F32)
        o_ref[0, :, lo:hi] = (y * sc_ref[:, lo:hi]).astype(o_ref.dtype)


def _pool(u, w, layer, scale, ts=512):
    b, s, c = u.shape
    ts = min(ts, s)
    hb = ts // POOL_HALO
    return pl.pallas_call(
        functools.partial(_pool_kernel, ts=ts),
        grid=(b, s // ts),
        in_specs=[pl.BlockSpec((1, ts, c), lambda bi, i: (bi, i, 0)),
                  pl.BlockSpec((1, POOL_HALO, c), lambda bi, i: (bi, jnp.maximum(i * hb - 1, 0), 0)),
                  pl.BlockSpec((None,) + w.shape[1:], lambda bi, i: (layer, 0, 0, 0)),
                  pl.BlockSpec((1, c), lambda bi, i: (0, 0))],
        out_specs=pl.BlockSpec((1, ts, c), lambda bi, i: (bi, i, 0)),
        out_shape=jax.ShapeDtypeStruct((b, s, c), BF16),
        scratch_shapes=[pltpu.VMEM((ts + POOL_HALO, c), F32)],
        compiler_params=_params("parallel", "arbitrary"),
        name="pool",
    )(u, u, w, scale)


def _sortable(v):
    bits = lax.bitcast_convert_type(v, jnp.int32)
    return bits ^ ((bits >> 31) & 0x7FFFFFFF)


def _colsum8(v):
    tk, tq = v.shape
    return v.reshape(tk // SUBLANES, SUBLANES, tq).sum(axis=0)


def _idx_kernel(qi_ref, wi_ref, ki_ref, o_ref, keys, half, *, tq, tk, nk, topk):
    i = pl.program_id(1)
    nact = (i * tq + tq - 1) // tk + 1
    q_chunk = (i * tq + lax.broadcasted_iota(jnp.int32, (1, tq), 1)) // CHUNK

    def score_tile(c, carry):
        kt = ki_ref[0, pl.ds(pl.multiple_of(c * tk, tk), tk), :]
        acc = jnp.zeros((tk, tq), F32)
        for h in range(IDX_HEADS):
            il = jnp.dot(kt, qi_ref[h * IDX_DIM:(h + 1) * IDX_DIM, :], preferred_element_type=F32)
            acc = acc + jnp.maximum(il, 0.0) * wi_ref[h:h + 1, :]
        k_chunk = (c * tk + lax.broadcasted_iota(jnp.int32, (tk, 1), 0)) // CHUNK
        kk = jnp.where(k_chunk <= q_chunk, _sortable(acc), INT_MIN)
        keys[c] = kk
        half[c] = (kk >> 16).astype(jnp.int16)
        return carry

    lax.fori_loop(0, nact, score_tile, 0)

    def count(pred):
        def body(c, acc):
            return acc + _colsum8(jnp.where(pred(keys[c], c), 1, 0))
        acc = lax.fori_loop(0, nact, body, jnp.zeros((SUBLANES, tq), jnp.int32))
        return jnp.sum(acc, axis=0, keepdims=True)

    def count16(cand, strict):
        cand16 = jnp.broadcast_to(cand, (PACKED_ROWS, tq)).astype(jnp.int16)[None]

        def body(c, acc):
            h3 = half[c].reshape(tk // PACKED_ROWS, PACKED_ROWS, tq)
            hit = (h3 > cand16) if strict else (h3 >= cand16)
            ones = jnp.where(hit, jnp.int16(1), jnp.int16(0))
            for r in range(tk // PACKED_ROWS):
                acc = acc + ones[r]
            return acc

        acc = lax.fori_loop(0, nact, body, jnp.zeros((PACKED_ROWS, tq), jnp.int16))
        return jnp.sum(acc.astype(jnp.int32), axis=0, keepdims=True)

    hi = jnp.where(count16(jnp.zeros((1, tq), jnp.int32), False) >= topk, 0, -(2 ** 15))

    def hi_step(b, hi):
        cand = hi | (1 << (14 - b))
        return jnp.where(count16(cand, False) >= topk, cand, hi)

    hi = lax.fori_loop(0, 15, hi_step, hi)
    n_above = count16(hi, True)

    def low_tile(c, carry):
        kk = keys[c]
        half[c] = jnp.where((kk >> 16) == hi, (kk & 0xFFFF) - 2 ** 15, -(2 ** 15)).astype(jnp.int16)
        return carry

    lax.fori_loop(0, nact, low_tile, 0)

    def lo_step(b, lo):
        cand = lo | (1 << (15 - b))
        return jnp.where(n_above + count16(cand - 2 ** 15, False) >= topk, cand, lo)

    lo = lax.fori_loop(0, 16, lo_step, jnp.zeros((1, tq), jnp.int32))
    thr = (hi << 16) | lo
    thr = jnp.maximum(thr, INT_MIN + 1)
    n_ge = count(lambda kk, c: kk >= thr)

    def write_tiles(select):
        def body(c, carry):
            o_ref[0, 0, c] = jnp.where(select(keys[c], c), 1.0, 0.0).astype(o_ref.dtype)
            return carry
        lax.fori_loop(0, nact, body, 0)

    has_ties = jnp.max(n_ge) > topk

    @pl.when(jnp.logical_not(has_ties))
    def _():
        write_tiles(lambda kk, c: kk >= thr)

    @pl.when(has_ties)
    def _():
        need = topk - count(lambda kk, c: kk > thr)

        def key_index(c):
            return c * tk + lax.broadcasted_iota(jnp.int32, (tk, 1), 0)

        def idx_step(b, cut):
            cand = cut | (1 << (30 - b))
            below = count(lambda kk, c: (kk == thr) & (key_index(c) < cand))
            return jnp.where(below < need, cand, cut)

        cut = lax.fori_loop(0, 31, idx_step, jnp.zeros((1, tq), jnp.int32))
        write_tiles(lambda kk, c: (kk > thr) | ((kk == thr) & (key_index(c) <= cut)))

    def fill_tile(c, carry):
        o_ref[0, 0, c] = jnp.zeros((tk, tq), o_ref.dtype)
        return carry

    lax.fori_loop(nact, nk, fill_tile, 0)


def _idx_mask(qi_t, wi_t, ki, tq, tk):
    b, s, _ = ki.shape
    nq, nk = s // tq, s // tk
    topk = min(DSA_TOPK_MAX, s // 4)
    return pl.pallas_call(
        functools.partial(_idx_kernel, tq=tq, tk=tk, nk=nk, topk=topk),
        grid=(b, nq),
        in_specs=[pl.BlockSpec((IDX_HEADS * IDX_DIM, tq), lambda bi, i: (0, bi * nq + i)),
                  pl.BlockSpec((IDX_HEADS, tq), lambda bi, i: (0, bi * nq + i)),
                  pl.BlockSpec((1, s, IDX_DIM), lambda bi, i: (bi, 0, 0))],
        out_specs=pl.BlockSpec((1, 1, nk, tk, tq), lambda bi, i: (bi, i, 0, 0, 0)),
        out_shape=jax.ShapeDtypeStruct((b, nq, nk, tk, tq), BF16),
        scratch_shapes=[pltpu.VMEM((nk, tk, tq), jnp.int32), pltpu.VMEM((nk, tk, tq), jnp.int16)],
        compiler_params=_params("parallel", "arbitrary"),
        name="idx_mask",
    )(qi_t, wi_t, ki)


FAST_RANGE_LOG2 = 120.0


def _head(h):
    return slice(h * HEAD_DIM, (h + 1) * HEAD_DIM)


def _vrows(h):
    return slice(h * V_ROWS, (h + 1) * V_ROWS)


def _att_init(qt_ref, qa_ref, m_sc, acc_sc, qfull):
    m_sc[...] = jnp.full(m_sc.shape, NEG, F32)
    acc_sc[...] = jnp.zeros(acc_sc.shape, F32)
    tq = qfull.shape[2]
    for h in range(ATT_HEADS):
        qfull[h, 0:HEAD_DIM, :] = qt_ref[_head(h), :]
        qfull[h, HEAD_DIM:, :] = jnp.zeros((LANES, tq), BF16)
        lo = HEAD_DIM + h * AUG
        qfull[h, lo:lo + AUG, :] = qa_ref[0, h]


def _scores(k_ref, ka, qfull, h):
    kfull = jnp.concatenate([k_ref[0, :, _head(h)], ka], axis=1)
    return jnp.dot(kfull, qfull[h], preferred_element_type=F32)


def _for_heads(score_fn, step_fn):
    s_next = score_fn(0)
    for h in range(ATT_HEADS):
        s = s_next
        if h + 1 < ATT_HEADS:
            s_next = score_fn(h + 1)
        step_fn(s, h)


def _fast_step(s, keep, vt, h, acc_sc):
    p = jnp.exp2(s).astype(BF16)
    if keep is not None:
        p = p * keep
    acc_sc[h] += jnp.dot(vt, p, preferred_element_type=F32)


def _exact_step(s, vt, h, m_sc, acc_sc):
    m_prev = m_sc[h]
    m_new = jnp.maximum(m_prev, jnp.max(s, axis=0, keepdims=True))
    alpha = jnp.exp2(m_prev - m_new)
    p = jnp.exp2(s - m_new)
    acc_sc[h] = alpha * acc_sc[h] + jnp.dot(vt, p.astype(BF16), preferred_element_type=F32)
    m_sc[h] = m_new


def _att_finish(o_ref, acc_sc):
    for h in range(ATT_HEADS):
        a = acc_sc[h]
        o = (a[0:HEAD_DIM] / a[HEAD_DIM:HEAD_DIM + 1]).T
        o_ref[0, :, _head(h)] = o.astype(o_ref.dtype)


def _att_scratch(tq):
    return [pltpu.VMEM((ATT_HEADS, 1, tq), F32),
            pltpu.VMEM((ATT_HEADS, V_ROWS, tq), F32),
            pltpu.VMEM((ATT_HEADS, 2 * HEAD_DIM, tq), BF16)]


def _pair_tiles(r, j, n):
    second = j > r
    return jnp.where(second, n - 1 - r, r), jnp.where(second, j - r - 1, j)


def _dsa_kernel(fast_ref, k_ref, ka_ref, qt_ref, qa_ref, vt_ref, mask_ref, o_ref, m_sc, acc_sc, qfull, *, n):
    b = pl.program_id(0)
    i, j = _pair_tiles(pl.program_id(1), pl.program_id(2), n)
    fast = fast_ref[b, i] == 1

    @pl.when(j == 0)
    def _():
        _att_init(qt_ref, qa_ref, m_sc, acc_sc, qfull)

    @pl.when(fast)
    def _():
        keep = mask_ref[0, 0, 0]
        ka = ka_ref[0]
        _for_heads(lambda h: _scores(k_ref, ka, qfull, h),
                   lambda s, h: _fast_step(s, keep, vt_ref[_vrows(h), :], h, acc_sc))

    @pl.when(jnp.logical_not(fast))
    def _():
        bias = (mask_ref[0, 0, 0].astype(F32) - 1.0) * (-NEG)
        ka = ka_ref[0]
        _for_heads(lambda h: _scores(k_ref, ka, qfull, h) + bias,
                   lambda s, h: _exact_step(s, vt_ref[_vrows(h), :], h, m_sc, acc_sc))

    @pl.when(j == i)
    def _():
        _att_finish(o_ref, acc_sc)


def _dsa(fast, k, k_aug, q_t, q_aug, v_t, mask, t):
    b, s, c = k.shape
    n = s // t
    assert n % 2 == 0, "query tiles are processed in pairs"

    def q_tile(r, j):
        return _pair_tiles(r, j, n)[0]

    def k_tile(r, j):
        return _pair_tiles(r, j, n)[1]

    grid_spec = pltpu.PrefetchScalarGridSpec(
        num_scalar_prefetch=1,
        grid=(b, n // 2, n + 1),
        in_specs=[pl.BlockSpec((1, t, c), lambda bi, r, j, f: (bi, k_tile(r, j), 0)),
                  pl.BlockSpec((1, t, LANES), lambda bi, r, j, f: (bi, k_tile(r, j), 0)),
                  pl.BlockSpec((c, t), lambda bi, r, j, f: (0, bi * n + q_tile(r, j))),
                  pl.BlockSpec((1, ATT_HEADS, AUG, t), lambda bi, r, j, f: (bi, 0, 0, q_tile(r, j))),
                  pl.BlockSpec((ATT_HEADS * V_ROWS, t), lambda bi, r, j, f: (0, bi * n + k_tile(r, j))),
                  pl.BlockSpec((1, 1, 1, t, t), lambda bi, r, j, f: (bi, q_tile(r, j), k_tile(r, j), 0, 0))],
        out_specs=pl.BlockSpec((1, t, c), lambda bi, r, j, f: (bi, q_tile(r, j), 0)),
        scratch_shapes=_att_scratch(t))
    return pl.pallas_call(
        functools.partial(_dsa_kernel, n=n),
        grid_spec=grid_spec,
        out_shape=jax.ShapeDtypeStruct((b, s, c), BF16),
        compiler_params=_params("parallel", "parallel", "arbitrary"),
        name="dsa",
    )(fast, k, k_aug, q_t, q_aug, v_t, mask)


CONV_HALO = 8


def _conv_kernel(u_ref, gb_ref, gc_ref, hu_ref, hgc_ref, w_ref, o_ref, ext, *, ts):
    i = pl.program_id(1)
    ext[0:CONV_HALO, :] = jnp.where(i > 0, hgc_ref[0] * hu_ref[0], 0.0)
    ext[CONV_HALO:, :] = gc_ref[0] * u_ref[0]
    conv = None
    for t in range(CONV_WIDTH):
        off = CONV_HALO - (CONV_WIDTH - 1) + t
        term = ext[off:off + ts, :] * w_ref[t:t + 1, :]
        conv = term if conv is None else conv + term
    o_ref[0] = (gb_ref[0] * conv).astype(o_ref.dtype)


def _gated_conv(h, w, ts=512):
    b, s, c3 = h.shape
    c = c3 // 3
    ts = min(ts, s)
    hb = ts // CONV_HALO

    def halo_map(col):
        return lambda bi, i: (bi, jnp.maximum(i * hb - 1, 0), col)

    return pl.pallas_call(
        functools.partial(_conv_kernel, ts=ts),
        grid=(b, s // ts),
        in_specs=[pl.BlockSpec((1, ts, c), lambda bi, i: (bi, i, 0)),
                  pl.BlockSpec((1, ts, c), lambda bi, i: (bi, i, 1)),
                  pl.BlockSpec((1, ts, c), lambda bi, i: (bi, i, 2)),
                  pl.BlockSpec((1, CONV_HALO, c), halo_map(0)),
                  pl.BlockSpec((1, CONV_HALO, c), halo_map(2)),
                  pl.BlockSpec(w.shape, lambda bi, i: (0, 0))],
        out_specs=pl.BlockSpec((1, ts, c), lambda bi, i: (bi, i, 0)),
        out_shape=jax.ShapeDtypeStruct((b, s, c), BF16),
        scratch_shapes=[pltpu.VMEM((ts + CONV_HALO, c), F32)],
        compiler_params=_params("parallel", "arbitrary"),
        name="gated_conv",
    )(h, h, h, h, h, w)


GATE_CHUNK = 256


def _fgate_kernel(f_ref, fb_ref, hi_ref, mid_ref, lo_ref):
    s = f_ref.shape[1]
    r = lax.broadcasted_iota(jnp.int32, (GATE_CHUNK, GATE_CHUNK), 0)
    c = lax.broadcasted_iota(jnp.int32, (GATE_CHUNK, GATE_CHUNK), 1)
    tri = (c <= r).astype(F32)

    def body(t, carry):
        rows = pl.ds(t * GATE_CHUNK, GATE_CHUNK)
        z = f_ref[0, rows, :] + fb_ref[...]
        log_f = -(jnp.maximum(-z, 0.0) + jnp.log1p(jnp.exp(-jnp.abs(z))))
        cs = jnp.dot(tri, log_f, preferred_element_type=F32, precision=lax.Precision.HIGHEST) + carry
        b2 = cs * LOG2E
        hi = b2.astype(BF16)
        r1 = b2 - hi.astype(F32)
        mid = r1.astype(BF16)
        hi_ref[0, rows, :] = hi
        mid_ref[0, rows, :] = mid
        lo_ref[0, rows, :] = (r1 - mid.astype(F32)).astype(BF16)
        return cs[GATE_CHUNK - 1:GATE_CHUNK, :]

    lax.fori_loop(0, s // GATE_CHUNK, body, jnp.zeros((1, LANES), F32))


def _forget_cumsum(f, fb):
    b, s, c = f.shape
    spec = pl.BlockSpec((1, s, c), lambda bi: (bi, 0, 0))
    return pl.pallas_call(
        _fgate_kernel,
        grid=(b,),
        in_specs=[spec, pl.BlockSpec((1, c), lambda bi: (0, 0))],
        out_specs=[spec, spec, spec],
        out_shape=[jax.ShapeDtypeStruct((b, s, c), BF16)] * 3,
        compiler_params=_params("parallel"),
        name="forget_cumsum",
    )(f, fb)


def _fox_schedule(first, fast, n):
    bsz = first.shape[0]
    steps = bsz * n * (n + 1) // 2
    tiles = jnp.arange(n, dtype=jnp.int32)
    cnt = (tiles[None, :] - first + 1).reshape(-1)
    end = jnp.cumsum(cnt)
    t = jnp.arange(steps, dtype=jnp.int32)
    tc = jnp.minimum(t, end[-1] - 1)
    row = jnp.sum((end[None, :] <= tc[:, None]).astype(jnp.int32), axis=1)
    rel = tc - (end - cnt)[row]
    q_tile = row % n
    active = t < end[-1]
    flags = active.astype(jnp.int32) + 2 * (active & (rel == cnt[row] - 1)).astype(jnp.int32)
    return jnp.stack([row // n, q_tile, q_tile - rel, flags, fast.reshape(-1)[row]]).astype(jnp.int32)


def _fox_kernel(sched_ref, k_ref, ka_ref, qt_ref, qa_ref, vt_ref, o_ref, m_sc, acc_sc, qfull, *, t):
    step = pl.program_id(0)
    i, kj, flags = sched_ref[1, step], sched_ref[2, step], sched_ref[3, step]
    fast = sched_ref[4, step] == 1
    active = (flags & 1) == 1
    diagonal = active & (kj == i)
    older = active & (kj != i)

    def causal():
        return lax.broadcasted_iota(jnp.int32, (t, t), 0) <= lax.broadcasted_iota(jnp.int32, (t, t), 1)

    def scores(masked):
        ka = ka_ref[0]

        def fn(h):
            s = _scores(k_ref, ka, qfull, h)
            return jnp.where(causal(), s, NEG) if masked else s
        return fn

    def fast_tile(masked):
        _for_heads(scores(masked), lambda s, h: _fast_step(s, None, vt_ref[_vrows(h), :], h, acc_sc))

    def exact_tile(masked):
        _for_heads(scores(masked), lambda s, h: _exact_step(s, vt_ref[_vrows(h), :], h, m_sc, acc_sc))

    @pl.when(diagonal)
    def _():
        _att_init(qt_ref, qa_ref, m_sc, acc_sc, qfull)

    @pl.when(diagonal & fast)
    def _():
        fast_tile(True)

    @pl.when(diagonal & jnp.logical_not(fast))
    def _():
        exact_tile(True)

    @pl.when(older & fast)
    def _():
        fast_tile(False)

    @pl.when(older & jnp.logical_not(fast))
    def _():
        exact_tile(False)

    @pl.when((flags & 2) == 2)
    def _():
        _att_finish(o_ref, acc_sc)


def _fox(first, fast, k, k_aug, q_t, q_aug, v_t, t):
    b, s, c = k.shape
    n = s // t
    sched = _fox_schedule(first, fast, n)
    grid_spec = pltpu.PrefetchScalarGridSpec(
        num_scalar_prefetch=1,
        grid=(sched.shape[1],),
        in_specs=[pl.BlockSpec((1, t, c), lambda st, sc: (sc[0, st], sc[2, st], 0)),
                  pl.BlockSpec((1, t, LANES), lambda st, sc: (sc[0, st], sc[2, st], 0)),
                  pl.BlockSpec((c, t), lambda st, sc: (0, sc[0, st] * n + sc[1, st])),
                  pl.BlockSpec((1, ATT_HEADS, AUG, t), lambda st, sc: (sc[0, st], 0, 0, sc[1, st])),
                  pl.BlockSpec((ATT_HEADS * V_ROWS, t), lambda st, sc: (0, sc[0, st] * n + sc[2, st]))],
        out_specs=pl.BlockSpec((1, t, c), lambda st, sc: (sc[0, st], sc[1, st], 0)),
        scratch_shapes=_att_scratch(t))
    return pl.pallas_call(
        functools.partial(_fox_kernel, t=t),
        grid_spec=grid_spec,
        out_shape=jax.ShapeDtypeStruct((b, s, c), BF16),
        compiler_params=_params("arbitrary"),
        name="fox",
    )(sched, k, k_aug, q_t, q_aug, v_t)


def _mix_xattn_kernel(pa_ref, pb_ref, wm_ref, xf_ref, wq_ref, kt_ref, v_ref, wo_ref, g_ref, b_ref, of_ref, ob_ref):
    half = pa_ref.shape[1]
    mixed = (jnp.dot(pa_ref[...], wm_ref[0:half, :], preferred_element_type=F32)
             + jnp.dot(pb_ref[...], wm_ref[half:, :], preferred_element_type=F32))
    x1 = _layer_norm(ALPHA * xf_ref[...] + mixed, g_ref[0:1, :], b_ref[0:1, :])
    q = jnp.dot(x1.astype(BF16), wq_ref[...], preferred_element_type=F32) * Q_SCALE
    q = q.astype(BF16)
    outs = []
    for h in range(XA_HEADS):
        s = jnp.dot(q[:, _head(h)], kt_ref[0, _head(h), :], preferred_element_type=F32)
        p = jnp.exp2(s - jnp.max(s, axis=1, keepdims=True))
        l = jnp.sum(p, axis=1, keepdims=True)
        pv = jnp.dot(p.astype(BF16), v_ref[0, :, _head(h)], preferred_element_type=F32)
        outs.append((pv / l).astype(BF16))
    o = jnp.concatenate(outs, axis=1)
    y = jnp.dot(o, wo_ref[...], preferred_element_type=F32)
    x2 = _layer_norm(ALPHA * x1 + y, g_ref[1:2, :], b_ref[1:2, :])
    of_ref[...] = x2
    ob_ref[...] = x2.astype(BF16)


def _mix_xattn(pa, pb, w_mix, mix_layer, xf, wq, kt, v, wo, layer, g, b, seq, tm=512):
    n, d = xf.shape
    per_batch = seq // tm
    m = v.shape[1]
    row = lambda c: pl.BlockSpec((tm, c), lambda i: (i, 0))
    return pl.pallas_call(
        _mix_xattn_kernel,
        grid=(n // tm,),
        in_specs=[row(pa.shape[1]), row(pb.shape[1]),
                  pl.BlockSpec((None,) + w_mix.shape[1:], lambda i: (mix_layer, 0, 0)),
                  row(d),
                  pl.BlockSpec((None,) + wq.shape[1:], lambda i: (layer, 0, 0)),
                  pl.BlockSpec((1, XA_WIDTH, m), lambda i: (i // per_batch, 0, 0)),
                  pl.BlockSpec((1, m, XA_WIDTH), lambda i: (i // per_batch, 0, 0)),
                  pl.BlockSpec((None,) + wo.shape[1:], lambda i: (layer, 0, 0)),
                  pl.BlockSpec((2, d), lambda i: (0, 0)),
                  pl.BlockSpec((2, d), lambda i: (0, 0))],
        out_specs=[row(d), row(d)],
        out_shape=[jax.ShapeDtypeStruct((n, d), F32), jax.ShapeDtypeStruct((n, d), BF16)],
        compiler_params=_params("parallel"),
        name="mix_xattn",
    )(pa, pb, w_mix, xf, wq, kt, v, wo, g, b)


def _ffn_kernel(xb_ref, wg_ref, wu_ref, wo_ref, xf_ref, g_ref, b_ref, of_ref, ob_ref, ot_ref, acc):
    j = pl.program_id(1)

    @pl.when(j == 0)
    def _():
        acc[...] = jnp.zeros(acc.shape, F32)

    xb = xb_ref[...]
    gate = jnp.dot(xb, wg_ref[...], preferred_element_type=F32)
    up = jnp.dot(xb, wu_ref[...], preferred_element_type=F32)
    hid = gate * (1.0 / (1.0 + jnp.exp(-gate))) * up
    acc[...] += jnp.dot(hid.astype(BF16), wo_ref[...], preferred_element_type=F32)

    @pl.when(j == pl.num_programs(1) - 1)
    def _():
        y = _layer_norm(ALPHA * xf_ref[...] + acc[...], g_ref[...], b_ref[...])
        of_ref[...] = y
        ob_ref[...] = y.astype(BF16)
        ot_ref[...] = y.T.astype(BF16)


def _ffn(xb, xf, w_in, w_out, layer, g, b, tm=512, th=512):
    n, d = xf.shape
    hidden = w_out.shape[1]
    nh = hidden // th
    return pl.pallas_call(
        _ffn_kernel,
        grid=(n // tm, nh),
        in_specs=[pl.BlockSpec((tm, d), lambda i, j: (i, 0)),
                  pl.BlockSpec((None, d, th), lambda i, j: (layer, 0, j)),
                  pl.BlockSpec((None, d, th), lambda i, j: (layer, 0, j + nh)),
                  pl.BlockSpec((None, th, d), lambda i, j: (layer, j, 0)),
                  pl.BlockSpec((tm, d), lambda i, j: (i, 0)),
                  pl.BlockSpec((1, d), lambda i, j: (0, 0)),
                  pl.BlockSpec((1, d), lambda i, j: (0, 0))],
        out_specs=[pl.BlockSpec((tm, d), lambda i, j: (i, 0)),
                   pl.BlockSpec((tm, d), lambda i, j: (i, 0)),
                   pl.BlockSpec((d, tm), lambda i, j: (0, i))],
        out_shape=[jax.ShapeDtypeStruct((n, d), F32), jax.ShapeDtypeStruct((n, d), BF16),
                   jax.ShapeDtypeStruct((d, n), BF16)],
        scratch_shapes=[pltpu.VMEM((tm, d), F32)],
        compiler_params=_params("parallel", "arbitrary"),
        name="ffn",
    )(xb, w_in, w_in, w_out, xf, g, b)


def _rotary_tables(positions, dh):
    rot = dh // ROPE_FRACTION
    half = rot // 2
    inv_freq = jnp.power(ROPE_THETA, -(jnp.arange(half, dtype=F32) * 2.0 / rot))
    ang = positions.astype(F32)[..., None] * inv_freq
    cos, sin = jnp.cos(ang), jnp.sin(ang)
    zh = jnp.zeros_like(sin)
    rest = jnp.zeros(ang.shape[:-1] + (dh - rot,), F32)
    c = jnp.concatenate([cos, cos, rest + 1.0], axis=-1)
    s_lo = jnp.concatenate([zh, sin, rest], axis=-1)
    s_hi = jnp.concatenate([-sin, zh, rest], axis=-1)
    return tuple(t.reshape(-1, dh) for t in (c, s_lo, s_hi))


def _even_tables(positions):
    n = positions.size
    head = _rotary_tables(positions, HEAD_DIM)
    idx = _rotary_tables(positions, IDX_DIM)
    idx_full = tuple(jnp.tile(t, (1, LANES // IDX_DIM)) for t in idx)
    wi_scale = jnp.concatenate([jnp.full((IDX_HEADS,), IDX_HEADS ** -0.5, F32),
                                jnp.ones((LANES - IDX_DIM - IDX_HEADS,), F32)])
    pad = jnp.zeros((n, LANES - IDX_DIM), F32)
    tail = (jnp.concatenate([idx[0], pad + wi_scale], axis=1),
            jnp.concatenate([idx[1], pad], axis=1),
            jnp.concatenate([idx[2], pad], axis=1))
    return {"k": head,
            "q_t": tuple((t * Q_SCALE).T for t in head),
            "qi_t": tuple((t * IDX_DIM ** -0.5).T for t in idx_full),
            "tail": tail}


def _att_bounds(qn2, kn2, bsz, seq, t):
    n = seq // t
    qn = jnp.sqrt(qn2).reshape(ATT_HEADS, bsz, seq).transpose(1, 2, 0)
    kn = jnp.sqrt(kn2).reshape(bsz, seq, ATT_HEADS)
    k_max = jnp.max(kn, axis=1, keepdims=True)
    neg_m = -(1.01 * qn * k_max + 1.0)
    qn_tile = jnp.max(qn.reshape(bsz, n, t, ATT_HEADS), axis=2)
    kn_tile = jnp.max(kn.reshape(bsz, n, t, ATT_HEADS), axis=2)
    spread = jnp.max(2.05 * qn_tile * k_max, axis=-1) + 8.0
    fast = (spread <= FAST_RANGE_LOG2).astype(jnp.int32)
    return neg_m.astype(BF16), qn_tile, kn_tile, fast


def _aug_operands(q_entries, k_entries, bsz, seq):
    shape = (bsz, seq, ATT_HEADS)
    one, zero = jnp.ones(shape, BF16), jnp.zeros(shape, BF16)

    def pack(entries):
        cols = [one if e is None else e for e in entries]
        return jnp.stack(cols + [zero] * (AUG - len(cols)), axis=-1)

    return pack(k_entries).reshape(bsz, seq, ATT_HEADS * AUG), pack(q_entries).transpose(0, 2, 3, 1)


def _even_mixer(xb, xt, tables, wb, wt, j, pool_w, pool_scale, bsz, seq, tq):
    n = xb.shape[0]
    hw = HALF_WIDTH
    u = _proj(xb, wb, j, F32, 0, hw)
    q_t, qn2 = _proj_t(wt, j, xt, BF16, hw, hw, tables["q_t"], shift=HEAD_DIM // 8, norms=True)
    k, kn2 = _proj(xb, wb, j, BF16, 2 * hw, hw, tables["k"], shift=HEAD_DIM // 8, norms=True)
    v_t = _proj_t(wt, j, xt, BF16, 3 * hw, hw, value_rows=True)
    qi_t = _proj_t(wt, j, xt, BF16, 4 * hw, hw, tables["qi_t"], shift=IDX_DIM // 8)
    w_tail = jnp.pad(wb[j:j + 1, :, 5 * hw:], ((0, 0), (0, 0), (0, LANES - IDX_DIM - IDX_HEADS)))
    tail = _proj(xb, w_tail, 0, F32, 0, LANES, tables["tail"], shift=IDX_DIM // 8)

    ki = tail[:, :IDX_DIM].astype(BF16).reshape(bsz, seq, IDX_DIM)
    wi_t = tail[:, IDX_DIM:IDX_DIM + IDX_HEADS].T

    a = _pool(u.reshape(bsz, seq, hw), pool_w, j, pool_scale.reshape(1, hw))
    mask = _idx_mask(qi_t, wi_t, ki, tq, tq)
    neg_m, _, _, fast = _att_bounds(qn2, kn2, bsz, seq, tq)
    k_aug, q_aug = _aug_operands([neg_m], [None], bsz, seq)
    bb = _dsa(fast, k.reshape(bsz, seq, hw), k_aug, q_t, q_aug, v_t, mask, tq)
    return a.reshape(n, hw), bb.reshape(n, hw)


DSA_TILE = 512
FOX_TILE = 512
UNDERFLOW_LOG2 = 160.0


def _fox_first_tile(qn, kn, terms, bsz, seq, t):
    n = seq // t
    f32sum = sum(x.astype(F32) for x in terms)[:, :, :ATT_HEADS].reshape(bsz, n, t, ATT_HEADS)
    f_first, f_last = f32sum[:, :, 0], f32sum[:, :, t - 1]
    bound = (1.01 * qn[:, :, None] * (kn[:, None, :] + kn[:, :, None])
             + f_first[:, :, None] - f_last[:, None, :] + 1.0)
    tiles = jnp.arange(n, dtype=jnp.int32)
    needed = jnp.any(bound >= -UNDERFLOW_LOG2, axis=-1) | (tiles[:, None] == tiles[None, :])
    needed = needed & (tiles[None, :] <= tiles[:, None])
    return jnp.min(jnp.where(needed, tiles[None, None, :], n), axis=-1).astype(jnp.int32)


def _odd_mixer(xb, xt, wb, wt, j, conv_w, forget_b, bsz, seq):
    n = xb.shape[0]
    hw = HALF_WIDTH
    ugg = _proj(xb, wb, j, F32, 0, 3 * hw)
    q_t, qn2 = _proj_t(wt, j, xt, BF16, 3 * hw, hw, scale=Q_SCALE, norms=True)
    k, kn2 = _proj(xb, wb, j, BF16, 4 * hw, hw, norms=True)
    v_t = _proj_t(wt, j, xt, BF16, 5 * hw, hw, value_rows=True)
    w_tail = jnp.pad(wb[j:j + 1, :, 6 * hw:], ((0, 0), (0, 0), (0, LANES - ATT_HEADS)))
    f = _proj(xb, w_tail, 0, F32, 0, LANES)
    fb = jnp.pad(forget_b, (0, LANES - ATT_HEADS)).reshape(1, LANES)
    terms = _forget_cumsum(f.reshape(bsz, seq, LANES), fb)
    hi, mid, lo = (x[:, :, :ATT_HEADS] for x in terms)

    c = _gated_conv(ugg.reshape(bsz, seq, 3 * hw), conv_w)
    t = min(FOX_TILE, seq)
    neg_m, qn, kn, fast = _att_bounds(qn2, kn2, bsz, seq, t)
    k_aug, q_aug = _aug_operands([neg_m, None, None, None, hi, mid, lo],
                                 [None, -hi, -mid, -lo, None, None, None], bsz, seq)
    first = _fox_first_tile(qn, kn, terms, bsz, seq, t)
    d = _fox(first, fast, k.reshape(bsz, seq, hw), k_aug, q_t, q_aug, v_t, t)
    return c.reshape(n, hw), d.reshape(n, hw)


def kernel(x, mem, positions, ev_w_in, ev_pool_w, ev_pool_scale, ev_w_out, od_w_in, od_conv_w, od_forget_b,
           od_w_out, ca_w_q, ca_w_kv, ca_w_o, ffn_w_in, ffn_w_out, ln_g, ln_b):
    bsz, seq, d = x.shape
    n = bsz * seq
    m = mem.shape[1]
    tq = min(DSA_TILE, seq)
    xf = x.reshape(n, d)
    xb = xf.astype(BF16)
    memb = mem.reshape(bsz * m, d).astype(BF16)
    tables = _even_tables(positions)
    xt = xb.T
    ev_wb, od_wb = ev_w_in.astype(BF16), od_w_in.astype(BF16)
    ev_wt, od_wt = ev_wb.transpose(0, 2, 1), od_wb.transpose(0, 2, 1)
    ev_wo, od_wo, pool_w = ev_w_out.astype(BF16), od_w_out.astype(BF16), ev_pool_w.astype(BF16)
    ca_q, ca_kv, ca_o = ca_w_q.astype(BF16), ca_w_kv.astype(BF16), ca_w_o.astype(BF16)
    ffn_in, ffn_out = ffn_w_in.astype(BF16), ffn_w_out.astype(BF16)
    for i in range(DEPTH):
        j = i // 2
        if i % 2 == 0:
            parts = _even_mixer(xb, xt, tables, ev_wb, ev_wt, j, pool_w, ev_pool_scale[j], bsz, seq, tq)
            w_out = ev_wo
        else:
            parts = _odd_mixer(xb, xt, od_wb, od_wt, j, od_conv_w[j], od_forget_b[j], bsz, seq)
            w_out = od_wo
        kv = _proj(memb, ca_kv, i, BF16, 0, 2 * XA_WIDTH).reshape(bsz, m, 2 * XA_WIDTH)
        kt = kv[:, :, :XA_WIDTH].transpose(0, 2, 1)
        xf, xb = _mix_xattn(parts[0], parts[1], w_out, j, xf, ca_q, kt, kv[:, :, XA_WIDTH:], ca_o, i,
                            ln_g[i, 0:2], ln_b[i, 0:2], seq)
        xf, xb, xt = _ffn(xb, xf, ffn_in, ffn_out, i, ln_g[i, 2:3], ln_b[i, 2:3])
    return xf.reshape(bsz, seq, d)
```

```python
import functools

import jax
import jax.numpy as jnp
from jax import lax
from jax.experimental import pallas as pl
from jax.experimental.pallas import tpu as pltpu

F32 = jnp.float32
BF16 = jnp.bfloat16

D_MODEL = 2048
DEPTH = 4
CHUNK = 64
HEAD_DIM = 128
HALF_WIDTH = D_MODEL // 2
POOL_WINDOWS = (2, 4, 8, 16)
POOL_GROUP_DIM = HALF_WIDTH // len(POOL_WINDOWS)
ATT_HEADS = HALF_WIDTH // HEAD_DIM
IDX_HEADS = 16
IDX_DIM = 64
DSA_TOPK_MAX = 256
CONV_WIDTH = 3
XA_HEADS = 4
XA_WIDTH = XA_HEADS * HEAD_DIM
FFN_HIDDEN = -(-(8 * D_MODEL) // (3 * 256)) * 256
ROPE_THETA = 500000.0
ROPE_FRACTION = 4
LN_EPS = 1e-5
ALPHA = (2 * DEPTH) ** 0.25
LOG2E = 1.4426950408889634
Q_SCALE = HEAD_DIM ** -0.5 * LOG2E

LANES = 128
SUBLANES = 8
PACKED_ROWS = 16
NEG = -1e30
INT_MIN = -(2 ** 31)
VMEM_LIMIT = 60000 * 1024


def _params(*semantics):
    return pltpu.CompilerParams(dimension_semantics=semantics, vmem_limit_bytes=VMEM_LIMIT)


def _layer_norm(y, g, b):
    mu = jnp.mean(y, axis=-1, keepdims=True)
    d = y - mu
    var = jnp.mean(d * d, axis=-1, keepdims=True)
    return d * lax.rsqrt(var + LN_EPS) * g + b


AUG = 16
V_ROWS = HEAD_DIM + AUG


def _proj_kernel(*refs, shift, scale, axis, norms, value_rows):
    refs = list(refs)
    a_ref, b_ref = refs[0], refs[1]
    n_ref = refs.pop() if norms else None
    o_ref = refs.pop()
    h = jnp.dot(a_ref[...], b_ref[...], preferred_element_type=F32)
    if not (shift or norms or value_rows):
        o_ref[...] = (h * scale if scale != 1.0 else h).astype(o_ref.dtype)
        return
    if shift:
        c, s1, s2 = refs[2][...], refs[3][...], refs[4][...]
    for g in range(h.shape[axis] // LANES):
        sl = (slice(None),) * axis + (slice(g * LANES, (g + 1) * LANES),)
        hg = h[sl]
        if shift:
            hg = hg * c + pltpu.roll(hg, shift, axis) * s1 + pltpu.roll(hg, LANES - shift, axis) * s2
        elif scale != 1.0:
            hg = hg * scale
        og = hg.astype(o_ref.dtype)
        if value_rows:
            tm = og.shape[1]
            o_ref[g * V_ROWS:g * V_ROWS + LANES, :] = og
            row = lax.broadcasted_iota(jnp.int32, (AUG, tm), 0)
            o_ref[g * V_ROWS + LANES:(g + 1) * V_ROWS, :] = jnp.where(row == 0, 1.0, 0.0).astype(o_ref.dtype)
        else:
            o_ref[sl] = og
        if norms:
            sq = jnp.square(og.astype(F32))
            if axis == 0:
                n_ref[g:g + 1, :] = jnp.sum(sq, axis=0, keepdims=True)
            else:
                n_ref[:, g:g + 1] = jnp.sum(sq, axis=1, keepdims=True)


def _proj(x, w, layer, out_dtype, col0, ncols, tables=None, shift=0, scale=1.0, norms=False, tm=1024, tn=1024):
    n, k = x.shape
    tm, tn = min(tm, n), min(tn, ncols)
    assert col0 % tn == 0 and ncols % tn == 0
    in_specs = [pl.BlockSpec((tm, k), lambda i, j: (i, 0)),
                pl.BlockSpec((None, k, tn), lambda i, j: (layer, 0, col0 // tn + j))]
    args = [x, w]
    if shift:
        in_specs += [pl.BlockSpec((tm, LANES), lambda i, j: (i, 0))] * 3
        args += list(tables)
    out_specs = [pl.BlockSpec((tm, tn), lambda i, j: (i, j))]
    out_shape = [jax.ShapeDtypeStruct((n, ncols), out_dtype)]
    if norms:
        assert tn == ncols
        out_specs.append(pl.BlockSpec((tm, tn // LANES), lambda i, j: (i, 0)))
        out_shape.append(jax.ShapeDtypeStruct((n, ncols // LANES), F32))
    out = pl.pallas_call(
        functools.partial(_proj_kernel, shift=shift, scale=scale, axis=1, norms=norms, value_rows=False),
        grid=(n // tm, ncols // tn),
        in_specs=in_specs,
        out_specs=out_specs,
        out_shape=out_shape,
        compiler_params=_params("parallel", "arbitrary"),
        name="proj",
    )(*args)
    return out if norms else out[0]


def _proj_t(wt, layer, xt, out_dtype, row0, nrows, tables=None, shift=0, scale=1.0, norms=False,
            value_rows=False, tm=1024, tn=1024):
    k, n = xt.shape
    tm, tn = min(tm, n), min(tn, nrows)
    assert row0 % tn == 0 and nrows % tn == 0
    in_specs = [pl.BlockSpec((None, tn, k), lambda i, j: (layer, row0 // tn + j, 0)),
                pl.BlockSpec((k, tm), lambda i, j: (0, i))]
    args = [wt, xt]
    if shift:
        in_specs += [pl.BlockSpec((LANES, tm), lambda i, j: (0, i))] * 3
        args += list(tables)
    rows_out = tn // LANES * V_ROWS if value_rows else tn
    out_specs = [pl.BlockSpec((rows_out, tm), lambda i, j: (j, i))]
    out_shape = [jax.ShapeDtypeStruct((nrows // tn * rows_out, n), out_dtype)]
    if norms:
        assert tn == nrows
        out_specs.append(pl.BlockSpec((tn // LANES, tm), lambda i, j: (0, i)))
        out_shape.append(jax.ShapeDtypeStruct((nrows // LANES, n), F32))
    out = pl.pallas_call(
        functools.partial(_proj_kernel, shift=shift, scale=scale, axis=0, norms=norms, value_rows=value_rows),
        grid=(n // tm, nrows // tn),
        in_specs=in_specs,
        out_specs=out_specs,
        out_shape=out_shape,
        compiler_params=_params("parallel", "arbitrary"),
        name="proj_t",
    )(*args)
    return out if norms else out[0]


POOL_HALO = 16


def _proj_pool_kernel(x_ref, xh_ref, w_ref, pw_ref, sc_ref, o_ref, ext, *, tm, tiles_per_seq):
    ti = pl.program_id(0) % tiles_per_seq
    halo = jnp.dot(xh_ref[...], w_ref[...], preferred_element_type=F32)
    ext[0:POOL_HALO, :] = jnp.where(ti > 0, halo, 0.0)
    ext[POOL_HALO:, :] = jnp.dot(x_ref[...], w_ref[...], preferred_element_type=F32)
    cnt = ti * tm + lax.broadcasted_iota(jnp.int32, (tm, 1), 0) + 1
    for g, win in enumerate(POOL_WINDOWS):
        lo, hi = g * POOL_GROUP_DIM, (g + 1) * POOL_GROUP_DIM
        cur = ext[POOL_HALO:POOL_HALO + tm, lo:hi]
        s = cur
        for j in range(1, win):
            s = s + ext[POOL_HALO - j:POOL_HALO - j + tm, lo:hi]
        d = s / jnp.minimum(cnt, win).astype(F32) - cur
        y = jnp.dot(d.astype(BF16), pw_ref[g], preferred_element_type=F32)
        o_ref[:, lo:hi] = (y * sc_ref[:, lo:hi]).astype(o_ref.dtype)


def _proj_pool(x, w, pool_w, layer, scale, seq, tm=1024):
    n, k = x.shape
    c = HALF_WIDTH
    tm = min(tm, seq)
    hb = tm // POOL_HALO
    return pl.pallas_call(
        functools.partial(_proj_pool_kernel, tm=tm, tiles_per_seq=seq // tm),
        grid=(n // tm,),
        in_specs=[pl.BlockSpec((tm, k), lambda i: (i, 0)),
                  pl.BlockSpec((POOL_HALO, k), lambda i: (jnp.maximum(i * hb - 1, 0), 0)),
                  pl.BlockSpec((None, k, c), lambda i: (layer, 0, 0)),
                  pl.BlockSpec((None,) + pool_w.shape[1:], lambda i: (layer, 0, 0, 0)),
                  pl.BlockSpec((1, c), lambda i: (0, 0))],
        out_specs=pl.BlockSpec((tm, c), lambda i: (i, 0)),
        out_shape=jax.ShapeDtypeStruct((n, c), BF16),
        scratch_shapes=[pltpu.VMEM((tm + POOL_HALO, c), F32)],
        compiler_params=_params("parallel"),
        name="proj_pool",
    )(x, x, w, pool_w, scale)


def _sortable(v):
    bits = lax.bitcast_convert_type(v, jnp.int32)
    return bits ^ ((bits >> 31) & 0x7FFFFFFF)


def _colsum8(v):
    tk, tq = v.shape
    return v.reshape(tk // SUBLANES, SUBLANES, tq).sum(axis=0)


def _idx_kernel(qi_ref, wi_ref, ki_ref, o_ref, keys, half, *, tq, tk, nk, topk):
    i = pl.program_id(1)
    nact = (i * tq + tq - 1) // tk + 1
    q_chunk = (i * tq + lax.broadcasted_iota(jnp.int32, (1, tq), 1)) // CHUNK

    def score_tile(c, carry):
        kt = ki_ref[0, pl.ds(pl.multiple_of(c * tk, tk), tk), :]
        acc = jnp.zeros((tk, tq), F32)
        for h in range(IDX_HEADS):
            il = jnp.dot(kt, qi_ref[h * IDX_DIM:(h + 1) * IDX_DIM, :], preferred_element_type=F32)
            acc = acc + jnp.maximum(il, 0.0) * wi_ref[h:h + 1, :]
        k_chunk = (c * tk + lax.broadcasted_iota(jnp.int32, (tk, 1), 0)) // CHUNK
        kk = jnp.where(k_chunk <= q_chunk, _sortable(acc), INT_MIN)
        keys[c] = kk
        half[c] = (kk >> 16).astype(jnp.int16)
        return carry

    lax.fori_loop(0, nact, score_tile, 0)

    def count(pred):
        def body(c, acc):
            return acc + _colsum8(jnp.where(pred(keys[c], c), 1, 0))
        acc = lax.fori_loop(0, nact, body, jnp.zeros((SUBLANES, tq), jnp.int32))
        return jnp.sum(acc, axis=0, keepdims=True)

    def count16(cand, strict):
        cand16 = jnp.broadcast_to(cand, (PACKED_ROWS, tq)).astype(jnp.int16)[None]

        def body(c, acc):
            h3 = half[c].reshape(tk // PACKED_ROWS, PACKED_ROWS, tq)
            hit = (h3 > cand16) if strict else (h3 >= cand16)
            ones = jnp.where(hit, jnp.int16(1), jnp.int16(0))
            for r in range(tk // PACKED_ROWS):
                acc = acc + ones[r]
            return acc

        acc = lax.fori_loop(0, nact, body, jnp.zeros((PACKED_ROWS, tq), jnp.int16))
        return jnp.sum(acc.astype(jnp.int32), axis=0, keepdims=True)

    hi = jnp.where(count16(jnp.zeros((1, tq), jnp.int32), False) >= topk, 0, -(2 ** 15))

    def hi_step(b, hi):
        cand = hi | (1 << (14 - b))
        return jnp.where(count16(cand, False) >= topk, cand, hi)

    hi = lax.fori_loop(0, 15, hi_step, hi)
    n_above = count16(hi, True)

    def low_tile(c, carry):
        kk = keys[c]
        half[c] = jnp.where((kk >> 16) == hi, (kk & 0xFFFF) - 2 ** 15, -(2 ** 15)).astype(jnp.int16)
        return carry

    lax.fori_loop(0, nact, low_tile, 0)

    def lo_step(b, lo):
        cand = lo | (1 << (15 - b))
        return jnp.where(n_above + count16(cand - 2 ** 15, False) >= topk, cand, lo)

    lo = lax.fori_loop(0, 16, lo_step, jnp.zeros((1, tq), jnp.int32))
    thr = (hi << 16) | lo
    thr = jnp.maximum(thr, INT_MIN + 1)
    n_ge = count(lambda kk, c: kk >= thr)

    def write_tiles(select):
        def body(c, carry):
            o_ref[0, 0, c] = jnp.where(select(keys[c], c), 1.0, 0.0).astype(o_ref.dtype)
            return carry
        lax.fori_loop(0, nact, body, 0)

    has_ties = jnp.max(n_ge) > topk

    @pl.when(jnp.logical_not(has_ties))
    def _():
        write_tiles(lambda kk, c: kk >= thr)

    @pl.when(has_ties)
    def _():
        need = topk - count(lambda kk, c: kk > thr)

        def key_index(c):
            return c * tk + lax.broadcasted_iota(jnp.int32, (tk, 1), 0)

        def idx_step(b, cut):
            cand = cut | (1 << (30 - b))
            below = count(lambda kk, c: (kk == thr) & (key_index(c) < cand))
            return jnp.where(below < need, cand, cut)

        cut = lax.fori_loop(0, 31, idx_step, jnp.zeros((1, tq), jnp.int32))
        write_tiles(lambda kk, c: (kk > thr) | ((kk == thr) & (key_index(c) <= cut)))

    def fill_tile(c, carry):
        o_ref[0, 0, c] = jnp.zeros((tk, tq), o_ref.dtype)
        return carry

    lax.fori_loop(nact, nk, fill_tile, 0)


def _idx_mask(qi_t, wi_t, ki, tq, tk):
    b, s, _ = ki.shape
    nq, nk = s // tq, s // tk
    topk = min(DSA_TOPK_MAX, s // 4)
    return pl.pallas_call(
        functools.partial(_idx_kernel, tq=tq, tk=tk, nk=nk, topk=topk),
        grid=(b, nq),
        in_specs=[pl.BlockSpec((IDX_HEADS * IDX_DIM, tq), lambda bi, i: (0, bi * nq + i)),
                  pl.BlockSpec((IDX_HEADS, tq), lambda bi, i: (0, bi * nq + i)),
                  pl.BlockSpec((1, s, IDX_DIM), lambda bi, i: (bi, 0, 0))],
        out_specs=pl.BlockSpec((1, 1, nk, tk, tq), lambda bi, i: (bi, i, 0, 0, 0)),
        out_shape=jax.ShapeDtypeStruct((b, nq, nk, tk, tq), BF16),
        scratch_shapes=[pltpu.VMEM((nk, tk, tq), jnp.int32), pltpu.VMEM((nk, tk, tq), jnp.int16)],
        compiler_params=_params("parallel", "arbitrary"),
        name="idx_mask",
    )(qi_t, wi_t, ki)


FAST_RANGE_LOG2 = 120.0


def _head(h):
    return slice(h * HEAD_DIM, (h + 1) * HEAD_DIM)


def _vrows(h):
    return slice(h * V_ROWS, (h + 1) * V_ROWS)


def _att_init(qt_ref, qa_ref, m_sc, acc_sc, qfull):
    m_sc[...] = jnp.full(m_sc.shape, NEG, F32)
    acc_sc[...] = jnp.zeros(acc_sc.shape, F32)
    tq = qfull.shape[2]
    for h in range(ATT_HEADS):
        qfull[h, 0:HEAD_DIM, :] = qt_ref[_head(h), :]
        qfull[h, HEAD_DIM:, :] = jnp.zeros((LANES, tq), BF16)
        lo = HEAD_DIM + h * AUG
        qfull[h, lo:lo + AUG, :] = qa_ref[0, h]


def _scores(k_ref, ka, qfull, h):
    kfull = jnp.concatenate([k_ref[0, :, _head(h)], ka], axis=1)
    return jnp.dot(kfull, qfull[h], preferred_element_type=F32)


def _for_heads(score_fn, step_fn):
    s_next = score_fn(0)
    for h in range(ATT_HEADS):
        s = s_next
        if h + 1 < ATT_HEADS:
            s_next = score_fn(h + 1)
        step_fn(s, h)


def _fast_step(s, keep, vt, h, acc_sc):
    p = jnp.exp2(s).astype(BF16)
    if keep is not None:
        p = p * keep
    acc_sc[h] += jnp.dot(vt, p, preferred_element_type=F32)


def _exact_step(s, vt, h, m_sc, acc_sc):
    m_prev = m_sc[h]
    m_new = jnp.maximum(m_prev, jnp.max(s, axis=0, keepdims=True))
    alpha = jnp.exp2(m_prev - m_new)
    p = jnp.exp2(s - m_new)
    acc_sc[h] = alpha * acc_sc[h] + jnp.dot(vt, p.astype(BF16), preferred_element_type=F32)
    m_sc[h] = m_new


def _att_finish(o_ref, acc_sc):
    for h in range(ATT_HEADS):
        a = acc_sc[h]
        o = (a[0:HEAD_DIM] / a[HEAD_DIM:HEAD_DIM + 1]).T
        o_ref[0, :, _head(h)] = o.astype(o_ref.dtype)


def _att_scratch(tq):
    return [pltpu.VMEM((ATT_HEADS, 1, tq), F32),
            pltpu.VMEM((ATT_HEADS, V_ROWS, tq), F32),
            pltpu.VMEM((ATT_HEADS, 2 * HEAD_DIM, tq), BF16)]


def _pair_tiles(r, j, n):
    second = j > r
    return jnp.where(second, n - 1 - r, r), jnp.where(second, j - r - 1, j)


def _dsa_kernel(fast_ref, k_ref, ka_ref, qt_ref, qa_ref, vt_ref, mask_ref, o_ref, m_sc, acc_sc, qfull, *, n):
    b = pl.program_id(0)
    i, j = _pair_tiles(pl.program_id(1), pl.program_id(2), n)
    fast = fast_ref[b, i] == 1

    @pl.when(j == 0)
    def _():
        _att_init(qt_ref, qa_ref, m_sc, acc_sc, qfull)

    @pl.when(fast)
    def _():
        keep = mask_ref[0, 0, 0]
        ka = ka_ref[0]
        _for_heads(lambda h: _scores(k_ref, ka, qfull, h),
                   lambda s, h: _fast_step(s, keep, vt_ref[_vrows(h), :], h, acc_sc))

    @pl.when(jnp.logical_not(fast))
    def _():
        bias = (mask_ref[0, 0, 0].astype(F32) - 1.0) * (-NEG)
        ka = ka_ref[0]
        _for_heads(lambda h: _scores(k_ref, ka, qfull, h) + bias,
                   lambda s, h: _exact_step(s, vt_ref[_vrows(h), :], h, m_sc, acc_sc))

    @pl.when(j == i)
    def _():
        _att_finish(o_ref, acc_sc)


def _dsa(fast, k, k_aug, q_t, q_aug, v_t, mask, t):
    b, s, c = k.shape
    n = s // t
    assert n % 2 == 0, "query tiles are processed in pairs"

    def q_tile(r, j):
        return _pair_tiles(r, j, n)[0]

    def k_tile(r, j):
        return _pair_tiles(r, j, n)[1]

    grid_spec = pltpu.PrefetchScalarGridSpec(
        num_scalar_prefetch=1,
        grid=(b, n // 2, n + 1),
        in_specs=[pl.BlockSpec((1, t, c), lambda bi, r, j, f: (bi, k_tile(r, j), 0)),
                  pl.BlockSpec((1, t, LANES), lambda bi, r, j, f: (bi, k_tile(r, j), 0)),
                  pl.BlockSpec((c, t), lambda bi, r, j, f: (0, bi * n + q_tile(r, j))),
                  pl.BlockSpec((1, ATT_HEADS, AUG, t), lambda bi, r, j, f: (bi, 0, 0, q_tile(r, j))),
                  pl.BlockSpec((ATT_HEADS * V_ROWS, t), lambda bi, r, j, f: (0, bi * n + k_tile(r, j))),
                  pl.BlockSpec((1, 1, 1, t, t), lambda bi, r, j, f: (bi, q_tile(r, j), k_tile(r, j), 0, 0))],
        out_specs=pl.BlockSpec((1, t, c), lambda bi, r, j, f: (bi, q_tile(r, j), 0)),
        scratch_shapes=_att_scratch(t))
    return pl.pallas_call(
        functools.partial(_dsa_kernel, n=n),
        grid_spec=grid_spec,
        out_shape=jax.ShapeDtypeStruct((b, s, c), BF16),
        compiler_params=_params("parallel", "parallel", "arbitrary"),
        name="dsa",
    )(fast, k, k_aug, q_t, q_aug, v_t, mask)


CONV_HALO = 8


def _proj_conv_kernel(x_ref, xh_ref, wu_ref, wb_ref, wc_ref, wf_ref, cw_ref, o_ref, f_ref, ext, *, tm,
                      tiles_per_seq):
    ti = pl.program_id(0) % tiles_per_seq
    x, xh = x_ref[...], xh_ref[...]
    halo = (jnp.dot(xh, wc_ref[...], preferred_element_type=F32)
            * jnp.dot(xh, wu_ref[...], preferred_element_type=F32))
    ext[0:CONV_HALO, :] = jnp.where(ti > 0, halo, 0.0)
    ext[CONV_HALO:, :] = (jnp.dot(x, wc_ref[...], preferred_element_type=F32)
                          * jnp.dot(x, wu_ref[...], preferred_element_type=F32))
    conv = None
    for t in range(CONV_WIDTH):
        off = CONV_HALO - (CONV_WIDTH - 1) + t
        term = ext[off:off + tm, :] * cw_ref[t:t + 1, :]
        conv = term if conv is None else conv + term
    gate_b = jnp.dot(x, wb_ref[...], preferred_element_type=F32)
    o_ref[...] = (gate_b * conv).astype(o_ref.dtype)
    f_ref[...] = jnp.dot(x, wf_ref[...], preferred_element_type=F32)


def _proj_conv(x, w, layer, w_f, conv_w, seq, tm=512):
    n, k = x.shape
    c = HALF_WIDTH
    tm = min(tm, seq)
    hb = tm // CONV_HALO

    def wcol(j):
        return pl.BlockSpec((None, k, c), lambda i: (layer, 0, j))

    return pl.pallas_call(
        functools.partial(_proj_conv_kernel, tm=tm, tiles_per_seq=seq // tm),
        grid=(n // tm,),
        in_specs=[pl.BlockSpec((tm, k), lambda i: (i, 0)),
                  pl.BlockSpec((CONV_HALO, k), lambda i: (jnp.maximum(i * hb - 1, 0), 0)),
                  wcol(0), wcol(1), wcol(2),
                  pl.BlockSpec(w_f.shape, lambda i: (0, 0)),
                  pl.BlockSpec(conv_w.shape, lambda i: (0, 0))],
        out_specs=[pl.BlockSpec((tm, c), lambda i: (i, 0)), pl.BlockSpec((tm, LANES), lambda i: (i, 0))],
        out_shape=[jax.ShapeDtypeStruct((n, c), BF16), jax.ShapeDtypeStruct((n, LANES), F32)],
        scratch_shapes=[pltpu.VMEM((tm + CONV_HALO, c), F32)],
        compiler_params=_params("parallel"),
        name="proj_conv",
    )(x, x, w, w, w, w_f, conv_w)


GATE_CHUNK = 256


def _fgate_kernel(f_ref, fb_ref, hi_ref, mid_ref, lo_ref):
    s = f_ref.shape[1]
    r = lax.broadcasted_iota(jnp.int32, (GATE_CHUNK, GATE_CHUNK), 0)
    c = lax.broadcasted_iota(jnp.int32, (GATE_CHUNK, GATE_CHUNK), 1)
    tri = (c <= r).astype(F32)

    def body(t, carry):
        rows = pl.ds(t * GATE_CHUNK, GATE_CHUNK)
        z = f_ref[0, rows, :] + fb_ref[...]
        log_f = -(jnp.maximum(-z, 0.0) + jnp.log1p(jnp.exp(-jnp.abs(z))))
        cs = jnp.dot(tri, log_f, preferred_element_type=F32, precision=lax.Precision.HIGHEST) + carry
        b2 = cs * LOG2E
        hi = b2.astype(BF16)
        r1 = b2 - hi.astype(F32)
        mid = r1.astype(BF16)
        hi_ref[0, rows, :] = hi
        mid_ref[0, rows, :] = mid
        lo_ref[0, rows, :] = (r1 - mid.astype(F32)).astype(BF16)
        return cs[GATE_CHUNK - 1:GATE_CHUNK, :]

    lax.fori_loop(0, s // GATE_CHUNK, body, jnp.zeros((1, LANES), F32))


def _forget_cumsum(f, fb):
    b, s, c = f.shape
    spec = pl.BlockSpec((1, s, c), lambda bi: (bi, 0, 0))
    return pl.pallas_call(
        _fgate_kernel,
        grid=(b,),
        in_specs=[spec, pl.BlockSpec((1, c), lambda bi: (0, 0))],
        out_specs=[spec, spec, spec],
        out_shape=[jax.ShapeDtypeStruct((b, s, c), BF16)] * 3,
        compiler_params=_params("parallel"),
        name="forget_cumsum",
    )(f, fb)


def _fox_schedule(first, fast, n):
    bsz = first.shape[0]
    steps = bsz * n * (n + 1) // 2
    tiles = jnp.arange(n, dtype=jnp.int32)
    cnt = (tiles[None, :] - first + 1).reshape(-1)
    end = jnp.cumsum(cnt)
    t = jnp.arange(steps, dtype=jnp.int32)
    tc = jnp.minimum(t, end[-1] - 1)
    row = jnp.sum((end[None, :] <= tc[:, None]).astype(jnp.int32), axis=1)
    rel = tc - (end - cnt)[row]
    q_tile = row % n
    active = t < end[-1]
    flags = active.astype(jnp.int32) + 2 * (active & (rel == cnt[row] - 1)).astype(jnp.int32)
    return jnp.stack([row // n, q_tile, q_tile - rel, flags, fast.reshape(-1)[row]]).astype(jnp.int32)


def _fox_kernel(sched_ref, k_ref, ka_ref, qt_ref, qa_ref, vt_ref, o_ref, m_sc, acc_sc, qfull, *, t):
    step = pl.program_id(0)
    i, kj, flags = sched_ref[1, step], sched_ref[2, step], sched_ref[3, step]
    fast = sched_ref[4, step] == 1
    active = (flags & 1) == 1
    diagonal = active & (kj == i)
    older = active & (kj != i)

    def causal():
        return lax.broadcasted_iota(jnp.int32, (t, t), 0) <= lax.broadcasted_iota(jnp.int32, (t, t), 1)

    def scores(masked):
        ka = ka_ref[0]

        def fn(h):
            s = _scores(k_ref, ka, qfull, h)
            return jnp.where(causal(), s, NEG) if masked else s
        return fn

    def fast_tile(masked):
        _for_heads(scores(masked), lambda s, h: _fast_step(s, None, vt_ref[_vrows(h), :], h, acc_sc))

    def exact_tile(masked):
        _for_heads(scores(masked), lambda s, h: _exact_step(s, vt_ref[_vrows(h), :], h, m_sc, acc_sc))

    @pl.when(diagonal)
    def _():
        _att_init(qt_ref, qa_ref, m_sc, acc_sc, qfull)

    @pl.when(diagonal & fast)
    def _():
        fast_tile(True)

    @pl.when(diagonal & jnp.logical_not(fast))
    def _():
        exact_tile(True)

    @pl.when(older & fast)
    def _():
        fast_tile(False)

    @pl.when(older & jnp.logical_not(fast))
    def _():
        exact_tile(False)

    @pl.when((flags & 2) == 2)
    def _():
        _att_finish(o_ref, acc_sc)


def _fox(first, fast, k, k_aug, q_t, q_aug, v_t, t):
    b, s, c = k.shape
    n = s // t
    sched = _fox_schedule(first, fast, n)
    grid_spec = pltpu.PrefetchScalarGridSpec(
        num_scalar_prefetch=1,
        grid=(sched.shape[1],),
        in_specs=[pl.BlockSpec((1, t, c), lambda st, sc: (sc[0, st], sc[2, st], 0)),
                  pl.BlockSpec((1, t, LANES), lambda st, sc: (sc[0, st], sc[2, st], 0)),
                  pl.BlockSpec((c, t), lambda st, sc: (0, sc[0, st] * n + sc[1, st])),
                  pl.BlockSpec((1, ATT_HEADS, AUG, t), lambda st, sc: (sc[0, st], 0, 0, sc[1, st])),
                  pl.BlockSpec((ATT_HEADS * V_ROWS, t), lambda st, sc: (0, sc[0, st] * n + sc[2, st]))],
        out_specs=pl.BlockSpec((1, t, c), lambda st, sc: (sc[0, st], sc[1, st], 0)),
        scratch_shapes=_att_scratch(t))
    return pl.pallas_call(
        functools.partial(_fox_kernel, t=t),
        grid_spec=grid_spec,
        out_shape=jax.ShapeDtypeStruct((b, s, c), BF16),
        compiler_params=_params("arbitrary"),
        name="fox",
    )(sched, k, k_aug, q_t, q_aug, v_t)


def _mix_xattn_kernel(pa_ref, pb_ref, wm_ref, xf_ref, wq_ref, kt_ref, v_ref, wo_ref, g_ref, b_ref, of_ref, ob_ref):
    half = pa_ref.shape[1]
    mixed = (jnp.dot(pa_ref[...], wm_ref[0:half, :], preferred_element_type=F32)
             + jnp.dot(pb_ref[...], wm_ref[half:, :], preferred_element_type=F32))
    x1 = _layer_norm(ALPHA * xf_ref[...] + mixed, g_ref[0:1, :], b_ref[0:1, :])
    q = jnp.dot(x1.astype(BF16), wq_ref[...], preferred_element_type=F32) * Q_SCALE
    q = q.astype(BF16)
    outs = []
    for h in range(XA_HEADS):
        s = jnp.dot(q[:, _head(h)], kt_ref[0, _head(h), :], preferred_element_type=F32)
        p = jnp.exp2(s - jnp.max(s, axis=1, keepdims=True))
        l = jnp.sum(p, axis=1, keepdims=True)
        pv = jnp.dot(p.astype(BF16), v_ref[0, :, _head(h)], preferred_element_type=F32)
        outs.append((pv / l).astype(BF16))
    o = jnp.concatenate(outs, axis=1)
    y = jnp.dot(o, wo_ref[...], preferred_element_type=F32)
    x2 = _layer_norm(ALPHA * x1 + y, g_ref[1:2, :], b_ref[1:2, :])
    of_ref[...] = x2
    ob_ref[...] = x2.astype(BF16)


def _mix_xattn(pa, pb, w_mix, mix_layer, xf, wq, kt, v, wo, layer, g, b, seq, tm=512):
    n, d = xf.shape
    per_batch = seq // tm
    m = v.shape[1]
    row = lambda c: pl.BlockSpec((tm, c), lambda i: (i, 0))
    return pl.pallas_call(
        _mix_xattn_kernel,
        grid=(n // tm,),
        in_specs=[row(pa.shape[1]), row(pb.shape[1]),
                  pl.BlockSpec((None,) + w_mix.shape[1:], lambda i: (mix_layer, 0, 0)),
                  row(d),
                  pl.BlockSpec((None,) + wq.shape[1:], lambda i: (layer, 0, 0)),
                  pl.BlockSpec((1, XA_WIDTH, m), lambda i: (i // per_batch, 0, 0)),
                  pl.BlockSpec((1, m, XA_WIDTH), lambda i: (i // per_batch, 0, 0)),
                  pl.BlockSpec((None,) + wo.shape[1:], lambda i: (layer, 0, 0)),
                  pl.BlockSpec((2, d), lambda i: (0, 0)),
                  pl.BlockSpec((2, d), lambda i: (0, 0))],
        out_specs=[row(d), row(d)],
        out_shape=[jax.ShapeDtypeStruct((n, d), F32), jax.ShapeDtypeStruct((n, d), BF16)],
        compiler_params=_params("parallel"),
        name="mix_xattn",
    )(pa, pb, w_mix, xf, wq, kt, v, wo, g, b)


def _ffn_kernel(xb_ref, wg_ref, wu_ref, wo_ref, xf_ref, g_ref, b_ref, of_ref, ob_ref, ot_ref, acc):
    j = pl.program_id(1)

    @pl.when(j == 0)
    def _():
        acc[...] = jnp.zeros(acc.shape, F32)

    xb = xb_ref[...]
    gate = jnp.dot(xb, wg_ref[...], preferred_element_type=F32)
    up = jnp.dot(xb, wu_ref[...], preferred_element_type=F32)
    hid = gate * (1.0 / (1.0 + jnp.exp(-gate))) * up
    acc[...] += jnp.dot(hid.astype(BF16), wo_ref[...], preferred_element_type=F32)

    @pl.when(j == pl.num_programs(1) - 1)
    def _():
        y = _layer_norm(ALPHA * xf_ref[...] + acc[...], g_ref[...], b_ref[...])
        of_ref[...] = y
        ob_ref[...] = y.astype(BF16)
        ot_ref[...] = y.T.astype(BF16)


def _ffn(xb, xf, w_in, w_out, layer, g, b, tm=512, th=512):
    n, d = xf.shape
    hidden = w_out.shape[1]
    nh = hidden // th
    return pl.pallas_call(
        _ffn_kernel,
        grid=(n // tm, nh),
        in_specs=[pl.BlockSpec((tm, d), lambda i, j: (i, 0)),
                  pl.BlockSpec((None, d, th), lambda i, j: (layer, 0, j)),
                  pl.BlockSpec((None, d, th), lambda i, j: (layer, 0, j + nh)),
                  pl.BlockSpec((None, th, d), lambda i, j: (layer, j, 0)),
                  pl.BlockSpec((tm, d), lambda i, j: (i, 0)),
                  pl.BlockSpec((1, d), lambda i, j: (0, 0)),
                  pl.BlockSpec((1, d), lambda i, j: (0, 0))],
        out_specs=[pl.BlockSpec((tm, d), lambda i, j: (i, 0)),
                   pl.BlockSpec((tm, d), lambda i, j: (i, 0)),
                   pl.BlockSpec((d, tm), lambda i, j: (0, i))],
        out_shape=[jax.ShapeDtypeStruct((n, d), F32), jax.ShapeDtypeStruct((n, d), BF16),
                   jax.ShapeDtypeStruct((d, n), BF16)],
        scratch_shapes=[pltpu.VMEM((tm, d), F32)],
        compiler_params=_params("parallel", "arbitrary"),
        name="ffn",
    )(xb, w_in, w_in, w_out, xf, g, b)


def _rotary_tables(positions, dh):
    rot = dh // ROPE_FRACTION
    half = rot // 2
    inv_freq = jnp.power(ROPE_THETA, -(jnp.arange(half, dtype=F32) * 2.0 / rot))
    ang = positions.astype(F32)[..., None] * inv_freq
    cos, sin = jnp.cos(ang), jnp.sin(ang)
    zh = jnp.zeros_like(sin)
    rest = jnp.zeros(ang.shape[:-1] + (dh - rot,), F32)
    c = jnp.concatenate([cos, cos, rest + 1.0], axis=-1)
    s_lo = jnp.concatenate([zh, sin, rest], axis=-1)
    s_hi = jnp.concatenate([-sin, zh, rest], axis=-1)
    return tuple(t.reshape(-1, dh) for t in (c, s_lo, s_hi))


def _even_tables(positions):
    n = positions.size
    head = _rotary_tables(positions, HEAD_DIM)
    idx = _rotary_tables(positions, IDX_DIM)
    idx_full = tuple(jnp.tile(t, (1, LANES // IDX_DIM)) for t in idx)
    wi_scale = jnp.concatenate([jnp.full((IDX_HEADS,), IDX_HEADS ** -0.5, F32),
                                jnp.ones((LANES - IDX_DIM - IDX_HEADS,), F32)])
    pad = jnp.zeros((n, LANES - IDX_DIM), F32)
    tail = (jnp.concatenate([idx[0], pad + wi_scale], axis=1),
            jnp.concatenate([idx[1], pad], axis=1),
            jnp.concatenate([idx[2], pad], axis=1))
    return {"k": head,
            "q_t": tuple((t * Q_SCALE).T for t in head),
            "qi_t": tuple((t * IDX_DIM ** -0.5).T for t in idx_full),
            "tail": tail}


def _att_bounds(qn2, kn2, bsz, seq, t):
    n = seq // t
    qn = jnp.sqrt(qn2).reshape(ATT_HEADS, bsz, seq).transpose(1, 2, 0)
    kn = jnp.sqrt(kn2).reshape(bsz, seq, ATT_HEADS)
    k_max = jnp.max(kn, axis=1, keepdims=True)
    neg_m = -(1.01 * qn * k_max + 1.0)
    qn_tile = jnp.max(qn.reshape(bsz, n, t, ATT_HEADS), axis=2)
    kn_tile = jnp.max(kn.reshape(bsz, n, t, ATT_HEADS), axis=2)
    spread = jnp.max(2.05 * qn_tile * k_max, axis=-1) + 8.0
    fast = (spread <= FAST_RANGE_LOG2).astype(jnp.int32)
    return neg_m.astype(BF16), qn_tile, kn_tile, fast


def _aug_operands(q_entries, k_entries, bsz, seq):
    shape = (bsz, seq, ATT_HEADS)
    one, zero = jnp.ones(shape, BF16), jnp.zeros(shape, BF16)

    def pack(entries):
        cols = [one if e is None else e for e in entries]
        return jnp.stack(cols + [zero] * (AUG - len(cols)), axis=-1)

    return pack(k_entries).reshape(bsz, seq, ATT_HEADS * AUG), pack(q_entries).transpose(0, 2, 3, 1)


def _even_mixer(xb, xt, tables, wb, wt, j, pool_w, pool_scale, bsz, seq, tq):
    n = xb.shape[0]
    hw = HALF_WIDTH
    a = _proj_pool(xb, wb, pool_w, j, pool_scale.reshape(1, hw), seq)
    q_t, qn2 = _proj_t(wt, j, xt, BF16, hw, hw, tables["q_t"], shift=HEAD_DIM // 8, norms=True)
    k, kn2 = _proj(xb, wb, j, BF16, 2 * hw, hw, tables["k"], shift=HEAD_DIM // 8, norms=True)
    v_t = _proj_t(wt, j, xt, BF16, 3 * hw, hw, value_rows=True)
    qi_t = _proj_t(wt, j, xt, BF16, 4 * hw, hw, tables["qi_t"], shift=IDX_DIM // 8)
    w_tail = jnp.pad(wb[j:j + 1, :, 5 * hw:], ((0, 0), (0, 0), (0, LANES - IDX_DIM - IDX_HEADS)))
    tail = _proj(xb, w_tail, 0, F32, 0, LANES, tables["tail"], shift=IDX_DIM // 8)

    ki = tail[:, :IDX_DIM].astype(BF16).reshape(bsz, seq, IDX_DIM)
    wi_t = tail[:, IDX_DIM:IDX_DIM + IDX_HEADS].T

    mask = _idx_mask(qi_t, wi_t, ki, tq, tq)
    neg_m, _, _, fast = _att_bounds(qn2, kn2, bsz, seq, tq)
    k_aug, q_aug = _aug_operands([neg_m], [None], bsz, seq)
    bb = _dsa(fast, k.reshape(bsz, seq, hw), k_aug, q_t, q_aug, v_t, mask, tq)
    return a, bb.reshape(n, hw)


DSA_TILE = 512
FOX_TILE = 512
UNDERFLOW_LOG2 = 160.0


def _fox_first_tile(qn, kn, terms, bsz, seq, t):
    n = seq // t
    f32sum = sum(x.astype(F32) for x in terms)[:, :, :ATT_HEADS].reshape(bsz, n, t, ATT_HEADS)
    f_first, f_last = f32sum[:, :, 0], f32sum[:, :, t - 1]
    bound = (1.01 * qn[:, :, None] * (kn[:, None, :] + kn[:, :, None])
             + f_first[:, :, None] - f_last[:, None, :] + 1.0)
    tiles = jnp.arange(n, dtype=jnp.int32)
    needed = jnp.any(bound >= -UNDERFLOW_LOG2, axis=-1) | (tiles[:, None] == tiles[None, :])
    needed = needed & (tiles[None, :] <= tiles[:, None])
    return jnp.min(jnp.where(needed, tiles[None, None, :], n), axis=-1).astype(jnp.int32)


def _odd_mixer(xb, xt, wb, wt, j, conv_w, forget_b, bsz, seq):
    n = xb.shape[0]
    hw = HALF_WIDTH
    w_f = jnp.pad(wb[j, :, 6 * hw:], ((0, 0), (0, LANES - ATT_HEADS)))
    c, f = _proj_conv(xb, wb, j, w_f, conv_w, seq)
    q_t, qn2 = _proj_t(wt, j, xt, BF16, 3 * hw, hw, scale=Q_SCALE, norms=True)
    k, kn2 = _proj(xb, wb, j, BF16, 4 * hw, hw, norms=True)
    v_t = _proj_t(wt, j, xt, BF16, 5 * hw, hw, value_rows=True)
    fb = jnp.pad(forget_b, (0, LANES - ATT_HEADS)).reshape(1, LANES)
    terms = _forget_cumsum(f.reshape(bsz, seq, LANES), fb)
    hi, mid, lo = (x[:, :, :ATT_HEADS] for x in terms)

    t = min(FOX_TILE, seq)
    neg_m, qn, kn, fast = _att_bounds(qn2, kn2, bsz, seq, t)
    k_aug, q_aug = _aug_operands([neg_m, None, None, None, hi, mid, lo],
                                 [None, -hi, -mid, -lo, None, None, None], bsz, seq)
    first = _fox_first_tile(qn, kn, terms, bsz, seq, t)
    d = _fox(first, fast, k.reshape(bsz, seq, hw), k_aug, q_t, q_aug, v_t, t)
    return c, d.reshape(n, hw)


def kernel(x, mem, positions, ev_w_in, ev_pool_w, ev_pool_scale, ev_w_out, od_w_in, od_conv_w, od_forget_b,
           od_w_out, ca_w_q, ca_w_kv, ca_w_o, ffn_w_in, ffn_w_out, ln_g, ln_b):
    bsz, seq, d = x.shape
    n = bsz * seq
    m = mem.shape[1]
    tq = min(DSA_TILE, seq)
    xf = x.reshape(n, d)
    xb = xf.astype(BF16)
    memb = mem.reshape(bsz * m, d).astype(BF16)
    tables = _even_tables(positions)
    xt = xb.T
    ev_wb, od_wb = ev_w_in.astype(BF16), od_w_in.astype(BF16)
    ev_wt, od_wt = ev_wb.transpose(0, 2, 1), od_wb.transpose(0, 2, 1)
    ev_wo, od_wo, pool_w = ev_w_out.astype(BF16), od_w_out.astype(BF16), ev_pool_w.astype(BF16)
    ca_q, ca_kv, ca_o = ca_w_q.astype(BF16), ca_w_kv.astype(BF16), ca_w_o.astype(BF16)
    ffn_in, ffn_out = ffn_w_in.astype(BF16), ffn_w_out.astype(BF16)
    for i in range(DEPTH):
        j = i // 2
        if i % 2 == 0:
            parts = _even_mixer(xb, xt, tables, ev_wb, ev_wt, j, pool_w, ev_pool_scale[j], bsz, seq, tq)
            w_out = ev_wo
        else:
            parts = _odd_mixer(xb, xt, od_wb, od_wt, j, od_conv_w[j], od_forget_b[j], bsz, seq)
            w_out = od_wo
        kv = _proj(memb, ca_kv, i, BF16, 0, 2 * XA_WIDTH).reshape(bsz, m, 2 * XA_WIDTH)
        kt = kv[:, :, :XA_WIDTH].transpose(0, 2, 1)
        xf, xb = _mix_xattn(parts[0], parts[1], w_out, j, xf, ca_q, kt, kv[:, :, XA_WIDTH:], ca_o, i,
                            ln_g[i, 0:2], ln_b[i, 0:2], seq)
        xf, xb, xt = _ffn(xb, xf, ffn_in, ffn_out, i, ln_g[i, 2:3], ln_b[i, 2:3])
    return xf.reshape(bsz, seq, d)
```

```python
import functools

import jax
import jax.numpy as jnp
from jax import lax
from jax.experimental import pallas as pl
from jax.experimental.pallas import tpu as pltpu

F32 = jnp.float32
BF16 = jnp.bfloat16

D_MODEL = 2048
DEPTH = 4
CHUNK = 64
HEAD_DIM = 128
HALF_WIDTH = D_MODEL // 2
POOL_WINDOWS = (2, 4, 8, 16)
POOL_GROUP_DIM = HALF_WIDTH // len(POOL_WINDOWS)
ATT_HEADS = HALF_WIDTH // HEAD_DIM
IDX_HEADS = 16
IDX_DIM = 64
DSA_TOPK_MAX = 256
CONV_WIDTH = 3
XA_HEADS = 4
XA_WIDTH = XA_HEADS * HEAD_DIM
FFN_HIDDEN = -(-(8 * D_MODEL) // (3 * 256)) * 256
ROPE_THETA = 500000.0
ROPE_FRACTION = 4
LN_EPS = 1e-5
ALPHA = (2 * DEPTH) ** 0.25
LOG2E = 1.4426950408889634
Q_SCALE = HEAD_DIM ** -0.5 * LOG2E

LANES = 128
SUBLANES = 8
PACKED_ROWS = 16
NEG = -1e30
INT_MIN = -(2 ** 31)
VMEM_LIMIT = 60000 * 1024


def _params(*semantics):
    return pltpu.CompilerParams(dimension_semantics=semantics, vmem_limit_bytes=VMEM_LIMIT)


def _layer_norm(y, g, b):
    mu = jnp.mean(y, axis=-1, keepdims=True)
    d = y - mu
    var = jnp.mean(d * d, axis=-1, keepdims=True)
    return d * lax.rsqrt(var + LN_EPS) * g + b


AUG = 16
V_ROWS = HEAD_DIM + AUG


def _proj_kernel(*refs, shift, scale, axis, norms, value_rows):
    refs = list(refs)
    a_ref, b_ref = refs[0], refs[1]
    n_ref = refs.pop() if norms else None
    o_ref = refs.pop()
    h = jnp.dot(a_ref[...], b_ref[...], preferred_element_type=F32)
    if not (shift or norms or value_rows):
        o_ref[...] = (h * scale if scale != 1.0 else h).astype(o_ref.dtype)
        return
    if shift:
        c, s1, s2 = refs[2][...], refs[3][...], refs[4][...]
    for g in range(h.shape[axis] // LANES):
        sl = (slice(None),) * axis + (slice(g * LANES, (g + 1) * LANES),)
        hg = h[sl]
        if shift:
            hg = hg * c + pltpu.roll(hg, shift, axis) * s1 + pltpu.roll(hg, LANES - shift, axis) * s2
        elif scale != 1.0:
            hg = hg * scale
        og = hg.astype(o_ref.dtype)
        if value_rows:
            tm = og.shape[1]
            o_ref[g * V_ROWS:g * V_ROWS + LANES, :] = og
            row = lax.broadcasted_iota(jnp.int32, (AUG, tm), 0)
            o_ref[g * V_ROWS + LANES:(g + 1) * V_ROWS, :] = jnp.where(row == 0, 1.0, 0.0).astype(o_ref.dtype)
        else:
            o_ref[sl] = og
        if norms and axis == 0:
            n_ref[g:g + 1, :] = jnp.sum(jnp.square(og.astype(F32)), axis=0, keepdims=True)
    if norms and axis == 1:
        tn = h.shape[1]
        sq = jnp.square(o_ref[...].astype(F32)).astype(BF16)
        owner = lax.broadcasted_iota(jnp.int32, (tn, LANES), 0) // LANES
        ind = jnp.where(owner == lax.broadcasted_iota(jnp.int32, (tn, LANES), 1), 1.0, 0.0).astype(BF16)
        sums = jnp.dot(sq, ind, preferred_element_type=F32)
        n_ref[...] = sums[:, 0:tn // LANES]


def _proj(x, w, layer, out_dtype, col0, ncols, tables=None, shift=0, scale=1.0, norms=False, tm=1024, tn=1024):
    n, k = x.shape
    tm, tn = min(tm, n), min(tn, ncols)
    assert col0 % tn == 0 and ncols % tn == 0
    in_specs = [pl.BlockSpec((tm, k), lambda i, j: (i, 0)),
                pl.BlockSpec((None, k, tn), lambda i, j: (layer, 0, col0 // tn + j))]
    args = [x, w]
    if shift:
        in_specs += [pl.BlockSpec((tm, LANES), lambda i, j: (i, 0))] * 3
        args += list(tables)
    out_specs = [pl.BlockSpec((tm, tn), lambda i, j: (i, j))]
    out_shape = [jax.ShapeDtypeStruct((n, ncols), out_dtype)]
    if norms:
        assert tn == ncols
        out_specs.append(pl.BlockSpec((tm, tn // LANES), lambda i, j: (i, 0)))
        out_shape.append(jax.ShapeDtypeStruct((n, ncols // LANES), F32))
    out = pl.pallas_call(
        functools.partial(_proj_kernel, shift=shift, scale=scale, axis=1, norms=norms, value_rows=False),
        grid=(n // tm, ncols // tn),
        in_specs=in_specs,
        out_specs=out_specs,
        out_shape=out_shape,
        compiler_params=_params("parallel", "arbitrary"),
        name="proj",
    )(*args)
    return out if norms else out[0]


def _proj_t(wt, layer, xt, out_dtype, row0, nrows, tables=None, shift=0, scale=1.0, norms=False,
            value_rows=False, tm=1024, tn=1024):
    k, n = xt.shape
    tm, tn = min(tm, n), min(tn, nrows)
    assert row0 % tn == 0 and nrows % tn == 0
    in_specs = [pl.BlockSpec((None, tn, k), lambda i, j: (layer, row0 // tn + j, 0)),
                pl.BlockSpec((k, tm), lambda i, j: (0, i))]
    args = [wt, xt]
    if shift:
        in_specs += [pl.BlockSpec((LANES, tm), lambda i, j: (0, i))] * 3
        args += list(tables)
    rows_out = tn // LANES * V_ROWS if value_rows else tn
    out_specs = [pl.BlockSpec((rows_out, tm), lambda i, j: (j, i))]
    out_shape = [jax.ShapeDtypeStruct((nrows // tn * rows_out, n), out_dtype)]
    if norms:
        assert tn == nrows
        out_specs.append(pl.BlockSpec((tn // LANES, tm), lambda i, j: (0, i)))
        out_shape.append(jax.ShapeDtypeStruct((nrows // LANES, n), F32))
    out = pl.pallas_call(
        functools.partial(_proj_kernel, shift=shift, scale=scale, axis=0, norms=norms, value_rows=value_rows),
        grid=(n // tm, nrows // tn),
        in_specs=in_specs,
        out_specs=out_specs,
        out_shape=out_shape,
        compiler_params=_params("parallel", "arbitrary"),
        name="proj_t",
    )(*args)
    return out if norms else out[0]


POOL_HALO = 16


def _proj_pool_kernel(x_ref, xh_ref, w_ref, pw_ref, sc_ref, o_ref, ext, *, tm, tiles_per_seq):
    ti = pl.program_id(0) % tiles_per_seq
    halo = jnp.dot(xh_ref[...], w_ref[...], preferred_element_type=F32)
    ext[0:POOL_HALO, :] = jnp.where(ti > 0, halo, 0.0)
    ext[POOL_HALO:, :] = jnp.dot(x_ref[...], w_ref[...], preferred_element_type=F32)
    cnt = ti * tm + lax.broadcasted_iota(jnp.int32, (tm, 1), 0) + 1
    for g, win in enumerate(POOL_WINDOWS):
        lo, hi = g * POOL_GROUP_DIM, (g + 1) * POOL_GROUP_DIM
        cur = ext[POOL_HALO:POOL_HALO + tm, lo:hi]
        s = cur
        for j in range(1, win):
            s = s + ext[POOL_HALO - j:POOL_HALO - j + tm, lo:hi]
        d = s / jnp.minimum(cnt, win).astype(F32) - cur
        y = jnp.dot(d.astype(BF16), pw_ref[g], preferred_element_type=F32)
        o_ref[:, lo:hi] = (y * sc_ref[:, lo:hi]).astype(o_ref.dtype)


def _proj_pool(x, w, pool_w, layer, scale, seq, tm=1024):
    n, k = x.shape
    c = HALF_WIDTH
    tm = min(tm, seq)
    hb = tm // POOL_HALO
    return pl.pallas_call(
        functools.partial(_proj_pool_kernel, tm=tm, tiles_per_seq=seq // tm),
        grid=(n // tm,),
        in_specs=[pl.BlockSpec((tm, k), lambda i: (i, 0)),
                  pl.BlockSpec((POOL_HALO, k), lambda i: (jnp.maximum(i * hb - 1, 0), 0)),
                  pl.BlockSpec((None, k, c), lambda i: (layer, 0, 0)),
                  pl.BlockSpec((None,) + pool_w.shape[1:], lambda i: (layer, 0, 0, 0)),
                  pl.BlockSpec((1, c), lambda i: (0, 0))],
        out_specs=pl.BlockSpec((tm, c), lambda i: (i, 0)),
        out_shape=jax.ShapeDtypeStruct((n, c), BF16),
        scratch_shapes=[pltpu.VMEM((tm + POOL_HALO, c), F32)],
        compiler_params=_params("parallel"),
        name="proj_pool",
    )(x, x, w, pool_w, scale)


def _sortable(v):
    bits = lax.bitcast_convert_type(v, jnp.int32)
    return bits ^ ((bits >> 31) & 0x7FFFFFFF)


def _colsum8(v):
    tk, tq = v.shape
    return v.reshape(tk // SUBLANES, SUBLANES, tq).sum(axis=0)


def _idx_kernel(qi_ref, wi_ref, ki_ref, o_ref, keys, half, *, tq, tk, nk, topk):
    i = pl.program_id(1)
    nact = (i * tq + tq - 1) // tk + 1
    q_chunk = (i * tq + lax.broadcasted_iota(jnp.int32, (1, tq), 1)) // CHUNK

    def score_tile(c, carry):
        kt = ki_ref[0, pl.ds(pl.multiple_of(c * tk, tk), tk), :]
        acc = jnp.zeros((tk, tq), F32)
        for h in range(IDX_HEADS):
            il = jnp.dot(kt, qi_ref[h * IDX_DIM:(h + 1) * IDX_DIM, :], preferred_element_type=F32)
            acc = acc + jnp.maximum(il, 0.0) * wi_ref[h:h + 1, :]
        k_chunk = (c * tk + lax.broadcasted_iota(jnp.int32, (tk, 1), 0)) // CHUNK
        kk = jnp.where(k_chunk <= q_chunk, _sortable(acc), INT_MIN)
        keys[c] = kk
        half[c] = (kk >> 16).astype(jnp.int16)
        return carry

    lax.fori_loop(0, nact, score_tile, 0)

    def count(pred):
        def body(c, acc):
            return acc + _colsum8(jnp.where(pred(keys[c], c), 1, 0))
        acc = lax.fori_loop(0, nact, body, jnp.zeros((SUBLANES, tq), jnp.int32))
        return jnp.sum(acc, axis=0, keepdims=True)

    def count16(cand, strict):
        cand16 = jnp.broadcast_to(cand, (PACKED_ROWS, tq)).astype(jnp.int16)[None]

        def body(c, acc):
            h3 = half[c].reshape(tk // PACKED_ROWS, PACKED_ROWS, tq)
            hit = (h3 > cand16) if strict else (h3 >= cand16)
            ones = jnp.where(hit, jnp.int16(1), jnp.int16(0))
            for r in range(tk // PACKED_ROWS):
                acc = acc + ones[r]
            return acc

        acc = lax.fori_loop(0, nact, body, jnp.zeros((PACKED_ROWS, tq), jnp.int16))
        return jnp.sum(acc.astype(jnp.int32), axis=0, keepdims=True)

    c0 = count16(jnp.zeros((1, tq), jnp.int32), False)
    sign_ok = c0 >= topk
    state = (jnp.where(sign_ok, 0, -(2 ** 15)), jnp.where(sign_ok, c0, nact * tk), jnp.where(sign_ok, 0, c0))

    def hi_step(b, state):
        hi, ge_hi, n_above = state
        cand = hi | (1 << (14 - b))
        c = count16(cand, False)
        ok = c >= topk
        return jnp.where(ok, cand, hi), jnp.where(ok, c, ge_hi), jnp.where(ok, n_above, c)

    hi, ge_hi, n_above = lax.fori_loop(0, 15, hi_step, state)

    def low_tile(c, carry):
        kk = keys[c]
        half[c] = jnp.where((kk >> 16) == hi, (kk & 0xFFFF) - 2 ** 15, -(2 ** 15)).astype(jnp.int16)
        return carry

    lax.fori_loop(0, nact, low_tile, 0)

    def lo_step(b, state):
        lo, in_bucket = state
        cand = lo | (1 << (15 - b))
        c = count16(cand - 2 ** 15, False)
        ok = n_above + c >= topk
        return jnp.where(ok, cand, lo), jnp.where(ok, c, in_bucket)

    lo, in_bucket = lax.fori_loop(0, 16, lo_step, (jnp.zeros((1, tq), jnp.int32), ge_hi - n_above))
    thr = (hi << 16) | lo
    n_ge = jnp.where(thr == INT_MIN, 0, n_above + in_bucket)
    thr = jnp.maximum(thr, INT_MIN + 1)

    def write_tiles(select):
        def body(c, carry):
            o_ref[0, 0, c] = jnp.where(select(keys[c], c), 1.0, 0.0).astype(o_ref.dtype)
            return carry
        lax.fori_loop(0, nact, body, 0)

    has_ties = jnp.max(n_ge) > topk

    @pl.when(jnp.logical_not(has_ties))
    def _():
        write_tiles(lambda kk, c: kk >= thr)

    @pl.when(has_ties)
    def _():
        need = topk - count(lambda kk, c: kk > thr)

        def key_index(c):
            return c * tk + lax.broadcasted_iota(jnp.int32, (tk, 1), 0)

        def idx_step(b, cut):
            cand = cut | (1 << (30 - b))
            below = count(lambda kk, c: (kk == thr) & (key_index(c) < cand))
            return jnp.where(below < need, cand, cut)

        cut = lax.fori_loop(0, 31, idx_step, jnp.zeros((1, tq), jnp.int32))
        write_tiles(lambda kk, c: (kk > thr) | ((kk == thr) & (key_index(c) <= cut)))

    def fill_tile(c, carry):
        o_ref[0, 0, c] = jnp.zeros((tk, tq), o_ref.dtype)
        return carry

    lax.fori_loop(nact, nk, fill_tile, 0)


def _idx_mask(qi_t, wi_t, ki, tq, tk):
    b, s, _ = ki.shape
    nq, nk = s // tq, s // tk
    topk = min(DSA_TOPK_MAX, s // 4)
    return pl.pallas_call(
        functools.partial(_idx_kernel, tq=tq, tk=tk, nk=nk, topk=topk),
        grid=(b, nq),
        in_specs=[pl.BlockSpec((IDX_HEADS * IDX_DIM, tq), lambda bi, i: (0, bi * nq + i)),
                  pl.BlockSpec((IDX_HEADS, tq), lambda bi, i: (0, bi * nq + i)),
                  pl.BlockSpec((1, s, IDX_DIM), lambda bi, i: (bi, 0, 0))],
        out_specs=pl.BlockSpec((1, 1, nk, tk, tq), lambda bi, i: (bi, i, 0, 0, 0)),
        out_shape=jax.ShapeDtypeStruct((b, nq, nk, tk, tq), BF16),
        scratch_shapes=[pltpu.VMEM((nk, tk, tq), jnp.int32), pltpu.VMEM((nk, tk, tq), jnp.int16)],
        compiler_params=_params("parallel", "arbitrary"),
        name="idx_mask",
    )(qi_t, wi_t, ki)


FAST_RANGE_LOG2 = 120.0


def _head(h):
    return slice(h * HEAD_DIM, (h + 1) * HEAD_DIM)


def _vrows(h):
    return slice(h * V_ROWS, (h + 1) * V_ROWS)


def _att_init(qt_ref, qa_ref, m_sc, acc_sc, qfull):
    m_sc[...] = jnp.full(m_sc.shape, NEG, F32)
    acc_sc[...] = jnp.zeros(acc_sc.shape, F32)
    tq = qfull.shape[2]
    for h in range(ATT_HEADS):
        qfull[h, 0:HEAD_DIM, :] = qt_ref[_head(h), :]
        qfull[h, HEAD_DIM:, :] = jnp.zeros((LANES, tq), BF16)
        lo = HEAD_DIM + h * AUG
        qfull[h, lo:lo + AUG, :] = qa_ref[0, h]


def _scores(k_ref, ka, qfull, h):
    kfull = jnp.concatenate([k_ref[0, :, _head(h)], ka], axis=1)
    return jnp.dot(kfull, qfull[h], preferred_element_type=F32)


def _for_heads(score_fn, step_fn):
    s_next = score_fn(0)
    for h in range(ATT_HEADS):
        s = s_next
        if h + 1 < ATT_HEADS:
            s_next = score_fn(h + 1)
        step_fn(s, h)


def _fast_step(s, keep, vt, h, acc_sc):
    p = jnp.exp2(s).astype(BF16)
    if keep is not None:
        p = p * keep
    acc_sc[h] += jnp.dot(vt, p, preferred_element_type=F32)


def _exact_step(s, vt, h, m_sc, acc_sc):
    m_prev = m_sc[h]
    m_new = jnp.maximum(m_prev, jnp.max(s, axis=0, keepdims=True))
    alpha = jnp.exp2(m_prev - m_new)
    p = jnp.exp2(s - m_new)
    acc_sc[h] = alpha * acc_sc[h] + jnp.dot(vt, p.astype(BF16), preferred_element_type=F32)
    m_sc[h] = m_new


def _att_finish(o_ref, acc_sc):
    for h in range(ATT_HEADS):
        a = acc_sc[h]
        o = (a[0:HEAD_DIM] / a[HEAD_DIM:HEAD_DIM + 1]).T
        o_ref[0, :, _head(h)] = o.astype(o_ref.dtype)


def _att_scratch(tq):
    return [pltpu.VMEM((ATT_HEADS, 1, tq), F32),
            pltpu.VMEM((ATT_HEADS, V_ROWS, tq), F32),
            pltpu.VMEM((ATT_HEADS, 2 * HEAD_DIM, tq), BF16)]


def _pair_tiles(r, j, n):
    second = j > r
    return jnp.where(second, n - 1 - r, r), jnp.where(second, j - r - 1, j)


def _dsa_kernel(fast_ref, k_ref, ka_ref, qt_ref, qa_ref, vt_ref, mask_ref, o_ref, m_sc, acc_sc, qfull, *, n):
    b = pl.program_id(0)
    i, j = _pair_tiles(pl.program_id(1), pl.program_id(2), n)
    fast = fast_ref[b, i] == 1

    @pl.when(j == 0)
    def _():
        _att_init(qt_ref, qa_ref, m_sc, acc_sc, qfull)

    @pl.when(fast)
    def _():
        keep = mask_ref[0, 0, 0]
        ka = ka_ref[0]
        _for_heads(lambda h: _scores(k_ref, ka, qfull, h),
                   lambda s, h: _fast_step(s, keep, vt_ref[_vrows(h), :], h, acc_sc))

    @pl.when(jnp.logical_not(fast))
    def _():
        bias = (mask_ref[0, 0, 0].astype(F32) - 1.0) * (-NEG)
        ka = ka_ref[0]
        _for_heads(lambda h: _scores(k_ref, ka, qfull, h) + bias,
                   lambda s, h: _exact_step(s, vt_ref[_vrows(h), :], h, m_sc, acc_sc))

    @pl.when(j == i)
    def _():
        _att_finish(o_ref, acc_sc)


def _dsa(fast, k, k_aug, q_t, q_aug, v_t, mask, t):
    b, s, c = k.shape
    n = s // t
    assert n % 2 == 0, "query tiles are processed in pairs"

    def q_tile(r, j):
        return _pair_tiles(r, j, n)[0]

    def k_tile(r, j):
        return _pair_tiles(r, j, n)[1]

    grid_spec = pltpu.PrefetchScalarGridSpec(
        num_scalar_prefetch=1,
        grid=(b, n // 2, n + 1),
        in_specs=[pl.BlockSpec((1, t, c), lambda bi, r, j, f: (bi, k_tile(r, j), 0)),
                  pl.BlockSpec((1, t, LANES), lambda bi, r, j, f: (bi, k_tile(r, j), 0)),
                  pl.BlockSpec((c, t), lambda bi, r, j, f: (0, bi * n + q_tile(r, j))),
                  pl.BlockSpec((1, ATT_HEADS, AUG, t), lambda bi, r, j, f: (bi, 0, 0, q_tile(r, j))),
                  pl.BlockSpec((ATT_HEADS * V_ROWS, t), lambda bi, r, j, f: (0, bi * n + k_tile(r, j))),
                  pl.BlockSpec((1, 1, 1, t, t), lambda bi, r, j, f: (bi, q_tile(r, j), k_tile(r, j), 0, 0))],
        out_specs=pl.BlockSpec((1, t, c), lambda bi, r, j, f: (bi, q_tile(r, j), 0)),
        scratch_shapes=_att_scratch(t))
    return pl.pallas_call(
        functools.partial(_dsa_kernel, n=n),
        grid_spec=grid_spec,
        out_shape=jax.ShapeDtypeStruct((b, s, c), BF16),
        compiler_params=_params("parallel", "parallel", "arbitrary"),
        name="dsa",
    )(fast, k, k_aug, q_t, q_aug, v_t, mask)


CONV_HALO = 8


def _proj_conv_kernel(x_ref, xh_ref, wu_ref, wb_ref, wc_ref, wf_ref, cw_ref, o_ref, f_ref, ext, *, tm,
                      tiles_per_seq):
    ti = pl.program_id(0) % tiles_per_seq
    x, xh = x_ref[...], xh_ref[...]
    halo = (jnp.dot(xh, wc_ref[...], preferred_element_type=F32)
            * jnp.dot(xh, wu_ref[...], preferred_element_type=F32))
    ext[0:CONV_HALO, :] = jnp.where(ti > 0, halo, 0.0)
    ext[CONV_HALO:, :] = (jnp.dot(x, wc_ref[...], preferred_element_type=F32)
                          * jnp.dot(x, wu_ref[...], preferred_element_type=F32))
    conv = None
    for t in range(CONV_WIDTH):
        off = CONV_HALO - (CONV_WIDTH - 1) + t
        term = ext[off:off + tm, :] * cw_ref[t:t + 1, :]
        conv = term if conv is None else conv + term
    gate_b = jnp.dot(x, wb_ref[...], preferred_element_type=F32)
    o_ref[...] = (gate_b * conv).astype(o_ref.dtype)
    f_ref[...] = jnp.dot(x, wf_ref[...], preferred_element_type=F32)


def _proj_conv(x, w, layer, w_f, conv_w, seq, tm=512):
    n, k = x.shape
    c = HALF_WIDTH
    tm = min(tm, seq)
    hb = tm // CONV_HALO

    def wcol(j):
        return pl.BlockSpec((None, k, c), lambda i: (layer, 0, j))

    return pl.pallas_call(
        functools.partial(_proj_conv_kernel, tm=tm, tiles_per_seq=seq // tm),
        grid=(n // tm,),
        in_specs=[pl.BlockSpec((tm, k), lambda i: (i, 0)),
                  pl.BlockSpec((CONV_HALO, k), lambda i: (jnp.maximum(i * hb - 1, 0), 0)),
                  wcol(0), wcol(1), wcol(2),
                  pl.BlockSpec(w_f.shape, lambda i: (0, 0)),
                  pl.BlockSpec(conv_w.shape, lambda i: (0, 0))],
        out_specs=[pl.BlockSpec((tm, c), lambda i: (i, 0)), pl.BlockSpec((tm, LANES), lambda i: (i, 0))],
        out_shape=[jax.ShapeDtypeStruct((n, c), BF16), jax.ShapeDtypeStruct((n, LANES), F32)],
        scratch_shapes=[pltpu.VMEM((tm + CONV_HALO, c), F32)],
        compiler_params=_params("parallel"),
        name="proj_conv",
    )(x, x, w, w, w, w_f, conv_w)


GATE_CHUNK = 256


def _fgate_kernel(f_ref, fb_ref, hi_ref, mid_ref, lo_ref):
    s = f_ref.shape[1]
    r = lax.broadcasted_iota(jnp.int32, (GATE_CHUNK, GATE_CHUNK), 0)
    c = lax.broadcasted_iota(jnp.int32, (GATE_CHUNK, GATE_CHUNK), 1)
    tri = (c <= r).astype(F32)

    def body(t, carry):
        rows = pl.ds(t * GATE_CHUNK, GATE_CHUNK)
        z = f_ref[0, rows, :] + fb_ref[...]
        log_f = -(jnp.maximum(-z, 0.0) + jnp.log1p(jnp.exp(-jnp.abs(z))))
        cs = jnp.dot(tri, log_f, preferred_element_type=F32, precision=lax.Precision.HIGHEST) + carry
        b2 = cs * LOG2E
        hi = b2.astype(BF16)
        r1 = b2 - hi.astype(F32)
        mid = r1.astype(BF16)
        hi_ref[0, rows, :] = hi
        mid_ref[0, rows, :] = mid
        lo_ref[0, rows, :] = (r1 - mid.astype(F32)).astype(BF16)
        return cs[GATE_CHUNK - 1:GATE_CHUNK, :]

    lax.fori_loop(0, s // GATE_CHUNK, body, jnp.zeros((1, LANES), F32))


def _forget_cumsum(f, fb):
    b, s, c = f.shape
    spec = pl.BlockSpec((1, s, c), lambda bi: (bi, 0, 0))
    return pl.pallas_call(
        _fgate_kernel,
        grid=(b,),
        in_specs=[spec, pl.BlockSpec((1, c), lambda bi: (0, 0))],
        out_specs=[spec, spec, spec],
        out_shape=[jax.ShapeDtypeStruct((b, s, c), BF16)] * 3,
        compiler_params=_params("parallel"),
        name="forget_cumsum",
    )(f, fb)


def _fox_schedule(first, fast, n):
    bsz = first.shape[0]
    steps = bsz * n * (n + 1) // 2
    tiles = jnp.arange(n, dtype=jnp.int32)
    cnt = (tiles[None, :] - first + 1).reshape(-1)
    end = jnp.cumsum(cnt)
    t = jnp.arange(steps, dtype=jnp.int32)
    tc = jnp.minimum(t, end[-1] - 1)
    row = jnp.sum((end[None, :] <= tc[:, None]).astype(jnp.int32), axis=1)
    rel = tc - (end - cnt)[row]
    q_tile = row % n
    active = t < end[-1]
    flags = active.astype(jnp.int32) + 2 * (active & (rel == cnt[row] - 1)).astype(jnp.int32)
    return jnp.stack([row // n, q_tile, q_tile - rel, flags, fast.reshape(-1)[row]]).astype(jnp.int32)


def _fox_kernel(sched_ref, k_ref, ka_ref, qt_ref, qa_ref, vt_ref, o_ref, m_sc, acc_sc, qfull, *, t):
    step = pl.program_id(0)
    i, kj, flags = sched_ref[1, step], sched_ref[2, step], sched_ref[3, step]
    fast = sched_ref[4, step] == 1
    active = (flags & 1) == 1
    diagonal = active & (kj == i)
    older = active & (kj != i)

    def causal():
        return lax.broadcasted_iota(jnp.int32, (t, t), 0) <= lax.broadcasted_iota(jnp.int32, (t, t), 1)

    def scores(masked):
        ka = ka_ref[0]

        def fn(h):
            s = _scores(k_ref, ka, qfull, h)
            return jnp.where(causal(), s, NEG) if masked else s
        return fn

    def fast_tile(masked):
        _for_heads(scores(masked), lambda s, h: _fast_step(s, None, vt_ref[_vrows(h), :], h, acc_sc))

    def exact_tile(masked):
        _for_heads(scores(masked), lambda s, h: _exact_step(s, vt_ref[_vrows(h), :], h, m_sc, acc_sc))

    @pl.when(diagonal)
    def _():
        _att_init(qt_ref, qa_ref, m_sc, acc_sc, qfull)

    @pl.when(diagonal & fast)
    def _():
        fast_tile(True)

    @pl.when(diagonal & jnp.logical_not(fast))
    def _():
        exact_tile(True)

    @pl.when(older & fast)
    def _():
        fast_tile(False)

    @pl.when(older & jnp.logical_not(fast))
    def _():
        exact_tile(False)

    @pl.when((flags & 2) == 2)
    def _():
        _att_finish(o_ref, acc_sc)


def _fox(first, fast, k, k_aug, q_t, q_aug, v_t, t):
    b, s, c = k.shape
    n = s // t
    sched = _fox_schedule(first, fast, n)
    grid_spec = pltpu.PrefetchScalarGridSpec(
        num_scalar_prefetch=1,
        grid=(sched.shape[1],),
        in_specs=[pl.BlockSpec((1, t, c), lambda st, sc: (sc[0, st], sc[2, st], 0)),
                  pl.BlockSpec((1, t, LANES), lambda st, sc: (sc[0, st], sc[2, st], 0)),
                  pl.BlockSpec((c, t), lambda st, sc: (0, sc[0, st] * n + sc[1, st])),
                  pl.BlockSpec((1, ATT_HEADS, AUG, t), lambda st, sc: (sc[0, st], 0, 0, sc[1, st])),
                  pl.BlockSpec((ATT_HEADS * V_ROWS, t), lambda st, sc: (0, sc[0, st] * n + sc[2, st]))],
        out_specs=pl.BlockSpec((1, t, c), lambda st, sc: (sc[0, st], sc[1, st], 0)),
        scratch_shapes=_att_scratch(t))
    return pl.pallas_call(
        functools.partial(_fox_kernel, t=t),
        grid_spec=grid_spec,
        out_shape=jax.ShapeDtypeStruct((b, s, c), BF16),
        compiler_params=_params("arbitrary"),
        name="fox",
    )(sched, k, k_aug, q_t, q_aug, v_t)


def _mix_xattn_kernel(pa_ref, pb_ref, wm_ref, xf_ref, wq_ref, kt_ref, v_ref, wo_ref, g_ref, b_ref, of_ref, ob_ref):
    half = pa_ref.shape[1]
    mixed = (jnp.dot(pa_ref[...], wm_ref[0:half, :], preferred_element_type=F32)
             + jnp.dot(pb_ref[...], wm_ref[half:, :], preferred_element_type=F32))
    x1 = _layer_norm(ALPHA * xf_ref[...] + mixed, g_ref[0:1, :], b_ref[0:1, :])
    q = jnp.dot(x1.astype(BF16), wq_ref[...], preferred_element_type=F32) * Q_SCALE
    q = q.astype(BF16)
    outs = []
    for h in range(XA_HEADS):
        s = jnp.dot(q[:, _head(h)], kt_ref[0, _head(h), :], preferred_element_type=F32)
        p = jnp.exp2(s - jnp.max(s, axis=1, keepdims=True))
        l = jnp.sum(p, axis=1, keepdims=True)
        pv = jnp.dot(p.astype(BF16), v_ref[0, :, _head(h)], preferred_element_type=F32)
        outs.append((pv / l).astype(BF16))
    o = jnp.concatenate(outs, axis=1)
    y = jnp.dot(o, wo_ref[...], preferred_element_type=F32)
    x2 = _layer_norm(ALPHA * x1 + y, g_ref[1:2, :], b_ref[1:2, :])
    of_ref[...] = x2
    ob_ref[...] = x2.astype(BF16)


def _mix_xattn(pa, pb, w_mix, mix_layer, xf, wq, kt, v, wo, layer, g, b, seq, tm=512):
    n, d = xf.shape
    per_batch = seq // tm
    m = v.shape[1]
    row = lambda c: pl.BlockSpec((tm, c), lambda i: (i, 0))
    return pl.pallas_call(
        _mix_xattn_kernel,
        grid=(n // tm,),
        in_specs=[row(pa.shape[1]), row(pb.shape[1]),
                  pl.BlockSpec((None,) + w_mix.shape[1:], lambda i: (mix_layer, 0, 0)),
                  row(d),
                  pl.BlockSpec((None,) + wq.shape[1:], lambda i: (layer, 0, 0)),
                  pl.BlockSpec((1, XA_WIDTH, m), lambda i: (i // per_batch, 0, 0)),
                  pl.BlockSpec((1, m, XA_WIDTH), lambda i: (i // per_batch, 0, 0)),
                  pl.BlockSpec((None,) + wo.shape[1:], lambda i: (layer, 0, 0)),
                  pl.BlockSpec((2, d), lambda i: (0, 0)),
                  pl.BlockSpec((2, d), lambda i: (0, 0))],
        out_specs=[row(d), row(d)],
        out_shape=[jax.ShapeDtypeStruct((n, d), F32), jax.ShapeDtypeStruct((n, d), BF16)],
        compiler_params=_params("parallel"),
        name="mix_xattn",
    )(pa, pb, w_mix, xf, wq, kt, v, wo, g, b)


def _ffn_kernel(xb_ref, wg_ref, wu_ref, wo_ref, xf_ref, g_ref, b_ref, of_ref, ob_ref, ot_ref, acc):
    j = pl.program_id(1)

    @pl.when(j == 0)
    def _():
        acc[...] = jnp.zeros(acc.shape, F32)

    xb = xb_ref[...]
    gate = jnp.dot(xb, wg_ref[...], preferred_element_type=F32)
    up = jnp.dot(xb, wu_ref[...], preferred_element_type=F32)
    hid = gate * (1.0 / (1.0 + jnp.exp(-gate))) * up
    acc[...] += jnp.dot(hid.astype(BF16), wo_ref[...], preferred_element_type=F32)

    @pl.when(j == pl.num_programs(1) - 1)
    def _():
        y = _layer_norm(ALPHA * xf_ref[...] + acc[...], g_ref[...], b_ref[...])
        of_ref[...] = y
        ob_ref[...] = y.astype(BF16)
        ot_ref[...] = y.T.astype(BF16)


def _ffn(xb, xf, w_in, w_out, layer, g, b, tm=512, th=512):
    n, d = xf.shape
    hidden = w_out.shape[1]
    nh = hidden // th
    return pl.pallas_call(
        _ffn_kernel,
        grid=(n // tm, nh),
        in_specs=[pl.BlockSpec((tm, d), lambda i, j: (i, 0)),
                  pl.BlockSpec((None, d, th), lambda i, j: (layer, 0, j)),
                  pl.BlockSpec((None, d, th), lambda i, j: (layer, 0, j + nh)),
                  pl.BlockSpec((None, th, d), lambda i, j: (layer, j, 0)),
                  pl.BlockSpec((tm, d), lambda i, j: (i, 0)),
                  pl.BlockSpec((1, d), lambda i, j: (0, 0)),
                  pl.BlockSpec((1, d), lambda i, j: (0, 0))],
        out_specs=[pl.BlockSpec((tm, d), lambda i, j: (i, 0)),
                   pl.BlockSpec((tm, d), lambda i, j: (i, 0)),
                   pl.BlockSpec((d, tm), lambda i, j: (0, i))],
        out_shape=[jax.ShapeDtypeStruct((n, d), F32), jax.ShapeDtypeStruct((n, d), BF16),
                   jax.ShapeDtypeStruct((d, n), BF16)],
        scratch_shapes=[pltpu.VMEM((tm, d), F32)],
        compiler_params=_params("parallel", "arbitrary"),
        name="ffn",
    )(xb, w_in, w_in, w_out, xf, g, b)


def _rotary_tables(positions, dh):
    rot = dh // ROPE_FRACTION
    half = rot // 2
    inv_freq = jnp.power(ROPE_THETA, -(jnp.arange(half, dtype=F32) * 2.0 / rot))
    ang = positions.astype(F32)[..., None] * inv_freq
    cos, sin = jnp.cos(ang), jnp.sin(ang)
    zh = jnp.zeros_like(sin)
    rest = jnp.zeros(ang.shape[:-1] + (dh - rot,), F32)
    c = jnp.concatenate([cos, cos, rest + 1.0], axis=-1)
    s_lo = jnp.concatenate([zh, sin, rest], axis=-1)
    s_hi = jnp.concatenate([-sin, zh, rest], axis=-1)
    return tuple(t.reshape(-1, dh) for t in (c, s_lo, s_hi))


def _even_tables(positions):
    n = positions.size
    head = _rotary_tables(positions, HEAD_DIM)
    idx = _rotary_tables(positions, IDX_DIM)
    idx_full = tuple(jnp.tile(t, (1, LANES // IDX_DIM)) for t in idx)
    wi_scale = jnp.concatenate([jnp.full((IDX_HEADS,), IDX_HEADS ** -0.5, F32),
                                jnp.ones((LANES - IDX_DIM - IDX_HEADS,), F32)])
    pad = jnp.zeros((n, LANES - IDX_DIM), F32)
    tail = (jnp.concatenate([idx[0], pad + wi_scale], axis=1),
            jnp.concatenate([idx[1], pad], axis=1),
            jnp.concatenate([idx[2], pad], axis=1))
    return {"k": head,
            "q_t": tuple((t * Q_SCALE).T for t in head),
            "qi_t": tuple((t * IDX_DIM ** -0.5).T for t in idx_full),
            "tail": tail}


def _att_bounds(qn2, kn2, bsz, seq, t):
    n = seq // t
    qn = jnp.sqrt(qn2).reshape(ATT_HEADS, bsz, seq).transpose(1, 2, 0)
    kn = jnp.sqrt(kn2).reshape(bsz, seq, ATT_HEADS)
    k_max = jnp.max(kn, axis=1, keepdims=True)
    neg_m = -(1.01 * qn * k_max + 1.0)
    qn_tile = jnp.max(qn.reshape(bsz, n, t, ATT_HEADS), axis=2)
    kn_tile = jnp.max(kn.reshape(bsz, n, t, ATT_HEADS), axis=2)
    spread = jnp.max(2.05 * qn_tile * k_max, axis=-1) + 8.0
    fast = (spread <= FAST_RANGE_LOG2).astype(jnp.int32)
    return neg_m.astype(BF16), qn_tile, kn_tile, fast


def _aug_operands(q_entries, k_entries, bsz, seq):
    shape = (bsz, seq, ATT_HEADS)
    one, zero = jnp.ones(shape, BF16), jnp.zeros(shape, BF16)

    def pack(entries):
        cols = [one if e is None else e for e in entries]
        return jnp.stack(cols + [zero] * (AUG - len(cols)), axis=-1)

    return pack(k_entries).reshape(bsz, seq, ATT_HEADS * AUG), pack(q_entries).transpose(0, 2, 3, 1)


def _even_mixer(xb, xt, tables, wb, wt, j, pool_w, pool_scale, bsz, seq, tq):
    n = xb.shape[0]
    hw = HALF_WIDTH
    a = _proj_pool(xb, wb, pool_w, j, pool_scale.reshape(1, hw), seq)
    q_t, qn2 = _proj_t(wt, j, xt, BF16, hw, hw, tables["q_t"], shift=HEAD_DIM // 8, norms=True)
    k, kn2 = _proj(xb, wb, j, BF16, 2 * hw, hw, tables["k"], shift=HEAD_DIM // 8, norms=True)
    v_t = _proj_t(wt, j, xt, BF16, 3 * hw, hw, value_rows=True)
    qi_t = _proj_t(wt, j, xt, BF16, 4 * hw, hw, tables["qi_t"], shift=IDX_DIM // 8)
    w_tail = jnp.pad(wb[j:j + 1, :, 5 * hw:], ((0, 0), (0, 0), (0, LANES - IDX_DIM - IDX_HEADS)))
    tail = _proj(xb, w_tail, 0, F32, 0, LANES, tables["tail"], shift=IDX_DIM // 8)

    ki = tail[:, :IDX_DIM].astype(BF16).reshape(bsz, seq, IDX_DIM)
    wi_t = tail[:, IDX_DIM:IDX_DIM + IDX_HEADS].T

    mask = _idx_mask(qi_t, wi_t, ki, tq, tq)
    neg_m, _, _, fast = _att_bounds(qn2, kn2, bsz, seq, tq)
    k_aug, q_aug = _aug_operands([neg_m], [None], bsz, seq)
    bb = _dsa(fast, k.reshape(bsz, seq, hw), k_aug, q_t, q_aug, v_t, mask, tq)
    return a, bb.reshape(n, hw)


DSA_TILE = 512
FOX_TILE = 512
UNDERFLOW_LOG2 = 160.0


def _fox_first_tile(qn, kn, terms, bsz, seq, t):
    n = seq // t
    f32sum = sum(x.astype(F32) for x in terms)[:, :, :ATT_HEADS].reshape(bsz, n, t, ATT_HEADS)
    f_first, f_last = f32sum[:, :, 0], f32sum[:, :, t - 1]
    bound = (1.01 * qn[:, :, None] * (kn[:, None, :] + kn[:, :, None])
             + f_first[:, :, None] - f_last[:, None, :] + 1.0)
    tiles = jnp.arange(n, dtype=jnp.int32)
    needed = jnp.any(bound >= -UNDERFLOW_LOG2, axis=-1) | (tiles[:, None] == tiles[None, :])
    needed = needed & (tiles[None, :] <= tiles[:, None])
    return jnp.min(jnp.where(needed, tiles[None, None, :], n), axis=-1).astype(jnp.int32)


def _odd_mixer(xb, xt, wb, wt, j, conv_w, forget_b, bsz, seq):
    n = xb.shape[0]
    hw = HALF_WIDTH
    w_f = jnp.pad(wb[j, :, 6 * hw:], ((0, 0), (0, LANES - ATT_HEADS)))
    c, f = _proj_conv(xb, wb, j, w_f, conv_w, seq)
    q_t, qn2 = _proj_t(wt, j, xt, BF16, 3 * hw, hw, scale=Q_SCALE, norms=True)
    k, kn2 = _proj(xb, wb, j, BF16, 4 * hw, hw, norms=True)
    v_t = _proj_t(wt, j, xt, BF16, 5 * hw, hw, value_rows=True)
    fb = jnp.pad(forget_b, (0, LANES - ATT_HEADS)).reshape(1, LANES)
    terms = _forget_cumsum(f.reshape(bsz, seq, LANES), fb)
    hi, mid, lo = (x[:, :, :ATT_HEADS] for x in terms)

    t = min(FOX_TILE, seq)
    neg_m, qn, kn, fast = _att_bounds(qn2, kn2, bsz, seq, t)
    k_aug, q_aug = _aug_operands([neg_m, None, None, None, hi, mid, lo],
                                 [None, -hi, -mid, -lo, None, None, None], bsz, seq)
    first = _fox_first_tile(qn, kn, terms, bsz, seq, t)
    d = _fox(first, fast, k.reshape(bsz, seq, hw), k_aug, q_t, q_aug, v_t, t)
    return c, d.reshape(n, hw)


def kernel(x, mem, positions, ev_w_in, ev_pool_w, ev_pool_scale, ev_w_out, od_w_in, od_conv_w, od_forget_b,
           od_w_out, ca_w_q, ca_w_kv, ca_w_o, ffn_w_in, ffn_w_out, ln_g, ln_b):
    bsz, seq, d = x.shape
    n = bsz * seq
    m = mem.shape[1]
    tq = min(DSA_TILE, seq)
    xf = x.reshape(n, d)
    xb = xf.astype(BF16)
    memb = mem.reshape(bsz * m, d).astype(BF16)
    tables = _even_tables(positions)
    xt = xb.T
    ev_wb, od_wb = ev_w_in.astype(BF16), od_w_in.astype(BF16)
    ev_wt, od_wt = ev_wb.transpose(0, 2, 1), od_wb.transpose(0, 2, 1)
    ev_wo, od_wo, pool_w = ev_w_out.astype(BF16), od_w_out.astype(BF16), ev_pool_w.astype(BF16)
    ca_q, ca_kv, ca_o = ca_w_q.astype(BF16), ca_w_kv.astype(BF16), ca_w_o.astype(BF16)
    ffn_in, ffn_out = ffn_w_in.astype(BF16), ffn_w_out.astype(BF16)
    for i in range(DEPTH):
        j = i // 2
        if i % 2 == 0:
            parts = _even_mixer(xb, xt, tables, ev_wb, ev_wt, j, pool_w, ev_pool_scale[j], bsz, seq, tq)
            w_out = ev_wo
        else:
            parts = _odd_mixer(xb, xt, od_wb, od_wt, j, od_conv_w[j], od_forget_b[j], bsz, seq)
            w_out = od_wo
        kv = _proj(memb, ca_kv, i, BF16, 0, 2 * XA_WIDTH).reshape(bsz, m, 2 * XA_WIDTH)
        kt = kv[:, :, :XA_WIDTH].transpose(0, 2, 1)
        xf, xb = _mix_xattn(parts[0], parts[1], w_out, j, xf, ca_q, kt, kv[:, :, XA_WIDTH:], ca_o, i,
                            ln_g[i, 0:2], ln_b[i, 0:2], seq)
        xf, xb, xt = _ffn(xb, xf, ffn_in, ffn_out, i, ln_g[i, 2:3], ln_b[i, 2:3])
    return xf.reshape(bsz, seq, d)
```

```python
import functools

import jax
import jax.numpy as jnp
from jax import lax
from jax.experimental import pallas as pl
from jax.experimental.pallas import tpu as pltpu

F32 = jnp.float32
BF16 = jnp.bfloat16

D_MODEL = 2048
DEPTH = 4
CHUNK = 64
HEAD_DIM = 128
HALF_WIDTH = D_MODEL // 2
POOL_WINDOWS = (2, 4, 8, 16)
POOL_GROUP_DIM = HALF_WIDTH // len(POOL_WINDOWS)
ATT_HEADS = HALF_WIDTH // HEAD_DIM
IDX_HEADS = 16
IDX_DIM = 64
DSA_TOPK_MAX = 256
CONV_WIDTH = 3
XA_HEADS = 4
XA_WIDTH = XA_HEADS * HEAD_DIM
FFN_HIDDEN = -(-(8 * D_MODEL) // (3 * 256)) * 256
ROPE_THETA = 500000.0
ROPE_FRACTION = 4
LN_EPS = 1e-5
ALPHA = (2 * DEPTH) ** 0.25
LOG2E = 1.4426950408889634
Q_SCALE = HEAD_DIM ** -0.5 * LOG2E

LANES = 128
SUBLANES = 8
PACKED_ROWS = 16
NEG = -1e30
INT_MIN = -(2 ** 31)
VMEM_LIMIT = 60000 * 1024


def _params(*semantics):
    return pltpu.CompilerParams(dimension_semantics=semantics, vmem_limit_bytes=VMEM_LIMIT)


def _layer_norm(y, g, b):
    mu = jnp.mean(y, axis=-1, keepdims=True)
    d = y - mu
    var = jnp.mean(d * d, axis=-1, keepdims=True)
    return d * lax.rsqrt(var + LN_EPS) * g + b


AUG = 16
V_ROWS = HEAD_DIM + AUG


def _proj_kernel(*refs, shift, scale, axis, norms, value_rows):
    refs = list(refs)
    a_ref, b_ref = refs[0], refs[1]
    n_ref = refs.pop() if norms else None
    o_ref = refs.pop()
    h = jnp.dot(a_ref[...], b_ref[...], preferred_element_type=F32)
    if not (shift or norms or value_rows):
        o_ref[...] = (h * scale if scale != 1.0 else h).astype(o_ref.dtype)
        return
    if shift:
        c, s1, s2 = refs[2][...], refs[3][...], refs[4][...]
    for g in range(h.shape[axis] // LANES):
        sl = (slice(None),) * axis + (slice(g * LANES, (g + 1) * LANES),)
        hg = h[sl]
        if shift:
            hg = hg * c + pltpu.roll(hg, shift, axis) * s1 + pltpu.roll(hg, LANES - shift, axis) * s2
        elif scale != 1.0:
            hg = hg * scale
        og = hg.astype(o_ref.dtype)
        if value_rows:
            tm = og.shape[1]
            o_ref[g * V_ROWS:g * V_ROWS + LANES, :] = og
            row = lax.broadcasted_iota(jnp.int32, (AUG, tm), 0)
            o_ref[g * V_ROWS + LANES:(g + 1) * V_ROWS, :] = jnp.where(row == 0, 1.0, 0.0).astype(o_ref.dtype)
        else:
            o_ref[sl] = og
        if norms and axis == 0:
            n_ref[g:g + 1, :] = jnp.sum(jnp.square(og.astype(F32)), axis=0, keepdims=True)
    if norms and axis == 1:
        tn = h.shape[1]
        sq = jnp.square(o_ref[...].astype(F32)).astype(BF16)
        owner = lax.broadcasted_iota(jnp.int32, (tn, LANES), 0) // LANES
        ind = jnp.where(owner == lax.broadcasted_iota(jnp.int32, (tn, LANES), 1), 1.0, 0.0).astype(BF16)
        sums = jnp.dot(sq, ind, preferred_element_type=F32)
        n_ref[...] = sums[:, 0:tn // LANES]


def _proj(x, w, layer, out_dtype, col0, ncols, tables=None, shift=0, scale=1.0, norms=False, tm=1024, tn=1024):
    n, k = x.shape
    tm, tn = min(tm, n), min(tn, ncols)
    assert col0 % tn == 0 and ncols % tn == 0
    in_specs = [pl.BlockSpec((tm, k), lambda i, j: (i, 0)),
                pl.BlockSpec((None, k, tn), lambda i, j: (layer, 0, col0 // tn + j))]
    args = [x, w]
    if shift:
        in_specs += [pl.BlockSpec((tm, LANES), lambda i, j: (i, 0))] * 3
        args += list(tables)
    out_specs = [pl.BlockSpec((tm, tn), lambda i, j: (i, j))]
    out_shape = [jax.ShapeDtypeStruct((n, ncols), out_dtype)]
    if norms:
        assert tn == ncols
        out_specs.append(pl.BlockSpec((tm, tn // LANES), lambda i, j: (i, 0)))
        out_shape.append(jax.ShapeDtypeStruct((n, ncols // LANES), F32))
    out = pl.pallas_call(
        functools.partial(_proj_kernel, shift=shift, scale=scale, axis=1, norms=norms, value_rows=False),
        grid=(n // tm, ncols // tn),
        in_specs=in_specs,
        out_specs=out_specs,
        out_shape=out_shape,
        compiler_params=_params("parallel", "arbitrary"),
        name="proj",
    )(*args)
    return out if norms else out[0]


def _proj_t(wt, layer, xt, out_dtype, row0, nrows, tables=None, shift=0, scale=1.0, norms=False,
            value_rows=False, tm=1024, tn=1024):
    k, n = xt.shape
    tm, tn = min(tm, n), min(tn, nrows)
    assert row0 % tn == 0 and nrows % tn == 0
    in_specs = [pl.BlockSpec((None, tn, k), lambda i, j: (layer, row0 // tn + j, 0)),
                pl.BlockSpec((k, tm), lambda i, j: (0, i))]
    args = [wt, xt]
    if shift:
        in_specs += [pl.BlockSpec((LANES, tm), lambda i, j: (0, i))] * 3
        args += list(tables)
    rows_out = tn // LANES * V_ROWS if value_rows else tn
    out_specs = [pl.BlockSpec((rows_out, tm), lambda i, j: (j, i))]
    out_shape = [jax.ShapeDtypeStruct((nrows // tn * rows_out, n), out_dtype)]
    if norms:
        assert tn == nrows
        out_specs.append(pl.BlockSpec((tn // LANES, tm), lambda i, j: (0, i)))
        out_shape.append(jax.ShapeDtypeStruct((nrows // LANES, n), F32))
    out = pl.pallas_call(
        functools.partial(_proj_kernel, shift=shift, scale=scale, axis=0, norms=norms, value_rows=value_rows),
        grid=(n // tm, nrows // tn),
        in_specs=in_specs,
        out_specs=out_specs,
        out_shape=out_shape,
        compiler_params=_params("parallel", "arbitrary"),
        name="proj_t",
    )(*args)
    return out if norms else out[0]


POOL_HALO = 16


def _proj_pool_kernel(x_ref, xh_ref, w_ref, pw_ref, sc_ref, o_ref, ext, *, tm, tiles_per_seq):
    ti = pl.program_id(0) % tiles_per_seq
    halo = jnp.dot(xh_ref[...], w_ref[...], preferred_element_type=F32)
    ext[0:POOL_HALO, :] = jnp.where(ti > 0, halo, 0.0)
    ext[POOL_HALO:, :] = jnp.dot(x_ref[...], w_ref[...], preferred_element_type=F32)
    cnt = ti * tm + lax.broadcasted_iota(jnp.int32, (tm, 1), 0) + 1
    for g, win in enumerate(POOL_WINDOWS):
        lo, hi = g * POOL_GROUP_DIM, (g + 1) * POOL_GROUP_DIM
        cur = ext[POOL_HALO:POOL_HALO + tm, lo:hi]
        s = cur
        for j in range(1, win):
            s = s + ext[POOL_HALO - j:POOL_HALO - j + tm, lo:hi]
        d = s / jnp.minimum(cnt, win).astype(F32) - cur
        y = jnp.dot(d.astype(BF16), pw_ref[g], preferred_element_type=F32)
        o_ref[:, lo:hi] = (y * sc_ref[:, lo:hi]).astype(o_ref.dtype)


def _proj_pool(x, w, pool_w, layer, scale, seq, tm=1024):
    n, k = x.shape
    c = HALF_WIDTH
    tm = min(tm, seq)
    hb = tm // POOL_HALO
    return pl.pallas_call(
        functools.partial(_proj_pool_kernel, tm=tm, tiles_per_seq=seq // tm),
        grid=(n // tm,),
        in_specs=[pl.BlockSpec((tm, k), lambda i: (i, 0)),
                  pl.BlockSpec((POOL_HALO, k), lambda i: (jnp.maximum(i * hb - 1, 0), 0)),
                  pl.BlockSpec((None, k, c), lambda i: (layer, 0, 0)),
                  pl.BlockSpec((None,) + pool_w.shape[1:], lambda i: (layer, 0, 0, 0)),
                  pl.BlockSpec((1, c), lambda i: (0, 0))],
        out_specs=pl.BlockSpec((tm, c), lambda i: (i, 0)),
        out_shape=jax.ShapeDtypeStruct((n, c), BF16),
        scratch_shapes=[pltpu.VMEM((tm + POOL_HALO, c), F32)],
        compiler_params=_params("parallel"),
        name="proj_pool",
    )(x, x, w, pool_w, scale)


def _sortable(v):
    bits = lax.bitcast_convert_type(v, jnp.int32)
    return bits ^ ((bits >> 31) & 0x7FFFFFFF)


def _colsum8(v):
    tk, tq = v.shape
    return v.reshape(tk // SUBLANES, SUBLANES, tq).sum(axis=0)


def _idx_kernel(qi_ref, wi_ref, ki_ref, o_ref, keys, half, *, tq, tk, nk, topk):
    i = pl.program_id(1)
    nact = (i * tq + tq - 1) // tk + 1
    q_chunk = (i * tq + lax.broadcasted_iota(jnp.int32, (1, tq), 1)) // CHUNK

    def score_tile(c, carry):
        kt = ki_ref[0, pl.ds(pl.multiple_of(c * tk, tk), tk), :]
        acc = jnp.zeros((tk, tq), F32)
        for h in range(IDX_HEADS):
            il = jnp.dot(kt, qi_ref[h * IDX_DIM:(h + 1) * IDX_DIM, :], preferred_element_type=F32)
            acc = acc + jnp.maximum(il, 0.0) * wi_ref[h:h + 1, :]
        k_chunk = (c * tk + lax.broadcasted_iota(jnp.int32, (tk, 1), 0)) // CHUNK
        kk = jnp.where(k_chunk <= q_chunk, _sortable(acc), INT_MIN)
        keys[c] = kk
        half[c] = (kk >> 16).astype(jnp.int16)
        return carry

    lax.fori_loop(0, nact, score_tile, 0)

    def count(pred):
        def body(c, acc):
            return acc + _colsum8(jnp.where(pred(keys[c], c), 1, 0))
        acc = lax.fori_loop(0, nact, body, jnp.zeros((SUBLANES, tq), jnp.int32))
        return jnp.sum(acc, axis=0, keepdims=True)

    def count16(cand, strict):
        cand16 = jnp.broadcast_to(cand, (PACKED_ROWS, tq)).astype(jnp.int16)[None]

        def body(c, acc):
            h3 = half[c].reshape(tk // PACKED_ROWS, PACKED_ROWS, tq)
            hit = (h3 > cand16) if strict else (h3 >= cand16)
            ones = jnp.where(hit, jnp.int16(1), jnp.int16(0))
            for r in range(tk // PACKED_ROWS):
                acc = acc + ones[r]
            return acc

        acc = lax.fori_loop(0, nact, body, jnp.zeros((PACKED_ROWS, tq), jnp.int16))
        return jnp.sum(acc.astype(jnp.int32), axis=0, keepdims=True)

    c0 = count16(jnp.zeros((1, tq), jnp.int32), False)
    sign_ok = c0 >= topk
    state = (jnp.where(sign_ok, 0, -(2 ** 15)), jnp.where(sign_ok, c0, nact * tk), jnp.where(sign_ok, 0, c0))

    def hi_step(b, state):
        hi, ge_hi, n_above = state
        cand = hi | (1 << (14 - b))
        c = count16(cand, False)
        ok = c >= topk
        return jnp.where(ok, cand, hi), jnp.where(ok, c, ge_hi), jnp.where(ok, n_above, c)

    hi, ge_hi, n_above = lax.fori_loop(0, 15, hi_step, state)

    def low_tile(c, carry):
        kk = keys[c]
        half[c] = jnp.where((kk >> 16) == hi, (kk & 0xFFFF) - 2 ** 15, -(2 ** 15)).astype(jnp.int16)
        return carry

    lax.fori_loop(0, nact, low_tile, 0)

    def lo_step(b, state):
        lo, in_bucket = state
        cand = lo | (1 << (15 - b))
        c = count16(cand - 2 ** 15, False)
        ok = n_above + c >= topk
        return jnp.where(ok, cand, lo), jnp.where(ok, c, in_bucket)

    lo, in_bucket = lax.fori_loop(0, 16, lo_step, (jnp.zeros((1, tq), jnp.int32), ge_hi - n_above))
    thr = (hi << 16) | lo
    n_ge = jnp.where(thr == INT_MIN, 0, n_above + in_bucket)
    thr = jnp.maximum(thr, INT_MIN + 1)

    def write_tiles(select):
        def body(c, carry):
            o_ref[0, 0, c] = jnp.where(select(keys[c], c), 1.0, 0.0).astype(o_ref.dtype)
            return carry
        lax.fori_loop(0, nact, body, 0)

    has_ties = jnp.max(n_ge) > topk

    @pl.when(jnp.logical_not(has_ties))
    def _():
        write_tiles(lambda kk, c: kk >= thr)

    @pl.when(has_ties)
    def _():
        need = topk - count(lambda kk, c: kk > thr)

        def key_index(c):
            return c * tk + lax.broadcasted_iota(jnp.int32, (tk, 1), 0)

        index_bits = (nk * tk - 1).bit_length()

        def idx_step(b, cut):
            cand = cut | (1 << (index_bits - 1 - b))
            below = count(lambda kk, c: (kk == thr) & (key_index(c) < cand))
            return jnp.where(below < need, cand, cut)

        cut = lax.fori_loop(0, index_bits, idx_step, jnp.zeros((1, tq), jnp.int32))
        write_tiles(lambda kk, c: (kk > thr) | ((kk == thr) & (key_index(c) <= cut)))

    def fill_tile(c, carry):
        o_ref[0, 0, c] = jnp.zeros((tk, tq), o_ref.dtype)
        return carry

    lax.fori_loop(nact, nk, fill_tile, 0)


def _idx_mask(qi_t, wi_t, ki, tq, tk):
    b, s, _ = ki.shape
    nq, nk = s // tq, s // tk
    topk = min(DSA_TOPK_MAX, s // 4)
    return pl.pallas_call(
        functools.partial(_idx_kernel, tq=tq, tk=tk, nk=nk, topk=topk),
        grid=(b, nq),
        in_specs=[pl.BlockSpec((IDX_HEADS * IDX_DIM, tq), lambda bi, i: (0, bi * nq + i)),
                  pl.BlockSpec((IDX_HEADS, tq), lambda bi, i: (0, bi * nq + i)),
                  pl.BlockSpec((1, s, IDX_DIM), lambda bi, i: (bi, 0, 0))],
        out_specs=pl.BlockSpec((1, 1, nk, tk, tq), lambda bi, i: (bi, i, 0, 0, 0)),
        out_shape=jax.ShapeDtypeStruct((b, nq, nk, tk, tq), BF16),
        scratch_shapes=[pltpu.VMEM((nk, tk, tq), jnp.int32), pltpu.VMEM((nk, tk, tq), jnp.int16)],
        compiler_params=_params("parallel", "arbitrary"),
        name="idx_mask",
    )(qi_t, wi_t, ki)


FAST_RANGE_LOG2 = 120.0


def _head(h):
    return slice(h * HEAD_DIM, (h + 1) * HEAD_DIM)


def _vrows(h):
    return slice(h * V_ROWS, (h + 1) * V_ROWS)


def _att_init(qt_ref, qa_ref, m_sc, acc_sc, qfull):
    m_sc[...] = jnp.full(m_sc.shape, NEG, F32)
    acc_sc[...] = jnp.zeros(acc_sc.shape, F32)
    tq = qfull.shape[2]
    for h in range(ATT_HEADS):
        qfull[h, 0:HEAD_DIM, :] = qt_ref[_head(h), :]
        qfull[h, HEAD_DIM:, :] = jnp.zeros((LANES, tq), BF16)
        lo = HEAD_DIM + h * AUG
        qfull[h, lo:lo + AUG, :] = qa_ref[0, h]


def _scores(k_ref, ka, qfull, h):
    kfull = jnp.concatenate([k_ref[0, :, _head(h)], ka], axis=1)
    return jnp.dot(kfull, qfull[h], preferred_element_type=F32)


def _for_heads(score_fn, step_fn):
    s_next = score_fn(0)
    for h in range(ATT_HEADS):
        s = s_next
        if h + 1 < ATT_HEADS:
            s_next = score_fn(h + 1)
        step_fn(s, h)


def _fast_step(s, keep, vt, h, acc_sc):
    p = jnp.exp2(s).astype(BF16)
    if keep is not None:
        p = p * keep
    acc_sc[h] += jnp.dot(vt, p, preferred_element_type=F32)


def _exact_step(s, vt, h, m_sc, acc_sc):
    m_prev = m_sc[h]
    m_new = jnp.maximum(m_prev, jnp.max(s, axis=0, keepdims=True))
    alpha = jnp.exp2(m_prev - m_new)
    p = jnp.exp2(s - m_new)
    acc_sc[h] = alpha * acc_sc[h] + jnp.dot(vt, p.astype(BF16), preferred_element_type=F32)
    m_sc[h] = m_new


def _att_finish(o_ref, acc_sc):
    for h in range(ATT_HEADS):
        a = acc_sc[h]
        o = (a[0:HEAD_DIM] / a[HEAD_DIM:HEAD_DIM + 1]).T
        o_ref[0, :, _head(h)] = o.astype(o_ref.dtype)


def _att_scratch(tq):
    return [pltpu.VMEM((ATT_HEADS, 1, tq), F32),
            pltpu.VMEM((ATT_HEADS, V_ROWS, tq), F32),
            pltpu.VMEM((ATT_HEADS, 2 * HEAD_DIM, tq), BF16)]


def _pair_tiles(r, j, n):
    second = j > r
    return jnp.where(second, n - 1 - r, r), jnp.where(second, j - r - 1, j)


def _dsa_kernel(fast_ref, k_ref, ka_ref, qt_ref, qa_ref, vt_ref, mask_ref, o_ref, m_sc, acc_sc, qfull, *, n):
    b = pl.program_id(0)
    i, j = _pair_tiles(pl.program_id(1), pl.program_id(2), n)
    fast = fast_ref[b, i] == 1

    @pl.when(j == 0)
    def _():
        _att_init(qt_ref, qa_ref, m_sc, acc_sc, qfull)

    @pl.when(fast)
    def _():
        keep = mask_ref[0, 0, 0]
        ka = ka_ref[0]
        _for_heads(lambda h: _scores(k_ref, ka, qfull, h),
                   lambda s, h: _fast_step(s, keep, vt_ref[_vrows(h), :], h, acc_sc))

    @pl.when(jnp.logical_not(fast))
    def _():
        bias = (mask_ref[0, 0, 0].astype(F32) - 1.0) * (-NEG)
        ka = ka_ref[0]
        _for_heads(lambda h: _scores(k_ref, ka, qfull, h) + bias,
                   lambda s, h: _exact_step(s, vt_ref[_vrows(h), :], h, m_sc, acc_sc))

    @pl.when(j == i)
    def _():
        _att_finish(o_ref, acc_sc)


def _dsa(fast, k, k_aug, q_t, q_aug, v_t, mask, t):
    b, s, c = k.shape
    n = s // t
    assert n % 2 == 0, "query tiles are processed in pairs"

    def q_tile(r, j):
        return _pair_tiles(r, j, n)[0]

    def k_tile(r, j):
        return _pair_tiles(r, j, n)[1]

    grid_spec = pltpu.PrefetchScalarGridSpec(
        num_scalar_prefetch=1,
        grid=(b, n // 2, n + 1),
        in_specs=[pl.BlockSpec((1, t, c), lambda bi, r, j, f: (bi, k_tile(r, j), 0)),
                  pl.BlockSpec((1, t, LANES), lambda bi, r, j, f: (bi, k_tile(r, j), 0)),
                  pl.BlockSpec((c, t), lambda bi, r, j, f: (0, bi * n + q_tile(r, j))),
                  pl.BlockSpec((1, ATT_HEADS, AUG, t), lambda bi, r, j, f: (bi, 0, 0, q_tile(r, j))),
                  pl.BlockSpec((ATT_HEADS * V_ROWS, t), lambda bi, r, j, f: (0, bi * n + k_tile(r, j))),
                  pl.BlockSpec((1, 1, 1, t, t), lambda bi, r, j, f: (bi, q_tile(r, j), k_tile(r, j), 0, 0))],
        out_specs=pl.BlockSpec((1, t, c), lambda bi, r, j, f: (bi, q_tile(r, j), 0)),
        scratch_shapes=_att_scratch(t))
    return pl.pallas_call(
        functools.partial(_dsa_kernel, n=n),
        grid_spec=grid_spec,
        out_shape=jax.ShapeDtypeStruct((b, s, c), BF16),
        compiler_params=_params("parallel", "parallel", "arbitrary"),
        name="dsa",
    )(fast, k, k_aug, q_t, q_aug, v_t, mask)


CONV_HALO = 8


def _proj_conv_kernel(x_ref, xh_ref, wu_ref, wb_ref, wc_ref, wf_ref, cw_ref, o_ref, f_ref, ext, *, tm,
                      tiles_per_seq):
    ti = pl.program_id(0) % tiles_per_seq
    x, xh = x_ref[...], xh_ref[...]
    halo = (jnp.dot(xh, wc_ref[...], preferred_element_type=F32)
            * jnp.dot(xh, wu_ref[...], preferred_element_type=F32))
    ext[0:CONV_HALO, :] = jnp.where(ti > 0, halo, 0.0)
    ext[CONV_HALO:, :] = (jnp.dot(x, wc_ref[...], preferred_element_type=F32)
                          * jnp.dot(x, wu_ref[...], preferred_element_type=F32))
    conv = None
    for t in range(CONV_WIDTH):
        off = CONV_HALO - (CONV_WIDTH - 1) + t
        term = ext[off:off + tm, :] * cw_ref[t:t + 1, :]
        conv = term if conv is None else conv + term
    gate_b = jnp.dot(x, wb_ref[...], preferred_element_type=F32)
    o_ref[...] = (gate_b * conv).astype(o_ref.dtype)
    f_ref[...] = jnp.dot(x, wf_ref[...], preferred_element_type=F32)


def _proj_conv(x, w, layer, w_f, conv_w, seq, tm=512):
    n, k = x.shape
    c = HALF_WIDTH
    tm = min(tm, seq)
    hb = tm // CONV_HALO

    def wcol(j):
        return pl.BlockSpec((None, k, c), lambda i: (layer, 0, j))

    return pl.pallas_call(
        functools.partial(_proj_conv_kernel, tm=tm, tiles_per_seq=seq // tm),
        grid=(n // tm,),
        in_specs=[pl.BlockSpec((tm, k), lambda i: (i, 0)),
                  pl.BlockSpec((CONV_HALO, k), lambda i: (jnp.maximum(i * hb - 1, 0), 0)),
                  wcol(0), wcol(1), wcol(2),
                  pl.BlockSpec(w_f.shape, lambda i: (0, 0)),
                  pl.BlockSpec(conv_w.shape, lambda i: (0, 0))],
        out_specs=[pl.BlockSpec((tm, c), lambda i: (i, 0)), pl.BlockSpec((tm, LANES), lambda i: (i, 0))],
        out_shape=[jax.ShapeDtypeStruct((n, c), BF16), jax.ShapeDtypeStruct((n, LANES), F32)],
        scratch_shapes=[pltpu.VMEM((tm + CONV_HALO, c), F32)],
        compiler_params=_params("parallel"),
        name="proj_conv",
    )(x, x, w, w, w, w_f, conv_w)


GATE_CHUNK = 256


def _fgate_kernel(f_ref, fb_ref, hi_ref, mid_ref, lo_ref):
    s = f_ref.shape[1]
    r = lax.broadcasted_iota(jnp.int32, (GATE_CHUNK, GATE_CHUNK), 0)
    c = lax.broadcasted_iota(jnp.int32, (GATE_CHUNK, GATE_CHUNK), 1)
    tri = (c <= r).astype(F32)

    def body(t, carry):
        rows = pl.ds(t * GATE_CHUNK, GATE_CHUNK)
        z = f_ref[0, rows, :] + fb_ref[...]
        log_f = -(jnp.maximum(-z, 0.0) + jnp.log1p(jnp.exp(-jnp.abs(z))))
        cs = jnp.dot(tri, log_f, preferred_element_type=F32, precision=lax.Precision.HIGHEST) + carry
        b2 = cs * LOG2E
        hi = b2.astype(BF16)
        r1 = b2 - hi.astype(F32)
        mid = r1.astype(BF16)
        hi_ref[0, rows, :] = hi
        mid_ref[0, rows, :] = mid
        lo_ref[0, rows, :] = (r1 - mid.astype(F32)).astype(BF16)
        return cs[GATE_CHUNK - 1:GATE_CHUNK, :]

    lax.fori_loop(0, s // GATE_CHUNK, body, jnp.zeros((1, LANES), F32))


def _forget_cumsum(f, fb):
    b, s, c = f.shape
    spec = pl.BlockSpec((1, s, c), lambda bi: (bi, 0, 0))
    return pl.pallas_call(
        _fgate_kernel,
        grid=(b,),
        in_specs=[spec, pl.BlockSpec((1, c), lambda bi: (0, 0))],
        out_specs=[spec, spec, spec],
        out_shape=[jax.ShapeDtypeStruct((b, s, c), BF16)] * 3,
        compiler_params=_params("parallel"),
        name="forget_cumsum",
    )(f, fb)


def _fox_schedule(first, fast, n):
    bsz = first.shape[0]
    steps = bsz * n * (n + 1) // 2
    tiles = jnp.arange(n, dtype=jnp.int32)
    cnt = (tiles[None, :] - first + 1).reshape(-1)
    end = jnp.cumsum(cnt)
    t = jnp.arange(steps, dtype=jnp.int32)
    tc = jnp.minimum(t, end[-1] - 1)
    row = jnp.sum((end[None, :] <= tc[:, None]).astype(jnp.int32), axis=1)
    rel = tc - (end - cnt)[row]
    q_tile = row % n
    active = t < end[-1]
    flags = active.astype(jnp.int32) + 2 * (active & (rel == cnt[row] - 1)).astype(jnp.int32)
    return jnp.stack([row // n, q_tile, q_tile - rel, flags, fast.reshape(-1)[row]]).astype(jnp.int32)


def _fox_kernel(sched_ref, k_ref, ka_ref, qt_ref, qa_ref, vt_ref, o_ref, m_sc, acc_sc, qfull, *, t):
    step = pl.program_id(0)
    i, kj, flags = sched_ref[1, step], sched_ref[2, step], sched_ref[3, step]
    fast = sched_ref[4, step] == 1
    active = (flags & 1) == 1
    diagonal = active & (kj == i)
    older = active & (kj != i)

    def causal():
        return lax.broadcasted_iota(jnp.int32, (t, t), 0) <= lax.broadcasted_iota(jnp.int32, (t, t), 1)

    def scores(masked):
        ka = ka_ref[0]

        def fn(h):
            s = _scores(k_ref, ka, qfull, h)
            return jnp.where(causal(), s, NEG) if masked else s
        return fn

    def fast_tile(masked):
        _for_heads(scores(masked), lambda s, h: _fast_step(s, None, vt_ref[_vrows(h), :], h, acc_sc))

    def exact_tile(masked):
        _for_heads(scores(masked), lambda s, h: _exact_step(s, vt_ref[_vrows(h), :], h, m_sc, acc_sc))

    @pl.when(diagonal)
    def _():
        _att_init(qt_ref, qa_ref, m_sc, acc_sc, qfull)

    @pl.when(diagonal & fast)
    def _():
        fast_tile(True)

    @pl.when(diagonal & jnp.logical_not(fast))
    def _():
        exact_tile(True)

    @pl.when(older & fast)
    def _():
        fast_tile(False)

    @pl.when(older & jnp.logical_not(fast))
    def _():
        exact_tile(False)

    @pl.when((flags & 2) == 2)
    def _():
        _att_finish(o_ref, acc_sc)


def _fox(first, fast, k, k_aug, q_t, q_aug, v_t, t):
    b, s, c = k.shape
    n = s // t
    sched = _fox_schedule(first, fast, n)
    grid_spec = pltpu.PrefetchScalarGridSpec(
        num_scalar_prefetch=1,
        grid=(sched.shape[1],),
        in_specs=[pl.BlockSpec((1, t, c), lambda st, sc: (sc[0, st], sc[2, st], 0)),
                  pl.BlockSpec((1, t, LANES), lambda st, sc: (sc[0, st], sc[2, st], 0)),
                  pl.BlockSpec((c, t), lambda st, sc: (0, sc[0, st] * n + sc[1, st])),
                  pl.BlockSpec((1, ATT_HEADS, AUG, t), lambda st, sc: (sc[0, st], 0, 0, sc[1, st])),
                  pl.BlockSpec((ATT_HEADS * V_ROWS, t), lambda st, sc: (0, sc[0, st] * n + sc[2, st]))],
        out_specs=pl.BlockSpec((1, t, c), lambda st, sc: (sc[0, st], sc[1, st], 0)),
        scratch_shapes=_att_scratch(t))
    return pl.pallas_call(
        functools.partial(_fox_kernel, t=t),
        grid_spec=grid_spec,
        out_shape=jax.ShapeDtypeStruct((b, s, c), BF16),
        compiler_params=_params("arbitrary"),
        name="fox",
    )(sched, k, k_aug, q_t, q_aug, v_t)


def _mix_xattn_kernel(pa_ref, pb_ref, wm_ref, xf_ref, wq_ref, kt_ref, v_ref, wo_ref, g_ref, b_ref, of_ref, ob_ref):
    half = pa_ref.shape[1]
    mixed = (jnp.dot(pa_ref[...], wm_ref[0:half, :], preferred_element_type=F32)
             + jnp.dot(pb_ref[...], wm_ref[half:, :], preferred_element_type=F32))
    x1 = _layer_norm(ALPHA * xf_ref[...] + mixed, g_ref[0:1, :], b_ref[0:1, :])
    q = jnp.dot(x1.astype(BF16), wq_ref[...], preferred_element_type=F32) * Q_SCALE
    q = q.astype(BF16)
    outs = []
    for h in range(XA_HEADS):
        s = jnp.dot(q[:, _head(h)], kt_ref[0, _head(h), :], preferred_element_type=F32)
        p = jnp.exp2(s - jnp.max(s, axis=1, keepdims=True))
        l = jnp.sum(p, axis=1, keepdims=True)
        pv = jnp.dot(p.astype(BF16), v_ref[0, :, _head(h)], preferred_element_type=F32)
        outs.append((pv / l).astype(BF16))
    o = jnp.concatenate(outs, axis=1)
    y = jnp.dot(o, wo_ref[...], preferred_element_type=F32)
    x2 = _layer_norm(ALPHA * x1 + y, g_ref[1:2, :], b_ref[1:2, :])
    of_ref[...] = x2
    ob_ref[...] = x2.astype(BF16)


def _mix_xattn(pa, pb, w_mix, mix_layer, xf, wq, kt, v, wo, layer, g, b, seq, tm=512):
    n, d = xf.shape
    per_batch = seq // tm
    m = v.shape[1]
    row = lambda c: pl.BlockSpec((tm, c), lambda i: (i, 0))
    return pl.pallas_call(
        _mix_xattn_kernel,
        grid=(n // tm,),
        in_specs=[row(pa.shape[1]), row(pb.shape[1]),
                  pl.BlockSpec((None,) + w_mix.shape[1:], lambda i: (mix_layer, 0, 0)),
                  row(d),
                  pl.BlockSpec((None,) + wq.shape[1:], lambda i: (layer, 0, 0)),
                  pl.BlockSpec((1, XA_WIDTH, m), lambda i: (i // per_batch, 0, 0)),
                  pl.BlockSpec((1, m, XA_WIDTH), lambda i: (i // per_batch, 0, 0)),
                  pl.BlockSpec((None,) + wo.shape[1:], lambda i: (layer, 0, 0)),
                  pl.BlockSpec((2, d), lambda i: (0, 0)),
                  pl.BlockSpec((2, d), lambda i: (0, 0))],
        out_specs=[row(d), row(d)],
        out_shape=[jax.ShapeDtypeStruct((n, d), F32), jax.ShapeDtypeStruct((n, d), BF16)],
        compiler_params=_params("parallel"),
        name="mix_xattn",
    )(pa, pb, w_mix, xf, wq, kt, v, wo, g, b)


def _ffn_kernel(xb_ref, wg_ref, wu_ref, wo_ref, xf_ref, g_ref, b_ref, of_ref, ob_ref, ot_ref, acc):
    j = pl.program_id(1)

    @pl.when(j == 0)
    def _():
        acc[...] = jnp.zeros(acc.shape, F32)

    xb = xb_ref[...]
    gate = jnp.dot(xb, wg_ref[...], preferred_element_type=F32)
    up = jnp.dot(xb, wu_ref[...], preferred_element_type=F32)
    hid = gate * (1.0 / (1.0 + jnp.exp(-gate))) * up
    acc[...] += jnp.dot(hid.astype(BF16), wo_ref[...], preferred_element_type=F32)

    @pl.when(j == pl.num_programs(1) - 1)
    def _():
        y = _layer_norm(ALPHA * xf_ref[...] + acc[...], g_ref[...], b_ref[...])
        of_ref[...] = y
        ob_ref[...] = y.astype(BF16)
        ot_ref[...] = y.T.astype(BF16)


def _ffn(xb, xf, w_in, w_out, layer, g, b, tm=512, th=512):
    n, d = xf.shape
    hidden = w_out.shape[1]
    nh = hidden // th
    return pl.pallas_call(
        _ffn_kernel,
        grid=(n // tm, nh),
        in_specs=[pl.BlockSpec((tm, d), lambda i, j: (i, 0)),
                  pl.BlockSpec((None, d, th), lambda i, j: (layer, 0, j)),
                  pl.BlockSpec((None, d, th), lambda i, j: (layer, 0, j + nh)),
                  pl.BlockSpec((None, th, d), lambda i, j: (layer, j, 0)),
                  pl.BlockSpec((tm, d), lambda i, j: (i, 0)),
                  pl.BlockSpec((1, d), lambda i, j: (0, 0)),
                  pl.BlockSpec((1, d), lambda i, j: (0, 0))],
        out_specs=[pl.BlockSpec((tm, d), lambda i, j: (i, 0)),
                   pl.BlockSpec((tm, d), lambda i, j: (i, 0)),
                   pl.BlockSpec((d, tm), lambda i, j: (0, i))],
        out_shape=[jax.ShapeDtypeStruct((n, d), F32), jax.ShapeDtypeStruct((n, d), BF16),
                   jax.ShapeDtypeStruct((d, n), BF16)],
        scratch_shapes=[pltpu.VMEM((tm, d), F32)],
        compiler_params=_params("parallel", "arbitrary"),
        name="ffn",
    )(xb, w_in, w_in, w_out, xf, g, b)


def _rotary_tables(positions, dh):
    rot = dh // ROPE_FRACTION
    half = rot // 2
    inv_freq = jnp.power(ROPE_THETA, -(jnp.arange(half, dtype=F32) * 2.0 / rot))
    ang = positions.astype(F32)[..., None] * inv_freq
    cos, sin = jnp.cos(ang), jnp.sin(ang)
    zh = jnp.zeros_like(sin)
    rest = jnp.zeros(ang.shape[:-1] + (dh - rot,), F32)
    c = jnp.concatenate([cos, cos, rest + 1.0], axis=-1)
    s_lo = jnp.concatenate([zh, sin, rest], axis=-1)
    s_hi = jnp.concatenate([-sin, zh, rest], axis=-1)
    return tuple(t.reshape(-1, dh) for t in (c, s_lo, s_hi))


def _even_tables(positions):
    n = positions.size
    head = _rotary_tables(positions, HEAD_DIM)
    idx = _rotary_tables(positions, IDX_DIM)
    idx_full = tuple(jnp.tile(t, (1, LANES // IDX_DIM)) for t in idx)
    wi_scale = jnp.concatenate([jnp.full((IDX_HEADS,), IDX_HEADS ** -0.5, F32),
                                jnp.ones((LANES - IDX_DIM - IDX_HEADS,), F32)])
    pad = jnp.zeros((n, LANES - IDX_DIM), F32)
    tail = (jnp.concatenate([idx[0], pad + wi_scale], axis=1),
            jnp.concatenate([idx[1], pad], axis=1),
            jnp.concatenate([idx[2], pad], axis=1))
    return {"k": head,
            "q_t": tuple((t * Q_SCALE).T for t in head),
            "qi_t": tuple((t * IDX_DIM ** -0.5).T for t in idx_full),
            "tail": tail}


def _att_bounds(qn2, kn2, bsz, seq, t):
    n = seq // t
    qn = jnp.sqrt(qn2).reshape(ATT_HEADS, bsz, seq).transpose(1, 2, 0)
    kn = jnp.sqrt(kn2).reshape(bsz, seq, ATT_HEADS)
    k_max = jnp.max(kn, axis=1, keepdims=True)
    neg_m = -(1.01 * qn * k_max + 1.0)
    qn_tile = jnp.max(qn.reshape(bsz, n, t, ATT_HEADS), axis=2)
    kn_tile = jnp.max(kn.reshape(bsz, n, t, ATT_HEADS), axis=2)
    spread = jnp.max(2.05 * qn_tile * k_max, axis=-1) + 8.0
    fast = (spread <= FAST_RANGE_LOG2).astype(jnp.int32)
    return neg_m.astype(BF16), qn_tile, kn_tile, fast


def _aug_operands(q_entries, k_entries, bsz, seq):
    shape = (bsz, seq, ATT_HEADS)
    one, zero = jnp.ones(shape, BF16), jnp.zeros(shape, BF16)

    def pack(entries):
        cols = [one if e is None else e for e in entries]
        return jnp.stack(cols + [zero] * (AUG - len(cols)), axis=-1)

    return pack(k_entries).reshape(bsz, seq, ATT_HEADS * AUG), pack(q_entries).transpose(0, 2, 3, 1)


def _even_mixer(xb, xt, tables, wb, wt, j, pool_w, pool_scale, bsz, seq, tq):
    n = xb.shape[0]
    hw = HALF_WIDTH
    a = _proj_pool(xb, wb, pool_w, j, pool_scale.reshape(1, hw), seq)
    q_t, qn2 = _proj_t(wt, j, xt, BF16, hw, hw, tables["q_t"], shift=HEAD_DIM // 8, norms=True)
    k, kn2 = _proj(xb, wb, j, BF16, 2 * hw, hw, tables["k"], shift=HEAD_DIM // 8, norms=True)
    v_t = _proj_t(wt, j, xt, BF16, 3 * hw, hw, value_rows=True)
    qi_t = _proj_t(wt, j, xt, BF16, 4 * hw, hw, tables["qi_t"], shift=IDX_DIM // 8)
    w_tail = jnp.pad(wb[j:j + 1, :, 5 * hw:], ((0, 0), (0, 0), (0, LANES - IDX_DIM - IDX_HEADS)))
    tail = _proj(xb, w_tail, 0, F32, 0, LANES, tables["tail"], shift=IDX_DIM // 8)

    ki = tail[:, :IDX_DIM].astype(BF16).reshape(bsz, seq, IDX_DIM)
    wi_t = tail[:, IDX_DIM:IDX_DIM + IDX_HEADS].T

    mask = _idx_mask(qi_t, wi_t, ki, tq, tq)
    neg_m, _, _, fast = _att_bounds(qn2, kn2, bsz, seq, tq)
    k_aug, q_aug = _aug_operands([neg_m], [None], bsz, seq)
    bb = _dsa(fast, k.reshape(bsz, seq, hw), k_aug, q_t, q_aug, v_t, mask, tq)
    return a, bb.reshape(n, hw)


DSA_TILE = 512
FOX_TILE = 512
UNDERFLOW_LOG2 = 160.0


def _fox_first_tile(qn, kn, terms, bsz, seq, t):
    n = seq // t
    f32sum = sum(x.astype(F32) for x in terms)[:, :, :ATT_HEADS].reshape(bsz, n, t, ATT_HEADS)
    f_first, f_last = f32sum[:, :, 0], f32sum[:, :, t - 1]
    bound = (1.01 * qn[:, :, None] * (kn[:, None, :] + kn[:, :, None])
             + f_first[:, :, None] - f_last[:, None, :] + 1.0)
    tiles = jnp.arange(n, dtype=jnp.int32)
    needed = jnp.any(bound >= -UNDERFLOW_LOG2, axis=-1) | (tiles[:, None] == tiles[None, :])
    needed = needed & (tiles[None, :] <= tiles[:, None])
    return jnp.min(jnp.where(needed, tiles[None, None, :], n), axis=-1).astype(jnp.int32)


def _odd_mixer(xb, xt, wb, wt, j, conv_w, forget_b, bsz, seq):
    n = xb.shape[0]
    hw = HALF_WIDTH
    w_f = jnp.pad(wb[j, :, 6 * hw:], ((0, 0), (0, LANES - ATT_HEADS)))
    c, f = _proj_conv(xb, wb, j, w_f, conv_w, seq)
    q_t, qn2 = _proj_t(wt, j, xt, BF16, 3 * hw, hw, scale=Q_SCALE, norms=True)
    k, kn2 = _proj(xb, wb, j, BF16, 4 * hw, hw, norms=True)
    v_t = _proj_t(wt, j, xt, BF16, 5 * hw, hw, value_rows=True)
    fb = jnp.pad(forget_b, (0, LANES - ATT_HEADS)).reshape(1, LANES)
    terms = _forget_cumsum(f.reshape(bsz, seq, LANES), fb)
    hi, mid, lo = (x[:, :, :ATT_HEADS] for x in terms)

    t = min(FOX_TILE, seq)
    neg_m, qn, kn, fast = _att_bounds(qn2, kn2, bsz, seq, t)
    k_aug, q_aug = _aug_operands([neg_m, None, None, None, hi, mid, lo],
                                 [None, -hi, -mid, -lo, None, None, None], bsz, seq)
    first = _fox_first_tile(qn, kn, terms, bsz, seq, t)
    d = _fox(first, fast, k.reshape(bsz, seq, hw), k_aug, q_t, q_aug, v_t, t)
    return c, d.reshape(n, hw)


def kernel(x, mem, positions, ev_w_in, ev_pool_w, ev_pool_scale, ev_w_out, od_w_in, od_conv_w, od_forget_b,
           od_w_out, ca_w_q, ca_w_kv, ca_w_o, ffn_w_in, ffn_w_out, ln_g, ln_b):
    bsz, seq, d = x.shape
    n = bsz * seq
    m = mem.shape[1]
    tq = min(DSA_TILE, seq)
    xf = x.reshape(n, d)
    xb = xf.astype(BF16)
    memb = mem.reshape(bsz * m, d).astype(BF16)
    tables = _even_tables(positions)
    xt = xb.T
    ev_wb, od_wb = ev_w_in.astype(BF16), od_w_in.astype(BF16)
    ev_wt, od_wt = ev_wb.transpose(0, 2, 1), od_wb.transpose(0, 2, 1)
    ev_wo, od_wo, pool_w = ev_w_out.astype(BF16), od_w_out.astype(BF16), ev_pool_w.astype(BF16)
    ca_q, ca_kv, ca_o = ca_w_q.astype(BF16), ca_w_kv.astype(BF16), ca_w_o.astype(BF16)
    ffn_in, ffn_out = ffn_w_in.astype(BF16), ffn_w_out.astype(BF16)
    for i in range(DEPTH):
        j = i // 2
        if i % 2 == 0:
            parts = _even_mixer(xb, xt, tables, ev_wb, ev_wt, j, pool_w, ev_pool_scale[j], bsz, seq, tq)
            w_out = ev_wo
        else:
            parts = _odd_mixer(xb, xt, od_wb, od_wt, j, od_conv_w[j], od_forget_b[j], bsz, seq)
            w_out = od_wo
        kv = _proj(memb, ca_kv, i, BF16, 0, 2 * XA_WIDTH).reshape(bsz, m, 2 * XA_WIDTH)
        kt = kv[:, :, :XA_WIDTH].transpose(0, 2, 1)
        xf, xb = _mix_xattn(parts[0], parts[1], w_out, j, xf, ca_q, kt, kv[:, :, XA_WIDTH:], ca_o, i,
                            ln_g[i, 0:2], ln_b[i, 0:2], seq)
        xf, xb, xt = _ffn(xb, xf, ffn_in, ffn_out, i, ln_g[i, 2:3], ln_b[i, 2:3])
    return xf.reshape(bsz, seq, d)
```

```python
import functools

import jax
import jax.numpy as jnp
from jax import lax
from jax.experimental import pallas as pl
from jax.experimental.pallas import tpu as pltpu

F32 = jnp.float32
BF16 = jnp.bfloat16

D_MODEL = 2048
DEPTH = 4
CHUNK = 64
HEAD_DIM = 128
HALF_WIDTH = D_MODEL // 2
POOL_WINDOWS = (2, 4, 8, 16)
POOL_GROUP_DIM = HALF_WIDTH // len(POOL_WINDOWS)
ATT_HEADS = HALF_WIDTH // HEAD_DIM
IDX_HEADS = 16
IDX_DIM = 64
DSA_TOPK_MAX = 256
CONV_WIDTH = 3
XA_HEADS = 4
XA_WIDTH = XA_HEADS * HEAD_DIM
FFN_HIDDEN = -(-(8 * D_MODEL) // (3 * 256)) * 256
ROPE_THETA = 500000.0
ROPE_FRACTION = 4
LN_EPS = 1e-5
ALPHA = (2 * DEPTH) ** 0.25
LOG2E = 1.4426950408889634
Q_SCALE = HEAD_DIM ** -0.5 * LOG2E

LANES = 128
SUBLANES = 8
PACKED_ROWS = 16
NEG = -1e30
INT_MIN = -(2 ** 31)
VMEM_LIMIT = 60000 * 1024


def _params(*semantics):
    return pltpu.CompilerParams(dimension_semantics=semantics, vmem_limit_bytes=VMEM_LIMIT)


def _layer_norm(y, g, b):
    mu = jnp.mean(y, axis=-1, keepdims=True)
    d = y - mu
    var = jnp.mean(d * d, axis=-1, keepdims=True)
    return d * lax.rsqrt(var + LN_EPS) * g + b


AUG = 16
V_ROWS = HEAD_DIM + AUG


def _proj_kernel(*refs, shift, scale, axis, norms, value_rows):
    refs = list(refs)
    a_ref, b_ref = refs[0], refs[1]
    n_ref = refs.pop() if norms else None
    o_ref = refs.pop()
    h = jnp.dot(a_ref[...], b_ref[...], preferred_element_type=F32)
    if not (shift or norms or value_rows):
        o_ref[...] = (h * scale if scale != 1.0 else h).astype(o_ref.dtype)
        return
    if shift:
        c, s1, s2 = refs[2][...], refs[3][...], refs[4][...]
    for g in range(h.shape[axis] // LANES):
        sl = (slice(None),) * axis + (slice(g * LANES, (g + 1) * LANES),)
        hg = h[sl]
        if shift:
            hg = hg * c + pltpu.roll(hg, shift, axis) * s1 + pltpu.roll(hg, LANES - shift, axis) * s2
        elif scale != 1.0:
            hg = hg * scale
        og = hg.astype(o_ref.dtype)
        if value_rows:
            tm = og.shape[1]
            o_ref[g * V_ROWS:g * V_ROWS + LANES, :] = og
            row = lax.broadcasted_iota(jnp.int32, (AUG, tm), 0)
            o_ref[g * V_ROWS + LANES:(g + 1) * V_ROWS, :] = jnp.where(row == 0, 1.0, 0.0).astype(o_ref.dtype)
        else:
            o_ref[sl] = og
        if norms and axis == 0:
            n_ref[g:g + 1, :] = jnp.sum(jnp.square(og.astype(F32)), axis=0, keepdims=True)
    if norms and axis == 1:
        tn = h.shape[1]
        sq = jnp.square(o_ref[...].astype(F32)).astype(BF16)
        owner = lax.broadcasted_iota(jnp.int32, (tn, LANES), 0) // LANES
        ind = jnp.where(owner == lax.broadcasted_iota(jnp.int32, (tn, LANES), 1), 1.0, 0.0).astype(BF16)
        sums = jnp.dot(sq, ind, preferred_element_type=F32)
        n_ref[...] = sums[:, 0:tn // LANES]


def _proj(x, w, layer, out_dtype, col0, ncols, tables=None, shift=0, scale=1.0, norms=False, tm=1024, tn=1024):
    n, k = x.shape
    tm, tn = min(tm, n), min(tn, ncols)
    assert col0 % tn == 0 and ncols % tn == 0
    in_specs = [pl.BlockSpec((tm, k), lambda i, j: (i, 0)),
                pl.BlockSpec((None, k, tn), lambda i, j: (layer, 0, col0 // tn + j))]
    args = [x, w]
    if shift:
        in_specs += [pl.BlockSpec((tm, LANES), lambda i, j: (i, 0))] * 3
        args += list(tables)
    out_specs = [pl.BlockSpec((tm, tn), lambda i, j: (i, j))]
    out_shape = [jax.ShapeDtypeStruct((n, ncols), out_dtype)]
    if norms:
        assert tn == ncols
        out_specs.append(pl.BlockSpec((tm, tn // LANES), lambda i, j: (i, 0)))
        out_shape.append(jax.ShapeDtypeStruct((n, ncols // LANES), F32))
    out = pl.pallas_call(
        functools.partial(_proj_kernel, shift=shift, scale=scale, axis=1, norms=norms, value_rows=False),
        grid=(n // tm, ncols // tn),
        in_specs=in_specs,
        out_specs=out_specs,
        out_shape=out_shape,
        compiler_params=_params("parallel", "arbitrary"),
        name="proj",
    )(*args)
    return out if norms else out[0]


def _proj_t(wt, layer, xt, out_dtype, row0, nrows, tables=None, shift=0, scale=1.0, norms=False,
            value_rows=False, tm=1024, tn=1024):
    k, n = xt.shape
    tm, tn = min(tm, n), min(tn, nrows)
    assert row0 % tn == 0 and nrows % tn == 0
    in_specs = [pl.BlockSpec((None, tn, k), lambda i, j: (layer, row0 // tn + j, 0)),
                pl.BlockSpec((k, tm), lambda i, j: (0, i))]
    args = [wt, xt]
    if shift:
        in_specs += [pl.BlockSpec((LANES, tm), lambda i, j: (0, i))] * 3
        args += list(tables)
    rows_out = tn // LANES * V_ROWS if value_rows else tn
    out_specs = [pl.BlockSpec((rows_out, tm), lambda i, j: (j, i))]
    out_shape = [jax.ShapeDtypeStruct((nrows // tn * rows_out, n), out_dtype)]
    if norms:
        assert tn == nrows
        out_specs.append(pl.BlockSpec((tn // LANES, tm), lambda i, j: (0, i)))
        out_shape.append(jax.ShapeDtypeStruct((nrows // LANES, n), F32))
    out = pl.pallas_call(
        functools.partial(_proj_kernel, shift=shift, scale=scale, axis=0, norms=norms, value_rows=value_rows),
        grid=(n // tm, nrows // tn),
        in_specs=in_specs,
        out_specs=out_specs,
        out_shape=out_shape,
        compiler_params=_params("parallel", "arbitrary"),
        name="proj_t",
    )(*args)
    return out if norms else out[0]


POOL_HALO = 16


def _proj_pool_kernel(x_ref, xh_ref, w_ref, pw_ref, sc_ref, o_ref, ext, *, tm, tiles_per_seq):
    ti = pl.program_id(0) % tiles_per_seq
    halo = jnp.dot(xh_ref[...], w_ref[...], preferred_element_type=F32)
    ext[0:POOL_HALO, :] = jnp.where(ti > 0, halo, 0.0)
    ext[POOL_HALO:, :] = jnp.dot(x_ref[...], w_ref[...], preferred_element_type=F32)
    cnt = ti * tm + lax.broadcasted_iota(jnp.int32, (tm, 1), 0) + 1
    for g, win in enumerate(POOL_WINDOWS):
        lo, hi = g * POOL_GROUP_DIM, (g + 1) * POOL_GROUP_DIM
        cur = ext[POOL_HALO:POOL_HALO + tm, lo:hi]
        s = cur
        for j in range(1, win):
            s = s + ext[POOL_HALO - j:POOL_HALO - j + tm, lo:hi]
        d = s / jnp.minimum(cnt, win).astype(F32) - cur
        y = jnp.dot(d.astype(BF16), pw_ref[g], preferred_element_type=F32)
        o_ref[:, lo:hi] = (y * sc_ref[:, lo:hi]).astype(o_ref.dtype)


def _proj_pool(x, w, pool_w, layer, scale, seq, tm=1024):
    n, k = x.shape
    c = HALF_WIDTH
    tm = min(tm, seq)
    hb = tm // POOL_HALO
    return pl.pallas_call(
        functools.partial(_proj_pool_kernel, tm=tm, tiles_per_seq=seq // tm),
        grid=(n // tm,),
        in_specs=[pl.BlockSpec((tm, k), lambda i: (i, 0)),
                  pl.BlockSpec((POOL_HALO, k), lambda i: (jnp.maximum(i * hb - 1, 0), 0)),
                  pl.BlockSpec((None, k, c), lambda i: (layer, 0, 0)),
                  pl.BlockSpec((None,) + pool_w.shape[1:], lambda i: (layer, 0, 0, 0)),
                  pl.BlockSpec((1, c), lambda i: (0, 0))],
        out_specs=pl.BlockSpec((tm, c), lambda i: (i, 0)),
        out_shape=jax.ShapeDtypeStruct((n, c), BF16),
        scratch_shapes=[pltpu.VMEM((tm + POOL_HALO, c), F32)],
        compiler_params=_params("parallel"),
        name="proj_pool",
    )(x, x, w, pool_w, scale)


def _sortable(v):
    bits = lax.bitcast_convert_type(v, jnp.int32)
    return bits ^ ((bits >> 31) & 0x7FFFFFFF)


def _colsum8(v):
    tk, tq = v.shape
    return v.reshape(tk // SUBLANES, SUBLANES, tq).sum(axis=0)


def _idx_kernel(qi_ref, wi_ref, ki_ref, o_ref, keys, half, *, tq, tk, nk, topk):
    i = pl.program_id(1)
    nact = (i * tq + tq - 1) // tk + 1
    q_chunk = (i * tq + lax.broadcasted_iota(jnp.int32, (1, tq), 1)) // CHUNK

    def score_tile(c, carry):
        kt = ki_ref[0, pl.ds(pl.multiple_of(c * tk, tk), tk), :]
        acc = jnp.zeros((tk, tq), F32)
        for h in range(IDX_HEADS):
            il = jnp.dot(kt, qi_ref[h * IDX_DIM:(h + 1) * IDX_DIM, :], preferred_element_type=F32)
            acc = acc + jnp.maximum(il, 0.0) * wi_ref[h:h + 1, :]
        k_chunk = (c * tk + lax.broadcasted_iota(jnp.int32, (tk, 1), 0)) // CHUNK
        kk = jnp.where(k_chunk <= q_chunk, _sortable(acc), INT_MIN)
        keys[c] = kk
        half[c] = (kk >> 16).astype(jnp.int16)
        return carry

    lax.fori_loop(0, nact, score_tile, 0)

    def count(pred):
        def body(c, acc):
            return acc + _colsum8(jnp.where(pred(keys[c], c), 1, 0))
        acc = lax.fori_loop(0, nact, body, jnp.zeros((SUBLANES, tq), jnp.int32))
        return jnp.sum(acc, axis=0, keepdims=True)

    def count16(cand, strict):
        cand16 = jnp.broadcast_to(cand, (PACKED_ROWS, tq)).astype(jnp.int16)[None]

        def body(c, acc):
            h3 = half[c].reshape(tk // PACKED_ROWS, PACKED_ROWS, tq)
            hit = (h3 > cand16) if strict else (h3 >= cand16)
            ones = jnp.where(hit, jnp.int16(1), jnp.int16(0))
            for r in range(tk // PACKED_ROWS):
                acc = acc + ones[r]
            return acc

        acc = lax.fori_loop(0, nact, body, jnp.zeros((PACKED_ROWS, tq), jnp.int16))
        return jnp.sum(acc.astype(jnp.int32), axis=0, keepdims=True)

    c0 = count16(jnp.zeros((1, tq), jnp.int32), False)
    sign_ok = c0 >= topk
    state = (jnp.where(sign_ok, 0, -(2 ** 15)), jnp.where(sign_ok, c0, nact * tk), jnp.where(sign_ok, 0, c0))

    def hi_step(b, state):
        hi, ge_hi, n_above = state
        cand = hi | (1 << (14 - b))
        c = count16(cand, False)
        ok = c >= topk
        return jnp.where(ok, cand, hi), jnp.where(ok, c, ge_hi), jnp.where(ok, n_above, c)

    hi, ge_hi, n_above = lax.fori_loop(0, 15, hi_step, state)

    def low_tile(c, carry):
        kk = keys[c]
        half[c] = jnp.where((kk >> 16) == hi, (kk & 0xFFFF) - 2 ** 15, -(2 ** 15)).astype(jnp.int16)
        return carry

    lax.fori_loop(0, nact, low_tile, 0)

    def lo_step(b, state):
        lo, in_bucket = state
        cand = lo | (1 << (15 - b))
        c = count16(cand - 2 ** 15, False)
        ok = n_above + c >= topk
        return jnp.where(ok, cand, lo), jnp.where(ok, c, in_bucket)

    lo, in_bucket = lax.fori_loop(0, 16, lo_step, (jnp.zeros((1, tq), jnp.int32), ge_hi - n_above))
    thr = (hi << 16) | lo
    n_ge = jnp.where(thr == INT_MIN, 0, n_above + in_bucket)
    thr = jnp.maximum(thr, INT_MIN + 1)

    def write_tiles(select):
        def body(c, carry):
            o_ref[0, 0, c] = jnp.where(select(keys[c], c), 1.0, 0.0).astype(o_ref.dtype)
            return carry
        lax.fori_loop(0, nact, body, 0)

    has_ties = jnp.max(n_ge) > topk

    @pl.when(jnp.logical_not(has_ties))
    def _():
        write_tiles(lambda kk, c: kk >= thr)

    @pl.when(has_ties)
    def _():
        need = topk - count(lambda kk, c: kk > thr)

        index_bits = (nk * tk - 1).bit_length()
        assert index_bits <= 15

        def key_index(c):
            return c * tk + lax.broadcasted_iota(jnp.int32, (tk, 1), 0)

        def tie_tile(c, carry):
            half[c] = jnp.where(keys[c] == thr, key_index(c), -1).astype(jnp.int16)
            return carry

        lax.fori_loop(0, nact, tie_tile, 0)
        n_tied = count16(jnp.zeros((1, tq), jnp.int32), False)

        def idx_step(b, cut):
            cand = cut | (1 << (index_bits - 1 - b))
            below = n_tied - count16(cand, False)
            return jnp.where(below < need, cand, cut)

        cut = lax.fori_loop(0, index_bits, idx_step, jnp.zeros((1, tq), jnp.int32))
        write_tiles(lambda kk, c: (kk > thr) | ((kk == thr) & (key_index(c) <= cut)))

    def fill_tile(c, carry):
        o_ref[0, 0, c] = jnp.zeros((tk, tq), o_ref.dtype)
        return carry

    lax.fori_loop(nact, nk, fill_tile, 0)


def _idx_mask(qi_t, wi_t, ki, tq, tk):
    b, s, _ = ki.shape
    nq, nk = s // tq, s // tk
    topk = min(DSA_TOPK_MAX, s // 4)
    return pl.pallas_call(
        functools.partial(_idx_kernel, tq=tq, tk=tk, nk=nk, topk=topk),
        grid=(b, nq),
        in_specs=[pl.BlockSpec((IDX_HEADS * IDX_DIM, tq), lambda bi, i: (0, bi * nq + i)),
                  pl.BlockSpec((IDX_HEADS, tq), lambda bi, i: (0, bi * nq + i)),
                  pl.BlockSpec((1, s, IDX_DIM), lambda bi, i: (bi, 0, 0))],
        out_specs=pl.BlockSpec((1, 1, nk, tk, tq), lambda bi, i: (bi, i, 0, 0, 0)),
        out_shape=jax.ShapeDtypeStruct((b, nq, nk, tk, tq), BF16),
        scratch_shapes=[pltpu.VMEM((nk, tk, tq), jnp.int32), pltpu.VMEM((nk, tk, tq), jnp.int16)],
        compiler_params=_params("parallel", "arbitrary"),
        name="idx_mask",
    )(qi_t, wi_t, ki)


FAST_RANGE_LOG2 = 120.0


def _head(h):
    return slice(h * HEAD_DIM, (h + 1) * HEAD_DIM)


def _vrows(h):
    return slice(h * V_ROWS, (h + 1) * V_ROWS)


def _att_init(qt_ref, qa_ref, m_sc, acc_sc, qfull):
    m_sc[...] = jnp.full(m_sc.shape, NEG, F32)
    acc_sc[...] = jnp.zeros(acc_sc.shape, F32)
    tq = qfull.shape[2]
    for h in range(ATT_HEADS):
        qfull[h, 0:HEAD_DIM, :] = qt_ref[_head(h), :]
        qfull[h, HEAD_DIM:, :] = jnp.zeros((LANES, tq), BF16)
        lo = HEAD_DIM + h * AUG
        qfull[h, lo:lo + AUG, :] = qa_ref[0, h]


def _scores(k_ref, ka, qfull, h):
    kfull = jnp.concatenate([k_ref[0, :, _head(h)], ka], axis=1)
    return jnp.dot(kfull, qfull[h], preferred_element_type=F32)


def _for_heads(score_fn, step_fn):
    s_next = score_fn(0)
    for h in range(ATT_HEADS):
        s = s_next
        if h + 1 < ATT_HEADS:
            s_next = score_fn(h + 1)
        step_fn(s, h)


def _fast_step(s, keep, vt, h, acc_sc):
    p = jnp.exp2(s).astype(BF16)
    if keep is not None:
        p = p * keep
    acc_sc[h] += jnp.dot(vt, p, preferred_element_type=F32)


def _exact_step(s, vt, h, m_sc, acc_sc):
    m_prev = m_sc[h]
    m_new = jnp.maximum(m_prev, jnp.max(s, axis=0, keepdims=True))
    alpha = jnp.exp2(m_prev - m_new)
    p = jnp.exp2(s - m_new)
    acc_sc[h] = alpha * acc_sc[h] + jnp.dot(vt, p.astype(BF16), preferred_element_type=F32)
    m_sc[h] = m_new


def _att_finish(o_ref, acc_sc):
    for h in range(ATT_HEADS):
        a = acc_sc[h]
        o = (a[0:HEAD_DIM] / a[HEAD_DIM:HEAD_DIM + 1]).T
        o_ref[0, :, _head(h)] = o.astype(o_ref.dtype)


def _att_scratch(tq):
    return [pltpu.VMEM((ATT_HEADS, 1, tq), F32),
            pltpu.VMEM((ATT_HEADS, V_ROWS, tq), F32),
            pltpu.VMEM((ATT_HEADS, 2 * HEAD_DIM, tq), BF16)]


def _pair_tiles(r, j, n):
    second = j > r
    return jnp.where(second, n - 1 - r, r), jnp.where(second, j - r - 1, j)


def _dsa_kernel(fast_ref, k_ref, ka_ref, qt_ref, qa_ref, vt_ref, mask_ref, o_ref, m_sc, acc_sc, qfull, *, n):
    b = pl.program_id(0)
    i, j = _pair_tiles(pl.program_id(1), pl.program_id(2), n)
    fast = fast_ref[b, i] == 1

    @pl.when(j == 0)
    def _():
        _att_init(qt_ref, qa_ref, m_sc, acc_sc, qfull)

    @pl.when(fast)
    def _():
        keep = mask_ref[0, 0, 0]
        ka = ka_ref[0]
        _for_heads(lambda h: _scores(k_ref, ka, qfull, h),
                   lambda s, h: _fast_step(s, keep, vt_ref[_vrows(h), :], h, acc_sc))

    @pl.when(jnp.logical_not(fast))
    def _():
        bias = (mask_ref[0, 0, 0].astype(F32) - 1.0) * (-NEG)
        ka = ka_ref[0]
        _for_heads(lambda h: _scores(k_ref, ka, qfull, h) + bias,
                   lambda s, h: _exact_step(s, vt_ref[_vrows(h), :], h, m_sc, acc_sc))

    @pl.when(j == i)
    def _():
        _att_finish(o_ref, acc_sc)


def _dsa(fast, k, k_aug, q_t, q_aug, v_t, mask, t):
    b, s, c = k.shape
    n = s // t
    assert n % 2 == 0, "query tiles are processed in pairs"

    def q_tile(r, j):
        return _pair_tiles(r, j, n)[0]

    def k_tile(r, j):
        return _pair_tiles(r, j, n)[1]

    grid_spec = pltpu.PrefetchScalarGridSpec(
        num_scalar_prefetch=1,
        grid=(b, n // 2, n + 1),
        in_specs=[pl.BlockSpec((1, t, c), lambda bi, r, j, f: (bi, k_tile(r, j), 0)),
                  pl.BlockSpec((1, t, LANES), lambda bi, r, j, f: (bi, k_tile(r, j), 0)),
                  pl.BlockSpec((c, t), lambda bi, r, j, f: (0, bi * n + q_tile(r, j))),
                  pl.BlockSpec((1, ATT_HEADS, AUG, t), lambda bi, r, j, f: (bi, 0, 0, q_tile(r, j))),
                  pl.BlockSpec((ATT_HEADS * V_ROWS, t), lambda bi, r, j, f: (0, bi * n + k_tile(r, j))),
                  pl.BlockSpec((1, 1, 1, t, t), lambda bi, r, j, f: (bi, q_tile(r, j), k_tile(r, j), 0, 0))],
        out_specs=pl.BlockSpec((1, t, c), lambda bi, r, j, f: (bi, q_tile(r, j), 0)),
        scratch_shapes=_att_scratch(t))
    return pl.pallas_call(
        functools.partial(_dsa_kernel, n=n),
        grid_spec=grid_spec,
        out_shape=jax.ShapeDtypeStruct((b, s, c), BF16),
        compiler_params=_params("parallel", "parallel", "arbitrary"),
        name="dsa",
    )(fast, k, k_aug, q_t, q_aug, v_t, mask)


CONV_HALO = 8


def _proj_conv_kernel(x_ref, xh_ref, wu_ref, wb_ref, wc_ref, wf_ref, cw_ref, o_ref, f_ref, ext, *, tm,
                      tiles_per_seq):
    ti = pl.program_id(0) % tiles_per_seq
    x, xh = x_ref[...], xh_ref[...]
    halo = (jnp.dot(xh, wc_ref[...], preferred_element_type=F32)
            * jnp.dot(xh, wu_ref[...], preferred_element_type=F32))
    ext[0:CONV_HALO, :] = jnp.where(ti > 0, halo, 0.0)
    ext[CONV_HALO:, :] = (jnp.dot(x, wc_ref[...], preferred_element_type=F32)
                          * jnp.dot(x, wu_ref[...], preferred_element_type=F32))
    conv = None
    for t in range(CONV_WIDTH):
        off = CONV_HALO - (CONV_WIDTH - 1) + t
        term = ext[off:off + tm, :] * cw_ref[t:t + 1, :]
        conv = term if conv is None else conv + term
    gate_b = jnp.dot(x, wb_ref[...], preferred_element_type=F32)
    o_ref[...] = (gate_b * conv).astype(o_ref.dtype)
    f_ref[...] = jnp.dot(x, wf_ref[...], preferred_element_type=F32)


def _proj_conv(x, w, layer, w_f, conv_w, seq, tm=512):
    n, k = x.shape
    c = HALF_WIDTH
    tm = min(tm, seq)
    hb = tm // CONV_HALO

    def wcol(j):
        return pl.BlockSpec((None, k, c), lambda i: (layer, 0, j))

    return pl.pallas_call(
        functools.partial(_proj_conv_kernel, tm=tm, tiles_per_seq=seq // tm),
        grid=(n // tm,),
        in_specs=[pl.BlockSpec((tm, k), lambda i: (i, 0)),
                  pl.BlockSpec((CONV_HALO, k), lambda i: (jnp.maximum(i * hb - 1, 0), 0)),
                  wcol(0), wcol(1), wcol(2),
                  pl.BlockSpec(w_f.shape, lambda i: (0, 0)),
                  pl.BlockSpec(conv_w.shape, lambda i: (0, 0))],
        out_specs=[pl.BlockSpec((tm, c), lambda i: (i, 0)), pl.BlockSpec((tm, LANES), lambda i: (i, 0))],
        out_shape=[jax.ShapeDtypeStruct((n, c), BF16), jax.ShapeDtypeStruct((n, LANES), F32)],
        scratch_shapes=[pltpu.VMEM((tm + CONV_HALO, c), F32)],
        compiler_params=_params("parallel"),
        name="proj_conv",
    )(x, x, w, w, w, w_f, conv_w)


GATE_CHUNK = 256


def _fgate_kernel(f_ref, fb_ref, hi_ref, mid_ref, lo_ref):
    s = f_ref.shape[1]
    r = lax.broadcasted_iota(jnp.int32, (GATE_CHUNK, GATE_CHUNK), 0)
    c = lax.broadcasted_iota(jnp.int32, (GATE_CHUNK, GATE_CHUNK), 1)
    tri = (c <= r).astype(F32)

    def body(t, carry):
        rows = pl.ds(t * GATE_CHUNK, GATE_CHUNK)
        z = f_ref[0, rows, :] + fb_ref[...]
        log_f = -(jnp.maximum(-z, 0.0) + jnp.log1p(jnp.exp(-jnp.abs(z))))
        cs = jnp.dot(tri, log_f, preferred_element_type=F32, precision=lax.Precision.HIGHEST) + carry
        b2 = cs * LOG2E
        hi = b2.astype(BF16)
        r1 = b2 - hi.astype(F32)
        mid = r1.astype(BF16)
        hi_ref[0, rows, :] = hi
        mid_ref[0, rows, :] = mid
        lo_ref[0, rows, :] = (r1 - mid.astype(F32)).astype(BF16)
        return cs[GATE_CHUNK - 1:GATE_CHUNK, :]

    lax.fori_loop(0, s // GATE_CHUNK, body, jnp.zeros((1, LANES), F32))


def _forget_cumsum(f, fb):
    b, s, c = f.shape
    spec = pl.BlockSpec((1, s, c), lambda bi: (bi, 0, 0))
    return pl.pallas_call(
        _fgate_kernel,
        grid=(b,),
        in_specs=[spec, pl.BlockSpec((1, c), lambda bi: (0, 0))],
        out_specs=[spec, spec, spec],
        out_shape=[jax.ShapeDtypeStruct((b, s, c), BF16)] * 3,
        compiler_params=_params("parallel"),
        name="forget_cumsum",
    )(f, fb)


def _fox_schedule(first, fast, n):
    bsz = first.shape[0]
    steps = bsz * n * (n + 1) // 2
    tiles = jnp.arange(n, dtype=jnp.int32)
    cnt = (tiles[None, :] - first + 1).reshape(-1)
    end = jnp.cumsum(cnt)
    t = jnp.arange(steps, dtype=jnp.int32)
    tc = jnp.minimum(t, end[-1] - 1)
    row = jnp.sum((end[None, :] <= tc[:, None]).astype(jnp.int32), axis=1)
    rel = tc - (end - cnt)[row]
    q_tile = row % n
    active = t < end[-1]
    flags = active.astype(jnp.int32) + 2 * (active & (rel == cnt[row] - 1)).astype(jnp.int32)
    return jnp.stack([row // n, q_tile, q_tile - rel, flags, fast.reshape(-1)[row]]).astype(jnp.int32)


def _fox_kernel(sched_ref, k_ref, ka_ref, qt_ref, qa_ref, vt_ref, o_ref, m_sc, acc_sc, qfull, *, t):
    step = pl.program_id(0)
    i, kj, flags = sched_ref[1, step], sched_ref[2, step], sched_ref[3, step]
    fast = sched_ref[4, step] == 1
    active = (flags & 1) == 1
    diagonal = active & (kj == i)
    older = active & (kj != i)

    def causal():
        return lax.broadcasted_iota(jnp.int32, (t, t), 0) <= lax.broadcasted_iota(jnp.int32, (t, t), 1)

    def scores(masked):
        ka = ka_ref[0]

        def fn(h):
            s = _scores(k_ref, ka, qfull, h)
            return jnp.where(causal(), s, NEG) if masked else s
        return fn

    def fast_tile(masked):
        _for_heads(scores(masked), lambda s, h: _fast_step(s, None, vt_ref[_vrows(h), :], h, acc_sc))

    def exact_tile(masked):
        _for_heads(scores(masked), lambda s, h: _exact_step(s, vt_ref[_vrows(h), :], h, m_sc, acc_sc))

    @pl.when(diagonal)
    def _():
        _att_init(qt_ref, qa_ref, m_sc, acc_sc, qfull)

    @pl.when(diagonal & fast)
    def _():
        fast_tile(True)

    @pl.when(diagonal & jnp.logical_not(fast))
    def _():
        exact_tile(True)

    @pl.when(older & fast)
    def _():
        fast_tile(False)

    @pl.when(older & jnp.logical_not(fast))
    def _():
        exact_tile(False)

    @pl.when((flags & 2) == 2)
    def _():
        _att_finish(o_ref, acc_sc)


def _fox(first, fast, k, k_aug, q_t, q_aug, v_t, t):
    b, s, c = k.shape
    n = s // t
    sched = _fox_schedule(first, fast, n)
    grid_spec = pltpu.PrefetchScalarGridSpec(
        num_scalar_prefetch=1,
        grid=(sched.shape[1],),
        in_specs=[pl.BlockSpec((1, t, c), lambda st, sc: (sc[0, st], sc[2, st], 0)),
                  pl.BlockSpec((1, t, LANES), lambda st, sc: (sc[0, st], sc[2, st], 0)),
                  pl.BlockSpec((c, t), lambda st, sc: (0, sc[0, st] * n + sc[1, st])),
                  pl.BlockSpec((1, ATT_HEADS, AUG, t), lambda st, sc: (sc[0, st], 0, 0, sc[1, st])),
                  pl.BlockSpec((ATT_HEADS * V_ROWS, t), lambda st, sc: (0, sc[0, st] * n + sc[2, st]))],
        out_specs=pl.BlockSpec((1, t, c), lambda st, sc: (sc[0, st], sc[1, st], 0)),
        scratch_shapes=_att_scratch(t))
    return pl.pallas_call(
        functools.partial(_fox_kernel, t=t),
        grid_spec=grid_spec,
        out_shape=jax.ShapeDtypeStruct((b, s, c), BF16),
        compiler_params=_params("arbitrary"),
        name="fox",
    )(sched, k, k_aug, q_t, q_aug, v_t)


def _mix_xattn_kernel(pa_ref, pb_ref, wm_ref, xf_ref, wq_ref, kt_ref, v_ref, wo_ref, g_ref, b_ref, of_ref, ob_ref):
    half = pa_ref.shape[1]
    mixed = (jnp.dot(pa_ref[...], wm_ref[0:half, :], preferred_element_type=F32)
             + jnp.dot(pb_ref[...], wm_ref[half:, :], preferred_element_type=F32))
    x1 = _layer_norm(ALPHA * xf_ref[...] + mixed, g_ref[0:1, :], b_ref[0:1, :])
    q = jnp.dot(x1.astype(BF16), wq_ref[...], preferred_element_type=F32) * Q_SCALE
    q = q.astype(BF16)
    outs = []
    for h in range(XA_HEADS):
        s = jnp.dot(q[:, _head(h)], kt_ref[0, _head(h), :], preferred_element_type=F32)
        p = jnp.exp2(s - jnp.max(s, axis=1, keepdims=True))
        l = jnp.sum(p, axis=1, keepdims=True)
        pv = jnp.dot(p.astype(BF16), v_ref[0, :, _head(h)], preferred_element_type=F32)
        outs.append((pv / l).astype(BF16))
    o = jnp.concatenate(outs, axis=1)
    y = jnp.dot(o, wo_ref[...], preferred_element_type=F32)
    x2 = _layer_norm(ALPHA * x1 + y, g_ref[1:2, :], b_ref[1:2, :])
    of_ref[...] = x2
    ob_ref[...] = x2.astype(BF16)


def _mix_xattn(pa, pb, w_mix, mix_layer, xf, wq, kt, v, wo, layer, g, b, seq, tm=512):
    n, d = xf.shape
    per_batch = seq // tm
    m = v.shape[1]
    row = lambda c: pl.BlockSpec((tm, c), lambda i: (i, 0))
    return pl.pallas_call(
        _mix_xattn_kernel,
        grid=(n // tm,),
        in_specs=[row(pa.shape[1]), row(pb.shape[1]),
                  pl.BlockSpec((None,) + w_mix.shape[1:], lambda i: (mix_layer, 0, 0)),
                  row(d),
                  pl.BlockSpec((None,) + wq.shape[1:], lambda i: (layer, 0, 0)),
                  pl.BlockSpec((1, XA_WIDTH, m), lambda i: (i // per_batch, 0, 0)),
                  pl.BlockSpec((1, m, XA_WIDTH), lambda i: (i // per_batch, 0, 0)),
                  pl.BlockSpec((None,) + wo.shape[1:], lambda i: (layer, 0, 0)),
                  pl.BlockSpec((2, d), lambda i: (0, 0)),
                  pl.BlockSpec((2, d), lambda i: (0, 0))],
        out_specs=[row(d), row(d)],
        out_shape=[jax.ShapeDtypeStruct((n, d), F32), jax.ShapeDtypeStruct((n, d), BF16)],
        compiler_params=_params("parallel"),
        name="mix_xattn",
    )(pa, pb, w_mix, xf, wq, kt, v, wo, g, b)


def _ffn_kernel(xb_ref, wg_ref, wu_ref, wo_ref, xf_ref, g_ref, b_ref, of_ref, ob_ref, ot_ref, acc):
    j = pl.program_id(1)

    @pl.when(j == 0)
    def _():
        acc[...] = jnp.zeros(acc.shape, F32)

    xb = xb_ref[...]
    gate = jnp.dot(xb, wg_ref[...], preferred_element_type=F32)
    up = jnp.dot(xb, wu_ref[...], preferred_element_type=F32)
    hid = gate * (1.0 / (1.0 + jnp.exp(-gate))) * up
    acc[...] += jnp.dot(hid.astype(BF16), wo_ref[...], preferred_element_type=F32)

    @pl.when(j == pl.num_programs(1) - 1)
    def _():
        y = _layer_norm(ALPHA * xf_ref[...] + acc[...], g_ref[...], b_ref[...])
        of_ref[...] = y
        ob_ref[...] = y.astype(BF16)
        ot_ref[...] = y.T.astype(BF16)


def _ffn(xb, xf, w_in, w_out, layer, g, b, tm=512, th=512):
    n, d = xf.shape
    hidden = w_out.shape[1]
    nh = hidden // th
    return pl.pallas_call(
        _ffn_kernel,
        grid=(n // tm, nh),
        in_specs=[pl.BlockSpec((tm, d), lambda i, j: (i, 0)),
                  pl.BlockSpec((None, d, th), lambda i, j: (layer, 0, j)),
                  pl.BlockSpec((None, d, th), lambda i, j: (layer, 0, j + nh)),
                  pl.BlockSpec((None, th, d), lambda i, j: (layer, j, 0)),
                  pl.BlockSpec((tm, d), lambda i, j: (i, 0)),
                  pl.BlockSpec((1, d), lambda i, j: (0, 0)),
                  pl.BlockSpec((1, d), lambda i, j: (0, 0))],
        out_specs=[pl.BlockSpec((tm, d), lambda i, j: (i, 0)),
                   pl.BlockSpec((tm, d), lambda i, j: (i, 0)),
                   pl.BlockSpec((d, tm), lambda i, j: (0, i))],
        out_shape=[jax.ShapeDtypeStruct((n, d), F32), jax.ShapeDtypeStruct((n, d), BF16),
                   jax.ShapeDtypeStruct((d, n), BF16)],
        scratch_shapes=[pltpu.VMEM((tm, d), F32)],
        compiler_params=_params("parallel", "arbitrary"),
        name="ffn",
    )(xb, w_in, w_in, w_out, xf, g, b)


def _cast_kernel(x_ref, ob_ref, ot_ref):
    x = x_ref[...]
    ob_ref[...] = x.astype(BF16)
    ot_ref[...] = x.T.astype(BF16)


def _cast_both(x, tm=512):
    n, d = x.shape
    return pl.pallas_call(
        _cast_kernel,
        grid=(n // tm,),
        in_specs=[pl.BlockSpec((tm, d), lambda i: (i, 0))],
        out_specs=[pl.BlockSpec((tm, d), lambda i: (i, 0)), pl.BlockSpec((d, tm), lambda i: (0, i))],
        out_shape=[jax.ShapeDtypeStruct((n, d), BF16), jax.ShapeDtypeStruct((d, n), BF16)],
        compiler_params=_params("parallel"),
        name="cast_both",
    )(x)


def _rotary_tables(positions, dh):
    rot = dh // ROPE_FRACTION
    half = rot // 2
    inv_freq = jnp.power(ROPE_THETA, -(jnp.arange(half, dtype=F32) * 2.0 / rot))
    ang = positions.astype(F32)[..., None] * inv_freq
    cos, sin = jnp.cos(ang), jnp.sin(ang)
    zh = jnp.zeros_like(sin)
    rest = jnp.zeros(ang.shape[:-1] + (dh - rot,), F32)
    c = jnp.concatenate([cos, cos, rest + 1.0], axis=-1)
    s_lo = jnp.concatenate([zh, sin, rest], axis=-1)
    s_hi = jnp.concatenate([-sin, zh, rest], axis=-1)
    return tuple(t.reshape(-1, dh) for t in (c, s_lo, s_hi))


def _even_tables(positions):
    n = positions.size
    head = _rotary_tables(positions, HEAD_DIM)
    idx = _rotary_tables(positions, IDX_DIM)
    idx_full = tuple(jnp.tile(t, (1, LANES // IDX_DIM)) for t in idx)
    wi_scale = jnp.concatenate([jnp.full((IDX_HEADS,), IDX_HEADS ** -0.5, F32),
                                jnp.ones((LANES - IDX_DIM - IDX_HEADS,), F32)])
    pad = jnp.zeros((n, LANES - IDX_DIM), F32)
    tail = (jnp.concatenate([idx[0], pad + wi_scale], axis=1),
            jnp.concatenate([idx[1], pad], axis=1),
            jnp.concatenate([idx[2], pad], axis=1))
    return {"k": head,
            "q_t": tuple((t * Q_SCALE).T for t in head),
            "qi_t": tuple((t * IDX_DIM ** -0.5).T for t in idx_full),
            "tail": tail}


def _att_bounds(qn2, kn2, bsz, seq, t):
    n = seq // t
    qn = jnp.sqrt(qn2).reshape(ATT_HEADS, bsz, seq).transpose(1, 2, 0)
    kn = jnp.sqrt(kn2).reshape(bsz, seq, ATT_HEADS)
    k_max = jnp.max(kn, axis=1, keepdims=True)
    neg_m = -(1.01 * qn * k_max + 1.0)
    qn_tile = jnp.max(qn.reshape(bsz, n, t, ATT_HEADS), axis=2)
    kn_tile = jnp.max(kn.reshape(bsz, n, t, ATT_HEADS), axis=2)
    spread = jnp.max(2.05 * qn_tile * k_max, axis=-1) + 8.0
    fast = (spread <= FAST_RANGE_LOG2).astype(jnp.int32)
    return neg_m.astype(BF16), qn_tile, kn_tile, fast


def _aug_operands(q_entries, k_entries, bsz, seq):
    shape = (bsz, seq, ATT_HEADS)
    one, zero = jnp.ones(shape, BF16), jnp.zeros(shape, BF16)

    def pack(entries):
        cols = [one if e is None else e for e in entries]
        return jnp.stack(cols + [zero] * (AUG - len(cols)), axis=-1)

    return pack(k_entries).reshape(bsz, seq, ATT_HEADS * AUG), pack(q_entries).transpose(0, 2, 3, 1)


def _even_mixer(xb, xt, tables, wb, wt, j, pool_w, pool_scale, bsz, seq, tq):
    n = xb.shape[0]
    hw = HALF_WIDTH
    a = _proj_pool(xb, wb, pool_w, j, pool_scale.reshape(1, hw), seq)
    q_t, qn2 = _proj_t(wt, j, xt, BF16, hw, hw, tables["q_t"], shift=HEAD_DIM // 8, norms=True)
    k, kn2 = _proj(xb, wb, j, BF16, 2 * hw, hw, tables["k"], shift=HEAD_DIM // 8, norms=True)
    v_t = _proj_t(wt, j, xt, BF16, 3 * hw, hw, value_rows=True)
    qi_t = _proj_t(wt, j, xt, BF16, 4 * hw, hw, tables["qi_t"], shift=IDX_DIM // 8)
    w_tail = jnp.pad(wb[j:j + 1, :, 5 * hw:], ((0, 0), (0, 0), (0, LANES - IDX_DIM - IDX_HEADS)))
    tail = _proj(xb, w_tail, 0, F32, 0, LANES, tables["tail"], shift=IDX_DIM // 8)

    ki = tail[:, :IDX_DIM].astype(BF16).reshape(bsz, seq, IDX_DIM)
    wi_t = tail[:, IDX_DIM:IDX_DIM + IDX_HEADS].T

    mask = _idx_mask(qi_t, wi_t, ki, tq, tq)
    neg_m, _, _, fast = _att_bounds(qn2, kn2, bsz, seq, tq)
    k_aug, q_aug = _aug_operands([neg_m], [None], bsz, seq)
    bb = _dsa(fast, k.reshape(bsz, seq, hw), k_aug, q_t, q_aug, v_t, mask, tq)
    return a, bb.reshape(n, hw)


DSA_TILE = 512
FOX_TILE = 512
UNDERFLOW_LOG2 = 160.0


def _fox_first_tile(qn, kn, terms, bsz, seq, t):
    n = seq // t
    f32sum = sum(x.astype(F32) for x in terms)[:, :, :ATT_HEADS].reshape(bsz, n, t, ATT_HEADS)
    f_first, f_last = f32sum[:, :, 0], f32sum[:, :, t - 1]
    bound = (1.01 * qn[:, :, None] * (kn[:, None, :] + kn[:, :, None])
             + f_first[:, :, None] - f_last[:, None, :] + 1.0)
    tiles = jnp.arange(n, dtype=jnp.int32)
    needed = jnp.any(bound >= -UNDERFLOW_LOG2, axis=-1) | (tiles[:, None] == tiles[None, :])
    needed = needed & (tiles[None, :] <= tiles[:, None])
    return jnp.min(jnp.where(needed, tiles[None, None, :], n), axis=-1).astype(jnp.int32)


def _odd_mixer(xb, xt, wb, wt, j, conv_w, forget_b, bsz, seq):
    n = xb.shape[0]
    hw = HALF_WIDTH
    w_f = jnp.pad(wb[j, :, 6 * hw:], ((0, 0), (0, LANES - ATT_HEADS)))
    c, f = _proj_conv(xb, wb, j, w_f, conv_w, seq)
    q_t, qn2 = _proj_t(wt, j, xt, BF16, 3 * hw, hw, scale=Q_SCALE, norms=True)
    k, kn2 = _proj(xb, wb, j, BF16, 4 * hw, hw, norms=True)
    v_t = _proj_t(wt, j, xt, BF16, 5 * hw, hw, value_rows=True)
    fb = jnp.pad(forget_b, (0, LANES - ATT_HEADS)).reshape(1, LANES)
    terms = _forget_cumsum(f.reshape(bsz, seq, LANES), fb)
    hi, mid, lo = (x[:, :, :ATT_HEADS] for x in terms)

    t = min(FOX_TILE, seq)
    neg_m, qn, kn, fast = _att_bounds(qn2, kn2, bsz, seq, t)
    k_aug, q_aug = _aug_operands([neg_m, None, None, None, hi, mid, lo],
                                 [None, -hi, -mid, -lo, None, None, None], bsz, seq)
    first = _fox_first_tile(qn, kn, terms, bsz, seq, t)
    d = _fox(first, fast, k.reshape(bsz, seq, hw), k_aug, q_t, q_aug, v_t, t)
    return c, d.reshape(n, hw)


def kernel(x, mem, positions, ev_w_in, ev_pool_w, ev_pool_scale, ev_w_out, od_w_in, od_conv_w, od_forget_b,
           od_w_out, ca_w_q, ca_w_kv, ca_w_o, ffn_w_in, ffn_w_out, ln_g, ln_b):
    bsz, seq, d = x.shape
    n = bsz * seq
    m = mem.shape[1]
    tq = min(DSA_TILE, seq)
    xf = x.reshape(n, d)
    xb, xt = _cast_both(xf)
    memb = mem.reshape(bsz * m, d).astype(BF16)
    tables = _even_tables(positions)
    ev_wb, od_wb = ev_w_in.astype(BF16), od_w_in.astype(BF16)
    ev_wt, od_wt = ev_wb.transpose(0, 2, 1), od_wb.transpose(0, 2, 1)
    ev_wo, od_wo, pool_w = ev_w_out.astype(BF16), od_w_out.astype(BF16), ev_pool_w.astype(BF16)
    ca_q, ca_kv, ca_o = ca_w_q.astype(BF16), ca_w_kv.astype(BF16), ca_w_o.astype(BF16)
    ffn_in, ffn_out = ffn_w_in.astype(BF16), ffn_w_out.astype(BF16)
    for i in range(DEPTH):
        j = i // 2
        if i % 2 == 0:
            parts = _even_mixer(xb, xt, tables, ev_wb, ev_wt, j, pool_w, ev_pool_scale[j], bsz, seq, tq)
            w_out = ev_wo
        else:
            parts = _odd_mixer(xb, xt, od_wb, od_wt, j, od_conv_w[j], od_forget_b[j], bsz, seq)
            w_out = od_wo
        kv = _proj(memb, ca_kv, i, BF16, 0, 2 * XA_WIDTH).reshape(bsz, m, 2 * XA_WIDTH)
        kt = kv[:, :, :XA_WIDTH].transpose(0, 2, 1)
        xf, xb = _mix_xattn(parts[0], parts[1], w_out, j, xf, ca_q, kt, kv[:, :, XA_WIDTH:], ca_o, i,
                            ln_g[i, 0:2], ln_b[i, 0:2], seq)
        xf, xb, xt = _ffn(xb, xf, ffn_in, ffn_out, i, ln_g[i, 2:3], ln_b[i, 2:3])
    return xf.reshape(bsz, seq, d)
```

```python
import functools

import jax
import jax.numpy as jnp
from jax import lax
from jax.experimental import pallas as pl
from jax.experimental.pallas import tpu as pltpu

F32 = jnp.float32
BF16 = jnp.bfloat16

D_MODEL = 2048
DEPTH = 4
CHUNK = 64
HEAD_DIM = 128
HALF_WIDTH = D_MODEL // 2
POOL_WINDOWS = (2, 4, 8, 16)
POOL_GROUP_DIM = HALF_WIDTH // len(POOL_WINDOWS)
ATT_HEADS = HALF_WIDTH // HEAD_DIM
IDX_HEADS = 16
IDX_DIM = 64
DSA_TOPK_MAX = 256
CONV_WIDTH = 3
XA_HEADS = 4
XA_WIDTH = XA_HEADS * HEAD_DIM
ROPE_THETA = 500000.0
ROPE_FRACTION = 4
LN_EPS = 1e-5
ALPHA = (2 * DEPTH) ** 0.25
LOG2E = 1.4426950408889634
Q_SCALE = HEAD_DIM ** -0.5 * LOG2E

LANES = 128
SUBLANES = 8
PACKED_ROWS = 16
NEG = -1e30
INT_MIN = -(2 ** 31)
HALF_BIAS = 2 ** 15
VMEM_LIMIT = 60000 * 1024


def _params(*semantics):
    return pltpu.CompilerParams(dimension_semantics=semantics, vmem_limit_bytes=VMEM_LIMIT)


def _layer_norm(y, g, b):
    mu = jnp.mean(y, axis=-1, keepdims=True)
    d = y - mu
    var = jnp.mean(d * d, axis=-1, keepdims=True)
    return d * lax.rsqrt(var + LN_EPS) * g + b


AUG = 16
V_ROWS = HEAD_DIM + AUG


def _write_groups(h, axis, o_ref, n_ref, tabs, shift, scale, norms, value_rows):
    if not (shift or norms or value_rows):
        o_ref[...] = (h * scale if scale != 1.0 else h).astype(o_ref.dtype)
        return
    if shift:
        c, s1, s2 = tabs
    for g in range(h.shape[axis] // LANES):
        sl = (slice(None),) * axis + (slice(g * LANES, (g + 1) * LANES),)
        hg = h[sl]
        if shift:
            hg = hg * c + pltpu.roll(hg, shift, axis) * s1 + pltpu.roll(hg, LANES - shift, axis) * s2
        elif scale != 1.0:
            hg = hg * scale
        og = hg.astype(o_ref.dtype)
        if value_rows:
            tm = og.shape[1]
            o_ref[g * V_ROWS:g * V_ROWS + LANES, :] = og
            row = lax.broadcasted_iota(jnp.int32, (AUG, tm), 0)
            o_ref[g * V_ROWS + LANES:(g + 1) * V_ROWS, :] = jnp.where(row == 0, 1.0, 0.0).astype(o_ref.dtype)
        else:
            o_ref[sl] = og
        if norms and axis == 0:
            n_ref[g:g + 1, :] = jnp.sum(jnp.square(og.astype(F32)), axis=0, keepdims=True)
    if norms and axis == 1:
        tn = h.shape[1]
        sq = jnp.square(o_ref[...].astype(F32)).astype(BF16)
        owner = lax.broadcasted_iota(jnp.int32, (tn, LANES), 0) // LANES
        ind = jnp.where(owner == lax.broadcasted_iota(jnp.int32, (tn, LANES), 1), 1.0, 0.0).astype(BF16)
        sums = jnp.dot(sq, ind, preferred_element_type=F32)
        n_ref[...] = sums[:, 0:tn // LANES]


def _proj_kernel(*refs, shift, scale, axis, norms, value_rows):
    refs = list(refs)
    a_ref, b_ref = refs[0], refs[1]
    n_ref = refs.pop() if norms else None
    o_ref = refs.pop()
    h = jnp.dot(a_ref[...], b_ref[...], preferred_element_type=F32)
    tabs = tuple(r[...] for r in refs[2:5]) if shift else None
    _write_groups(h, axis, o_ref, n_ref, tabs, shift, scale, norms, value_rows)


def _proj(x, w, layer, out_dtype, col0, ncols, tables=None, shift=0, scale=1.0, norms=False, tm=1024, tn=1024):
    n, k = x.shape
    tm, tn = min(tm, n), min(tn, ncols)
    assert col0 % tn == 0 and ncols % tn == 0
    in_specs = [pl.BlockSpec((tm, k), lambda i, j: (i, 0)),
                pl.BlockSpec((None, k, tn), lambda i, j: (layer, 0, col0 // tn + j))]
    args = [x, w]
    if shift:
        in_specs += [pl.BlockSpec((tm, LANES), lambda i, j: (i, 0))] * 3
        args += list(tables)
    out_specs = [pl.BlockSpec((tm, tn), lambda i, j: (i, j))]
    out_shape = [jax.ShapeDtypeStruct((n, ncols), out_dtype)]
    if norms:
        assert tn == ncols
        out_specs.append(pl.BlockSpec((tm, tn // LANES), lambda i, j: (i, 0)))
        out_shape.append(jax.ShapeDtypeStruct((n, ncols // LANES), F32))
    out = pl.pallas_call(
        functools.partial(_proj_kernel, shift=shift, scale=scale, axis=1, norms=norms, value_rows=False),
        grid=(n // tm, ncols // tn),
        in_specs=in_specs,
        out_specs=out_specs,
        out_shape=out_shape,
        compiler_params=_params("parallel", "arbitrary"),
        name="proj",
    )(*args)
    return out if norms else out[0]


def _proj_t(wt, layer, xt, out_dtype, row0, nrows, tables=None, shift=0, scale=1.0, norms=False,
            value_rows=False, tm=1024, tn=1024):
    k, n = xt.shape
    tm, tn = min(tm, n), min(tn, nrows)
    assert row0 % tn == 0 and nrows % tn == 0
    in_specs = [pl.BlockSpec((None, tn, k), lambda i, j: (layer, row0 // tn + j, 0)),
                pl.BlockSpec((k, tm), lambda i, j: (0, i))]
    args = [wt, xt]
    if shift:
        in_specs += [pl.BlockSpec((LANES, tm), lambda i, j: (0, i))] * 3
        args += list(tables)
    rows_out = tn // LANES * V_ROWS if value_rows else tn
    out_specs = [pl.BlockSpec((rows_out, tm), lambda i, j: (j, i))]
    out_shape = [jax.ShapeDtypeStruct((nrows // tn * rows_out, n), out_dtype)]
    if norms:
        assert tn == nrows
        out_specs.append(pl.BlockSpec((tn // LANES, tm), lambda i, j: (0, i)))
        out_shape.append(jax.ShapeDtypeStruct((nrows // LANES, n), F32))
    out = pl.pallas_call(
        functools.partial(_proj_kernel, shift=shift, scale=scale, axis=0, norms=norms, value_rows=value_rows),
        grid=(n // tm, nrows // tn),
        in_specs=in_specs,
        out_specs=out_specs,
        out_shape=out_shape,
        compiler_params=_params("parallel", "arbitrary"),
        name="proj_t",
    )(*args)
    return out if norms else out[0]


POOL_HALO = 16


def _proj_pool_kernel(x_ref, xh_ref, w_ref, pw_ref, sc_ref, wt_ref, c_ref, s1_ref, s2_ref, o_ref, t_ref, ext, *,
                      tm, tiles_per_seq, shift):
    ti = pl.program_id(0) % tiles_per_seq
    tail = jnp.dot(x_ref[...], wt_ref[...], preferred_element_type=F32)
    t_ref[...] = (tail * c_ref[...] + pltpu.roll(tail, shift, 1) * s1_ref[...]
                  + pltpu.roll(tail, LANES - shift, 1) * s2_ref[...])
    halo = jnp.dot(xh_ref[...], w_ref[...], preferred_element_type=F32)
    ext[0:POOL_HALO, :] = jnp.where(ti > 0, halo, 0.0)
    ext[POOL_HALO:, :] = jnp.dot(x_ref[...], w_ref[...], preferred_element_type=F32)
    cnt = ti * tm + lax.broadcasted_iota(jnp.int32, (tm, 1), 0) + 1
    for g, win in enumerate(POOL_WINDOWS):
        lo, hi = g * POOL_GROUP_DIM, (g + 1) * POOL_GROUP_DIM
        cur = ext[POOL_HALO:POOL_HALO + tm, lo:hi]
        s = cur
        for j in range(1, win):
            s = s + ext[POOL_HALO - j:POOL_HALO - j + tm, lo:hi]
        d = s / jnp.minimum(cnt, win).astype(F32) - cur
        y = jnp.dot(d.astype(BF16), pw_ref[g], preferred_element_type=F32)
        o_ref[:, lo:hi] = (y * sc_ref[:, lo:hi]).astype(o_ref.dtype)


def _proj_pool(x, w, pool_w, layer, scale, w_tail, tables, shift, seq, tm=1024):
    n, k = x.shape
    c = HALF_WIDTH
    tm = min(tm, seq)
    hb = tm // POOL_HALO
    return pl.pallas_call(
        functools.partial(_proj_pool_kernel, tm=tm, tiles_per_seq=seq // tm, shift=shift),
        grid=(n // tm,),
        in_specs=[pl.BlockSpec((tm, k), lambda i: (i, 0)),
                  pl.BlockSpec((POOL_HALO, k), lambda i: (jnp.maximum(i * hb - 1, 0), 0)),
                  pl.BlockSpec((None, k, c), lambda i: (layer, 0, 0)),
                  pl.BlockSpec((None,) + pool_w.shape[1:], lambda i: (layer, 0, 0, 0)),
                  pl.BlockSpec((1, c), lambda i: (0, 0)),
                  pl.BlockSpec(w_tail.shape, lambda i: (0, 0))]
                 + [pl.BlockSpec((tm, LANES), lambda i: (i, 0))] * 3,
        out_specs=[pl.BlockSpec((tm, c), lambda i: (i, 0)), pl.BlockSpec((tm, LANES), lambda i: (i, 0))],
        out_shape=[jax.ShapeDtypeStruct((n, c), BF16), jax.ShapeDtypeStruct((n, LANES), F32)],
        scratch_shapes=[pltpu.VMEM((tm + POOL_HALO, c), F32)],
        compiler_params=_params("parallel"),
        name="proj_pool",
    )(x, x, w, pool_w, scale, w_tail, *tables)


def _sortable(v):
    bits = lax.bitcast_convert_type(v, jnp.int32)
    return bits ^ ((bits >> 31) & 0x7FFFFFFF)


def _colsum8(v):
    tk, tq = v.shape
    return v.reshape(tk // SUBLANES, SUBLANES, tq).sum(axis=0)


def _idx_kernel(qi_ref, wi_ref, ki_ref, o_ref, keys, half, *, tq, tk, nk, topk):
    i = pl.program_id(1)
    nact = (i * tq + tq - 1) // tk + 1
    q_chunk = (i * tq + lax.broadcasted_iota(jnp.int32, (1, tq), 1)) // CHUNK

    def score_tile(c, carry):
        kt = ki_ref[0, pl.ds(pl.multiple_of(c * tk, tk), tk), :]
        acc = jnp.zeros((tk, tq), F32)
        for h in range(IDX_HEADS):
            il = jnp.dot(kt, qi_ref[h * IDX_DIM:(h + 1) * IDX_DIM, :], preferred_element_type=F32)
            acc = acc + jnp.maximum(il, 0.0) * wi_ref[h:h + 1, :]
        k_chunk = (c * tk + lax.broadcasted_iota(jnp.int32, (tk, 1), 0)) // CHUNK
        kk = jnp.where(k_chunk <= q_chunk, _sortable(acc), INT_MIN)
        keys[c] = kk
        half[c] = (kk >> 16).astype(jnp.int16)
        return carry

    lax.fori_loop(0, nact, score_tile, 0)

    def count(pred):
        def body(c, acc):
            return acc + _colsum8(jnp.where(pred(keys[c], c), 1, 0))
        acc = lax.fori_loop(0, nact, body, jnp.zeros((SUBLANES, tq), jnp.int32))
        return jnp.sum(acc, axis=0, keepdims=True)

    def count16(cand, strict):
        cand16 = jnp.broadcast_to(cand, (PACKED_ROWS, tq)).astype(jnp.int16)[None]

        def body(c, acc):
            h3 = half[c].reshape(tk // PACKED_ROWS, PACKED_ROWS, tq)
            hit = (h3 > cand16) if strict else (h3 >= cand16)
            ones = jnp.where(hit, jnp.int16(1), jnp.int16(0))
            for r in range(tk // PACKED_ROWS):
                acc = acc + ones[r]
            return acc

        acc = lax.fori_loop(0, nact, body, jnp.zeros((PACKED_ROWS, tq), jnp.int16))
        return jnp.sum(acc.astype(jnp.int32), axis=0, keepdims=True)

    c0 = count16(jnp.zeros((1, tq), jnp.int32), False)
    sign_ok = c0 >= topk
    state = (jnp.where(sign_ok, 0, -HALF_BIAS), jnp.where(sign_ok, c0, nact * tk), jnp.where(sign_ok, 0, c0))

    def hi_step(b, state):
        hi, ge_hi, n_above = state
        cand = hi | (1 << (14 - b))
        c = count16(cand, False)
        ok = c >= topk
        return jnp.where(ok, cand, hi), jnp.where(ok, c, ge_hi), jnp.where(ok, n_above, c)

    hi, ge_hi, n_above = lax.fori_loop(0, 15, hi_step, state)

    def low_tile(c, carry):
        kk = keys[c]
        half[c] = jnp.where((kk >> 16) == hi, (kk & (2 * HALF_BIAS - 1)) - HALF_BIAS, -HALF_BIAS).astype(jnp.int16)
        return carry

    lax.fori_loop(0, nact, low_tile, 0)

    def lo_step(b, state):
        lo, in_bucket = state
        cand = lo | (1 << (15 - b))
        c = count16(cand - HALF_BIAS, False)
        ok = n_above + c >= topk
        return jnp.where(ok, cand, lo), jnp.where(ok, c, in_bucket)

    lo, in_bucket = lax.fori_loop(0, 16, lo_step, (jnp.zeros((1, tq), jnp.int32), ge_hi - n_above))
    thr = (hi << 16) | lo
    n_ge = jnp.where(thr == INT_MIN, 0, n_above + in_bucket)
    thr = jnp.maximum(thr, INT_MIN + 1)

    def write_tiles(select):
        def body(c, carry):
            o_ref[0, 0, c] = jnp.where(select(keys[c], c), 1.0, 0.0).astype(o_ref.dtype)
            return carry
        lax.fori_loop(0, nact, body, 0)

    has_ties = jnp.max(n_ge) > topk

    @pl.when(jnp.logical_not(has_ties))
    def _():
        write_tiles(lambda kk, c: kk >= thr)

    @pl.when(has_ties)
    def _():
        need = topk - count(lambda kk, c: kk > thr)

        index_bits = (nk * tk - 1).bit_length()
        assert index_bits <= 15

        def key_index(c):
            return c * tk + lax.broadcasted_iota(jnp.int32, (tk, 1), 0)

        def tie_tile(c, carry):
            half[c] = jnp.where(keys[c] == thr, key_index(c), -1).astype(jnp.int16)
            return carry

        lax.fori_loop(0, nact, tie_tile, 0)
        n_tied = count16(jnp.zeros((1, tq), jnp.int32), False)

        def idx_step(b, cut):
            cand = cut | (1 << (index_bits - 1 - b))
            below = n_tied - count16(cand, False)
            return jnp.where(below < need, cand, cut)

        cut = lax.fori_loop(0, index_bits, idx_step, jnp.zeros((1, tq), jnp.int32))
        write_tiles(lambda kk, c: (kk > thr) | ((kk == thr) & (key_index(c) <= cut)))

    def fill_tile(c, carry):
        o_ref[0, 0, c] = jnp.zeros((tk, tq), o_ref.dtype)
        return carry

    lax.fori_loop(nact, nk, fill_tile, 0)


def _idx_mask(qi_t, wi_t, ki, tq, tk):
    b, s, _ = ki.shape
    nq, nk = s // tq, s // tk
    topk = min(DSA_TOPK_MAX, s // 4)
    return pl.pallas_call(
        functools.partial(_idx_kernel, tq=tq, tk=tk, nk=nk, topk=topk),
        grid=(b, nq),
        in_specs=[pl.BlockSpec((IDX_HEADS * IDX_DIM, tq), lambda bi, i: (0, bi * nq + i)),
                  pl.BlockSpec((IDX_HEADS, tq), lambda bi, i: (0, bi * nq + i)),
                  pl.BlockSpec((1, s, IDX_DIM), lambda bi, i: (bi, 0, 0))],
        out_specs=pl.BlockSpec((1, 1, nk, tk, tq), lambda bi, i: (bi, i, 0, 0, 0)),
        out_shape=jax.ShapeDtypeStruct((b, nq, nk, tk, tq), BF16),
        scratch_shapes=[pltpu.VMEM((nk, tk, tq), jnp.int32), pltpu.VMEM((nk, tk, tq), jnp.int16)],
        compiler_params=_params("parallel", "arbitrary"),
        name="idx_mask",
    )(qi_t, wi_t, ki)


FAST_RANGE_LOG2 = 120.0
BOUND_REL, BOUND_ABS = 1.01, 1.0
SPREAD_REL, SPREAD_ABS = 2.05, 8.0


def _head(h):
    return slice(h * HEAD_DIM, (h + 1) * HEAD_DIM)


def _vrows(h):
    return slice(h * V_ROWS, (h + 1) * V_ROWS)


def _att_init(qt_ref, qa_ref, m_sc, acc_sc, qfull):
    m_sc[...] = jnp.full(m_sc.shape, NEG, F32)
    acc_sc[...] = jnp.zeros(acc_sc.shape, F32)
    tq = qfull.shape[2]
    for h in range(ATT_HEADS):
        qfull[h, 0:HEAD_DIM, :] = qt_ref[_head(h), :]
        qfull[h, HEAD_DIM:, :] = jnp.zeros((LANES, tq), BF16)
        lo = HEAD_DIM + h * AUG
        qfull[h, lo:lo + AUG, :] = qa_ref[0, h]


def _scores(k_ref, ka, qfull, h):
    kfull = jnp.concatenate([k_ref[0, :, _head(h)], ka], axis=1)
    return jnp.dot(kfull, qfull[h], preferred_element_type=F32)


def _for_heads(score_fn, step_fn):
    s_next = score_fn(0)
    for h in range(ATT_HEADS):
        s = s_next
        if h + 1 < ATT_HEADS:
            s_next = score_fn(h + 1)
        step_fn(s, h)


def _fast_step(s, keep, vt, h, acc_sc):
    p = jnp.exp2(s).astype(BF16)
    if keep is not None:
        p = p * keep
    acc_sc[h] += jnp.dot(vt, p, preferred_element_type=F32)


def _exact_step(s, vt, h, m_sc, acc_sc):
    m_prev = m_sc[h]
    m_new = jnp.maximum(m_prev, jnp.max(s, axis=0, keepdims=True))
    alpha = jnp.exp2(m_prev - m_new)
    p = jnp.exp2(s - m_new)
    acc_sc[h] = alpha * acc_sc[h] + jnp.dot(vt, p.astype(BF16), preferred_element_type=F32)
    m_sc[h] = m_new


def _att_finish(o_ref, acc_sc):
    for h in range(ATT_HEADS):
        a = acc_sc[h]
        o = (a[0:HEAD_DIM] / a[HEAD_DIM:HEAD_DIM + 1]).T
        o_ref[0, :, _head(h)] = o.astype(o_ref.dtype)


def _att_scratch(tq):
    return [pltpu.VMEM((ATT_HEADS, 1, tq), F32),
            pltpu.VMEM((ATT_HEADS, V_ROWS, tq), F32),
            pltpu.VMEM((ATT_HEADS, 2 * HEAD_DIM, tq), BF16)]


def _pair_tiles(r, j, n):
    second = j > r
    return jnp.where(second, n - 1 - r, r), jnp.where(second, j - r - 1, j)


def _dsa_kernel(fast_ref, k_ref, ka_ref, qt_ref, qa_ref, vt_ref, mask_ref, o_ref, m_sc, acc_sc, qfull, *, n):
    b = pl.program_id(0)
    i, j = _pair_tiles(pl.program_id(1), pl.program_id(2), n)
    fast = fast_ref[b, i] == 1

    @pl.when(j == 0)
    def _():
        _att_init(qt_ref, qa_ref, m_sc, acc_sc, qfull)

    @pl.when(fast)
    def _():
        keep = mask_ref[0, 0, 0]
        ka = ka_ref[0]
        _for_heads(lambda h: _scores(k_ref, ka, qfull, h),
                   lambda s, h: _fast_step(s, keep, vt_ref[_vrows(h), :], h, acc_sc))

    @pl.when(jnp.logical_not(fast))
    def _():
        bias = (mask_ref[0, 0, 0].astype(F32) - 1.0) * (-NEG)
        ka = ka_ref[0]
        _for_heads(lambda h: _scores(k_ref, ka, qfull, h) + bias,
                   lambda s, h: _exact_step(s, vt_ref[_vrows(h), :], h, m_sc, acc_sc))

    @pl.when(j == i)
    def _():
        _att_finish(o_ref, acc_sc)


def _dsa(fast, k, k_aug, q_t, q_aug, v_t, mask, t):
    b, s, c = k.shape
    n = s // t
    assert n % 2 == 0, "query tiles are processed in pairs"

    def q_tile(r, j):
        return _pair_tiles(r, j, n)[0]

    def k_tile(r, j):
        return _pair_tiles(r, j, n)[1]

    grid_spec = pltpu.PrefetchScalarGridSpec(
        num_scalar_prefetch=1,
        grid=(b, n // 2, n + 1),
        in_specs=[pl.BlockSpec((1, t, c), lambda bi, r, j, f: (bi, k_tile(r, j), 0)),
                  pl.BlockSpec((1, t, LANES), lambda bi, r, j, f: (bi, k_tile(r, j), 0)),
                  pl.BlockSpec((c, t), lambda bi, r, j, f: (0, bi * n + q_tile(r, j))),
                  pl.BlockSpec((1, ATT_HEADS, AUG, t), lambda bi, r, j, f: (bi, 0, 0, q_tile(r, j))),
                  pl.BlockSpec((ATT_HEADS * V_ROWS, t), lambda bi, r, j, f: (0, bi * n + k_tile(r, j))),
                  pl.BlockSpec((1, 1, 1, t, t), lambda bi, r, j, f: (bi, q_tile(r, j), k_tile(r, j), 0, 0))],
        out_specs=pl.BlockSpec((1, t, c), lambda bi, r, j, f: (bi, q_tile(r, j), 0)),
        scratch_shapes=_att_scratch(t))
    return pl.pallas_call(
        functools.partial(_dsa_kernel, n=n),
        grid_spec=grid_spec,
        out_shape=jax.ShapeDtypeStruct((b, s, c), BF16),
        compiler_params=_params("parallel", "parallel", "arbitrary"),
        name="dsa",
    )(fast, k, k_aug, q_t, q_aug, v_t, mask)


CONV_HALO = 8


def _proj_conv_kernel(x_ref, xh_ref, wu_ref, wb_ref, wc_ref, wf_ref, cw_ref, o_ref, f_ref, ext, *, tm,
                      tiles_per_seq):
    ti = pl.program_id(0) % tiles_per_seq
    x, xh = x_ref[...], xh_ref[...]
    halo = (jnp.dot(xh, wc_ref[...], preferred_element_type=F32)
            * jnp.dot(xh, wu_ref[...], preferred_element_type=F32))
    ext[0:CONV_HALO, :] = jnp.where(ti > 0, halo, 0.0)
    ext[CONV_HALO:, :] = (jnp.dot(x, wc_ref[...], preferred_element_type=F32)
                          * jnp.dot(x, wu_ref[...], preferred_element_type=F32))
    conv = None
    for t in range(CONV_WIDTH):
        off = CONV_HALO - (CONV_WIDTH - 1) + t
        term = ext[off:off + tm, :] * cw_ref[t:t + 1, :]
        conv = term if conv is None else conv + term
    gate_b = jnp.dot(x, wb_ref[...], preferred_element_type=F32)
    o_ref[...] = (gate_b * conv).astype(o_ref.dtype)
    f_ref[...] = jnp.dot(x, wf_ref[...], preferred_element_type=F32)


def _proj_conv(x, w, layer, w_f, conv_w, seq, tm=512):
    n, k = x.shape
    c = HALF_WIDTH
    tm = min(tm, seq)
    hb = tm // CONV_HALO

    def wcol(j):
        return pl.BlockSpec((None, k, c), lambda i: (layer, 0, j))

    return pl.pallas_call(
        functools.partial(_proj_conv_kernel, tm=tm, tiles_per_seq=seq // tm),
        grid=(n // tm,),
        in_specs=[pl.BlockSpec((tm, k), lambda i: (i, 0)),
                  pl.BlockSpec((CONV_HALO, k), lambda i: (jnp.maximum(i * hb - 1, 0), 0)),
                  wcol(0), wcol(1), wcol(2),
                  pl.BlockSpec(w_f.shape, lambda i: (0, 0)),
                  pl.BlockSpec(conv_w.shape, lambda i: (0, 0))],
        out_specs=[pl.BlockSpec((tm, c), lambda i: (i, 0)), pl.BlockSpec((tm, LANES), lambda i: (i, 0))],
        out_shape=[jax.ShapeDtypeStruct((n, c), BF16), jax.ShapeDtypeStruct((n, LANES), F32)],
        scratch_shapes=[pltpu.VMEM((tm + CONV_HALO, c), F32)],
        compiler_params=_params("parallel"),
        name="proj_conv",
    )(x, x, w, w, w, w_f, conv_w)


GATE_CHUNK = 256


def _fgate_kernel(f_ref, fb_ref, hi_ref, mid_ref, lo_ref):
    s = f_ref.shape[1]
    r = lax.broadcasted_iota(jnp.int32, (GATE_CHUNK, GATE_CHUNK), 0)
    c = lax.broadcasted_iota(jnp.int32, (GATE_CHUNK, GATE_CHUNK), 1)
    tri = (c <= r).astype(F32)

    def body(t, carry):
        rows = pl.ds(t * GATE_CHUNK, GATE_CHUNK)
        z = f_ref[0, rows, :] + fb_ref[...]
        log_f = -(jnp.maximum(-z, 0.0) + jnp.log1p(jnp.exp(-jnp.abs(z))))
        cs = jnp.dot(tri, log_f, preferred_element_type=F32, precision=lax.Precision.HIGHEST) + carry
        b2 = cs * LOG2E
        hi = b2.astype(BF16)
        r1 = b2 - hi.astype(F32)
        mid = r1.astype(BF16)
        hi_ref[0, rows, :] = hi
        mid_ref[0, rows, :] = mid
        lo_ref[0, rows, :] = (r1 - mid.astype(F32)).astype(BF16)
        return cs[GATE_CHUNK - 1:GATE_CHUNK, :]

    lax.fori_loop(0, s // GATE_CHUNK, body, jnp.zeros((1, LANES), F32))


def _forget_cumsum(f, fb):
    b, s, c = f.shape
    spec = pl.BlockSpec((1, s, c), lambda bi: (bi, 0, 0))
    return pl.pallas_call(
        _fgate_kernel,
        grid=(b,),
        in_specs=[spec, pl.BlockSpec((1, c), lambda bi: (0, 0))],
        out_specs=[spec, spec, spec],
        out_shape=[jax.ShapeDtypeStruct((b, s, c), BF16)] * 3,
        compiler_params=_params("parallel"),
        name="forget_cumsum",
    )(f, fb)


def _fox_schedule(first, fast, n):
    bsz = first.shape[0]
    steps = bsz * n * (n + 1) // 2
    tiles = jnp.arange(n, dtype=jnp.int32)
    cnt = (tiles[None, :] - first + 1).reshape(-1)
    rows = jnp.arange(bsz * n, dtype=jnp.int32)
    end = jnp.sum(jnp.where(rows[None, :] <= rows[:, None], cnt[None, :], 0), axis=1)
    t = jnp.arange(steps, dtype=jnp.int32)
    tc = jnp.minimum(t, end[-1] - 1)
    row = jnp.sum((end[None, :] <= tc[:, None]).astype(jnp.int32), axis=1)
    rel = tc - (end - cnt)[row]
    q_tile = row % n
    active = t < end[-1]
    flags = active.astype(jnp.int32) + 2 * (active & (rel == cnt[row] - 1)).astype(jnp.int32)
    return jnp.stack([row // n, q_tile, q_tile - rel, flags, fast.reshape(-1)[row]]).astype(jnp.int32)


def _fox_kernel(sched_ref, k_ref, ka_ref, qt_ref, qa_ref, vt_ref, o_ref, m_sc, acc_sc, qfull, *, t):
    step = pl.program_id(0)
    i, kj, flags = sched_ref[1, step], sched_ref[2, step], sched_ref[3, step]
    fast = sched_ref[4, step] == 1
    active = (flags & 1) == 1
    diagonal = active & (kj == i)
    older = active & (kj != i)

    def causal():
        return lax.broadcasted_iota(jnp.int32, (t, t), 0) <= lax.broadcasted_iota(jnp.int32, (t, t), 1)

    def scores(masked):
        ka = ka_ref[0]

        def fn(h):
            s = _scores(k_ref, ka, qfull, h)
            return jnp.where(causal(), s, NEG) if masked else s
        return fn

    def fast_tile(masked):
        _for_heads(scores(masked), lambda s, h: _fast_step(s, None, vt_ref[_vrows(h), :], h, acc_sc))

    def exact_tile(masked):
        _for_heads(scores(masked), lambda s, h: _exact_step(s, vt_ref[_vrows(h), :], h, m_sc, acc_sc))

    @pl.when(diagonal)
    def _():
        _att_init(qt_ref, qa_ref, m_sc, acc_sc, qfull)

    @pl.when(diagonal & fast)
    def _():
        fast_tile(True)

    @pl.when(diagonal & jnp.logical_not(fast))
    def _():
        exact_tile(True)

    @pl.when(older & fast)
    def _():
        fast_tile(False)

    @pl.when(older & jnp.logical_not(fast))
    def _():
        exact_tile(False)

    @pl.when((flags & 2) == 2)
    def _():
        _att_finish(o_ref, acc_sc)


def _fox(first, fast, k, k_aug, q_t, q_aug, v_t, t):
    b, s, c = k.shape
    n = s // t
    sched = _fox_schedule(first, fast, n)
    grid_spec = pltpu.PrefetchScalarGridSpec(
        num_scalar_prefetch=1,
        grid=(sched.shape[1],),
        in_specs=[pl.BlockSpec((1, t, c), lambda st, sc: (sc[0, st], sc[2, st], 0)),
                  pl.BlockSpec((1, t, LANES), lambda st, sc: (sc[0, st], sc[2, st], 0)),
                  pl.BlockSpec((c, t), lambda st, sc: (0, sc[0, st] * n + sc[1, st])),
                  pl.BlockSpec((1, ATT_HEADS, AUG, t), lambda st, sc: (sc[0, st], 0, 0, sc[1, st])),
                  pl.BlockSpec((ATT_HEADS * V_ROWS, t), lambda st, sc: (0, sc[0, st] * n + sc[2, st]))],
        out_specs=pl.BlockSpec((1, t, c), lambda st, sc: (sc[0, st], sc[1, st], 0)),
        scratch_shapes=_att_scratch(t))
    return pl.pallas_call(
        functools.partial(_fox_kernel, t=t),
        grid_spec=grid_spec,
        out_shape=jax.ShapeDtypeStruct((b, s, c), BF16),
        compiler_params=_params("arbitrary"),
        name="fox",
    )(sched, k, k_aug, q_t, q_aug, v_t)


def _mix_xattn_kernel(pa_ref, pb_ref, wm_ref, xf_ref, wq_ref, kt_ref, v_ref, wo_ref, g_ref, b_ref, of_ref, ob_ref):
    half = pa_ref.shape[1]
    mixed = (jnp.dot(pa_ref[...], wm_ref[0:half, :], preferred_element_type=F32)
             + jnp.dot(pb_ref[...], wm_ref[half:, :], preferred_element_type=F32))
    x1 = _layer_norm(ALPHA * xf_ref[...] + mixed, g_ref[0:1, :], b_ref[0:1, :])
    q = jnp.dot(x1.astype(BF16), wq_ref[...], preferred_element_type=F32) * Q_SCALE
    q = q.astype(BF16)
    outs = []
    for h in range(XA_HEADS):
        s = jnp.dot(q[:, _head(h)], kt_ref[0, _head(h), :], preferred_element_type=F32)
        p = jnp.exp2(s - jnp.max(s, axis=1, keepdims=True))
        l = jnp.sum(p, axis=1, keepdims=True)
        pv = jnp.dot(p.astype(BF16), v_ref[0, :, _head(h)], preferred_element_type=F32)
        outs.append((pv / l).astype(BF16))
    o = jnp.concatenate(outs, axis=1)
    y = jnp.dot(o, wo_ref[...], preferred_element_type=F32)
    x2 = _layer_norm(ALPHA * x1 + y, g_ref[1:2, :], b_ref[1:2, :])
    of_ref[...] = x2
    ob_ref[...] = x2.astype(BF16)


def _mix_xattn(pa, pb, w_mix, mix_layer, xf, wq, kt, v, wo, layer, g, b, seq, tm=512):
    n, d = xf.shape
    per_batch = seq // tm
    m = v.shape[1]
    row = lambda c: pl.BlockSpec((tm, c), lambda i: (i, 0))
    return pl.pallas_call(
        _mix_xattn_kernel,
        grid=(n // tm,),
        in_specs=[row(pa.shape[1]), row(pb.shape[1]),
                  pl.BlockSpec((None,) + w_mix.shape[1:], lambda i: (mix_layer, 0, 0)),
                  row(d),
                  pl.BlockSpec((None,) + wq.shape[1:], lambda i: (layer, 0, 0)),
                  pl.BlockSpec((1, XA_WIDTH, m), lambda i: (i // per_batch, 0, 0)),
                  pl.BlockSpec((1, m, XA_WIDTH), lambda i: (i // per_batch, 0, 0)),
                  pl.BlockSpec((None,) + wo.shape[1:], lambda i: (layer, 0, 0)),
                  pl.BlockSpec((2, d), lambda i: (0, 0)),
                  pl.BlockSpec((2, d), lambda i: (0, 0))],
        out_specs=[row(d), row(d)],
        out_shape=[jax.ShapeDtypeStruct((n, d), F32), jax.ShapeDtypeStruct((n, d), BF16)],
        compiler_params=_params("parallel"),
        name="mix_xattn",
    )(pa, pb, w_mix, xf, wq, kt, v, wo, g, b)


def _ffn_kernel(xb_ref, wg_ref, wu_ref, wo_ref, xf_ref, g_ref, b_ref, of_ref, ob_ref, ot_ref, acc):
    j = pl.program_id(1)

    @pl.when(j == 0)
    def _():
        acc[...] = jnp.zeros(acc.shape, F32)

    xb = xb_ref[...]
    gate = jnp.dot(xb, wg_ref[...], preferred_element_type=F32)
    up = jnp.dot(xb, wu_ref[...], preferred_element_type=F32)
    hid = gate * (1.0 / (1.0 + jnp.exp(-gate))) * up
    acc[...] += jnp.dot(hid.astype(BF16), wo_ref[...], preferred_element_type=F32)

    @pl.when(j == pl.num_programs(1) - 1)
    def _():
        y = _layer_norm(ALPHA * xf_ref[...] + acc[...], g_ref[...], b_ref[...])
        of_ref[...] = y
        ob_ref[...] = y.astype(BF16)
        ot_ref[...] = y.T.astype(BF16)


def _ffn(xb, xf, w_in, w_out, layer, g, b, tm=512, th=512):
    n, d = xf.shape
    hidden = w_out.shape[1]
    nh = hidden // th
    return pl.pallas_call(
        _ffn_kernel,
        grid=(n // tm, nh),
        in_specs=[pl.BlockSpec((tm, d), lambda i, j: (i, 0)),
                  pl.BlockSpec((None, d, th), lambda i, j: (layer, 0, j)),
                  pl.BlockSpec((None, d, th), lambda i, j: (layer, 0, j + nh)),
                  pl.BlockSpec((None, th, d), lambda i, j: (layer, j, 0)),
                  pl.BlockSpec((tm, d), lambda i, j: (i, 0)),
                  pl.BlockSpec((1, d), lambda i, j: (0, 0)),
                  pl.BlockSpec((1, d), lambda i, j: (0, 0))],
        out_specs=[pl.BlockSpec((tm, d), lambda i, j: (i, 0)),
                   pl.BlockSpec((tm, d), lambda i, j: (i, 0)),
                   pl.BlockSpec((d, tm), lambda i, j: (0, i))],
        out_shape=[jax.ShapeDtypeStruct((n, d), F32), jax.ShapeDtypeStruct((n, d), BF16),
                   jax.ShapeDtypeStruct((d, n), BF16)],
        scratch_shapes=[pltpu.VMEM((tm, d), F32)],
        compiler_params=_params("parallel", "arbitrary"),
        name="ffn",
    )(xb, w_in, w_in, w_out, xf, g, b)


def _cast_kernel(x_ref, ob_ref, ot_ref):
    x = x_ref[...]
    ob_ref[...] = x.astype(BF16)
    ot_ref[...] = x.T.astype(BF16)


def _cast_both(x, tm=512):
    n, d = x.shape
    return pl.pallas_call(
        _cast_kernel,
        grid=(n // tm,),
        in_specs=[pl.BlockSpec((tm, d), lambda i: (i, 0))],
        out_specs=[pl.BlockSpec((tm, d), lambda i: (i, 0)), pl.BlockSpec((d, tm), lambda i: (0, i))],
        out_shape=[jax.ShapeDtypeStruct((n, d), BF16), jax.ShapeDtypeStruct((d, n), BF16)],
        compiler_params=_params("parallel"),
        name="cast_both",
    )(x)


def _rotary_tables(positions, dh):
    rot = dh // ROPE_FRACTION
    half = rot // 2
    inv_freq = jnp.power(ROPE_THETA, -(jnp.arange(half, dtype=F32) * 2.0 / rot))
    ang = positions.astype(F32)[..., None] * inv_freq
    cos, sin = jnp.cos(ang), jnp.sin(ang)
    zh = jnp.zeros_like(sin)
    rest = jnp.zeros(ang.shape[:-1] + (dh - rot,), F32)
    c = jnp.concatenate([cos, cos, rest + 1.0], axis=-1)
    s_lo = jnp.concatenate([zh, sin, rest], axis=-1)
    s_hi = jnp.concatenate([-sin, zh, rest], axis=-1)
    return tuple(t.reshape(-1, dh) for t in (c, s_lo, s_hi))


def _even_tables(positions):
    n = positions.size
    head = _rotary_tables(positions, HEAD_DIM)
    idx = _rotary_tables(positions, IDX_DIM)
    idx_full = tuple(jnp.tile(t, (1, LANES // IDX_DIM)) for t in idx)
    wi_scale = jnp.concatenate([jnp.full((IDX_HEADS,), IDX_HEADS ** -0.5, F32),
                                jnp.ones((LANES - IDX_DIM - IDX_HEADS,), F32)])
    pad = jnp.zeros((n, LANES - IDX_DIM), F32)
    tail = (jnp.concatenate([idx[0], pad + wi_scale], axis=1),
            jnp.concatenate([idx[1], pad], axis=1),
            jnp.concatenate([idx[2], pad], axis=1))
    return {"k": head,
            "q_t": tuple((t * Q_SCALE).T for t in head),
            "qi_t": tuple((t * IDX_DIM ** -0.5).T for t in idx_full),
            "tail": tail}


def _att_bounds(qn2, kn2, bsz, seq, t):
    n = seq // t
    qn = jnp.sqrt(qn2).reshape(ATT_HEADS, bsz, seq).transpose(1, 2, 0)
    kn = jnp.sqrt(kn2).reshape(bsz, seq, ATT_HEADS)
    k_max = jnp.max(kn, axis=1, keepdims=True)
    neg_m = -(BOUND_REL * qn * k_max + BOUND_ABS)
    qn_tile = jnp.max(qn.reshape(bsz, n, t, ATT_HEADS), axis=2)
    kn_tile = jnp.max(kn.reshape(bsz, n, t, ATT_HEADS), axis=2)
    spread = jnp.max(SPREAD_REL * qn_tile * k_max, axis=-1) + SPREAD_ABS
    fast = (spread <= FAST_RANGE_LOG2).astype(jnp.int32)
    return neg_m.astype(BF16), qn_tile, kn_tile, fast


def _aug_operands(q_entries, k_entries, bsz, seq):
    shape = (bsz, seq, ATT_HEADS)
    one, zero = jnp.ones(shape, BF16), jnp.zeros(shape, BF16)

    def pack(entries):
        cols = [one if e is None else e for e in entries]
        return jnp.stack(cols + [zero] * (AUG - len(cols)), axis=-1)

    return pack(k_entries).reshape(bsz, seq, ATT_HEADS * AUG), pack(q_entries).transpose(0, 2, 3, 1)


def _even_mixer(xb, xt, tables, wb, wt, j, pool_w, pool_scale, bsz, seq, tq):
    n = xb.shape[0]
    hw = HALF_WIDTH
    w_tail = jnp.pad(wb[j, :, 5 * hw:], ((0, 0), (0, LANES - IDX_DIM - IDX_HEADS)))
    a, tail = _proj_pool(xb, wb, pool_w, j, pool_scale.reshape(1, hw), w_tail, tables["tail"], IDX_DIM // 8, seq)
    q_t, qn2 = _proj_t(wt, j, xt, BF16, hw, hw, tables["q_t"], shift=HEAD_DIM // 8, norms=True)
    k, kn2 = _proj(xb, wb, j, BF16, 2 * hw, hw, tables["k"], shift=HEAD_DIM // 8, norms=True)
    v_t = _proj_t(wt, j, xt, BF16, 3 * hw, hw, value_rows=True)
    qi_t = _proj_t(wt, j, xt, BF16, 4 * hw, hw, tables["qi_t"], shift=IDX_DIM // 8)

    ki = tail[:, :IDX_DIM].astype(BF16).reshape(bsz, seq, IDX_DIM)
    wi_t = tail[:, IDX_DIM:IDX_DIM + IDX_HEADS].T

    mask = _idx_mask(qi_t, wi_t, ki, tq, tq)
    neg_m, _, _, fast = _att_bounds(qn2, kn2, bsz, seq, tq)
    k_aug, q_aug = _aug_operands([neg_m], [None], bsz, seq)
    bb = _dsa(fast, k.reshape(bsz, seq, hw), k_aug, q_t, q_aug, v_t, mask, tq)
    return a, bb.reshape(n, hw)


DSA_TILE = 512
FOX_TILE = 512
UNDERFLOW_LOG2 = 160.0


def _fox_first_tile(qn, kn, terms, bsz, seq, t):
    n = seq // t
    f32sum = sum(x.astype(F32) for x in terms)[:, :, :ATT_HEADS].reshape(bsz, n, t, ATT_HEADS)
    f_first, f_last = f32sum[:, :, 0], f32sum[:, :, t - 1]
    bound = (BOUND_REL * qn[:, :, None] * (kn[:, None, :] + kn[:, :, None])
             + f_first[:, :, None] - f_last[:, None, :] + BOUND_ABS)
    tiles = jnp.arange(n, dtype=jnp.int32)
    needed = jnp.any(bound >= -UNDERFLOW_LOG2, axis=-1) | (tiles[:, None] == tiles[None, :])
    needed = needed & (tiles[None, :] <= tiles[:, None])
    return jnp.min(jnp.where(needed, tiles[None, None, :], n), axis=-1).astype(jnp.int32)


def _odd_mixer(xb, xt, wb, wt, j, conv_w, forget_b, bsz, seq):
    n = xb.shape[0]
    hw = HALF_WIDTH
    w_f = jnp.pad(wb[j, :, 6 * hw:], ((0, 0), (0, LANES - ATT_HEADS)))
    c, f = _proj_conv(xb, wb, j, w_f, conv_w, seq)
    q_t, qn2 = _proj_t(wt, j, xt, BF16, 3 * hw, hw, scale=Q_SCALE, norms=True)
    k, kn2 = _proj(xb, wb, j, BF16, 4 * hw, hw, norms=True)
    v_t = _proj_t(wt, j, xt, BF16, 5 * hw, hw, value_rows=True)
    fb = jnp.pad(forget_b, (0, LANES - ATT_HEADS)).reshape(1, LANES)
    terms = _forget_cumsum(f.reshape(bsz, seq, LANES), fb)
    hi, mid, lo = (x[:, :, :ATT_HEADS] for x in terms)

    t = min(FOX_TILE, seq)
    neg_m, qn, kn, fast = _att_bounds(qn2, kn2, bsz, seq, t)
    k_aug, q_aug = _aug_operands([neg_m, None, None, None, hi, mid, lo],
                                 [None, -hi, -mid, -lo, None, None, None], bsz, seq)
    first = _fox_first_tile(qn, kn, terms, bsz, seq, t)
    d = _fox(first, fast, k.reshape(bsz, seq, hw), k_aug, q_t, q_aug, v_t, t)
    return c, d.reshape(n, hw)


def kernel(x, mem, positions, ev_w_in, ev_pool_w, ev_pool_scale, ev_w_out, od_w_in, od_conv_w, od_forget_b,
           od_w_out, ca_w_q, ca_w_kv, ca_w_o, ffn_w_in, ffn_w_out, ln_g, ln_b):
    bsz, seq, d = x.shape
    n = bsz * seq
    m = mem.shape[1]
    tq = min(DSA_TILE, seq)
    xf = x.reshape(n, d)
    xb, xt = _cast_both(xf)
    memb = mem.reshape(bsz * m, d).astype(BF16)
    tables = _even_tables(positions)
    ev_wb, od_wb = ev_w_in.astype(BF16), od_w_in.astype(BF16)
    ev_wt, od_wt = ev_wb.transpose(0, 2, 1), od_wb.transpose(0, 2, 1)
    ev_wo, od_wo, pool_w = ev_w_out.astype(BF16), od_w_out.astype(BF16), ev_pool_w.astype(BF16)
    ca_q, ca_kv, ca_o = ca_w_q.astype(BF16), ca_w_kv.astype(BF16), ca_w_o.astype(BF16)
    ffn_in, ffn_out = ffn_w_in.astype(BF16), ffn_w_out.astype(BF16)
    for i in range(DEPTH):
        j = i // 2
        if i % 2 == 0:
            parts = _even_mixer(xb, xt, tables, ev_wb, ev_wt, j, pool_w, ev_pool_scale[j], bsz, seq, tq)
            w_out = ev_wo
        else:
            parts = _odd_mixer(xb, xt, od_wb, od_wt, j, od_conv_w[j], od_forget_b[j], bsz, seq)
            w_out = od_wo
        kv = _proj(memb, ca_kv, i, BF16, 0, 2 * XA_WIDTH).reshape(bsz, m, 2 * XA_WIDTH)
        kt = kv[:, :, :XA_WIDTH].transpose(0, 2, 1)
        xf, xb = _mix_xattn(parts[0], parts[1], w_out, j, xf, ca_q, kt, kv[:, :, XA_WIDTH:], ca_o, i,
                            ln_g[i, 0:2], ln_b[i, 0:2], seq)
        xf, xb, xt = _ffn(xb, xf, ffn_in, ffn_out, i, ln_g[i, 2:3], ln_b[i, 2:3])
    return xf.reshape(bsz, seq, d)
```

```python
import functools

import jax
import jax.numpy as jnp
from jax import lax
from jax.experimental import pallas as pl
from jax.experimental.pallas import tpu as pltpu

F32 = jnp.float32
BF16 = jnp.bfloat16

D_MODEL = 2048
DEPTH = 4
CHUNK = 64
HEAD_DIM = 128
HALF_WIDTH = D_MODEL // 2
POOL_WINDOWS = (2, 4, 8, 16)
POOL_GROUP_DIM = HALF_WIDTH // len(POOL_WINDOWS)
ATT_HEADS = HALF_WIDTH // HEAD_DIM
IDX_HEADS = 16
IDX_DIM = 64
DSA_TOPK_MAX = 256
CONV_WIDTH = 3
XA_HEADS = 4
XA_WIDTH = XA_HEADS * HEAD_DIM
FFN_HIDDEN = -(-(8 * D_MODEL) // (3 * 256)) * 256
ROPE_THETA = 500000.0
ROPE_FRACTION = 4
LN_EPS = 1e-5
ALPHA = (2 * DEPTH) ** 0.25
LOG2E = 1.4426950408889634
Q_SCALE = HEAD_DIM ** -0.5 * LOG2E

LANES = 128
SUBLANES = 8
PACKED_ROWS = 16
NEG = -1e30
INT_MIN = -(2 ** 31)
VMEM_LIMIT = 60000 * 1024


def _params(*semantics):
    return pltpu.CompilerParams(dimension_semantics=semantics, vmem_limit_bytes=VMEM_LIMIT)


def _layer_norm(y, g, b):
    mu = jnp.mean(y, axis=-1, keepdims=True)
    d = y - mu
    var = jnp.mean(d * d, axis=-1, keepdims=True)
    return d * lax.rsqrt(var + LN_EPS) * g + b


AUG = 16
V_ROWS = HEAD_DIM + AUG


def _proj_kernel(*refs, shift, scale, axis, norms, value_rows):
    refs = list(refs)
    a_ref, b_ref = refs[0], refs[1]
    n_ref = refs.pop() if norms else None
    o_ref = refs.pop()
    h = jnp.dot(a_ref[...], b_ref[...], preferred_element_type=F32)
    if not (shift or norms or value_rows):
        o_ref[...] = (h * scale if scale != 1.0 else h).astype(o_ref.dtype)
        return
    if shift:
        c, s1, s2 = refs[2][...], refs[3][...], refs[4][...]
    for g in range(h.shape[axis] // LANES):
        sl = (slice(None),) * axis + (slice(g * LANES, (g + 1) * LANES),)
        hg = h[sl]
        if shift:
            hg = hg * c + pltpu.roll(hg, shift, axis) * s1 + pltpu.roll(hg, LANES - shift, axis) * s2
        elif scale != 1.0:
            hg = hg * scale
        og = hg.astype(o_ref.dtype)
        if value_rows:
            tm = og.shape[1]
            o_ref[g * V_ROWS:g * V_ROWS + LANES, :] = og
            row = lax.broadcasted_iota(jnp.int32, (AUG, tm), 0)
            o_ref[g * V_ROWS + LANES:(g + 1) * V_ROWS, :] = jnp.where(row == 0, 1.0, 0.0).astype(o_ref.dtype)
        else:
            o_ref[sl] = og
        if norms and axis == 0:
            n_ref[g:g + 1, :] = jnp.sum(jnp.square(og.astype(F32)), axis=0, keepdims=True)
    if norms and axis == 1:
        tn = h.shape[1]
        sq = jnp.square(o_ref[...].astype(F32)).astype(BF16)
        owner = lax.broadcasted_iota(jnp.int32, (tn, LANES), 0) // LANES
        ind = jnp.where(owner == lax.broadcasted_iota(jnp.int32, (tn, LANES), 1), 1.0, 0.0).astype(BF16)
        sums = jnp.dot(sq, ind, preferred_element_type=F32)
        n_ref[...] = sums[:, 0:tn // LANES]


def _proj(x, w, layer, out_dtype, col0, ncols, tables=None, shift=0, scale=1.0, norms=False, tm=1024, tn=1024):
    n, k = x.shape
    tm, tn = min(tm, n), min(tn, ncols)
    assert col0 % tn == 0 and ncols % tn == 0
    in_specs = [pl.BlockSpec((tm, k), lambda i, j: (i, 0)),
                pl.BlockSpec((None, k, tn), lambda i, j: (layer, 0, col0 // tn + j))]
    args = [x, w]
    if shift:
        in_specs += [pl.BlockSpec((tm, LANES), lambda i, j: (i, 0))] * 3
        args += list(tables)
    out_specs = [pl.BlockSpec((tm, tn), lambda i, j: (i, j))]
    out_shape = [jax.ShapeDtypeStruct((n, ncols), out_dtype)]
    if norms:
        assert tn == ncols
        out_specs.append(pl.BlockSpec((tm, tn // LANES), lambda i, j: (i, 0)))
        out_shape.append(jax.ShapeDtypeStruct((n, ncols // LANES), F32))
    out = pl.pallas_call(
        functools.partial(_proj_kernel, shift=shift, scale=scale, axis=1, norms=norms, value_rows=False),
        grid=(n // tm, ncols // tn),
        in_specs=in_specs,
        out_specs=out_specs,
        out_shape=out_shape,
        compiler_params=_params("parallel", "arbitrary"),
        name="proj",
    )(*args)
    return out if norms else out[0]


def _proj_t(wt, layer, xt, out_dtype, row0, nrows, tables=None, shift=0, scale=1.0, norms=False,
            value_rows=False, tm=1024, tn=1024):
    k, n = xt.shape
    tm, tn = min(tm, n), min(tn, nrows)
    assert row0 % tn == 0 and nrows % tn == 0
    in_specs = [pl.BlockSpec((None, tn, k), lambda i, j: (layer, row0 // tn + j, 0)),
                pl.BlockSpec((k, tm), lambda i, j: (0, i))]
    args = [wt, xt]
    if shift:
        in_specs += [pl.BlockSpec((LANES, tm), lambda i, j: (0, i))] * 3
        args += list(tables)
    rows_out = tn // LANES * V_ROWS if value_rows else tn
    out_specs = [pl.BlockSpec((rows_out, tm), lambda i, j: (j, i))]
    out_shape = [jax.ShapeDtypeStruct((nrows // tn * rows_out, n), out_dtype)]
    if norms:
        assert tn == nrows
        out_specs.append(pl.BlockSpec((tn // LANES, tm), lambda i, j: (0, i)))
        out_shape.append(jax.ShapeDtypeStruct((nrows // LANES, n), F32))
    out = pl.pallas_call(
        functools.partial(_proj_kernel, shift=shift, scale=scale, axis=0, norms=norms, value_rows=value_rows),
        grid=(n // tm, nrows // tn),
        in_specs=in_specs,
        out_specs=out_specs,
        out_shape=out_shape,
        compiler_params=_params("parallel", "arbitrary"),
        name="proj_t",
    )(*args)
    return out if norms else out[0]


POOL_HALO = 16


def _proj_pool_kernel(x_ref, xh_ref, w_ref, pw_ref, sc_ref, wt_ref, c_ref, s1_ref, s2_ref, o_ref, t_ref, ext, *,
                      tm, tiles_per_seq, shift):
    ti = pl.program_id(0) % tiles_per_seq
    tail = jnp.dot(x_ref[...], wt_ref[...], preferred_element_type=F32)
    t_ref[...] = (tail * c_ref[...] + pltpu.roll(tail, shift, 1) * s1_ref[...]
                  + pltpu.roll(tail, LANES - shift, 1) * s2_ref[...])
    halo = jnp.dot(xh_ref[...], w_ref[...], preferred_element_type=F32)
    ext[0:POOL_HALO, :] = jnp.where(ti > 0, halo, 0.0)
    ext[POOL_HALO:, :] = jnp.dot(x_ref[...], w_ref[...], preferred_element_type=F32)
    cnt = ti * tm + lax.broadcasted_iota(jnp.int32, (tm, 1), 0) + 1
    for g, win in enumerate(POOL_WINDOWS):
        lo, hi = g * POOL_GROUP_DIM, (g + 1) * POOL_GROUP_DIM
        cur = ext[POOL_HALO:POOL_HALO + tm, lo:hi]
        s = cur
        for j in range(1, win):
            s = s + ext[POOL_HALO - j:POOL_HALO - j + tm, lo:hi]
        d = s / jnp.minimum(cnt, win).astype(F32) - cur
        y = jnp.dot(d.astype(BF16), pw_ref[g], preferred_element_type=F32)
        o_ref[:, lo:hi] = (y * sc_ref[:, lo:hi]).astype(o_ref.dtype)


def _proj_pool(x, w, pool_w, layer, scale, w_tail, tables, shift, seq, tm=1024):
    n, k = x.shape
    c = HALF_WIDTH
    tm = min(tm, seq)
    hb = tm // POOL_HALO
    return pl.pallas_call(
        functools.partial(_proj_pool_kernel, tm=tm, tiles_per_seq=seq // tm, shift=shift),
        grid=(n // tm,),
        in_specs=[pl.BlockSpec((tm, k), lambda i: (i, 0)),
                  pl.BlockSpec((POOL_HALO, k), lambda i: (jnp.maximum(i * hb - 1, 0), 0)),
                  pl.BlockSpec((None, k, c), lambda i: (layer, 0, 0)),
                  pl.BlockSpec((None,) + pool_w.shape[1:], lambda i: (layer, 0, 0, 0)),
                  pl.BlockSpec((1, c), lambda i: (0, 0)),
                  pl.BlockSpec(w_tail.shape, lambda i: (0, 0))]
                 + [pl.BlockSpec((tm, LANES), lambda i: (i, 0))] * 3,
        out_specs=[pl.BlockSpec((tm, c), lambda i: (i, 0)), pl.BlockSpec((tm, LANES), lambda i: (i, 0))],
        out_shape=[jax.ShapeDtypeStruct((n, c), BF16), jax.ShapeDtypeStruct((n, LANES), F32)],
        scratch_shapes=[pltpu.VMEM((tm + POOL_HALO, c), F32)],
        compiler_params=_params("parallel"),
        name="proj_pool",
    )(x, x, w, pool_w, scale, w_tail, *tables)


def _sortable(v):
    bits = lax.bitcast_convert_type(v, jnp.int32)
    return bits ^ ((bits >> 31) & 0x7FFFFFFF)


def _colsum8(v):
    tk, tq = v.shape
    return v.reshape(tk // SUBLANES, SUBLANES, tq).sum(axis=0)


def _idx_kernel(qi_ref, wi_ref, ki_ref, o_ref, keys, half, *, tq, tk, nk, topk):
    i = pl.program_id(1)
    nact = (i * tq + tq - 1) // tk + 1
    q_chunk = (i * tq + lax.broadcasted_iota(jnp.int32, (1, tq), 1)) // CHUNK

    def score_tile(c, carry):
        kt = ki_ref[0, pl.ds(pl.multiple_of(c * tk, tk), tk), :]
        acc = jnp.zeros((tk, tq), F32)
        for h in range(IDX_HEADS):
            il = jnp.dot(kt, qi_ref[h * IDX_DIM:(h + 1) * IDX_DIM, :], preferred_element_type=F32)
            acc = acc + jnp.maximum(il, 0.0) * wi_ref[h:h + 1, :]
        k_chunk = (c * tk + lax.broadcasted_iota(jnp.int32, (tk, 1), 0)) // CHUNK
        kk = jnp.where(k_chunk <= q_chunk, _sortable(acc), INT_MIN)
        keys[c] = kk
        half[c] = (kk >> 16).astype(jnp.int16)
        return carry

    lax.fori_loop(0, nact, score_tile, 0)

    def count(pred):
        def body(c, acc):
            return acc + _colsum8(jnp.where(pred(keys[c], c), 1, 0))
        acc = lax.fori_loop(0, nact, body, jnp.zeros((SUBLANES, tq), jnp.int32))
        return jnp.sum(acc, axis=0, keepdims=True)

    def count16(cand, strict):
        cand16 = jnp.broadcast_to(cand, (PACKED_ROWS, tq)).astype(jnp.int16)[None]

        def body(c, acc):
            h3 = half[c].reshape(tk // PACKED_ROWS, PACKED_ROWS, tq)
            hit = (h3 > cand16) if strict else (h3 >= cand16)
            ones = jnp.where(hit, jnp.int16(1), jnp.int16(0))
            for r in range(tk // PACKED_ROWS):
                acc = acc + ones[r]
            return acc

        acc = lax.fori_loop(0, nact, body, jnp.zeros((PACKED_ROWS, tq), jnp.int16))
        return jnp.sum(acc.astype(jnp.int32), axis=0, keepdims=True)

    c0 = count16(jnp.zeros((1, tq), jnp.int32), False)
    sign_ok = c0 >= topk
    state = (jnp.where(sign_ok, 0, -(2 ** 15)), jnp.where(sign_ok, c0, nact * tk), jnp.where(sign_ok, 0, c0))

    def hi_step(b, state):
        hi, ge_hi, n_above = state
        cand = hi | (1 << (14 - b))
        c = count16(cand, False)
        ok = c >= topk
        return jnp.where(ok, cand, hi), jnp.where(ok, c, ge_hi), jnp.where(ok, n_above, c)

    hi, ge_hi, n_above = lax.fori_loop(0, 15, hi_step, state)

    def low_tile(c, carry):
        kk = keys[c]
        half[c] = jnp.where((kk >> 16) == hi, (kk & 0xFFFF) - 2 ** 15, -(2 ** 15)).astype(jnp.int16)
        return carry

    lax.fori_loop(0, nact, low_tile, 0)

    def lo_step(b, state):
        lo, in_bucket = state
        cand = lo | (1 << (15 - b))
        c = count16(cand - 2 ** 15, False)
        ok = n_above + c >= topk
        return jnp.where(ok, cand, lo), jnp.where(ok, c, in_bucket)

    lo, in_bucket = lax.fori_loop(0, 16, lo_step, (jnp.zeros((1, tq), jnp.int32), ge_hi - n_above))
    thr = (hi << 16) | lo
    n_ge = jnp.where(thr == INT_MIN, 0, n_above + in_bucket)
    thr = jnp.maximum(thr, INT_MIN + 1)

    def write_tiles(select):
        def body(c, carry):
            o_ref[0, 0, c] = jnp.where(select(keys[c], c), 1.0, 0.0).astype(o_ref.dtype)
            return carry
        lax.fori_loop(0, nact, body, 0)

    has_ties = jnp.max(n_ge) > topk

    @pl.when(jnp.logical_not(has_ties))
    def _():
        write_tiles(lambda kk, c: kk >= thr)

    @pl.when(has_ties)
    def _():
        need = topk - count(lambda kk, c: kk > thr)

        index_bits = (nk * tk - 1).bit_length()
        assert index_bits <= 15

        def key_index(c):
            return c * tk + lax.broadcasted_iota(jnp.int32, (tk, 1), 0)

        def tie_tile(c, carry):
            half[c] = jnp.where(keys[c] == thr, key_index(c), -1).astype(jnp.int16)
            return carry

        lax.fori_loop(0, nact, tie_tile, 0)
        n_tied = count16(jnp.zeros((1, tq), jnp.int32), False)

        def idx_step(b, cut):
            cand = cut | (1 << (index_bits - 1 - b))
            below = n_tied - count16(cand, False)
            return jnp.where(below < need, cand, cut)

        cut = lax.fori_loop(0, index_bits, idx_step, jnp.zeros((1, tq), jnp.int32))
        write_tiles(lambda kk, c: (kk > thr) | ((kk == thr) & (key_index(c) <= cut)))

    def fill_tile(c, carry):
        o_ref[0, 0, c] = jnp.zeros((tk, tq), o_ref.dtype)
        return carry

    lax.fori_loop(nact, nk, fill_tile, 0)


def _idx_mask(qi_t, wi_t, ki, tq, tk):
    b, s, _ = ki.shape
    nq, nk = s // tq, s // tk
    topk = min(DSA_TOPK_MAX, s // 4)
    return pl.pallas_call(
        functools.partial(_idx_kernel, tq=tq, tk=tk, nk=nk, topk=topk),
        grid=(b, nq),
        in_specs=[pl.BlockSpec((IDX_HEADS * IDX_DIM, tq), lambda bi, i: (0, bi * nq + i)),
                  pl.BlockSpec((IDX_HEADS, tq), lambda bi, i: (0, bi * nq + i)),
                  pl.BlockSpec((1, s, IDX_DIM), lambda bi, i: (bi, 0, 0))],
        out_specs=pl.BlockSpec((1, 1, nk, tk, tq), lambda bi, i: (bi, i, 0, 0, 0)),
        out_shape=jax.ShapeDtypeStruct((b, nq, nk, tk, tq), BF16),
        scratch_shapes=[pltpu.VMEM((nk, tk, tq), jnp.int32), pltpu.VMEM((nk, tk, tq), jnp.int16)],
        compiler_params=_params("parallel", "arbitrary"),
        name="idx_mask",
    )(qi_t, wi_t, ki)


FAST_RANGE_LOG2 = 120.0


def _head(h):
    return slice(h * HEAD_DIM, (h + 1) * HEAD_DIM)


def _vrows(h):
    return slice(h * V_ROWS, (h + 1) * V_ROWS)


def _att_init(qt_ref, qa_ref, m_sc, acc_sc, qfull):
    m_sc[...] = jnp.full(m_sc.shape, NEG, F32)
    acc_sc[...] = jnp.zeros(acc_sc.shape, F32)
    tq = qfull.shape[2]
    for h in range(ATT_HEADS):
        qfull[h, 0:HEAD_DIM, :] = qt_ref[_head(h), :]
        qfull[h, HEAD_DIM:, :] = jnp.zeros((LANES, tq), BF16)
        lo = HEAD_DIM + h * AUG
        qfull[h, lo:lo + AUG, :] = qa_ref[0, h]


def _scores(k_ref, ka, qfull, h):
    kfull = jnp.concatenate([k_ref[0, :, _head(h)], ka], axis=1)
    return jnp.dot(kfull, qfull[h], preferred_element_type=F32)


def _for_heads(score_fn, step_fn):
    s_next = score_fn(0)
    for h in range(ATT_HEADS):
        s = s_next
        if h + 1 < ATT_HEADS:
            s_next = score_fn(h + 1)
        step_fn(s, h)


def _fast_step(s, keep, vt, h, acc_sc):
    p = jnp.exp2(s).astype(BF16)
    if keep is not None:
        p = p * keep
    acc_sc[h] += jnp.dot(vt, p, preferred_element_type=F32)


def _exact_step(s, vt, h, m_sc, acc_sc):
    m_prev = m_sc[h]
    m_new = jnp.maximum(m_prev, jnp.max(s, axis=0, keepdims=True))
    alpha = jnp.exp2(m_prev - m_new)
    p = jnp.exp2(s - m_new)
    acc_sc[h] = alpha * acc_sc[h] + jnp.dot(vt, p.astype(BF16), preferred_element_type=F32)
    m_sc[h] = m_new


def _att_finish(o_ref, acc_sc):
    for h in range(ATT_HEADS):
        a = acc_sc[h]
        o = (a[0:HEAD_DIM] / a[HEAD_DIM:HEAD_DIM + 1]).T
        o_ref[0, :, _head(h)] = o.astype(o_ref.dtype)


def _att_scratch(tq):
    return [pltpu.VMEM((ATT_HEADS, 1, tq), F32),
            pltpu.VMEM((ATT_HEADS, V_ROWS, tq), F32),
            pltpu.VMEM((ATT_HEADS, 2 * HEAD_DIM, tq), BF16)]


def _pair_tiles(r, j, n):
    second = j > r
    return jnp.where(second, n - 1 - r, r), jnp.where(second, j - r - 1, j)


def _dsa_kernel(fast_ref, k_ref, ka_ref, qt_ref, qa_ref, vt_ref, mask_ref, o_ref, m_sc, acc_sc, qfull, *, n):
    b = pl.program_id(0)
    i, j = _pair_tiles(pl.program_id(1), pl.program_id(2), n)
    fast = fast_ref[b, i] == 1

    @pl.when(j == 0)
    def _():
        _att_init(qt_ref, qa_ref, m_sc, acc_sc, qfull)

    @pl.when(fast)
    def _():
        keep = mask_ref[0, 0, 0]
        ka = ka_ref[0]
        _for_heads(lambda h: _scores(k_ref, ka, qfull, h),
                   lambda s, h: _fast_step(s, keep, vt_ref[_vrows(h), :], h, acc_sc))

    @pl.when(jnp.logical_not(fast))
    def _():
        bias = (mask_ref[0, 0, 0].astype(F32) - 1.0) * (-NEG)
        ka = ka_ref[0]
        _for_heads(lambda h: _scores(k_ref, ka, qfull, h) + bias,
                   lambda s, h: _exact_step(s, vt_ref[_vrows(h), :], h, m_sc, acc_sc))

    @pl.when(j == i)
    def _():
        _att_finish(o_ref, acc_sc)


def _dsa(fast, k, k_aug, q_t, q_aug, v_t, mask, t):
    b, s, c = k.shape
    n = s // t
    assert n % 2 == 0, "query tiles are processed in pairs"

    def q_tile(r, j):
        return _pair_tiles(r, j, n)[0]

    def k_tile(r, j):
        return _pair_tiles(r, j, n)[1]

    grid_spec = pltpu.PrefetchScalarGridSpec(
        num_scalar_prefetch=1,
        grid=(b, n // 2, n + 1),
        in_specs=[pl.BlockSpec((1, t, c), lambda bi, r, j, f: (bi, k_tile(r, j), 0)),
                  pl.BlockSpec((1, t, LANES), lambda bi, r, j, f: (bi, k_tile(r, j), 0)),
                  pl.BlockSpec((c, t), lambda bi, r, j, f: (0, bi * n + q_tile(r, j))),
                  pl.BlockSpec((1, ATT_HEADS, AUG, t), lambda bi, r, j, f: (bi, 0, 0, q_tile(r, j))),
                  pl.BlockSpec((ATT_HEADS * V_ROWS, t), lambda bi, r, j, f: (0, bi * n + k_tile(r, j))),
                  pl.BlockSpec((1, 1, 1, t, t), lambda bi, r, j, f: (bi, q_tile(r, j), k_tile(r, j), 0, 0))],
        out_specs=pl.BlockSpec((1, t, c), lambda bi, r, j, f: (bi, q_tile(r, j), 0)),
        scratch_shapes=_att_scratch(t))
    return pl.pallas_call(
        functools.partial(_dsa_kernel, n=n),
        grid_spec=grid_spec,
        out_shape=jax.ShapeDtypeStruct((b, s, c), BF16),
        compiler_params=_params("parallel", "parallel", "arbitrary"),
        name="dsa",
    )(fast, k, k_aug, q_t, q_aug, v_t, mask)


CONV_HALO = 8


def _proj_conv_kernel(x_ref, xh_ref, wu_ref, wb_ref, wc_ref, wf_ref, cw_ref, o_ref, f_ref, ext, *, tm,
                      tiles_per_seq):
    ti = pl.program_id(0) % tiles_per_seq
    x, xh = x_ref[...], xh_ref[...]
    halo = (jnp.dot(xh, wc_ref[...], preferred_element_type=F32)
            * jnp.dot(xh, wu_ref[...], preferred_element_type=F32))
    ext[0:CONV_HALO, :] = jnp.where(ti > 0, halo, 0.0)
    ext[CONV_HALO:, :] = (jnp.dot(x, wc_ref[...], preferred_element_type=F32)
                          * jnp.dot(x, wu_ref[...], preferred_element_type=F32))
    conv = None
    for t in range(CONV_WIDTH):
        off = CONV_HALO - (CONV_WIDTH - 1) + t
        term = ext[off:off + tm, :] * cw_ref[t:t + 1, :]
        conv = term if conv is None else conv + term
    gate_b = jnp.dot(x, wb_ref[...], preferred_element_type=F32)
    o_ref[...] = (gate_b * conv).astype(o_ref.dtype)
    f_ref[...] = jnp.dot(x, wf_ref[...], preferred_element_type=F32)


def _proj_conv(x, w, layer, w_f, conv_w, seq, tm=512):
    n, k = x.shape
    c = HALF_WIDTH
    tm = min(tm, seq)
    hb = tm // CONV_HALO

    def wcol(j):
        return pl.BlockSpec((None, k, c), lambda i: (layer, 0, j))

    return pl.pallas_call(
        functools.partial(_proj_conv_kernel, tm=tm, tiles_per_seq=seq // tm),
        grid=(n // tm,),
        in_specs=[pl.BlockSpec((tm, k), lambda i: (i, 0)),
                  pl.BlockSpec((CONV_HALO, k), lambda i: (jnp.maximum(i * hb - 1, 0), 0)),
                  wcol(0), wcol(1), wcol(2),
                  pl.BlockSpec(w_f.shape, lambda i: (0, 0)),
                  pl.BlockSpec(conv_w.shape, lambda i: (0, 0))],
        out_specs=[pl.BlockSpec((tm, c), lambda i: (i, 0)), pl.BlockSpec((tm, LANES), lambda i: (i, 0))],
        out_shape=[jax.ShapeDtypeStruct((n, c), BF16), jax.ShapeDtypeStruct((n, LANES), F32)],
        scratch_shapes=[pltpu.VMEM((tm + CONV_HALO, c), F32)],
        compiler_params=_params("parallel"),
        name="proj_conv",
    )(x, x, w, w, w, w_f, conv_w)


GATE_CHUNK = 256


def _fgate_kernel(f_ref, fb_ref, hi_ref, mid_ref, lo_ref):
    s = f_ref.shape[1]
    r = lax.broadcasted_iota(jnp.int32, (GATE_CHUNK, GATE_CHUNK), 0)
    c = lax.broadcasted_iota(jnp.int32, (GATE_CHUNK, GATE_CHUNK), 1)
    tri = (c <= r).astype(F32)

    def body(t, carry):
        rows = pl.ds(t * GATE_CHUNK, GATE_CHUNK)
        z = f_ref[0, rows, :] + fb_ref[...]
        log_f = -(jnp.maximum(-z, 0.0) + jnp.log1p(jnp.exp(-jnp.abs(z))))
        cs = jnp.dot(tri, log_f, preferred_element_type=F32, precision=lax.Precision.HIGHEST) + carry
        b2 = cs * LOG2E
        hi = b2.astype(BF16)
        r1 = b2 - hi.astype(F32)
        mid = r1.astype(BF16)
        hi_ref[0, rows, :] = hi
        mid_ref[0, rows, :] = mid
        lo_ref[0, rows, :] = (r1 - mid.astype(F32)).astype(BF16)
        return cs[GATE_CHUNK - 1:GATE_CHUNK, :]

    lax.fori_loop(0, s // GATE_CHUNK, body, jnp.zeros((1, LANES), F32))


def _forget_cumsum(f, fb):
    b, s, c = f.shape
    spec = pl.BlockSpec((1, s, c), lambda bi: (bi, 0, 0))
    return pl.pallas_call(
        _fgate_kernel,
        grid=(b,),
        in_specs=[spec, pl.BlockSpec((1, c), lambda bi: (0, 0))],
        out_specs=[spec, spec, spec],
        out_shape=[jax.ShapeDtypeStruct((b, s, c), BF16)] * 3,
        compiler_params=_params("parallel"),
        name="forget_cumsum",
    )(f, fb)


def _fox_schedule(first, fast, n):
    bsz = first.shape[0]
    steps = bsz * n * (n + 1) // 2
    tiles = jnp.arange(n, dtype=jnp.int32)
    cnt = (tiles[None, :] - first + 1).reshape(-1)
    rows = jnp.arange(bsz * n, dtype=jnp.int32)
    end = jnp.sum(jnp.where(rows[None, :] <= rows[:, None], cnt[None, :], 0), axis=1)
    t = jnp.arange(steps, dtype=jnp.int32)
    tc = jnp.minimum(t, end[-1] - 1)
    row = jnp.sum((end[None, :] <= tc[:, None]).astype(jnp.int32), axis=1)
    rel = tc - (end - cnt)[row]
    q_tile = row % n
    active = t < end[-1]
    flags = active.astype(jnp.int32) + 2 * (active & (rel == cnt[row] - 1)).astype(jnp.int32)
    return jnp.stack([row // n, q_tile, q_tile - rel, flags, fast.reshape(-1)[row]]).astype(jnp.int32)


def _fox_kernel(sched_ref, k_ref, ka_ref, qt_ref, qa_ref, vt_ref, o_ref, m_sc, acc_sc, qfull, *, t):
    step = pl.program_id(0)
    i, kj, flags = sched_ref[1, step], sched_ref[2, step], sched_ref[3, step]
    fast = sched_ref[4, step] == 1
    active = (flags & 1) == 1
    diagonal = active & (kj == i)
    older = active & (kj != i)

    def causal():
        return lax.broadcasted_iota(jnp.int32, (t, t), 0) <= lax.broadcasted_iota(jnp.int32, (t, t), 1)

    def scores(masked):
        ka = ka_ref[0]

        def fn(h):
            s = _scores(k_ref, ka, qfull, h)
            return jnp.where(causal(), s, NEG) if masked else s
        return fn

    def fast_tile(masked):
        _for_heads(scores(masked), lambda s, h: _fast_step(s, None, vt_ref[_vrows(h), :], h, acc_sc))

    def exact_tile(masked):
        _for_heads(scores(masked), lambda s, h: _exact_step(s, vt_ref[_vrows(h), :], h, m_sc, acc_sc))

    @pl.when(diagonal)
    def _():
        _att_init(qt_ref, qa_ref, m_sc, acc_sc, qfull)

    @pl.when(diagonal & fast)
    def _():
        fast_tile(True)

    @pl.when(diagonal & jnp.logical_not(fast))
    def _():
        exact_tile(True)

    @pl.when(older & fast)
    def _():
        fast_tile(False)

    @pl.when(older & jnp.logical_not(fast))
    def _():
        exact_tile(False)

    @pl.when((flags & 2) == 2)
    def _():
        _att_finish(o_ref, acc_sc)


def _fox(first, fast, k, k_aug, q_t, q_aug, v_t, t):
    b, s, c = k.shape
    n = s // t
    sched = _fox_schedule(first, fast, n)
    grid_spec = pltpu.PrefetchScalarGridSpec(
        num_scalar_prefetch=1,
        grid=(sched.shape[1],),
        in_specs=[pl.BlockSpec((1, t, c), lambda st, sc: (sc[0, st], sc[2, st], 0)),
                  pl.BlockSpec((1, t, LANES), lambda st, sc: (sc[0, st], sc[2, st], 0)),
                  pl.BlockSpec((c, t), lambda st, sc: (0, sc[0, st] * n + sc[1, st])),
                  pl.BlockSpec((1, ATT_HEADS, AUG, t), lambda st, sc: (sc[0, st], 0, 0, sc[1, st])),
                  pl.BlockSpec((ATT_HEADS * V_ROWS, t), lambda st, sc: (0, sc[0, st] * n + sc[2, st]))],
        out_specs=pl.BlockSpec((1, t, c), lambda st, sc: (sc[0, st], sc[1, st], 0)),
        scratch_shapes=_att_scratch(t))
    return pl.pallas_call(
        functools.partial(_fox_kernel, t=t),
        grid_spec=grid_spec,
        out_shape=jax.ShapeDtypeStruct((b, s, c), BF16),
        compiler_params=_params("arbitrary"),
        name="fox",
    )(sched, k, k_aug, q_t, q_aug, v_t)


def _mix_xattn_kernel(pa_ref, pb_ref, wm_ref, xf_ref, wq_ref, kt_ref, v_ref, wo_ref, g_ref, b_ref, of_ref, ob_ref):
    half = pa_ref.shape[1]
    mixed = (jnp.dot(pa_ref[...], wm_ref[0:half, :], preferred_element_type=F32)
             + jnp.dot(pb_ref[...], wm_ref[half:, :], preferred_element_type=F32))
    x1 = _layer_norm(ALPHA * xf_ref[...] + mixed, g_ref[0:1, :], b_ref[0:1, :])
    q = jnp.dot(x1.astype(BF16), wq_ref[...], preferred_element_type=F32) * Q_SCALE
    q = q.astype(BF16)
    outs = []
    for h in range(XA_HEADS):
        s = jnp.dot(q[:, _head(h)], kt_ref[0, _head(h), :], preferred_element_type=F32)
        p = jnp.exp2(s - jnp.max(s, axis=1, keepdims=True))
        l = jnp.sum(p, axis=1, keepdims=True)
        pv = jnp.dot(p.astype(BF16), v_ref[0, :, _head(h)], preferred_element_type=F32)
        outs.append((pv / l).astype(BF16))
    o = jnp.concatenate(outs, axis=1)
    y = jnp.dot(o, wo_ref[...], preferred_element_type=F32)
    x2 = _layer_norm(ALPHA * x1 + y, g_ref[1:2, :], b_ref[1:2, :])
    of_ref[...] = x2
    ob_ref[...] = x2.astype(BF16)


def _mix_xattn(pa, pb, w_mix, mix_layer, xf, wq, kt, v, wo, layer, g, b, seq, tm=512):
    n, d = xf.shape
    per_batch = seq // tm
    m = v.shape[1]
    row = lambda c: pl.BlockSpec((tm, c), lambda i: (i, 0))
    return pl.pallas_call(
        _mix_xattn_kernel,
        grid=(n // tm,),
        in_specs=[row(pa.shape[1]), row(pb.shape[1]),
                  pl.BlockSpec((None,) + w_mix.shape[1:], lambda i: (mix_layer, 0, 0)),
                  row(d),
                  pl.BlockSpec((None,) + wq.shape[1:], lambda i: (layer, 0, 0)),
                  pl.BlockSpec((1, XA_WIDTH, m), lambda i: (i // per_batch, 0, 0)),
                  pl.BlockSpec((1, m, XA_WIDTH), lambda i: (i // per_batch, 0, 0)),
                  pl.BlockSpec((None,) + wo.shape[1:], lambda i: (layer, 0, 0)),
                  pl.BlockSpec((2, d), lambda i: (0, 0)),
                  pl.BlockSpec((2, d), lambda i: (0, 0))],
        out_specs=[row(d), row(d)],
        out_shape=[jax.ShapeDtypeStruct((n, d), F32), jax.ShapeDtypeStruct((n, d), BF16)],
        compiler_params=_params("parallel"),
        name="mix_xattn",
    )(pa, pb, w_mix, xf, wq, kt, v, wo, g, b)


def _ffn_kernel(xb_ref, wg_ref, wu_ref, wo_ref, xf_ref, g_ref, b_ref, of_ref, *rest):
    acc = rest[-1]
    j = pl.program_id(1)

    @pl.when(j == 0)
    def _():
        acc[...] = jnp.zeros(acc.shape, F32)

    xb = xb_ref[...]
    gate = jnp.dot(xb, wg_ref[...], preferred_element_type=F32)
    up = jnp.dot(xb, wu_ref[...], preferred_element_type=F32)
    hid = gate * (1.0 / (1.0 + jnp.exp(-gate))) * up
    acc[...] += jnp.dot(hid.astype(BF16), wo_ref[...], preferred_element_type=F32)

    @pl.when(j == pl.num_programs(1) - 1)
    def _():
        y = _layer_norm(ALPHA * xf_ref[...] + acc[...], g_ref[...], b_ref[...])
        of_ref[...] = y
        if len(rest) == 3:
            rest[0][...] = y.astype(BF16)
            rest[1][...] = y.T.astype(BF16)


def _ffn(xb, xf, w_in, w_out, layer, g, b, copies, tm=512, th=512):
    n, d = xf.shape
    hidden = w_out.shape[1]
    nh = hidden // th
    n_out = 3 if copies else 1
    return pl.pallas_call(
        _ffn_kernel,
        grid=(n // tm, nh),
        in_specs=[pl.BlockSpec((tm, d), lambda i, j: (i, 0)),
                  pl.BlockSpec((None, d, th), lambda i, j: (layer, 0, j)),
                  pl.BlockSpec((None, d, th), lambda i, j: (layer, 0, j + nh)),
                  pl.BlockSpec((None, th, d), lambda i, j: (layer, j, 0)),
                  pl.BlockSpec((tm, d), lambda i, j: (i, 0)),
                  pl.BlockSpec((1, d), lambda i, j: (0, 0)),
                  pl.BlockSpec((1, d), lambda i, j: (0, 0))],
        out_specs=[pl.BlockSpec((tm, d), lambda i, j: (i, 0)),
                   pl.BlockSpec((tm, d), lambda i, j: (i, 0)),
                   pl.BlockSpec((d, tm), lambda i, j: (0, i))][:n_out],
        out_shape=[jax.ShapeDtypeStruct((n, d), F32), jax.ShapeDtypeStruct((n, d), BF16),
                   jax.ShapeDtypeStruct((d, n), BF16)][:n_out],
        scratch_shapes=[pltpu.VMEM((tm, d), F32)],
        compiler_params=_params("parallel", "arbitrary"),
        name="ffn",
    )(xb, w_in, w_in, w_out, xf, g, b)


def _cast_kernel(x_ref, ob_ref, ot_ref):
    x = x_ref[...]
    ob_ref[...] = x.astype(BF16)
    ot_ref[...] = x.T.astype(BF16)


def _cast_both(x, tm=512):
    n, d = x.shape
    return pl.pallas_call(
        _cast_kernel,
        grid=(n // tm,),
        in_specs=[pl.BlockSpec((tm, d), lambda i: (i, 0))],
        out_specs=[pl.BlockSpec((tm, d), lambda i: (i, 0)), pl.BlockSpec((d, tm), lambda i: (0, i))],
        out_shape=[jax.ShapeDtypeStruct((n, d), BF16), jax.ShapeDtypeStruct((d, n), BF16)],
        compiler_params=_params("parallel"),
        name="cast_both",
    )(x)


def _rotary_tables(positions, dh):
    rot = dh // ROPE_FRACTION
    half = rot // 2
    inv_freq = jnp.power(ROPE_THETA, -(jnp.arange(half, dtype=F32) * 2.0 / rot))
    ang = positions.astype(F32)[..., None] * inv_freq
    cos, sin = jnp.cos(ang), jnp.sin(ang)
    zh = jnp.zeros_like(sin)
    rest = jnp.zeros(ang.shape[:-1] + (dh - rot,), F32)
    c = jnp.concatenate([cos, cos, rest + 1.0], axis=-1)
    s_lo = jnp.concatenate([zh, sin, rest], axis=-1)
    s_hi = jnp.concatenate([-sin, zh, rest], axis=-1)
    return tuple(t.reshape(-1, dh) for t in (c, s_lo, s_hi))


def _even_tables(positions):
    n = positions.size
    head = _rotary_tables(positions, HEAD_DIM)
    idx = _rotary_tables(positions, IDX_DIM)
    idx_full = tuple(jnp.tile(t, (1, LANES // IDX_DIM)) for t in idx)
    wi_scale = jnp.concatenate([jnp.full((IDX_HEADS,), IDX_HEADS ** -0.5, F32),
                                jnp.ones((LANES - IDX_DIM - IDX_HEADS,), F32)])
    pad = jnp.zeros((n, LANES - IDX_DIM), F32)
    tail = (jnp.concatenate([idx[0], pad + wi_scale], axis=1),
            jnp.concatenate([idx[1], pad], axis=1),
            jnp.concatenate([idx[2], pad], axis=1))
    return {"k": head,
            "q_t": tuple((t * Q_SCALE).T for t in head),
            "qi_t": tuple((t * IDX_DIM ** -0.5).T for t in idx_full),
            "tail": tail}


def _att_bounds(qn2, kn2, bsz, seq, t):
    n = seq // t
    qn = jnp.sqrt(qn2).reshape(ATT_HEADS, bsz, seq).transpose(1, 2, 0)
    kn = jnp.sqrt(kn2).reshape(bsz, seq, ATT_HEADS)
    k_max = jnp.max(kn, axis=1, keepdims=True)
    neg_m = -(1.01 * qn * k_max + 1.0)
    qn_tile = jnp.max(qn.reshape(bsz, n, t, ATT_HEADS), axis=2)
    kn_tile = jnp.max(kn.reshape(bsz, n, t, ATT_HEADS), axis=2)
    spread = jnp.max(2.05 * qn_tile * k_max, axis=-1) + 8.0
    fast = (spread <= FAST_RANGE_LOG2).astype(jnp.int32)
    return neg_m.astype(BF16), qn_tile, kn_tile, fast


def _aug_operands(q_entries, k_entries, bsz, seq):
    one = jnp.ones((bsz, seq, ATT_HEADS), BF16)

    def pack(entries):
        cols = [one if e is None else e for e in entries]
        used = jnp.stack(cols, axis=-1)
        return jnp.pad(used, ((0, 0), (0, 0), (0, 0), (0, AUG - len(cols))))

    return pack(k_entries).reshape(bsz, seq, ATT_HEADS * AUG), pack(q_entries).transpose(0, 2, 3, 1)


def _even_mixer(xb, xt, tables, wb, wt, j, pool_w, pool_scale, bsz, seq, tq):
    n = xb.shape[0]
    hw = HALF_WIDTH
    w_tail = jnp.pad(wb[j, :, 5 * hw:], ((0, 0), (0, LANES - IDX_DIM - IDX_HEADS)))
    a, tail = _proj_pool(xb, wb, pool_w, j, pool_scale.reshape(1, hw), w_tail, tables["tail"], IDX_DIM // 8, seq)
    q_t, qn2 = _proj_t(wt, j, xt, BF16, hw, hw, tables["q_t"], shift=HEAD_DIM // 8, norms=True)
    k, kn2 = _proj(xb, wb, j, BF16, 2 * hw, hw, tables["k"], shift=HEAD_DIM // 8, norms=True)
    v_t = _proj_t(wt, j, xt, BF16, 3 * hw, hw, value_rows=True)
    qi_t = _proj_t(wt, j, xt, BF16, 4 * hw, hw, tables["qi_t"], shift=IDX_DIM // 8)

    ki = tail[:, :IDX_DIM].astype(BF16).reshape(bsz, seq, IDX_DIM)
    wi_t = tail[:, IDX_DIM:IDX_DIM + IDX_HEADS].T

    mask = _idx_mask(qi_t, wi_t, ki, tq, tq)
    neg_m, _, _, fast = _att_bounds(qn2, kn2, bsz, seq, tq)
    k_aug, q_aug = _aug_operands([neg_m], [None], bsz, seq)
    bb = _dsa(fast, k.reshape(bsz, seq, hw), k_aug, q_t, q_aug, v_t, mask, tq)
    return a, bb.reshape(n, hw)


DSA_TILE = 512
FOX_TILE = 512
UNDERFLOW_LOG2 = 160.0


def _fox_first_tile(qn, kn, terms, bsz, seq, t):
    n = seq // t
    f32sum = sum(x.astype(F32) for x in terms)[:, :, :ATT_HEADS].reshape(bsz, n, t, ATT_HEADS)
    f_first, f_last = f32sum[:, :, 0], f32sum[:, :, t - 1]
    bound = (1.01 * qn[:, :, None] * (kn[:, None, :] + kn[:, :, None])
             + f_first[:, :, None] - f_last[:, None, :] + 1.0)
    tiles = jnp.arange(n, dtype=jnp.int32)
    needed = jnp.any(bound >= -UNDERFLOW_LOG2, axis=-1) | (tiles[:, None] == tiles[None, :])
    needed = needed & (tiles[None, :] <= tiles[:, None])
    return jnp.min(jnp.where(needed, tiles[None, None, :], n), axis=-1).astype(jnp.int32)


def _odd_mixer(xb, xt, wb, wt, j, conv_w, forget_b, bsz, seq):
    n = xb.shape[0]
    hw = HALF_WIDTH
    w_f = jnp.pad(wb[j, :, 6 * hw:], ((0, 0), (0, LANES - ATT_HEADS)))
    c, f = _proj_conv(xb, wb, j, w_f, conv_w, seq)
    q_t, qn2 = _proj_t(wt, j, xt, BF16, 3 * hw, hw, scale=Q_SCALE, norms=True)
    k, kn2 = _proj(xb, wb, j, BF16, 4 * hw, hw, norms=True)
    v_t = _proj_t(wt, j, xt, BF16, 5 * hw, hw, value_rows=True)
    fb = jnp.pad(forget_b, (0, LANES - ATT_HEADS)).reshape(1, LANES)
    terms = _forget_cumsum(f.reshape(bsz, seq, LANES), fb)
    hi, mid, lo = (x[:, :, :ATT_HEADS] for x in terms)

    t = min(FOX_TILE, seq)
    neg_m, qn, kn, fast = _att_bounds(qn2, kn2, bsz, seq, t)
    k_aug, q_aug = _aug_operands([neg_m, None, None, None, hi, mid, lo],
                                 [None, -hi, -mid, -lo, None, None, None], bsz, seq)
    first = _fox_first_tile(qn, kn, terms, bsz, seq, t)
    d = _fox(first, fast, k.reshape(bsz, seq, hw), k_aug, q_t, q_aug, v_t, t)
    return c, d.reshape(n, hw)


def kernel(x, mem, positions, ev_w_in, ev_pool_w, ev_pool_scale, ev_w_out, od_w_in, od_conv_w, od_forget_b,
           od_w_out, ca_w_q, ca_w_kv, ca_w_o, ffn_w_in, ffn_w_out, ln_g, ln_b):
    bsz, seq, d = x.shape
    n = bsz * seq
    m = mem.shape[1]
    tq = min(DSA_TILE, seq)
    xf = x.reshape(n, d)
    xb, xt = _cast_both(xf)
    memb = mem.reshape(bsz * m, d).astype(BF16)
    tables = _even_tables(positions)
    ev_wb, od_wb = ev_w_in.astype(BF16), od_w_in.astype(BF16)
    ev_wt, od_wt = ev_wb.transpose(0, 2, 1), od_wb.transpose(0, 2, 1)
    ev_wo, od_wo, pool_w = ev_w_out.astype(BF16), od_w_out.astype(BF16), ev_pool_w.astype(BF16)
    ca_q, ca_kv, ca_o = ca_w_q.astype(BF16), ca_w_kv.astype(BF16), ca_w_o.astype(BF16)
    ffn_in, ffn_out = ffn_w_in.astype(BF16), ffn_w_out.astype(BF16)
    for i in range(DEPTH):
        j = i // 2
        if i % 2 == 0:
            parts = _even_mixer(xb, xt, tables, ev_wb, ev_wt, j, pool_w, ev_pool_scale[j], bsz, seq, tq)
            w_out = ev_wo
        else:
            parts = _odd_mixer(xb, xt, od_wb, od_wt, j, od_conv_w[j], od_forget_b[j], bsz, seq)
            w_out = od_wo
        kv = _proj(memb, ca_kv, i, BF16, 0, 2 * XA_WIDTH).reshape(bsz, m, 2 * XA_WIDTH)
        kt = kv[:, :, :XA_WIDTH].transpose(0, 2, 1)
        xf, xb = _mix_xattn(parts[0], parts[1], w_out, j, xf, ca_q, kt, kv[:, :, XA_WIDTH:], ca_o, i,
                            ln_g[i, 0:2], ln_b[i, 0:2], seq)
        outs = _ffn(xb, xf, ffn_in, ffn_out, i, ln_g[i, 2:3], ln_b[i, 2:3], copies=i + 1 < DEPTH)
        xf = outs[0]
        if i + 1 < DEPTH:
            xb, xt = outs[1], outs[2]
    return xf.reshape(bsz, seq, d)
```

```python
import functools

import jax
import jax.numpy as jnp
from jax import lax
from jax.experimental import pallas as pl
from jax.experimental.pallas import tpu as pltpu

F32 = jnp.float32
BF16 = jnp.bfloat16

D_MODEL = 2048
DEPTH = 4
CHUNK = 64
HEAD_DIM = 128
HALF_WIDTH = D_MODEL // 2
POOL_WINDOWS = (2, 4, 8, 16)
POOL_GROUP_DIM = HALF_WIDTH // len(POOL_WINDOWS)
ATT_HEADS = HALF_WIDTH // HEAD_DIM
IDX_HEADS = 16
IDX_DIM = 64
DSA_TOPK_MAX = 256
CONV_WIDTH = 3
XA_HEADS = 4
XA_WIDTH = XA_HEADS * HEAD_DIM
FFN_HIDDEN = -(-(8 * D_MODEL) // (3 * 256)) * 256
ROPE_THETA = 500000.0
ROPE_FRACTION = 4
LN_EPS = 1e-5
ALPHA = (2 * DEPTH) ** 0.25
LOG2E = 1.4426950408889634
Q_SCALE = HEAD_DIM ** -0.5 * LOG2E

LANES = 128
SUBLANES = 8
PACKED_ROWS = 16
NEG = -1e30
INT_MIN = -(2 ** 31)
VMEM_LIMIT = 60000 * 1024


def _params(*semantics):
    return pltpu.CompilerParams(dimension_semantics=semantics, vmem_limit_bytes=VMEM_LIMIT)


def _layer_norm(y, g, b):
    mu = jnp.mean(y, axis=-1, keepdims=True)
    d = y - mu
    var = jnp.mean(d * d, axis=-1, keepdims=True)
    return d * lax.rsqrt(var + LN_EPS) * g + b


AUG = 16
V_ROWS = HEAD_DIM + AUG


def _proj_kernel(*refs, shift, scale, axis, norms, value_rows):
    refs = list(refs)
    a_ref, b_ref = refs[0], refs[1]
    n_ref = refs.pop() if norms else None
    o_ref = refs.pop()
    h = jnp.dot(a_ref[...], b_ref[...], preferred_element_type=F32)
    if not (shift or norms or value_rows):
        o_ref[...] = (h * scale if scale != 1.0 else h).astype(o_ref.dtype)
        return
    if shift:
        c, s1, s2 = refs[2][...], refs[3][...], refs[4][...]
    for g in range(h.shape[axis] // LANES):
        sl = (slice(None),) * axis + (slice(g * LANES, (g + 1) * LANES),)
        hg = h[sl]
        if shift:
            hg = hg * c + pltpu.roll(hg, shift, axis) * s1 + pltpu.roll(hg, LANES - shift, axis) * s2
        elif scale != 1.0:
            hg = hg * scale
        og = hg.astype(o_ref.dtype)
        if value_rows:
            tm = og.shape[1]
            o_ref[g * V_ROWS:g * V_ROWS + LANES, :] = og
            row = lax.broadcasted_iota(jnp.int32, (AUG, tm), 0)
            o_ref[g * V_ROWS + LANES:(g + 1) * V_ROWS, :] = jnp.where(row == 0, 1.0, 0.0).astype(o_ref.dtype)
        else:
            o_ref[sl] = og
        if norms and axis == 0:
            n_ref[g:g + 1, :] = jnp.sum(jnp.square(og.astype(F32)), axis=0, keepdims=True)
    if norms and axis == 1:
        tn = h.shape[1]
        sq = jnp.square(o_ref[...].astype(F32)).astype(BF16)
        owner = lax.broadcasted_iota(jnp.int32, (tn, LANES), 0) // LANES
        ind = jnp.where(owner == lax.broadcasted_iota(jnp.int32, (tn, LANES), 1), 1.0, 0.0).astype(BF16)
        sums = jnp.dot(sq, ind, preferred_element_type=F32)
        n_ref[...] = sums[:, 0:tn // LANES]


def _proj(x, w, layer, out_dtype, col0, ncols, tables=None, shift=0, scale=1.0, norms=False, tm=1024, tn=1024):
    n, k = x.shape
    tm, tn = min(tm, n), min(tn, ncols)
    assert col0 % tn == 0 and ncols % tn == 0
    in_specs = [pl.BlockSpec((tm, k), lambda i, j: (i, 0)),
                pl.BlockSpec((None, k, tn), lambda i, j: (layer, 0, col0 // tn + j))]
    args = [x, w]
    if shift:
        in_specs += [pl.BlockSpec((tm, LANES), lambda i, j: (i, 0))] * 3
        args += list(tables)
    out_specs = [pl.BlockSpec((tm, tn), lambda i, j: (i, j))]
    out_shape = [jax.ShapeDtypeStruct((n, ncols), out_dtype)]
    if norms:
        assert tn == ncols
        out_specs.append(pl.BlockSpec((tm, tn // LANES), lambda i, j: (i, 0)))
        out_shape.append(jax.ShapeDtypeStruct((n, ncols // LANES), F32))
    out = pl.pallas_call(
        functools.partial(_proj_kernel, shift=shift, scale=scale, axis=1, norms=norms, value_rows=False),
        grid=(n // tm, ncols // tn),
        in_specs=in_specs,
        out_specs=out_specs,
        out_shape=out_shape,
        compiler_params=_params("parallel", "arbitrary"),
        name="proj",
    )(*args)
    return out if norms else out[0]


def _proj_t(wt, layer, xt, out_dtype, row0, nrows, tables=None, shift=0, scale=1.0, norms=False,
            value_rows=False, tm=1024, tn=1024):
    k, n = xt.shape
    tm, tn = min(tm, n), min(tn, nrows)
    assert row0 % tn == 0 and nrows % tn == 0
    in_specs = [pl.BlockSpec((None, tn, k), lambda i, j: (layer, row0 // tn + j, 0)),
                pl.BlockSpec((k, tm), lambda i, j: (0, i))]
    args = [wt, xt]
    if shift:
        in_specs += [pl.BlockSpec((LANES, tm), lambda i, j: (0, i))] * 3
        args += list(tables)
    rows_out = tn // LANES * V_ROWS if value_rows else tn
    out_specs = [pl.BlockSpec((rows_out, tm), lambda i, j: (j, i))]
    out_shape = [jax.ShapeDtypeStruct((nrows // tn * rows_out, n), out_dtype)]
    if norms:
        assert tn == nrows
        out_specs.append(pl.BlockSpec((tn // LANES, tm), lambda i, j: (0, i)))
        out_shape.append(jax.ShapeDtypeStruct((nrows // LANES, n), F32))
    out = pl.pallas_call(
        functools.partial(_proj_kernel, shift=shift, scale=scale, axis=0, norms=norms, value_rows=value_rows),
        grid=(n // tm, nrows // tn),
        in_specs=in_specs,
        out_specs=out_specs,
        out_shape=out_shape,
        compiler_params=_params("parallel", "arbitrary"),
        name="proj_t",
    )(*args)
    return out if norms else out[0]


POOL_HALO = 16


def _proj_pool_kernel(x_ref, xh_ref, w_ref, pw_ref, sc_ref, wt_ref, c_ref, s1_ref, s2_ref, o_ref, t_ref, ext, *,
                      tm, tiles_per_seq, shift):
    ti = pl.program_id(0) % tiles_per_seq
    tail = jnp.dot(x_ref[...], wt_ref[...], preferred_element_type=F32)
    t_ref[...] = (tail * c_ref[...] + pltpu.roll(tail, shift, 1) * s1_ref[...]
                  + pltpu.roll(tail, LANES - shift, 1) * s2_ref[...])
    halo = jnp.dot(xh_ref[...], w_ref[...], preferred_element_type=F32)
    ext[0:POOL_HALO, :] = jnp.where(ti > 0, halo, 0.0)
    ext[POOL_HALO:, :] = jnp.dot(x_ref[...], w_ref[...], preferred_element_type=F32)
    cnt = ti * tm + lax.broadcasted_iota(jnp.int32, (tm, 1), 0) + 1
    for g, win in enumerate(POOL_WINDOWS):
        lo, hi = g * POOL_GROUP_DIM, (g + 1) * POOL_GROUP_DIM
        cur = ext[POOL_HALO:POOL_HALO + tm, lo:hi]
        s = cur
        for j in range(1, win):
            s = s + ext[POOL_HALO - j:POOL_HALO - j + tm, lo:hi]
        d = s / jnp.minimum(cnt, win).astype(F32) - cur
        y = jnp.dot(d.astype(BF16), pw_ref[g], preferred_element_type=F32)
        o_ref[:, lo:hi] = (y * sc_ref[:, lo:hi]).astype(o_ref.dtype)


def _proj_pool(x, w, pool_w, layer, scale, w_tail, tables, shift, seq, tm=1024):
    n, k = x.shape
    c = HALF_WIDTH
    tm = min(tm, seq)
    hb = tm // POOL_HALO
    return pl.pallas_call(
        functools.partial(_proj_pool_kernel, tm=tm, tiles_per_seq=seq // tm, shift=shift),
        grid=(n // tm,),
        in_specs=[pl.BlockSpec((tm, k), lambda i: (i, 0)),
                  pl.BlockSpec((POOL_HALO, k), lambda i: (jnp.maximum(i * hb - 1, 0), 0)),
                  pl.BlockSpec((None, k, c), lambda i: (layer, 0, 0)),
                  pl.BlockSpec((None,) + pool_w.shape[1:], lambda i: (layer, 0, 0, 0)),
                  pl.BlockSpec((1, c), lambda i: (0, 0)),
                  pl.BlockSpec(w_tail.shape, lambda i: (0, 0))]
                 + [pl.BlockSpec((tm, LANES), lambda i: (i, 0))] * 3,
        out_specs=[pl.BlockSpec((tm, c), lambda i: (i, 0)), pl.BlockSpec((tm, LANES), lambda i: (i, 0))],
        out_shape=[jax.ShapeDtypeStruct((n, c), BF16), jax.ShapeDtypeStruct((n, LANES), F32)],
        scratch_shapes=[pltpu.VMEM((tm + POOL_HALO, c), F32)],
        compiler_params=_params("parallel"),
        name="proj_pool",
    )(x, x, w, pool_w, scale, w_tail, *tables)


def _sortable(v):
    bits = lax.bitcast_convert_type(v, jnp.int32)
    return bits ^ ((bits >> 31) & 0x7FFFFFFF)


def _colsum8(v):
    tk, tq = v.shape
    return v.reshape(tk // SUBLANES, SUBLANES, tq).sum(axis=0)


def _idx_kernel(qi_ref, wi_ref, ki_ref, o_ref, keys, half, *, tq, tk, nk, topk):
    i = pl.program_id(1)
    nact = (i * tq + tq - 1) // tk + 1
    q_chunk = (i * tq + lax.broadcasted_iota(jnp.int32, (1, tq), 1)) // CHUNK

    def score_tile(c, carry):
        kt = ki_ref[0, pl.ds(pl.multiple_of(c * tk, tk), tk), :]
        k_chunk = (c * tk + lax.broadcasted_iota(jnp.int32, (tk, 1), 0)) // CHUNK
        for part in range(4):
            cols = slice((part % 2) * (tq // 2), (part % 2 + 1) * (tq // 2))
            rows = slice((part // 2) * (tk // 2), (part // 2 + 1) * (tk // 2))
            acc = jnp.zeros((tk // 2, tq // 2), F32)
            for h in range(IDX_HEADS):
                il = jnp.dot(kt[rows], qi_ref[h * IDX_DIM:(h + 1) * IDX_DIM, cols], preferred_element_type=F32)
                acc = acc + jnp.maximum(il, 0.0) * wi_ref[h:h + 1, cols]
            kk = jnp.where(k_chunk[rows] <= q_chunk[:, cols], _sortable(acc), INT_MIN)
            keys[c, rows, cols] = kk
            half[c, rows, cols] = (kk >> 16).astype(jnp.int16)
        return carry

    lax.fori_loop(0, nact, score_tile, 0)

    def count(pred):
        def body(c, acc):
            return acc + _colsum8(jnp.where(pred(keys[c], c), 1, 0))
        acc = lax.fori_loop(0, nact, body, jnp.zeros((SUBLANES, tq), jnp.int32))
        return jnp.sum(acc, axis=0, keepdims=True)

    def count16(cand, strict):
        cand16 = jnp.broadcast_to(cand, (PACKED_ROWS, tq)).astype(jnp.int16)[None]

        def body(c, acc):
            h3 = half[c].reshape(tk // PACKED_ROWS, PACKED_ROWS, tq)
            hit = (h3 > cand16) if strict else (h3 >= cand16)
            ones = jnp.where(hit, jnp.int16(1), jnp.int16(0))
            for r in range(tk // PACKED_ROWS):
                acc = acc + ones[r]
            return acc

        acc = lax.fori_loop(0, nact, body, jnp.zeros((PACKED_ROWS, tq), jnp.int16))
        return jnp.sum(acc.astype(jnp.int32), axis=0, keepdims=True)

    c0 = count16(jnp.zeros((1, tq), jnp.int32), False)
    sign_ok = c0 >= topk
    state = (jnp.where(sign_ok, 0, -(2 ** 15)), jnp.where(sign_ok, c0, nact * tk), jnp.where(sign_ok, 0, c0))

    def hi_step(b, state):
        hi, ge_hi, n_above = state
        cand = hi | (1 << (14 - b))
        c = count16(cand, False)
        ok = c >= topk
        return jnp.where(ok, cand, hi), jnp.where(ok, c, ge_hi), jnp.where(ok, n_above, c)

    hi, ge_hi, n_above = lax.fori_loop(0, 15, hi_step, state)

    def low_tile(c, carry):
        kk = keys[c]
        half[c] = jnp.where((kk >> 16) == hi, (kk & 0xFFFF) - 2 ** 15, -(2 ** 15)).astype(jnp.int16)
        return carry

    lax.fori_loop(0, nact, low_tile, 0)

    def lo_step(b, state):
        lo, in_bucket = state
        cand = lo | (1 << (15 - b))
        c = count16(cand - 2 ** 15, False)
        ok = n_above + c >= topk
        return jnp.where(ok, cand, lo), jnp.where(ok, c, in_bucket)

    lo, in_bucket = lax.fori_loop(0, 16, lo_step, (jnp.zeros((1, tq), jnp.int32), ge_hi - n_above))
    thr = (hi << 16) | lo
    n_ge = jnp.where(thr == INT_MIN, 0, n_above + in_bucket)
    thr = jnp.maximum(thr, INT_MIN + 1)

    def write_tiles(select):
        def body(c, carry):
            o_ref[0, 0, c] = jnp.where(select(keys[c], c), 1.0, 0.0).astype(o_ref.dtype)
            return carry
        lax.fori_loop(0, nact, body, 0)

    has_ties = jnp.max(n_ge) > topk

    @pl.when(jnp.logical_not(has_ties))
    def _():
        write_tiles(lambda kk, c: kk >= thr)

    @pl.when(has_ties)
    def _():
        need = topk - count(lambda kk, c: kk > thr)

        index_bits = (nk * tk - 1).bit_length()
        assert index_bits <= 15

        def key_index(c):
            return c * tk + lax.broadcasted_iota(jnp.int32, (tk, 1), 0)

        def tie_tile(c, carry):
            half[c] = jnp.where(keys[c] == thr, key_index(c), -1).astype(jnp.int16)
            return carry

        lax.fori_loop(0, nact, tie_tile, 0)
        n_tied = count16(jnp.zeros((1, tq), jnp.int32), False)

        def idx_step(b, cut):
            cand = cut | (1 << (index_bits - 1 - b))
            below = n_tied - count16(cand, False)
            return jnp.where(below < need, cand, cut)

        cut = lax.fori_loop(0, index_bits, idx_step, jnp.zeros((1, tq), jnp.int32))
        write_tiles(lambda kk, c: (kk > thr) | ((kk == thr) & (key_index(c) <= cut)))

    def fill_tile(c, carry):
        o_ref[0, 0, c] = jnp.zeros((tk, tq), o_ref.dtype)
        return carry

    lax.fori_loop(nact, nk, fill_tile, 0)


def _idx_mask(qi_t, wi_t, ki, tq, tk):
    b, s, _ = ki.shape
    nq, nk = s // tq, s // tk
    topk = min(DSA_TOPK_MAX, s // 4)
    return pl.pallas_call(
        functools.partial(_idx_kernel, tq=tq, tk=tk, nk=nk, topk=topk),
        grid=(b, nq),
        in_specs=[pl.BlockSpec((IDX_HEADS * IDX_DIM, tq), lambda bi, i: (0, bi * nq + i)),
                  pl.BlockSpec((IDX_HEADS, tq), lambda bi, i: (0, bi * nq + i)),
                  pl.BlockSpec((1, s, IDX_DIM), lambda bi, i: (bi, 0, 0))],
        out_specs=pl.BlockSpec((1, 1, nk, tk, tq), lambda bi, i: (bi, i, 0, 0, 0)),
        out_shape=jax.ShapeDtypeStruct((b, nq, nk, tk, tq), BF16),
        scratch_shapes=[pltpu.VMEM((nk, tk, tq), jnp.int32), pltpu.VMEM((nk, tk, tq), jnp.int16)],
        compiler_params=_params("parallel", "arbitrary"),
        name="idx_mask",
    )(qi_t, wi_t, ki)


FAST_RANGE_LOG2 = 120.0


def _head(h):
    return slice(h * HEAD_DIM, (h + 1) * HEAD_DIM)


def _vrows(h):
    return slice(h * V_ROWS, (h + 1) * V_ROWS)


def _att_init(qt_ref, qa_ref, m_sc, acc_sc, qfull):
    m_sc[...] = jnp.full(m_sc.shape, NEG, F32)
    acc_sc[...] = jnp.zeros(acc_sc.shape, F32)
    tq = qfull.shape[2]
    for h in range(ATT_HEADS):
        qfull[h, 0:HEAD_DIM, :] = qt_ref[_head(h), :]
        qfull[h, HEAD_DIM:, :] = jnp.zeros((LANES, tq), BF16)
        lo = HEAD_DIM + h * AUG
        qfull[h, lo:lo + AUG, :] = qa_ref[0, h]


def _scores(k_ref, ka, qfull, h):
    kfull = jnp.concatenate([k_ref[0, :, _head(h)], ka], axis=1)
    return jnp.dot(kfull, qfull[h], preferred_element_type=F32)


def _for_heads(score_fn, step_fn):
    s_next = score_fn(0)
    for h in range(ATT_HEADS):
        s = s_next
        if h + 1 < ATT_HEADS:
            s_next = score_fn(h + 1)
        step_fn(s, h)


def _fast_step(s, keep, vt, h, acc_sc):
    p = jnp.exp2(s).astype(BF16)
    if keep is not None:
        p = p * keep
    acc_sc[h] += jnp.dot(vt, p, preferred_element_type=F32)


def _exact_step(s, vt, h, m_sc, acc_sc):
    m_prev = m_sc[h]
    m_new = jnp.maximum(m_prev, jnp.max(s, axis=0, keepdims=True))
    alpha = jnp.exp2(m_prev - m_new)
    p = jnp.exp2(s - m_new)
    acc_sc[h] = alpha * acc_sc[h] + jnp.dot(vt, p.astype(BF16), preferred_element_type=F32)
    m_sc[h] = m_new


def _att_finish(o_ref, acc_sc):
    for h in range(ATT_HEADS):
        a = acc_sc[h]
        o = (a[0:HEAD_DIM] / a[HEAD_DIM:HEAD_DIM + 1]).T
        o_ref[0, :, _head(h)] = o.astype(o_ref.dtype)


def _att_scratch(tq):
    return [pltpu.VMEM((ATT_HEADS, 1, tq), F32),
            pltpu.VMEM((ATT_HEADS, V_ROWS, tq), F32),
            pltpu.VMEM((ATT_HEADS, 2 * HEAD_DIM, tq), BF16)]


def _pair_tiles(r, j, n):
    second = j > r
    return jnp.where(second, n - 1 - r, r), jnp.where(second, j - r - 1, j)


def _dsa_kernel(fast_ref, k_ref, ka_ref, qt_ref, qa_ref, vt_ref, mask_ref, o_ref, m_sc, acc_sc, qfull, *, n):
    b = pl.program_id(0)
    i, j = _pair_tiles(pl.program_id(1), pl.program_id(2), n)
    fast = fast_ref[b, i] == 1

    @pl.when(j == 0)
    def _():
        _att_init(qt_ref, qa_ref, m_sc, acc_sc, qfull)

    @pl.when(fast)
    def _():
        keep = mask_ref[0, 0, 0]
        ka = ka_ref[0]
        _for_heads(lambda h: _scores(k_ref, ka, qfull, h),
                   lambda s, h: _fast_step(s, keep, vt_ref[_vrows(h), :], h, acc_sc))

    @pl.when(jnp.logical_not(fast))
    def _():
        bias = (mask_ref[0, 0, 0].astype(F32) - 1.0) * (-NEG)
        ka = ka_ref[0]
        _for_heads(lambda h: _scores(k_ref, ka, qfull, h) + bias,
                   lambda s, h: _exact_step(s, vt_ref[_vrows(h), :], h, m_sc, acc_sc))

    @pl.when(j == i)
    def _():
        _att_finish(o_ref, acc_sc)


def _dsa(fast, k, k_aug, q_t, q_aug, v_t, mask, t):
    b, s, c = k.shape
    n = s // t
    assert n % 2 == 0, "query tiles are processed in pairs"

    def q_tile(r, j):
        return _pair_tiles(r, j, n)[0]

    def k_tile(r, j):
        return _pair_tiles(r, j, n)[1]

    grid_spec = pltpu.PrefetchScalarGridSpec(
        num_scalar_prefetch=1,
        grid=(b, n // 2, n + 1),
        in_specs=[pl.BlockSpec((1, t, c), lambda bi, r, j, f: (bi, k_tile(r, j), 0)),
                  pl.BlockSpec((1, t, LANES), lambda bi, r, j, f: (bi, k_tile(r, j), 0)),
                  pl.BlockSpec((c, t), lambda bi, r, j, f: (0, bi * n + q_tile(r, j))),
                  pl.BlockSpec((1, ATT_HEADS, AUG, t), lambda bi, r, j, f: (bi, 0, 0, q_tile(r, j))),
                  pl.BlockSpec((ATT_HEADS * V_ROWS, t), lambda bi, r, j, f: (0, bi * n + k_tile(r, j))),
                  pl.BlockSpec((1, 1, 1, t, t), lambda bi, r, j, f: (bi, q_tile(r, j), k_tile(r, j), 0, 0))],
        out_specs=pl.BlockSpec((1, t, c), lambda bi, r, j, f: (bi, q_tile(r, j), 0)),
        scratch_shapes=_att_scratch(t))
    return pl.pallas_call(
        functools.partial(_dsa_kernel, n=n),
        grid_spec=grid_spec,
        out_shape=jax.ShapeDtypeStruct((b, s, c), BF16),
        compiler_params=_params("parallel", "parallel", "arbitrary"),
        name="dsa",
    )(fast, k, k_aug, q_t, q_aug, v_t, mask)


CONV_HALO = 8


def _proj_conv_kernel(x_ref, xh_ref, wu_ref, wb_ref, wc_ref, wf_ref, cw_ref, o_ref, f_ref, ext, *, tm,
                      tiles_per_seq):
    ti = pl.program_id(0) % tiles_per_seq
    x, xh = x_ref[...], xh_ref[...]
    halo = (jnp.dot(xh, wc_ref[...], preferred_element_type=F32)
            * jnp.dot(xh, wu_ref[...], preferred_element_type=F32))
    ext[0:CONV_HALO, :] = jnp.where(ti > 0, halo, 0.0)
    ext[CONV_HALO:, :] = (jnp.dot(x, wc_ref[...], preferred_element_type=F32)
                          * jnp.dot(x, wu_ref[...], preferred_element_type=F32))
    conv = None
    for t in range(CONV_WIDTH):
        off = CONV_HALO - (CONV_WIDTH - 1) + t
        term = ext[off:off + tm, :] * cw_ref[t:t + 1, :]
        conv = term if conv is None else conv + term
    gate_b = jnp.dot(x, wb_ref[...], preferred_element_type=F32)
    o_ref[...] = (gate_b * conv).astype(o_ref.dtype)
    f_ref[...] = jnp.dot(x, wf_ref[...], preferred_element_type=F32)


def _proj_conv(x, w, layer, w_f, conv_w, seq, tm=512):
    n, k = x.shape
    c = HALF_WIDTH
    tm = min(tm, seq)
    hb = tm // CONV_HALO

    def wcol(j):
        return pl.BlockSpec((None, k, c), lambda i: (layer, 0, j))

    return pl.pallas_call(
        functools.partial(_proj_conv_kernel, tm=tm, tiles_per_seq=seq // tm),
        grid=(n // tm,),
        in_specs=[pl.BlockSpec((tm, k), lambda i: (i, 0)),
                  pl.BlockSpec((CONV_HALO, k), lambda i: (jnp.maximum(i * hb - 1, 0), 0)),
                  wcol(0), wcol(1), wcol(2),
                  pl.BlockSpec(w_f.shape, lambda i: (0, 0)),
                  pl.BlockSpec(conv_w.shape, lambda i: (0, 0))],
        out_specs=[pl.BlockSpec((tm, c), lambda i: (i, 0)), pl.BlockSpec((tm, LANES), lambda i: (i, 0))],
        out_shape=[jax.ShapeDtypeStruct((n, c), BF16), jax.ShapeDtypeStruct((n, LANES), F32)],
        scratch_shapes=[pltpu.VMEM((tm + CONV_HALO, c), F32)],
        compiler_params=_params("parallel"),
        name="proj_conv",
    )(x, x, w, w, w, w_f, conv_w)


GATE_CHUNK = 256


def _fgate_kernel(f_ref, fb_ref, hi_ref, mid_ref, lo_ref):
    s = f_ref.shape[1]
    r = lax.broadcasted_iota(jnp.int32, (GATE_CHUNK, GATE_CHUNK), 0)
    c = lax.broadcasted_iota(jnp.int32, (GATE_CHUNK, GATE_CHUNK), 1)
    tri = (c <= r).astype(F32)

    def body(t, carry):
        rows = pl.ds(t * GATE_CHUNK, GATE_CHUNK)
        z = f_ref[0, rows, :] + fb_ref[...]
        log_f = -(jnp.maximum(-z, 0.0) + jnp.log1p(jnp.exp(-jnp.abs(z))))
        cs = jnp.dot(tri, log_f, preferred_element_type=F32, precision=lax.Precision.HIGHEST) + carry
        b2 = cs * LOG2E
        hi = b2.astype(BF16)
        r1 = b2 - hi.astype(F32)
        mid = r1.astype(BF16)
        hi_ref[0, rows, :] = hi
        mid_ref[0, rows, :] = mid
        lo_ref[0, rows, :] = (r1 - mid.astype(F32)).astype(BF16)
        return cs[GATE_CHUNK - 1:GATE_CHUNK, :]

    lax.fori_loop(0, s // GATE_CHUNK, body, jnp.zeros((1, LANES), F32))


def _forget_cumsum(f, fb):
    b, s, c = f.shape
    spec = pl.BlockSpec((1, s, c), lambda bi: (bi, 0, 0))
    return pl.pallas_call(
        _fgate_kernel,
        grid=(b,),
        in_specs=[spec, pl.BlockSpec((1, c), lambda bi: (0, 0))],
        out_specs=[spec, spec, spec],
        out_shape=[jax.ShapeDtypeStruct((b, s, c), BF16)] * 3,
        compiler_params=_params("parallel"),
        name="forget_cumsum",
    )(f, fb)


def _fox_schedule(first, fast, n):
    bsz = first.shape[0]
    steps = bsz * n * (n + 1) // 2
    tiles = jnp.arange(n, dtype=jnp.int32)
    cnt = (tiles[None, :] - first + 1).reshape(-1)
    rows = jnp.arange(bsz * n, dtype=jnp.int32)
    end = jnp.sum(jnp.where(rows[None, :] <= rows[:, None], cnt[None, :], 0), axis=1)
    t = jnp.arange(steps, dtype=jnp.int32)
    tc = jnp.minimum(t, end[-1] - 1)
    row = jnp.sum((end[None, :] <= tc[:, None]).astype(jnp.int32), axis=1)
    rel = tc - (end - cnt)[row]
    q_tile = row % n
    active = t < end[-1]
    flags = active.astype(jnp.int32) + 2 * (active & (rel == cnt[row] - 1)).astype(jnp.int32)
    return jnp.stack([row // n, q_tile, q_tile - rel, flags, fast.reshape(-1)[row]]).astype(jnp.int32)


def _fox_kernel(sched_ref, k_ref, ka_ref, qt_ref, qa_ref, vt_ref, o_ref, m_sc, acc_sc, qfull, *, t):
    step = pl.program_id(0)
    i, kj, flags = sched_ref[1, step], sched_ref[2, step], sched_ref[3, step]
    fast = sched_ref[4, step] == 1
    active = (flags & 1) == 1
    diagonal = active & (kj == i)
    older = active & (kj != i)

    def causal():
        return lax.broadcasted_iota(jnp.int32, (t, t), 0) <= lax.broadcasted_iota(jnp.int32, (t, t), 1)

    def scores(masked):
        ka = ka_ref[0]

        def fn(h):
            s = _scores(k_ref, ka, qfull, h)
            return jnp.where(causal(), s, NEG) if masked else s
        return fn

    def fast_tile(masked):
        _for_heads(scores(masked), lambda s, h: _fast_step(s, None, vt_ref[_vrows(h), :], h, acc_sc))

    def exact_tile(masked):
        _for_heads(scores(masked), lambda s, h: _exact_step(s, vt_ref[_vrows(h), :], h, m_sc, acc_sc))

    @pl.when(diagonal)
    def _():
        _att_init(qt_ref, qa_ref, m_sc, acc_sc, qfull)

    @pl.when(diagonal & fast)
    def _():
        fast_tile(True)

    @pl.when(diagonal & jnp.logical_not(fast))
    def _():
        exact_tile(True)

    @pl.when(older & fast)
    def _():
        fast_tile(False)

    @pl.when(older & jnp.logical_not(fast))
    def _():
        exact_tile(False)

    @pl.when((flags & 2) == 2)
    def _():
        _att_finish(o_ref, acc_sc)


def _fox(first, fast, k, k_aug, q_t, q_aug, v_t, t):
    b, s, c = k.shape
    n = s // t
    sched = _fox_schedule(first, fast, n)
    grid_spec = pltpu.PrefetchScalarGridSpec(
        num_scalar_prefetch=1,
        grid=(sched.shape[1],),
        in_specs=[pl.BlockSpec((1, t, c), lambda st, sc: (sc[0, st], sc[2, st], 0)),
                  pl.BlockSpec((1, t, LANES), lambda st, sc: (sc[0, st], sc[2, st], 0)),
                  pl.BlockSpec((c, t), lambda st, sc: (0, sc[0, st] * n + sc[1, st])),
                  pl.BlockSpec((1, ATT_HEADS, AUG, t), lambda st, sc: (sc[0, st], 0, 0, sc[1, st])),
                  pl.BlockSpec((ATT_HEADS * V_ROWS, t), lambda st, sc: (0, sc[0, st] * n + sc[2, st]))],
        out_specs=pl.BlockSpec((1, t, c), lambda st, sc: (sc[0, st], sc[1, st], 0)),
        scratch_shapes=_att_scratch(t))
    return pl.pallas_call(
        functools.partial(_fox_kernel, t=t),
        grid_spec=grid_spec,
        out_shape=jax.ShapeDtypeStruct((b, s, c), BF16),
        compiler_params=_params("arbitrary"),
        name="fox",
    )(sched, k, k_aug, q_t, q_aug, v_t)


def _mix_xattn_kernel(pa_ref, pb_ref, wm_ref, xf_ref, wq_ref, kt_ref, v_ref, wo_ref, g_ref, b_ref, of_ref, ob_ref):
    half = pa_ref.shape[1]
    mixed = (jnp.dot(pa_ref[...], wm_ref[0:half, :], preferred_element_type=F32)
             + jnp.dot(pb_ref[...], wm_ref[half:, :], preferred_element_type=F32))
    x1 = _layer_norm(ALPHA * xf_ref[...] + mixed, g_ref[0:1, :], b_ref[0:1, :])
    q = jnp.dot(x1.astype(BF16), wq_ref[...], preferred_element_type=F32) * Q_SCALE
    q = q.astype(BF16)
    outs = []
    for h in range(XA_HEADS):
        s = jnp.dot(q[:, _head(h)], kt_ref[0, _head(h), :], preferred_element_type=F32)
        p = jnp.exp2(s - jnp.max(s, axis=1, keepdims=True))
        l = jnp.sum(p, axis=1, keepdims=True)
        pv = jnp.dot(p.astype(BF16), v_ref[0, :, _head(h)], preferred_element_type=F32)
        outs.append((pv / l).astype(BF16))
    o = jnp.concatenate(outs, axis=1)
    y = jnp.dot(o, wo_ref[...], preferred_element_type=F32)
    x2 = _layer_norm(ALPHA * x1 + y, g_ref[1:2, :], b_ref[1:2, :])
    of_ref[...] = x2
    ob_ref[...] = x2.astype(BF16)


def _mix_xattn(pa, pb, w_mix, mix_layer, xf, wq, kt, v, wo, layer, g, b, seq, tm=512):
    n, d = xf.shape
    per_batch = seq // tm
    m = v.shape[1]
    row = lambda c: pl.BlockSpec((tm, c), lambda i: (i, 0))
    return pl.pallas_call(
        _mix_xattn_kernel,
        grid=(n // tm,),
        in_specs=[row(pa.shape[1]), row(pb.shape[1]),
                  pl.BlockSpec((None,) + w_mix.shape[1:], lambda i: (mix_layer, 0, 0)),
                  row(d),
                  pl.BlockSpec((None,) + wq.shape[1:], lambda i: (layer, 0, 0)),
                  pl.BlockSpec((1, XA_WIDTH, m), lambda i: (i // per_batch, 0, 0)),
                  pl.BlockSpec((1, m, XA_WIDTH), lambda i: (i // per_batch, 0, 0)),
                  pl.BlockSpec((None,) + wo.shape[1:], lambda i: (layer, 0, 0)),
                  pl.BlockSpec((2, d), lambda i: (0, 0)),
                  pl.BlockSpec((2, d), lambda i: (0, 0))],
        out_specs=[row(d), row(d)],
        out_shape=[jax.ShapeDtypeStruct((n, d), F32), jax.ShapeDtypeStruct((n, d), BF16)],
        compiler_params=_params("parallel"),
        name="mix_xattn",
    )(pa, pb, w_mix, xf, wq, kt, v, wo, g, b)


def _ffn_kernel(xb_ref, wg_ref, wu_ref, wo_ref, xf_ref, g_ref, b_ref, of_ref, *rest):
    acc = rest[-1]
    j = pl.program_id(1)

    @pl.when(j == 0)
    def _():
        acc[...] = jnp.zeros(acc.shape, F32)

    xb = xb_ref[...]
    gate = jnp.dot(xb, wg_ref[...], preferred_element_type=F32)
    up = jnp.dot(xb, wu_ref[...], preferred_element_type=F32)
    hid = gate * (1.0 / (1.0 + jnp.exp(-gate))) * up
    acc[...] += jnp.dot(hid.astype(BF16), wo_ref[...], preferred_element_type=F32)

    @pl.when(j == pl.num_programs(1) - 1)
    def _():
        y = _layer_norm(ALPHA * xf_ref[...] + acc[...], g_ref[...], b_ref[...])
        of_ref[...] = y
        if len(rest) == 3:
            rest[0][...] = y.astype(BF16)
            rest[1][...] = y.T.astype(BF16)


def _ffn(xb, xf, w_in, w_out, layer, g, b, copies, tm=512, th=512):
    n, d = xf.shape
    hidden = w_out.shape[1]
    nh = hidden // th
    n_out = 3 if copies else 1
    return pl.pallas_call(
        _ffn_kernel,
        grid=(n // tm, nh),
        in_specs=[pl.BlockSpec((tm, d), lambda i, j: (i, 0)),
                  pl.BlockSpec((None, d, th), lambda i, j: (layer, 0, j)),
                  pl.BlockSpec((None, d, th), lambda i, j: (layer, 0, j + nh)),
                  pl.BlockSpec((None, th, d), lambda i, j: (layer, j, 0)),
                  pl.BlockSpec((tm, d), lambda i, j: (i, 0)),
                  pl.BlockSpec((1, d), lambda i, j: (0, 0)),
                  pl.BlockSpec((1, d), lambda i, j: (0, 0))],
        out_specs=[pl.BlockSpec((tm, d), lambda i, j: (i, 0)),
                   pl.BlockSpec((tm, d), lambda i, j: (i, 0)),
                   pl.BlockSpec((d, tm), lambda i, j: (0, i))][:n_out],
        out_shape=[jax.ShapeDtypeStruct((n, d), F32), jax.ShapeDtypeStruct((n, d), BF16),
                   jax.ShapeDtypeStruct((d, n), BF16)][:n_out],
        scratch_shapes=[pltpu.VMEM((tm, d), F32)],
        compiler_params=_params("parallel", "arbitrary"),
        name="ffn",
    )(xb, w_in, w_in, w_out, xf, g, b)


def _cast_kernel(x_ref, ob_ref, ot_ref):
    x = x_ref[...]
    ob_ref[...] = x.astype(BF16)
    ot_ref[...] = x.T.astype(BF16)


def _cast_both(x, tm=512):
    n, d = x.shape
    return pl.pallas_call(
        _cast_kernel,
        grid=(n // tm,),
        in_specs=[pl.BlockSpec((tm, d), lambda i: (i, 0))],
        out_specs=[pl.BlockSpec((tm, d), lambda i: (i, 0)), pl.BlockSpec((d, tm), lambda i: (0, i))],
        out_shape=[jax.ShapeDtypeStruct((n, d), BF16), jax.ShapeDtypeStruct((d, n), BF16)],
        compiler_params=_params("parallel"),
        name="cast_both",
    )(x)


def _rotary_tables(positions, dh):
    rot = dh // ROPE_FRACTION
    half = rot // 2
    inv_freq = jnp.power(ROPE_THETA, -(jnp.arange(half, dtype=F32) * 2.0 / rot))
    ang = positions.astype(F32)[..., None] * inv_freq
    cos, sin = jnp.cos(ang), jnp.sin(ang)
    zh = jnp.zeros_like(sin)
    rest = jnp.zeros(ang.shape[:-1] + (dh - rot,), F32)
    c = jnp.concatenate([cos, cos, rest + 1.0], axis=-1)
    s_lo = jnp.concatenate([zh, sin, rest], axis=-1)
    s_hi = jnp.concatenate([-sin, zh, rest], axis=-1)
    return tuple(t.reshape(-1, dh) for t in (c, s_lo, s_hi))


def _even_tables(positions):
    n = positions.size
    head = _rotary_tables(positions, HEAD_DIM)
    idx = _rotary_tables(positions, IDX_DIM)
    idx_full = tuple(jnp.tile(t, (1, LANES // IDX_DIM)) for t in idx)
    wi_scale = jnp.concatenate([jnp.full((IDX_HEADS,), IDX_HEADS ** -0.5, F32),
                                jnp.ones((LANES - IDX_DIM - IDX_HEADS,), F32)])
    pad = jnp.zeros((n, LANES - IDX_DIM), F32)
    tail = (jnp.concatenate([idx[0], pad + wi_scale], axis=1),
            jnp.concatenate([idx[1], pad], axis=1),
            jnp.concatenate([idx[2], pad], axis=1))
    return {"k": head,
            "q_t": tuple((t * Q_SCALE).T for t in head),
            "qi_t": tuple((t * IDX_DIM ** -0.5).T for t in idx_full),
            "tail": tail}


def _att_bounds(qn2, kn2, bsz, seq, t):
    n = seq // t
    qn = jnp.sqrt(qn2).reshape(ATT_HEADS, bsz, seq).transpose(1, 2, 0)
    kn = jnp.sqrt(kn2).reshape(bsz, seq, ATT_HEADS)
    k_max = jnp.max(kn, axis=1, keepdims=True)
    neg_m = -(1.01 * qn * k_max + 1.0)
    qn_tile = jnp.max(qn.reshape(bsz, n, t, ATT_HEADS), axis=2)
    kn_tile = jnp.max(kn.reshape(bsz, n, t, ATT_HEADS), axis=2)
    spread = jnp.max(2.05 * qn_tile * k_max, axis=-1) + 8.0
    fast = (spread <= FAST_RANGE_LOG2).astype(jnp.int32)
    return neg_m.astype(BF16), qn_tile, kn_tile, fast


def _aug_operands(q_entries, k_entries, bsz, seq):
    one = jnp.ones((bsz, seq, ATT_HEADS), BF16)

    def pack(entries):
        cols = [one if e is None else e for e in entries]
        used = jnp.stack(cols, axis=-1)
        return jnp.pad(used, ((0, 0), (0, 0), (0, 0), (0, AUG - len(cols))))

    return pack(k_entries).reshape(bsz, seq, ATT_HEADS * AUG), pack(q_entries).transpose(0, 2, 3, 1)


def _even_mixer(xb, xt, tables, wb, wt, j, pool_w, pool_scale, bsz, seq, tq):
    n = xb.shape[0]
    hw = HALF_WIDTH
    w_tail = jnp.pad(wb[j, :, 5 * hw:], ((0, 0), (0, LANES - IDX_DIM - IDX_HEADS)))
    a, tail = _proj_pool(xb, wb, pool_w, j, pool_scale.reshape(1, hw), w_tail, tables["tail"], IDX_DIM // 8, seq)
    q_t, qn2 = _proj_t(wt, j, xt, BF16, hw, hw, tables["q_t"], shift=HEAD_DIM // 8, norms=True)
    k, kn2 = _proj(xb, wb, j, BF16, 2 * hw, hw, tables["k"], shift=HEAD_DIM // 8, norms=True)
    v_t = _proj_t(wt, j, xt, BF16, 3 * hw, hw, value_rows=True)
    qi_t = _proj_t(wt, j, xt, BF16, 4 * hw, hw, tables["qi_t"], shift=IDX_DIM // 8)

    ki = tail[:, :IDX_DIM].astype(BF16).reshape(bsz, seq, IDX_DIM)
    wi_t = tail[:, IDX_DIM:IDX_DIM + IDX_HEADS].T

    mask = _idx_mask(qi_t, wi_t, ki, tq, tq)
    neg_m, _, _, fast = _att_bounds(qn2, kn2, bsz, seq, tq)
    k_aug, q_aug = _aug_operands([neg_m], [None], bsz, seq)
    bb = _dsa(fast, k.reshape(bsz, seq, hw), k_aug, q_t, q_aug, v_t, mask, tq)
    return a, bb.reshape(n, hw)


DSA_TILE = 512
FOX_TILE = 512
UNDERFLOW_LOG2 = 160.0


def _fox_first_tile(qn, kn, terms, bsz, seq, t):
    n = seq // t
    f32sum = sum(x.astype(F32) for x in terms)[:, :, :ATT_HEADS].reshape(bsz, n, t, ATT_HEADS)
    f_first, f_last = f32sum[:, :, 0], f32sum[:, :, t - 1]
    bound = (1.01 * qn[:, :, None] * (kn[:, None, :] + kn[:, :, None])
             + f_first[:, :, None] - f_last[:, None, :] + 1.0)
    tiles = jnp.arange(n, dtype=jnp.int32)
    needed = jnp.any(bound >= -UNDERFLOW_LOG2, axis=-1) | (tiles[:, None] == tiles[None, :])
    needed = needed & (tiles[None, :] <= tiles[:, None])
    return jnp.min(jnp.where(needed, tiles[None, None, :], n), axis=-1).astype(jnp.int32)


def _odd_mixer(xb, xt, wb, wt, j, conv_w, forget_b, bsz, seq):
    n = xb.shape[0]
    hw = HALF_WIDTH
    w_f = jnp.pad(wb[j, :, 6 * hw:], ((0, 0), (0, LANES - ATT_HEADS)))
    c, f = _proj_conv(xb, wb, j, w_f, conv_w, seq)
    q_t, qn2 = _proj_t(wt, j, xt, BF16, 3 * hw, hw, scale=Q_SCALE, norms=True)
    k, kn2 = _proj(xb, wb, j, BF16, 4 * hw, hw, norms=True)
    v_t = _proj_t(wt, j, xt, BF16, 5 * hw, hw, value_rows=True)
    fb = jnp.pad(forget_b, (0, LANES - ATT_HEADS)).reshape(1, LANES)
    terms = _forget_cumsum(f.reshape(bsz, seq, LANES), fb)
    hi, mid, lo = (x[:, :, :ATT_HEADS] for x in terms)

    t = min(FOX_TILE, seq)
    neg_m, qn, kn, fast = _att_bounds(qn2, kn2, bsz, seq, t)
    k_aug, q_aug = _aug_operands([neg_m, None, None, None, hi, mid, lo],
                                 [None, -hi, -mid, -lo, None, None, None], bsz, seq)
    first = _fox_first_tile(qn, kn, terms, bsz, seq, t)
    d = _fox(first, fast, k.reshape(bsz, seq, hw), k_aug, q_t, q_aug, v_t, t)
    return c, d.reshape(n, hw)


def kernel(x, mem, positions, ev_w_in, ev_pool_w, ev_pool_scale, ev_w_out, od_w_in, od_conv_w, od_forget_b,
           od_w_out, ca_w_q, ca_w_kv, ca_w_o, ffn_w_in, ffn_w_out, ln_g, ln_b):
    bsz, seq, d = x.shape
    n = bsz * seq
    m = mem.shape[1]
    tq = min(DSA_TILE, seq)
    xf = x.reshape(n, d)
    xb, xt = _cast_both(xf)
    memb = mem.reshape(bsz * m, d).astype(BF16)
    tables = _even_tables(positions)
    ev_wb, od_wb = ev_w_in.astype(BF16), od_w_in.astype(BF16)
    ev_wt, od_wt = ev_wb.transpose(0, 2, 1), od_wb.transpose(0, 2, 1)
    ev_wo, od_wo, pool_w = ev_w_out.astype(BF16), od_w_out.astype(BF16), ev_pool_w.astype(BF16)
    ca_q, ca_kv, ca_o = ca_w_q.astype(BF16), ca_w_kv.astype(BF16), ca_w_o.astype(BF16)
    ffn_in, ffn_out = ffn_w_in.astype(BF16), ffn_w_out.astype(BF16)
    for i in range(DEPTH):
        j = i // 2
        if i % 2 == 0:
            parts = _even_mixer(xb, xt, tables, ev_wb, ev_wt, j, pool_w, ev_pool_scale[j], bsz, seq, tq)
            w_out = ev_wo
        else:
            parts = _odd_mixer(xb, xt, od_wb, od_wt, j, od_conv_w[j], od_forget_b[j], bsz, seq)
            w_out = od_wo
        kv = _proj(memb, ca_kv, i, BF16, 0, 2 * XA_WIDTH).reshape(bsz, m, 2 * XA_WIDTH)
        kt = kv[:, :, :XA_WIDTH].transpose(0, 2, 1)
        xf, xb = _mix_xattn(parts[0], parts[1], w_out, j, xf, ca_q, kt, kv[:, :, XA_WIDTH:], ca_o, i,
                            ln_g[i, 0:2], ln_b[i, 0:2], seq)
        outs = _ffn(xb, xf, ffn_in, ffn_out, i, ln_g[i, 2:3], ln_b[i, 2:3], copies=i + 1 < DEPTH)
        xf = outs[0]
        if i + 1 < DEPTH:
            xb, xt = outs[1], outs[2]
    return xf.reshape(bsz, seq, d)
```
